```python
import jax, jax.numpy as jnp
from jax import lax
import numpy as np

D_MODEL = 1024
BATCH = 8
SEQ = 16384
DEPTH = 4

N_MIXERS = 2
N_A_LAYERS = (DEPTH + 1) // 2
N_B_LAYERS = DEPTH // 2
RMS_EPS = 1e-6
NEG_INF = -1e30

A_GROUPS = ((128, 1), (512, 4), (2048, 16))
A_N_GROUPS = len(A_GROUPS)
A_HEADS = 16
A_HEAD_DIM = D_MODEL // A_HEADS
A_WIDTH = A_HEADS * A_HEAD_DIM
A_IN_WIDTH = A_N_GROUPS * 3 * A_WIDTH
ROPE_THETA = 10000.0

B_HEADS = 4
B_KEY_DIM = D_MODEL // 2 // B_HEADS
B_VAL_DIM = D_MODEL // B_HEADS
B_QK_WIDTH = B_HEADS * B_KEY_DIM
B_V_WIDTH = B_HEADS * B_VAL_DIM
B_GATE_RANK = 16
B_GATE_TAU = 16.0
B_CHUNK = 64
B_IN_WIDTH = 2 * B_QK_WIDTH + 2 * B_V_WIDTH + 2 * B_GATE_RANK

FFN_HIDDEN = -(-8 * D_MODEL // (3 * 256)) * 256

kernel_name = "hybrid_dilated_attn_gla_encoder"


def rms_norm(x, gain):
    xf = x.astype(jnp.float32)
    y = xf * lax.rsqrt(jnp.mean(xf * xf, axis=-1, keepdims=True) + RMS_EPS)
    return (y * gain.astype(jnp.float32)).astype(x.dtype)


def rope(x, positions):
    half = x.shape[-1] // 2
    inv_freq = ROPE_THETA ** (-jnp.arange(half, dtype=jnp.float32) / half)
    ang = positions.astype(jnp.float32)[:, None] * inv_freq[None, :]
    cos = jnp.cos(ang)[:, None, :]
    sin = jnp.sin(ang)[:, None, :]
    xf = x.astype(jnp.float32)
    x1, x2 = xf[..., :half], xf[..., half:]
    return jnp.concatenate([x1 * cos - x2 * sin, x2 * cos + x1 * sin], axis=-1).astype(x.dtype)


def dilated_window_attention(q, k, v, window, dilation):
    bsz, seq, nh, dh = q.shape
    half = window // (2 * dilation)
    L = seq // dilation
    nb = -(-L // half)
    Lp = nb * half

    def to_phase(t):
        t = t.reshape(bsz, L, dilation, nh, dh)
        return jnp.moveaxis(t, 2, 1).reshape(bsz * dilation, L, nh, dh)

    n = bsz * dilation
    qp = jnp.pad(to_phase(q), ((0, 0), (0, Lp - L), (0, 0), (0, 0))).reshape(n, nb, half, nh, dh)

    def key_blocks(t):
        t = jnp.pad(to_phase(t), ((0, 0), (half, Lp - L + half), (0, 0), (0, 0)))
        t = t.reshape(n, nb + 2, half, nh, dh)
        return jnp.concatenate([t[:, :-2], t[:, 1:-1], t[:, 2:]], axis=2)

    kb = key_blocks(k)
    vb = key_blocks(v)
    tq = jnp.arange(nb)[:, None] * half + jnp.arange(half)[None, :]
    tk = jnp.arange(nb)[:, None] * half + jnp.arange(3 * half)[None, :] - half
    dist = tk[:, None, :] - tq[:, :, None]
    valid = (tk[:, None, :] >= 0) & (tk[:, None, :] < L) & (jnp.abs(dist) <= half)

    scores = jnp.einsum("nbqhd,nbkhd->nbhqk", qp.astype(jnp.float32), kb.astype(jnp.float32)) * (dh ** -0.5)
    scores = jnp.where(valid[None, :, None], scores, NEG_INF)
    m = jnp.max(scores, axis=-1, keepdims=True)
    p = jnp.exp(scores - m)
    l = jnp.sum(p, axis=-1)
    o = jnp.einsum("nbhqk,nbkhd->nbqhd", p, vb.astype(jnp.float32))
    o = o / jnp.moveaxis(l, 2, 3)[..., None]
    lse = jnp.moveaxis(m[..., 0] + jnp.log(l), 2, 3)

    def from_phase(t):
        rest = t.shape[4:]
        t = t.reshape(bsz, dilation, Lp, nh, *rest)[:, :, :L]
        return jnp.moveaxis(t, 1, 2).reshape(bsz, seq, nh, *rest)

    return from_phase(o), from_phase(lse)


def dilated_attention_mixer(h, w_in, q_gain, k_gain, w_out, positions):
    bsz, seq, _ = h.shape
    qkv = (h @ w_in).reshape(bsz, seq, A_N_GROUPS, 3, A_HEADS, A_HEAD_DIM)
    outs, lses = [], []
    for g, (window, dilation) in enumerate(A_GROUPS):
        q = rope(rms_norm(qkv[:, :, g, 0], q_gain[g]), positions)
        k = rope(rms_norm(qkv[:, :, g, 1], k_gain[g]), positions)
        v = qkv[:, :, g, 2]
        o, lse = dilated_window_attention(q, k, v, window, dilation)
        outs.append(o)
        lses.append(lse)
    alpha = jax.nn.softmax(jnp.stack(lses, axis=0), axis=0)
    out = jnp.sum(alpha[..., None] * jnp.stack(outs, axis=0), axis=0)
    return out.reshape(bsz, seq, A_WIDTH).astype(h.dtype) @ w_out


def gla_chunk(q, k, v, log_a, strict):
    bsz, nh, seq, dk = q.shape
    dv = v.shape[-1]
    nc = seq // B_CHUNK
    q = q.astype(jnp.float32).reshape(bsz, nh, nc, B_CHUNK, dk)
    k = k.astype(jnp.float32).reshape(bsz, nh, nc, B_CHUNK, dk)
    v = v.astype(jnp.float32).reshape(bsz, nh, nc, B_CHUNK, dv)
    b = jnp.cumsum(log_a.astype(jnp.float32).reshape(bsz, nh, nc, B_CHUNK, dk), axis=3)
    b_last = b[..., -1:, :]
    q_t = q * jnp.exp(b)
    k_t = k * jnp.exp(-b)
    k_end = k * jnp.exp(b_last - b)
    mask = jnp.tril(jnp.ones((B_CHUNK, B_CHUNK), dtype=bool), k=-1 if strict else 0)
    attn = jnp.where(mask, jnp.einsum("bhncd,bhnsd->bhncs", q_t, k_t), 0.0)
    o_intra = jnp.einsum("bhncs,bhnse->bhnce", attn, v)
    chunk_kv = jnp.einsum("bhncd,bhnce->bhnde", k_end, v)
    decay = jnp.exp(b_last[..., 0, :])

    def step(state, inp):
        kv_n, dec_n = inp
        return dec_n[..., None] * state + kv_n, state

    init = jnp.zeros((bsz, nh, dk, dv), jnp.float32)
    _, s_in = lax.scan(step, init, (jnp.moveaxis(chunk_kv, 2, 0), jnp.moveaxis(decay, 2, 0)))
    s_in = jnp.moveaxis(s_in, 0, 2)
    o_inter = jnp.einsum("bhncd,bhnde->bhnce", q_t, s_in)
    return (o_intra + o_inter).reshape(bsz, nh, seq, dv)


def _heads(t, nh):
    bsz, seq, _ = t.shape
    return t.reshape(bsz, seq, nh, -1).transpose(0, 2, 1, 3)


def gla_mixer(h, w_in, w_gate_f, bias_gate_f, w_gate_b, bias_gate_b, out_gain, w_out):
    bsz, seq, _ = h.shape
    proj = h @ w_in
    cuts = np.cumsum([B_QK_WIDTH, B_QK_WIDTH, B_V_WIDTH, B_V_WIDTH, B_GATE_RANK]).tolist()
    q, k, v, r, zf, zb = jnp.split(proj, cuts, axis=-1)
    q = _heads(q, B_HEADS) * (B_KEY_DIM ** -0.5)
    k = _heads(k, B_HEADS)
    v = _heads(v, B_HEADS)
    log_af = jax.nn.log_sigmoid((zf @ w_gate_f + bias_gate_f).astype(jnp.float32)) / B_GATE_TAU
    log_ab = jax.nn.log_sigmoid((zb @ w_gate_b + bias_gate_b).astype(jnp.float32)) / B_GATE_TAU
    log_af = _heads(log_af, B_HEADS)
    log_ab = _heads(log_ab, B_HEADS)
    o_f = gla_chunk(q, k, v, log_af, strict=False)
    flip = lambda t: jnp.flip(t, axis=2)
    o_b = flip(gla_chunk(flip(q), flip(k), flip(v), flip(log_ab), strict=True))
    o = (o_f + o_b).transpose(0, 2, 1, 3)
    o = rms_norm(o, out_gain).reshape(bsz, seq, B_V_WIDTH)
    o = o * jax.nn.silu(r.astype(jnp.float32))
    return o.astype(h.dtype) @ w_out


def swiglu(h, w_gate_up, w_down):
    g, u = jnp.split(h @ w_gate_up, 2, axis=-1)
    return (jax.nn.silu(g) * u) @ w_down


def _fwd_setup_inputs(seed: int = 0) -> dict:
    key = jax.random.key(seed)
    ks = jax.random.split(key, 16)

    def nrm(k, shape, scale):
        return jax.random.normal(k, shape, jnp.float32) * scale

    return {
        "x": nrm(ks[0], (BATCH, SEQ, D_MODEL), 1.0),
        "attn_norm": 1.0 + nrm(ks[1], (DEPTH, D_MODEL), 0.02),
        "ffn_norm": 1.0 + nrm(ks[2], (DEPTH, D_MODEL), 0.02),
        "a_w_in": nrm(ks[3], (N_A_LAYERS, D_MODEL, A_IN_WIDTH), D_MODEL ** -0.5),
        "a_q_norm": 1.0 + nrm(ks[4], (N_A_LAYERS, A_N_GROUPS, A_HEAD_DIM), 0.02),
        "a_k_norm": 1.0 + nrm(ks[5], (N_A_LAYERS, A_N_GROUPS, A_HEAD_DIM), 0.02),
        "a_w_out": nrm(ks[6], (N_A_LAYERS, A_WIDTH, D_MODEL), A_WIDTH ** -0.5),
        "b_w_in": nrm(ks[7], (N_B_LAYERS, D_MODEL, B_IN_WIDTH), D_MODEL ** -0.5),
        "b_w_gate_f": nrm(ks[8], (N_B_LAYERS, B_GATE_RANK, B_QK_WIDTH), B_GATE_RANK ** -0.5),
        "b_gate_bias_f": nrm(ks[9], (N_B_LAYERS, B_QK_WIDTH), 0.1),
        "b_w_gate_b": nrm(ks[10], (N_B_LAYERS, B_GATE_RANK, B_QK_WIDTH), B_GATE_RANK ** -0.5),
        "b_gate_bias_b": nrm(ks[11], (N_B_LAYERS, B_QK_WIDTH), 0.1),
        "b_out_norm": 1.0 + nrm(ks[12], (N_B_LAYERS, B_HEADS, B_VAL_DIM), 0.02),
        "b_w_out": nrm(ks[13], (N_B_LAYERS, B_V_WIDTH, D_MODEL), B_V_WIDTH ** -0.5),
        "ffn_w_gate_up": nrm(ks[14], (DEPTH, D_MODEL, 2 * FFN_HIDDEN), D_MODEL ** -0.5),
        "ffn_w_down": nrm(ks[15], (DEPTH, FFN_HIDDEN, D_MODEL), FFN_HIDDEN ** -0.5),
    }


def _fwd_reference(x, attn_norm, ffn_norm, a_w_in, a_q_norm, a_k_norm, a_w_out, b_w_in, b_w_gate_f, b_gate_bias_f, b_w_gate_b, b_gate_bias_b, b_out_norm, b_w_out, ffn_w_gate_up, ffn_w_down):
    positions = jnp.arange(x.shape[1])
    h = x
    for i in range(DEPTH):
        j = i // N_MIXERS
        hn = rms_norm(h, attn_norm[i])
        if i % N_MIXERS == 0:
            mix = dilated_attention_mixer(hn, a_w_in[j], a_q_norm[j], a_k_norm[j], a_w_out[j], positions)
        else:
            mix = gla_mixer(hn, b_w_in[j], b_w_gate_f[j], b_gate_bias_f[j], b_w_gate_b[j], b_gate_bias_b[j], b_out_norm[j], b_w_out[j])
        h = h + mix.astype(h.dtype)
        h = h + swiglu(rms_norm(h, ffn_norm[i]), ffn_w_gate_up[i], ffn_w_down[i]).astype(h.dtype)
    return h


import jax as _jax
import jax.numpy as _jnp

TWIN_FORMAT = 'train_step'
FWD_PARAMS = ['x', 'attn_norm', 'ffn_norm', 'a_w_in', 'a_q_norm', 'a_k_norm', 'a_w_out', 'b_w_in', 'b_w_gate_f', 'b_gate_bias_f', 'b_w_gate_b', 'b_gate_bias_b', 'b_out_norm', 'b_w_out', 'ffn_w_gate_up', 'ffn_w_down']
TWIN_WEIGHTS = ['attn_norm', 'ffn_norm', 'a_w_in', 'a_q_norm', 'a_k_norm', 'a_w_out', 'b_w_in', 'b_w_gate_f', 'b_gate_bias_f', 'b_w_gate_b', 'b_gate_bias_b', 'b_out_norm', 'b_w_out', 'ffn_w_gate_up', 'ffn_w_down']
TWIN_DIFF_INPUT = 'x'
TWIN_INPUTS = ['x', 'attn_norm', 'ffn_norm', 'a_w_in', 'a_q_norm', 'a_k_norm', 'a_w_out', 'b_w_in', 'b_w_gate_f', 'b_gate_bias_f', 'b_w_gate_b', 'b_gate_bias_b', 'b_out_norm', 'b_w_out', 'ffn_w_gate_up', 'ffn_w_down', 'loss_target', 'm_attn_norm', 'm_ffn_norm', 'm_a_w_in', 'm_a_q_norm', 'm_a_k_norm', 'm_a_w_out', 'm_b_w_in', 'm_b_w_gate_f', 'm_b_gate_bias_f', 'm_b_w_gate_b', 'm_b_gate_bias_b', 'm_b_out_norm', 'm_b_w_out', 'm_ffn_w_gate_up', 'm_ffn_w_down', 'v_attn_norm', 'v_ffn_norm', 'v_a_w_in', 'v_a_q_norm', 'v_a_k_norm', 'v_a_w_out', 'v_b_w_in', 'v_b_w_gate_f', 'v_b_gate_bias_f', 'v_b_w_gate_b', 'v_b_gate_bias_b', 'v_b_out_norm', 'v_b_w_out', 'v_ffn_w_gate_up', 'v_ffn_w_down']
TWIN_OUTPUTS = ['loss', 'grad_x', 'grad_attn_norm', 'grad_ffn_norm', 'grad_a_w_in', 'grad_a_q_norm', 'grad_a_k_norm', 'grad_a_w_out', 'grad_b_w_in', 'grad_b_w_gate_f', 'grad_b_gate_bias_f', 'grad_b_w_gate_b', 'grad_b_gate_bias_b', 'grad_b_out_norm', 'grad_b_w_out', 'grad_ffn_w_gate_up', 'grad_ffn_w_down', 'delta_attn_norm', 'delta_ffn_norm', 'delta_a_w_in', 'delta_a_q_norm', 'delta_a_k_norm', 'delta_a_w_out', 'delta_b_w_in', 'delta_b_w_gate_f', 'delta_b_gate_bias_f', 'delta_b_w_gate_b', 'delta_b_gate_bias_b', 'delta_b_out_norm', 'delta_b_w_out', 'delta_ffn_w_gate_up', 'delta_ffn_w_down', 'new_m_attn_norm', 'new_m_ffn_norm', 'new_m_a_w_in', 'new_m_a_q_norm', 'new_m_a_k_norm', 'new_m_a_w_out', 'new_m_b_w_in', 'new_m_b_w_gate_f', 'new_m_b_gate_bias_f', 'new_m_b_w_gate_b', 'new_m_b_gate_bias_b', 'new_m_b_out_norm', 'new_m_b_w_out', 'new_m_ffn_w_gate_up', 'new_m_ffn_w_down', 'new_v_attn_norm', 'new_v_ffn_norm', 'new_v_a_w_in', 'new_v_a_q_norm', 'new_v_a_k_norm', 'new_v_a_w_out', 'new_v_b_w_in', 'new_v_b_w_gate_f', 'new_v_b_gate_bias_f', 'new_v_b_w_gate_b', 'new_v_b_gate_bias_b', 'new_v_b_out_norm', 'new_v_b_w_out', 'new_v_ffn_w_gate_up', 'new_v_ffn_w_down']
TWIN_LEAF_KINDS = {'loss': 'loss', 'grad_x': 'grad_x', 'grad_attn_norm': 'grad_w', 'grad_ffn_norm': 'grad_w', 'grad_a_w_in': 'grad_w', 'grad_a_q_norm': 'grad_w', 'grad_a_k_norm': 'grad_w', 'grad_a_w_out': 'grad_w', 'grad_b_w_in': 'grad_w', 'grad_b_w_gate_f': 'grad_w', 'grad_b_gate_bias_f': 'grad_w', 'grad_b_w_gate_b': 'grad_w', 'grad_b_gate_bias_b': 'grad_w', 'grad_b_out_norm': 'grad_w', 'grad_b_w_out': 'grad_w', 'grad_ffn_w_gate_up': 'grad_w', 'grad_ffn_w_down': 'grad_w', 'delta_attn_norm': 'delta_w', 'delta_ffn_norm': 'delta_w', 'delta_a_w_in': 'delta_w', 'delta_a_q_norm': 'delta_w', 'delta_a_k_norm': 'delta_w', 'delta_a_w_out': 'delta_w', 'delta_b_w_in': 'delta_w', 'delta_b_w_gate_f': 'delta_w', 'delta_b_gate_bias_f': 'delta_w', 'delta_b_w_gate_b': 'delta_w', 'delta_b_gate_bias_b': 'delta_w', 'delta_b_out_norm': 'delta_w', 'delta_b_w_out': 'delta_w', 'delta_ffn_w_gate_up': 'delta_w', 'delta_ffn_w_down': 'delta_w', 'new_m_attn_norm': 'new_m', 'new_m_ffn_norm': 'new_m', 'new_m_a_w_in': 'new_m', 'new_m_a_q_norm': 'new_m', 'new_m_a_k_norm': 'new_m', 'new_m_a_w_out': 'new_m', 'new_m_b_w_in': 'new_m', 'new_m_b_w_gate_f': 'new_m', 'new_m_b_gate_bias_f': 'new_m', 'new_m_b_w_gate_b': 'new_m', 'new_m_b_gate_bias_b': 'new_m', 'new_m_b_out_norm': 'new_m', 'new_m_b_w_out': 'new_m', 'new_m_ffn_w_gate_up': 'new_m', 'new_m_ffn_w_down': 'new_m', 'new_v_attn_norm': 'new_v', 'new_v_ffn_norm': 'new_v', 'new_v_a_w_in': 'new_v', 'new_v_a_q_norm': 'new_v', 'new_v_a_k_norm': 'new_v', 'new_v_a_w_out': 'new_v', 'new_v_b_w_in': 'new_v', 'new_v_b_w_gate_f': 'new_v', 'new_v_b_gate_bias_f': 'new_v', 'new_v_b_w_gate_b': 'new_v', 'new_v_b_gate_bias_b': 'new_v', 'new_v_b_out_norm': 'new_v', 'new_v_b_w_out': 'new_v', 'new_v_ffn_w_gate_up': 'new_v', 'new_v_ffn_w_down': 'new_v'}


def _forward(args):
    return _fwd_reference(*[args[k] for k in FWD_PARAMS])


def _output_shape():
    def fwd():
        inp = _fwd_setup_inputs(0)
        return _fwd_reference(*[inp[k] for k in FWD_PARAMS])
    out = _jax.eval_shape(fwd)
    return out.shape, out.dtype

N_MICROBATCH = 1
ADAM_LR = 0.001
ADAM_B1 = 0.9
ADAM_B2 = 0.999
ADAM_EPS = 1e-08
ADAM_WD = 0.01
ADAM_STEP = 10
PER_EXAMPLE_BATCH_AXIS = {'x': 0, 'loss_target': 0}
SHARED_INPUTS = []
_WEIGHT_DTYPES = {'attn_norm': _jnp.float32, 'ffn_norm': _jnp.float32, 'a_w_in': _jnp.float32, 'a_q_norm': _jnp.float32, 'a_k_norm': _jnp.float32, 'a_w_out': _jnp.float32, 'b_w_in': _jnp.float32, 'b_w_gate_f': _jnp.float32, 'b_gate_bias_f': _jnp.float32, 'b_w_gate_b': _jnp.float32, 'b_gate_bias_b': _jnp.float32, 'b_out_norm': _jnp.float32, 'b_w_out': _jnp.float32, 'ffn_w_gate_up': _jnp.float32, 'ffn_w_down': _jnp.float32}
MOMENT_SCALE = {'attn_norm': 3.619578e+01, 'ffn_norm': 9.824308e+01, 'a_w_in': 1.624210e-01, 'a_q_norm': 1.519301e+00, 'a_k_norm': 1.542594e+00, 'a_w_out': 3.012605e-01, 'b_w_in': 1.070294e+00, 'b_w_gate_f': 9.201998e-02, 'b_gate_bias_f': 3.848406e-01, 'b_w_gate_b': 9.584059e-02, 'b_gate_bias_b': 3.950345e-01, 'b_out_norm': 4.456394e+01, 'b_w_out': 1.018985e+00, 'ffn_w_gate_up': 6.738857e-01, 'ffn_w_down': 1.140499e+00}


def _to_microbatches(a, axis):
    t = _jnp.moveaxis(a, axis, 0)
    t = t.reshape((N_MICROBATCH, t.shape[0] // N_MICROBATCH) + t.shape[1:])
    return _jnp.moveaxis(t, 1, axis + 1)


def setup_inputs(seed: int = 0) -> dict:
    inp = _fwd_setup_inputs(seed)
    key = _jax.random.fold_in(_jax.random.key(seed), 7919)
    shape, _ = _output_shape()
    out = dict(inp)
    out["loss_target"] = _jax.random.normal(_jax.random.fold_in(key, 0), shape, _jnp.float32)
    for i, name in enumerate(TWIN_WEIGHTS):
        w = inp[name].astype(_jnp.float32)
        if MOMENT_SCALE is None:
            s = _jnp.sqrt(_jnp.mean(_jnp.square(w)) + 1e-30)
        else:
            s = MOMENT_SCALE[name]
        km, kv = _jax.random.split(_jax.random.fold_in(key, i + 1))
        out[name] = w
        out["m_" + name] = s * _jax.random.normal(km, w.shape, _jnp.float32)
        out["v_" + name] = (s * s) * _jax.random.uniform(kv, w.shape, _jnp.float32, 0.5, 1.5)
    if N_MICROBATCH > 1:
        for name, axis in PER_EXAMPLE_BATCH_AXIS.items():
            out[name] = _to_microbatches(out[name], axis)
    return {'x': out['x'], 'attn_norm': out['attn_norm'], 'ffn_norm': out['ffn_norm'], 'a_w_in': out['a_w_in'], 'a_q_norm': out['a_q_norm'], 'a_k_norm': out['a_k_norm'], 'a_w_out': out['a_w_out'], 'b_w_in': out['b_w_in'], 'b_w_gate_f': out['b_w_gate_f'], 'b_gate_bias_f': out['b_gate_bias_f'], 'b_w_gate_b': out['b_w_gate_b'], 'b_gate_bias_b': out['b_gate_bias_b'], 'b_out_norm': out['b_out_norm'], 'b_w_out': out['b_w_out'], 'ffn_w_gate_up': out['ffn_w_gate_up'], 'ffn_w_down': out['ffn_w_down'], 'loss_target': out['loss_target'], 'm_attn_norm': out['m_attn_norm'], 'm_ffn_norm': out['m_ffn_norm'], 'm_a_w_in': out['m_a_w_in'], 'm_a_q_norm': out['m_a_q_norm'], 'm_a_k_norm': out['m_a_k_norm'], 'm_a_w_out': out['m_a_w_out'], 'm_b_w_in': out['m_b_w_in'], 'm_b_w_gate_f': out['m_b_w_gate_f'], 'm_b_gate_bias_f': out['m_b_gate_bias_f'], 'm_b_w_gate_b': out['m_b_w_gate_b'], 'm_b_gate_bias_b': out['m_b_gate_bias_b'], 'm_b_out_norm': out['m_b_out_norm'], 'm_b_w_out': out['m_b_w_out'], 'm_ffn_w_gate_up': out['m_ffn_w_gate_up'], 'm_ffn_w_down': out['m_ffn_w_down'], 'v_attn_norm': out['v_attn_norm'], 'v_ffn_norm': out['v_ffn_norm'], 'v_a_w_in': out['v_a_w_in'], 'v_a_q_norm': out['v_a_q_norm'], 'v_a_k_norm': out['v_a_k_norm'], 'v_a_w_out': out['v_a_w_out'], 'v_b_w_in': out['v_b_w_in'], 'v_b_w_gate_f': out['v_b_w_gate_f'], 'v_b_gate_bias_f': out['v_b_gate_bias_f'], 'v_b_w_gate_b': out['v_b_w_gate_b'], 'v_b_gate_bias_b': out['v_b_gate_bias_b'], 'v_b_out_norm': out['v_b_out_norm'], 'v_b_w_out': out['v_b_w_out'], 'v_ffn_w_gate_up': out['v_ffn_w_gate_up'], 'v_ffn_w_down': out['v_ffn_w_down']}


def _loss(weights, diff, rest, loss_target):
    with _jax.named_scope("forward"):
        args = {**rest, TWIN_DIFF_INPUT: diff, **{k: w.astype(_WEIGHT_DTYPES[k]) for k, w in weights.items()}}
        y = _forward(args)
    with _jax.named_scope("loss_head"):
        err = _jnp.square(y.astype(_jnp.float32) - loss_target)
        return 0.5 * _jnp.sum(_jnp.mean(err, axis=-1)) if err.ndim else 0.5 * err


def _adamw(w, g, m, v):
    m = ADAM_B1 * m + (1.0 - ADAM_B1) * g
    v = ADAM_B2 * v + (1.0 - ADAM_B2) * _jnp.square(g)
    m_hat = m / (1.0 - ADAM_B1 ** ADAM_STEP)
    v_hat = v / (1.0 - ADAM_B2 ** ADAM_STEP)
    delta = -ADAM_LR * (m_hat / (_jnp.sqrt(v_hat) + ADAM_EPS) + ADAM_WD * w)
    return delta, m, v


def reference(x, attn_norm, ffn_norm, a_w_in, a_q_norm, a_k_norm, a_w_out, b_w_in, b_w_gate_f, b_gate_bias_f, b_w_gate_b, b_gate_bias_b, b_out_norm, b_w_out, ffn_w_gate_up, ffn_w_down, loss_target, m_attn_norm, m_ffn_norm, m_a_w_in, m_a_q_norm, m_a_k_norm, m_a_w_out, m_b_w_in, m_b_w_gate_f, m_b_gate_bias_f, m_b_w_gate_b, m_b_gate_bias_b, m_b_out_norm, m_b_w_out, m_ffn_w_gate_up, m_ffn_w_down, v_attn_norm, v_ffn_norm, v_a_w_in, v_a_q_norm, v_a_k_norm, v_a_w_out, v_b_w_in, v_b_w_gate_f, v_b_gate_bias_f, v_b_w_gate_b, v_b_gate_bias_b, v_b_out_norm, v_b_w_out, v_ffn_w_gate_up, v_ffn_w_down):
    given = dict(x=x, attn_norm=attn_norm, ffn_norm=ffn_norm, a_w_in=a_w_in, a_q_norm=a_q_norm, a_k_norm=a_k_norm, a_w_out=a_w_out, b_w_in=b_w_in, b_w_gate_f=b_w_gate_f, b_gate_bias_f=b_gate_bias_f, b_w_gate_b=b_w_gate_b, b_gate_bias_b=b_gate_bias_b, b_out_norm=b_out_norm, b_w_out=b_w_out, ffn_w_gate_up=ffn_w_gate_up, ffn_w_down=ffn_w_down, loss_target=loss_target, m_attn_norm=m_attn_norm, m_ffn_norm=m_ffn_norm, m_a_w_in=m_a_w_in, m_a_q_norm=m_a_q_norm, m_a_k_norm=m_a_k_norm, m_a_w_out=m_a_w_out, m_b_w_in=m_b_w_in, m_b_w_gate_f=m_b_w_gate_f, m_b_gate_bias_f=m_b_gate_bias_f, m_b_w_gate_b=m_b_w_gate_b, m_b_gate_bias_b=m_b_gate_bias_b, m_b_out_norm=m_b_out_norm, m_b_w_out=m_b_w_out, m_ffn_w_gate_up=m_ffn_w_gate_up, m_ffn_w_down=m_ffn_w_down, v_attn_norm=v_attn_norm, v_ffn_norm=v_ffn_norm, v_a_w_in=v_a_w_in, v_a_q_norm=v_a_q_norm, v_a_k_norm=v_a_k_norm, v_a_w_out=v_a_w_out, v_b_w_in=v_b_w_in, v_b_w_gate_f=v_b_w_gate_f, v_b_gate_bias_f=v_b_gate_bias_f, v_b_w_gate_b=v_b_w_gate_b, v_b_gate_bias_b=v_b_gate_bias_b, v_b_out_norm=v_b_out_norm, v_b_w_out=v_b_w_out, v_ffn_w_gate_up=v_ffn_w_gate_up, v_ffn_w_down=v_ffn_w_down)
    weights = {n: given[n] for n in TWIN_WEIGHTS}
    shared = {n: given[n] for n in SHARED_INPUTS}
    per_example = {n: given[n] for n in ['x']}
    grad_fn = _jax.value_and_grad(_loss, argnums=(0, 1))

    def one_microbatch(ex, loss_target):
        ex = dict(ex)
        diff = ex.pop(TWIN_DIFF_INPUT)
        return grad_fn(weights, diff, {**shared, **ex}, loss_target)

    if N_MICROBATCH == 1:
        loss, (grad_w, grad_x) = one_microbatch(per_example, given["loss_target"])
    else:
        def body(carry, xs):
            loss_sum, grad_sum = carry
            l_k, (gw_k, gx_k) = one_microbatch(xs[0], xs[1])
            with _jax.named_scope("update"):
                return (loss_sum + l_k, _jax.tree.map(_jnp.add, grad_sum, gw_k)), gx_k

        init = (_jnp.zeros((), _jnp.float32), _jax.tree.map(_jnp.zeros_like, weights))
        (loss, grad_w), grad_x = _jax.lax.scan(body, init, (per_example, given["loss_target"]))
    with _jax.named_scope("update"):
        delta_w, new_m, new_v = {}, {}, {}
        for n in TWIN_WEIGHTS:
            delta_w[n], new_m[n], new_v[n] = _adamw(weights[n], grad_w[n], given["m_" + n], given["v_" + n])
    return (loss, grad_x, *[grad_w[n] for n in TWIN_WEIGHTS], *[delta_w[n] for n in TWIN_WEIGHTS],
            *[new_m[n] for n in TWIN_WEIGHTS], *[new_v[n] for n in TWIN_WEIGHTS])
```

```python
import functools

import numpy as np
import jax
import jax.numpy as jnp
from jax import lax
from jax.experimental import pallas as pl
from jax.experimental.pallas import tpu as pltpu

F32, BF16 = jnp.float32, jnp.bfloat16
HI = lax.Precision.HIGHEST
MESH = pl.DeviceIdType.MESH

D_MODEL = 1024
DEPTH = 4
RMS_EPS = 1e-6
NEG_INF = -1e30
A_GROUPS = ((128, 1), (512, 4), (2048, 16))
A_HALF = 64
A_HEADS = 16
A_HEAD_DIM = 64
A_WIDTH = 1024
A_IN_WIDTH = 9216
ROPE_THETA = 10000.0
B_HEADS = 4
B_KEY_DIM = 128
B_VAL_DIM = 256
B_QK_WIDTH = 512
B_V_WIDTH = 1024
B_GATE_RANK = 16
B_GATE_TAU = 16.0
B_CHUNK = 64
B_IN_WIDTH = 3104
B_IN_PAD = 3200
FFN_HIDDEN = 2816
ADAM_LR, ADAM_B1, ADAM_B2, ADAM_EPS, ADAM_WD, ADAM_STEP = 0.001, 0.9, 0.999, 1e-08, 0.01, 10
LANES = 128
VMEM_LIMIT = 48 * 1024 * 1024
FLAT_COLS = 1024
N_CHIPS = 4

WEIGHTS = ['attn_norm', 'ffn_norm', 'a_w_in', 'a_q_norm', 'a_k_norm', 'a_w_out', 'b_w_in', 'b_w_gate_f',
           'b_gate_bias_f', 'b_w_gate_b', 'b_gate_bias_b', 'b_out_norm', 'b_w_out', 'ffn_w_gate_up', 'ffn_w_down']
REPLICATED = ('attn_norm', 'ffn_norm', 'a_q_norm', 'a_k_norm')
SHARD_AXIS = {'a_w_in': 2, 'a_w_out': 1, 'b_w_in': 2, 'b_w_gate_f': 2, 'b_gate_bias_f': 1, 'b_w_gate_b': 2,
              'b_gate_bias_b': 1, 'b_out_norm': 2, 'b_w_out': 1, 'ffn_w_gate_up': 2, 'ffn_w_down': 1}
BIG = ('a_w_in', 'a_w_out', 'b_w_in', 'b_w_gate_f', 'b_w_gate_b', 'b_w_out', 'ffn_w_gate_up', 'ffn_w_down')
SMALL_SHARDED = ('b_gate_bias_f', 'b_gate_bias_b', 'b_out_norm')


def _params(sem):
    return pltpu.CompilerParams(dimension_semantics=sem, vmem_limit_bytes=VMEM_LIMIT)


def _tile(n, pref):
    t = min(n, pref)
    while n % t:
        t //= 2
    return t


def _const_spec(shape):
    nd = len(shape)
    return pl.BlockSpec(shape, lambda *_: (0,) * nd)


def matmul(a, b, *, name, trans_a=False, out_dtype=F32, res=None, tm=1024, tn=512, tk=1024):
    if trans_a:
        K, M = a.shape
    else:
        M, K = a.shape
    N = b.shape[1]
    assert b.shape[0] == K
    tm, tn, tk = _tile(M, tm), _tile(N, tn), _tile(K, tk)
    nk = K // tk
    dims = (((0,), (0,)), ((), ())) if trans_a else (((1,), (0,)), ((), ()))

    def body(*refs):
        if res is None:
            a_ref, b_ref, o_ref, acc_ref = refs
        else:
            a_ref, b_ref, r_ref, o_ref, acc_ref = refs
        k = pl.program_id(2)

        @pl.when(k == 0)
        def _():
            acc_ref[...] = jnp.zeros_like(acc_ref)

        acc_ref[...] += lax.dot_general(a_ref[...], b_ref[...], dims, preferred_element_type=F32)

        @pl.when(k == nk - 1)
        def _():
            v = acc_ref[...]
            if res is not None:
                v = v + r_ref[...]
            o_ref[...] = v.astype(o_ref.dtype)

    a_spec = pl.BlockSpec((tk, tm), lambda i, j, k: (k, i)) if trans_a else pl.BlockSpec((tm, tk), lambda i, j, k: (i, k))
    in_specs = [a_spec, pl.BlockSpec((tk, tn), lambda i, j, k: (k, j))]
    args = [a, b]
    if res is not None:
        in_specs.append(pl.BlockSpec((tm, tn), lambda i, j, k: (i, j)))
        args.append(res)
    return pl.pallas_call(
        body, name=name, grid=(M // tm, N // tn, nk), in_specs=in_specs,
        out_specs=pl.BlockSpec((tm, tn), lambda i, j, k: (i, j)),
        out_shape=jax.ShapeDtypeStruct((M, N), out_dtype),
        scratch_shapes=[pltpu.VMEM((tm, tn), F32)],
        compiler_params=_params(("parallel", "parallel", "arbitrary")),
    )(*args)


def rmsnorm_fwd(x, gain, *, name):
    S, Dm = x.shape
    ts = _tile(S, 512)

    def body(x_ref, g_ref, o_ref):
        xv = x_ref[...]
        r = lax.rsqrt(jnp.mean(xv * xv, axis=-1, keepdims=True) + RMS_EPS)
        o_ref[...] = ((xv * r) * g_ref[...]).astype(o_ref.dtype)

    return pl.pallas_call(
        body, name=name, grid=(S // ts,),
        in_specs=[pl.BlockSpec((ts, Dm), lambda i: (i, 0)), _const_spec((1, Dm))],
        out_specs=pl.BlockSpec((ts, Dm), lambda i: (i, 0)),
        out_shape=jax.ShapeDtypeStruct((S, Dm), BF16),
        compiler_params=_params(("parallel",)),
    )(x, gain)


def rmsnorm_bwd(x, gain, dy, dres, *, name):
    S, Dm = x.shape
    ts = _tile(S, 512)

    def body(x_ref, g_ref, dy_ref, dr_ref, dx_ref, dg_ref):
        @pl.when(pl.program_id(0) == 0)
        def _():
            dg_ref[...] = jnp.zeros_like(dg_ref)

        xv = x_ref[...]
        r = lax.rsqrt(jnp.mean(xv * xv, axis=-1, keepdims=True) + RMS_EPS)
        xhat = xv * r
        dyv = dy_ref[...].astype(F32)
        dyg = dyv * g_ref[...]
        dx = r * (dyg - xhat * jnp.mean(dyg * xhat, axis=-1, keepdims=True))
        dx_ref[...] = dr_ref[...] + dx
        dg_ref[0:1, :] += jnp.sum(dyv * xhat, axis=0, keepdims=True)

    return pl.pallas_call(
        body, name=name, grid=(S // ts,),
        in_specs=[pl.BlockSpec((ts, Dm), lambda i: (i, 0)), _const_spec((1, Dm)),
                  pl.BlockSpec((ts, Dm), lambda i: (i, 0)), pl.BlockSpec((ts, Dm), lambda i: (i, 0))],
        out_specs=[pl.BlockSpec((ts, Dm), lambda i: (i, 0)), _const_spec((8, Dm))],
        out_shape=[jax.ShapeDtypeStruct((S, Dm), F32), jax.ShapeDtypeStruct((8, Dm), F32)],
        compiler_params=_params(("arbitrary",)),
    )(x, gain, dy, dres)


def swiglu_fwd(gu, *, name):
    S, F2 = gu.shape
    Fh = F2 // 2
    ts = _tile(S, 512)

    def body(g_ref, u_ref, o_ref):
        g = g_ref[...].astype(F32)
        u = u_ref[...].astype(F32)
        o_ref[...] = (g * (1.0 / (1.0 + jnp.exp(-g))) * u).astype(o_ref.dtype)

    return pl.pallas_call(
        body, name=name, grid=(S // ts,),
        in_specs=[pl.BlockSpec((ts, Fh), lambda i: (i, 0)), pl.BlockSpec((ts, Fh), lambda i: (i, 1))],
        out_specs=pl.BlockSpec((ts, Fh), lambda i: (i, 0)),
        out_shape=jax.ShapeDtypeStruct((S, Fh), BF16),
        compiler_params=_params(("parallel",)),
    )(gu, gu)


def swiglu_bwd(gu, dact, *, name):
    S, F2 = gu.shape
    Fh = F2 // 2
    ts = _tile(S, 512)

    def body(g_ref, u_ref, d_ref, o_ref):
        g = g_ref[...].astype(F32)
        u = u_ref[...].astype(F32)
        d = d_ref[...].astype(F32)
        sig = 1.0 / (1.0 + jnp.exp(-g))
        j = pl.program_id(1)

        @pl.when(j == 0)
        def _():
            o_ref[...] = (d * u * (sig * (1.0 + g * (1.0 - sig)))).astype(o_ref.dtype)

        @pl.when(j == 1)
        def _():
            o_ref[...] = (d * (g * sig)).astype(o_ref.dtype)

    return pl.pallas_call(
        body, name=name, grid=(S // ts, 2),
        in_specs=[pl.BlockSpec((ts, Fh), lambda i, j: (i, 0)), pl.BlockSpec((ts, Fh), lambda i, j: (i, 1)),
                  pl.BlockSpec((ts, Fh), lambda i, j: (i, 0))],
        out_specs=pl.BlockSpec((ts, Fh), lambda i, j: (i, j)),
        out_shape=jax.ShapeDtypeStruct((S, F2), BF16),
        compiler_params=_params(("parallel", "arbitrary")),
    )(gu, gu, dact)


def loss_head(y, target, *, name):
    S, Dm = y.shape
    ts = _tile(S, 512)

    def body(y_ref, t_ref, l_ref, d_ref):
        @pl.when(pl.program_id(0) == 0)
        def _():
            l_ref[...] = jnp.zeros_like(l_ref)

        e = y_ref[...] - t_ref[...]
        d_ref[...] = e * (1.0 / Dm)
        l_ref[...] += jnp.sum(e * e)

    return pl.pallas_call(
        body, name=name, grid=(S // ts,),
        in_specs=[pl.BlockSpec((ts, Dm), lambda i: (i, 0)), pl.BlockSpec((ts, Dm), lambda i: (i, 0))],
        out_specs=[_const_spec((8, LANES)), pl.BlockSpec((ts, Dm), lambda i: (i, 0))],
        out_shape=[jax.ShapeDtypeStruct((8, LANES), F32), jax.ShapeDtypeStruct((S, Dm), F32)],
        compiler_params=_params(("arbitrary",)),
    )(y, target)


def _head_block_ones():
    i = np.arange(LANES)
    return jnp.asarray((i[:, None] // A_HEAD_DIM == i[None, :] // A_HEAD_DIM).astype(np.float32))


def _rope_tables(S):
    half = A_HEAD_DIM // 2
    inv_freq = ROPE_THETA ** (-jnp.arange(half, dtype=F32) / half)
    ang = jnp.arange(S).astype(F32)[:, None] * inv_freq[None, :]
    cos = jnp.tile(jnp.cos(ang), (1, LANES // half))
    sin = jnp.tile(jnp.sin(ang), (1, LANES // half))
    return cos, sin


def _rot_half(x, lo):
    return jnp.where(lo, -pltpu.roll(x, LANES - 32, 1), pltpu.roll(x, 32, 1))


def _seg_mean(v, ones_ref):
    return jnp.dot(v, ones_ref[...], precision=HI, preferred_element_type=F32) * (1.0 / A_HEAD_DIM)


def qk_prep_fwd(qkv, gain, cos, sin, ones, *, name):
    S, W = qkv.shape
    ts = _tile(S, 256)
    nchunk = A_WIDTH // LANES

    def body(x_ref, g_ref, c_ref, s_ref, ones_ref, o_ref):
        kind = pl.program_id(1) % 3

        @pl.when(kind < 2)
        def _():
            scale = jnp.where(kind == 0, A_HEAD_DIM ** -0.5, 1.0).astype(F32)
            lo = (lax.broadcasted_iota(jnp.int32, (ts, LANES), 1) % A_HEAD_DIM) < (A_HEAD_DIM // 2)
            cv, sv = c_ref[...], s_ref[...]
            for c in range(nchunk):
                sl = slice(c * LANES, (c + 1) * LANES)
                xv = x_ref[:, sl]
                r = lax.rsqrt(_seg_mean(xv * xv, ones_ref) + RMS_EPS)
                y = (xv * r) * g_ref[:, sl]
                y = y * cv + _rot_half(y, lo) * sv
                o_ref[:, sl] = (y * scale).astype(o_ref.dtype)

        @pl.when(kind == 2)
        def _():
            o_ref[...] = x_ref[...].astype(o_ref.dtype)

    return pl.pallas_call(
        body, name=name, grid=(S // ts, W // A_WIDTH),
        in_specs=[pl.BlockSpec((ts, A_WIDTH), lambda i, j: (i, j)), pl.BlockSpec((1, A_WIDTH), lambda i, j: (0, j)),
                  pl.BlockSpec((ts, LANES), lambda i, j: (i, 0)), pl.BlockSpec((ts, LANES), lambda i, j: (i, 0)),
                  _const_spec((LANES, LANES))],
        out_specs=pl.BlockSpec((ts, A_WIDTH), lambda i, j: (i, j)),
        out_shape=jax.ShapeDtypeStruct((S, W), BF16),
        compiler_params=_params(("parallel", "arbitrary")),
    )(qkv, gain, cos, sin, ones)


def qk_prep_bwd(qkv, gain, cos, sin, ones, grads, *, name):
    S, W = qkv.shape
    ts = _tile(S, 256)
    nchunk = A_WIDTH // LANES
    nj = W // A_WIDTH

    def body(x_ref, g_ref, c_ref, s_ref, ones_ref, *rest):
        g_refs, (o_ref, dg_ref) = rest[:nj], rest[nj:]
        j = pl.program_id(0)
        kind = j % 3

        @pl.when(pl.program_id(1) == 0)
        def _():
            dg_ref[...] = jnp.zeros_like(dg_ref)

        for n in range(nj):
            @pl.when(j == n)
            def _(n=n):
                d_ref = g_refs[n]
                if n % 3 == 2:
                    o_ref[...] = d_ref[...].astype(o_ref.dtype)
                    return
                scale = A_HEAD_DIM ** -0.5 if n % 3 == 0 else 1.0
                lo = (lax.broadcasted_iota(jnp.int32, (ts, LANES), 1) % A_HEAD_DIM) < (A_HEAD_DIM // 2)
                cv, sv = c_ref[...], s_ref[...]
                for c in range(nchunk):
                    sl = slice(c * LANES, (c + 1) * LANES)
                    dy = d_ref[:, sl].astype(F32) * scale
                    dn = dy * cv - _rot_half(dy, lo) * sv
                    xv = x_ref[:, sl]
                    r = lax.rsqrt(_seg_mean(xv * xv, ones_ref) + RMS_EPS)
                    xhat = xv * r
                    dyg = dn * g_ref[:, sl]
                    dx = r * (dyg - xhat * _seg_mean(dyg * xhat, ones_ref))
                    o_ref[:, sl] = dx.astype(o_ref.dtype)
                    dg_ref[0:1, sl] += jnp.sum(dn * xhat, axis=0, keepdims=True)

    def gspec(n):
        return pl.BlockSpec((ts, A_WIDTH), lambda j, i: (jnp.where(j == n, i, 0), 0))

    return pl.pallas_call(
        body, name=name, grid=(nj, S // ts),
        in_specs=[pl.BlockSpec((ts, A_WIDTH), lambda j, i: (i, j)), pl.BlockSpec((1, A_WIDTH), lambda j, i: (0, j)),
                  pl.BlockSpec((ts, LANES), lambda j, i: (i, 0)), pl.BlockSpec((ts, LANES), lambda j, i: (i, 0)),
                  _const_spec((LANES, LANES))] + [gspec(n) for n in range(nj)],
        out_specs=[pl.BlockSpec((ts, A_WIDTH), lambda j, i: (i, j)), pl.BlockSpec((8, A_WIDTH), lambda j, i: (0, j))],
        out_shape=[jax.ShapeDtypeStruct((S, W), BF16), jax.ShapeDtypeStruct((8, W), F32)],
        compiler_params=_params(("arbitrary", "arbitrary")),
    )(qkv, gain, cos, sin, ones, *grads)


def _attn_cols(g, kind, dil):
    per_row = A_IN_WIDTH // LANES
    base = (g * 3 + kind) * (A_WIDTH // LANES)
    return lambda ph, hp: ph * per_row + base + hp


def _band_specs(g, kind, dil, tq, nlb, wide):
    col = _attn_cols(g, kind, dil) if wide else (lambda ph, hp: ph * (A_WIDTH // LANES) + hp)
    nhb = tq // A_HALF
    return [pl.BlockSpec((A_HALF, LANES), lambda ph, b, hp: (jnp.maximum(b * nhb - 1, 0), col(ph, hp))),
            pl.BlockSpec((tq, LANES), lambda ph, b, hp: (b, col(ph, hp))),
            pl.BlockSpec((A_HALF, LANES), lambda ph, b, hp: (jnp.minimum((b + 1) * nhb, nlb - 1), col(ph, hp)))]


def _cat3(a_ref, b_ref, c_ref):
    return jnp.concatenate([a_ref[...], b_ref[...], c_ref[...]], axis=0)


def _lane_lo(rows):
    return lax.broadcasted_iota(jnp.int32, (rows, LANES), 1) < A_HEAD_DIM


NT = (((1,), (1,)), ((), ()))
TN = (((0,), (0,)), ((), ()))


def attn_fwd(qkvp, g, *, name):
    S = qkvp.shape[0]
    dil = A_GROUPS[g][1]
    L = S // dil
    tq = _tile(L, 256)
    nlb = L // A_HALF
    tk = tq + 2 * A_HALF
    view = qkvp.reshape(L, dil * A_IN_WIDTH)

    def body(q_ref, kp_ref, ko_ref, kn_ref, vp_ref, vo_ref, vn_ref, o_ref, l_ref):
        b = pl.program_id(1)
        q = q_ref[...]
        K = _cat3(kp_ref, ko_ref, kn_ref)
        V = _cat3(vp_ref, vo_ref, vn_ref)
        row = lax.broadcasted_iota(jnp.int32, (tq, tk), 0)
        col = lax.broadcasted_iota(jnp.int32, (tq, tk), 1)
        kpos = b * tq - A_HALF + col
        valid = (jnp.abs(col - row - A_HALF) <= A_HALF) & (kpos >= 0) & (kpos < L)
        lo_k = _lane_lo(tk)
        outs, lses = [], []
        for hh in range(2):
            sel = lo_k if hh == 0 else ~lo_k
            Km = jnp.where(sel, K, jnp.zeros_like(K))
            Vm = jnp.where(sel, V, jnp.zeros_like(V))
            s = lax.dot_general(q, Km, NT, preferred_element_type=F32)
            s = jnp.where(valid, s, NEG_INF)
            m = jnp.max(s, axis=1, keepdims=True)
            p = jnp.exp(s - m)
            l = jnp.sum(p, axis=1, keepdims=True)
            outs.append(jnp.dot(p.astype(BF16), Vm, preferred_element_type=F32) / l)
            lses.append(m + jnp.log(l))
        lo_q = _lane_lo(tq)
        o_ref[...] = outs[0] + outs[1]
        l_ref[...] = jnp.where(lo_q, lses[0], lses[1])

    ospec = pl.BlockSpec((tq, LANES), lambda ph, b, hp: (b, ph * (A_WIDTH // LANES) + hp))
    o, lse = pl.pallas_call(
        body, name=name, grid=(dil, L // tq, A_WIDTH // LANES),
        in_specs=[_band_specs(g, 0, dil, tq, nlb, True)[1]] + _band_specs(g, 1, dil, tq, nlb, True)
        + _band_specs(g, 2, dil, tq, nlb, True),
        out_specs=[ospec, ospec],
        out_shape=[jax.ShapeDtypeStruct((L, dil * A_WIDTH), F32)] * 2,
        compiler_params=_params(("parallel", "parallel", "parallel")),
    )(*([view] * 7))
    return o.reshape(S, A_WIDTH), lse.reshape(S, A_WIDTH)


def attn_merge(os_, lses, *, name):
    S = os_[0].shape[0]
    ts = _tile(S, 512)
    ng = len(os_)

    def body(*refs):
        o_refs, l_refs, (out_ref, lse_ref) = refs[:ng], refs[ng:2 * ng], refs[2 * ng:]
        ls = [r[...] for r in l_refs]
        m = functools.reduce(jnp.maximum, ls)
        es = [jnp.exp(l - m) for l in ls]
        tot = functools.reduce(jnp.add, es)
        acc = None
        for e, o_ref in zip(es, o_refs):
            t = (e / tot) * o_ref[...]
            acc = t if acc is None else acc + t
        out_ref[...] = acc.astype(out_ref.dtype)
        lse_ref[...] = m + jnp.log(tot)

    spec = pl.BlockSpec((ts, A_WIDTH), lambda i: (i, 0))
    return pl.pallas_call(
        body, name=name, grid=(S // ts,), in_specs=[spec] * (2 * ng), out_specs=[spec, spec],
        out_shape=[jax.ShapeDtypeStruct((S, A_WIDTH), BF16), jax.ShapeDtypeStruct((S, A_WIDTH), F32)],
        compiler_params=_params(("parallel",)),
    )(*os_, *lses)


def attn_delta(dout, out, ones, *, name):
    S = dout.shape[0]
    ts = _tile(S, 512)

    def body(d_ref, o_ref, ones_ref, r_ref):
        for c in range(A_WIDTH // LANES):
            sl = slice(c * LANES, (c + 1) * LANES)
            prod = d_ref[:, sl].astype(F32) * o_ref[:, sl].astype(F32)
            r_ref[:, sl] = jnp.dot(prod, ones_ref[...], precision=HI, preferred_element_type=F32)

    spec = pl.BlockSpec((ts, A_WIDTH), lambda i: (i, 0))
    return pl.pallas_call(
        body, name=name, grid=(S // ts,), in_specs=[spec, spec, _const_spec((LANES, LANES))], out_specs=spec,
        out_shape=jax.ShapeDtypeStruct((S, A_WIDTH), F32),
        compiler_params=_params(("parallel",)),
    )(dout, out, ones)


def _head_col(x, hh):
    c = hh * A_HEAD_DIM
    return x[:, c:c + 1]


def attn_bwd_dq(qkvp, dout, lse, delta, g, *, name):
    S = qkvp.shape[0]
    dil = A_GROUPS[g][1]
    L = S // dil
    tq = _tile(L, 256)
    nlb = L // A_HALF
    tk = tq + 2 * A_HALF
    view = qkvp.reshape(L, dil * A_IN_WIDTH)
    nview = lambda t: t.reshape(L, dil * A_WIDTH)

    def body(q_ref, kp_ref, ko_ref, kn_ref, vp_ref, vo_ref, vn_ref, do_ref, l_ref, d_ref, dq_ref):
        b = pl.program_id(1)
        q = q_ref[...]
        K = _cat3(kp_ref, ko_ref, kn_ref)
        V = _cat3(vp_ref, vo_ref, vn_ref)
        do = do_ref[...]
        lse_v, dl_v = l_ref[...], d_ref[...]
        row = lax.broadcasted_iota(jnp.int32, (tq, tk), 0)
        col = lax.broadcasted_iota(jnp.int32, (tq, tk), 1)
        kpos = b * tq - A_HALF + col
        valid = (jnp.abs(col - row - A_HALF) <= A_HALF) & (kpos >= 0) & (kpos < L)
        lo_k = _lane_lo(tk)
        acc = None
        for hh in range(2):
            sel = lo_k if hh == 0 else ~lo_k
            Km = jnp.where(sel, K, jnp.zeros_like(K))
            Vm = jnp.where(sel, V, jnp.zeros_like(V))
            s = lax.dot_general(q, Km, NT, preferred_element_type=F32)
            p = jnp.where(valid, jnp.exp(s - _head_col(lse_v, hh)), 0.0)
            dp = lax.dot_general(do, Vm, NT, preferred_element_type=F32)
            ds = p * (dp - _head_col(dl_v, hh))
            t = jnp.dot(ds.astype(BF16), Km, preferred_element_type=F32)
            acc = t if acc is None else acc + t
        dq_ref[...] = acc.astype(dq_ref.dtype)

    nspec = pl.BlockSpec((tq, LANES), lambda ph, b, hp: (b, ph * (A_WIDTH // LANES) + hp))
    dq = pl.pallas_call(
        body, name=name, grid=(dil, L // tq, A_WIDTH // LANES),
        in_specs=[_band_specs(g, 0, dil, tq, nlb, True)[1]] + _band_specs(g, 1, dil, tq, nlb, True)
        + _band_specs(g, 2, dil, tq, nlb, True) + [nspec, nspec, nspec],
        out_specs=nspec,
        out_shape=jax.ShapeDtypeStruct((L, dil * A_WIDTH), BF16),
        compiler_params=_params(("parallel", "parallel", "parallel")),
    )(*([view] * 7), nview(dout), nview(lse), nview(delta))
    return dq.reshape(S, A_WIDTH)


def attn_bwd_dkv(qkvp, dout, lse, delta, g, *, name):
    S = qkvp.shape[0]
    dil = A_GROUPS[g][1]
    L = S // dil
    tk = _tile(L, 256)
    nlb = L // A_HALF
    tq = tk + 2 * A_HALF
    view = qkvp.reshape(L, dil * A_IN_WIDTH)
    nview = lambda t: t.reshape(L, dil * A_WIDTH)

    def body(qp_ref, qo_ref, qn_ref, k_ref, v_ref, dp_ref, do_ref, dn_ref, lp_ref, lo_ref, ln_ref,
             ep_ref, eo_ref, en_ref, dk_ref, dv_ref):
        b = pl.program_id(1)
        Q = _cat3(qp_ref, qo_ref, qn_ref)
        DO = _cat3(dp_ref, do_ref, dn_ref)
        lse_v = _cat3(lp_ref, lo_ref, ln_ref)
        dl_v = _cat3(ep_ref, eo_ref, en_ref)
        K, V = k_ref[...], v_ref[...]
        row = lax.broadcasted_iota(jnp.int32, (tq, tk), 0)
        col = lax.broadcasted_iota(jnp.int32, (tq, tk), 1)
        qpos = b * tk - A_HALF + row
        valid = (jnp.abs(col - row + A_HALF) <= A_HALF) & (qpos >= 0) & (qpos < L)
        lo_q = _lane_lo(tq)
        dk = dv = None
        for hh in range(2):
            sel = lo_q if hh == 0 else ~lo_q
            Qm = jnp.where(sel, Q, jnp.zeros_like(Q))
            DOm = jnp.where(sel, DO, jnp.zeros_like(DO))
            s = lax.dot_general(Qm, K, NT, preferred_element_type=F32)
            p = jnp.where(valid, jnp.exp(s - _head_col(lse_v, hh)), 0.0)
            dp = lax.dot_general(DOm, V, NT, preferred_element_type=F32)
            ds = p * (dp - _head_col(dl_v, hh))
            tv = lax.dot_general(p.astype(BF16), DOm, TN, preferred_element_type=F32)
            tk_ = lax.dot_general(ds.astype(BF16), Qm, TN, preferred_element_type=F32)
            dv = tv if dv is None else dv + tv
            dk = tk_ if dk is None else dk + tk_
        dk_ref[...] = dk.astype(dk_ref.dtype)
        dv_ref[...] = dv.astype(dv_ref.dtype)

    nspec = pl.BlockSpec((tk, LANES), lambda ph, b, hp: (b, ph * (A_WIDTH // LANES) + hp))
    dk, dv = pl.pallas_call(
        body, name=name, grid=(dil, L // tk, A_WIDTH // LANES),
        in_specs=_band_specs(g, 0, dil, tk, nlb, True) + [_band_specs(g, 1, dil, tk, nlb, True)[1],
                                                         _band_specs(g, 2, dil, tk, nlb, True)[1]]
        + _band_specs(g, 0, dil, tk, nlb, False) * 3,
        out_specs=[nspec, nspec],
        out_shape=[jax.ShapeDtypeStruct((L, dil * A_WIDTH), BF16)] * 2,
        compiler_params=_params(("parallel", "parallel", "parallel")),
    )(*([view] * 5), *([nview(dout)] * 3), *([nview(lse)] * 3), *([nview(delta)] * 3))
    return dk.reshape(S, A_WIDTH), dv.reshape(S, A_WIDTH)


def _gate_block_weight(wf, wb):
    w = jnp.zeros((LANES, 2 * B_QK_WIDTH), F32)
    w = w.at[:B_GATE_RANK, :B_QK_WIDTH].set(wf)
    w = w.at[B_GATE_RANK:2 * B_GATE_RANK, B_QK_WIDTH:].set(wb)
    return w.astype(BF16)


def gate_fwd(proj, wblk, bias, *, name):
    S = proj.shape[0]
    ts = _tile(S, 512)
    W = 2 * B_QK_WIDTH
    zcol = (2 * B_QK_WIDTH + 2 * B_V_WIDTH) // LANES

    def body(z_ref, w_ref, b_ref, o_ref):
        x = jnp.dot(z_ref[...].astype(BF16), w_ref[...], preferred_element_type=F32) + b_ref[...]
        o_ref[...] = (jnp.minimum(x, 0.0) - jnp.log(1.0 + jnp.exp(-jnp.abs(x)))) * (1.0 / B_GATE_TAU)

    return pl.pallas_call(
        body, name=name, grid=(S // ts,),
        in_specs=[pl.BlockSpec((ts, LANES), lambda i: (i, zcol)), _const_spec((LANES, W)), _const_spec((1, W))],
        out_specs=pl.BlockSpec((ts, W), lambda i: (i, 0)),
        out_shape=jax.ShapeDtypeStruct((S, W), F32),
        compiler_params=_params(("parallel",)),
    )(proj, wblk, bias)


def gate_bwd(proj, wblk, wblk_t, bias, dloga, *, name):
    S = proj.shape[0]
    ts = _tile(S, 512)
    W = 2 * B_QK_WIDTH
    zcol = (2 * B_QK_WIDTH + 2 * B_V_WIDTH) // LANES

    def body(z_ref, w_ref, wt_ref, b_ref, d_ref, dz_ref, dw_ref, db_ref):
        @pl.when(pl.program_id(0) == 0)
        def _():
            dw_ref[...] = jnp.zeros_like(dw_ref)
            db_ref[...] = jnp.zeros_like(db_ref)

        z = z_ref[...].astype(BF16)
        x = jnp.dot(z, w_ref[...], preferred_element_type=F32) + b_ref[...]
        e = jnp.exp(-jnp.abs(x))
        sig_neg = jnp.where(x >= 0, e, 1.0) / (1.0 + e)
        dx = d_ref[...] * (1.0 / B_GATE_TAU) * sig_neg
        dxb = dx.astype(BF16)
        dz_ref[...] = jnp.dot(dxb, wt_ref[...], preferred_element_type=F32)
        dw_ref[...] += lax.dot_general(z, dxb, TN, preferred_element_type=F32)
        db_ref[0:1, :] += jnp.sum(dx, axis=0, keepdims=True)

    return pl.pallas_call(
        body, name=name, grid=(S // ts,),
        in_specs=[pl.BlockSpec((ts, LANES), lambda i: (i, zcol)), _const_spec((LANES, W)), _const_spec((W, LANES)),
                  _const_spec((1, W)), pl.BlockSpec((ts, W), lambda i: (i, 0))],
        out_specs=[pl.BlockSpec((ts, LANES), lambda i: (i, 0)), _const_spec((LANES, W)), _const_spec((8, W))],
        out_shape=[jax.ShapeDtypeStruct((S, LANES), F32), jax.ShapeDtypeStruct((LANES, W), F32),
                   jax.ShapeDtypeStruct((8, W), F32)],
        compiler_params=_params(("arbitrary",)),
    )(proj, wblk, wblk_t, bias, dloga)


def _tri(reverse):
    i = np.arange(B_CHUNK)
    t = (i[None, :] >= i[:, None]) if reverse else (i[None, :] <= i[:, None])
    return jnp.asarray(t.astype(np.float32))


def _gla_terms(q, k, la, t_ref, reverse):
    b = jnp.dot(t_ref[...], la, precision=HI, preferred_element_type=F32)
    b_last = b[0:1, :] if reverse else b[B_CHUNK - 1:B_CHUNK, :]
    e_b = jnp.exp(b)
    qt = (q * (B_KEY_DIM ** -0.5)) * e_b
    e_nb = jnp.exp(-b)
    kt = k * e_nb
    e_end = jnp.exp(b_last - b)
    kend = k * e_end
    dec = jnp.exp(b_last)
    return e_nb, e_b, qt, kt, e_end, kend, dec


def _chunk_mask(reverse, transpose=False):
    r = lax.broadcasted_iota(jnp.int32, (B_CHUNK, B_CHUNK), 0)
    c = lax.broadcasted_iota(jnp.int32, (B_CHUNK, B_CHUNK), 1)
    if transpose:
        r, c = c, r
    return (c > r) if reverse else (c <= r)


def gla_fwd(proj, loga, tmat, reverse, *, name):
    S = proj.shape[0]
    tb = _tile(S, 512)
    nb = S // tb
    cpb = tb // B_CHUNK
    nc = S // B_CHUNK
    qb, kb_, vb = 0, B_QK_WIDTH // B_KEY_DIM, 2 * B_QK_WIDTH // B_VAL_DIM
    lb = (B_QK_WIDTH // B_KEY_DIM) if reverse else 0
    blk = (lambda i: nb - 1 - i) if reverse else (lambda i: i)

    def body(q_ref, k_ref, v_ref, la_ref, t_ref, o_ref, st_ref, s_scr):
        @pl.when(pl.program_id(1) == 0)
        def _():
            s_scr[...] = jnp.zeros_like(s_scr)

        mask = _chunk_mask(reverse)
        order = range(cpb - 1, -1, -1) if reverse else range(cpb)
        for c in order:
            rows = slice(c * B_CHUNK, (c + 1) * B_CHUNK)
            v = v_ref[rows, :].astype(BF16)
            _, _, qt, kt, _, kend, dec = _gla_terms(q_ref[rows, :], k_ref[rows, :], la_ref[rows, :], t_ref, reverse)
            qt, kt, kend = qt.astype(BF16), kt.astype(BF16), kend.astype(BF16)
            st = s_scr[...]
            st_ref[0, c] = st
            a = jnp.where(mask, lax.dot_general(qt, kt, NT, preferred_element_type=F32), 0.0)
            o = jnp.dot(a.astype(BF16), v, preferred_element_type=F32)
            o = o + lax.dot_general(qt, st.astype(BF16), NT, preferred_element_type=F32)
            o_ref[rows, :] = o
            s_scr[...] = st * dec + lax.dot_general(v, kend, TN, preferred_element_type=F32)

    return pl.pallas_call(
        body, name=name, grid=(B_HEADS, nb),
        in_specs=[pl.BlockSpec((tb, B_KEY_DIM), lambda h, i: (blk(i), qb + h)),
                  pl.BlockSpec((tb, B_KEY_DIM), lambda h, i: (blk(i), kb_ + h)),
                  pl.BlockSpec((tb, B_VAL_DIM), lambda h, i: (blk(i), vb + h)),
                  pl.BlockSpec((tb, B_KEY_DIM), lambda h, i: (blk(i), lb + h)),
                  _const_spec((B_CHUNK, B_CHUNK))],
        out_specs=[pl.BlockSpec((tb, B_VAL_DIM), lambda h, i: (blk(i), h)),
                   pl.BlockSpec((1, cpb, B_VAL_DIM, B_KEY_DIM), lambda h, i: (h, blk(i), 0, 0))],
        out_shape=[jax.ShapeDtypeStruct((S, B_V_WIDTH), F32),
                   jax.ShapeDtypeStruct((B_HEADS, nc, B_VAL_DIM, B_KEY_DIM), F32)],
        scratch_shapes=[pltpu.VMEM((B_VAL_DIM, B_KEY_DIM), F32)],
        compiler_params=_params(("parallel", "arbitrary")),
    )(proj, proj, proj, loga, tmat)


def gla_bwd(proj, loga, states, do, tmat, tmat_t, reverse, *, name):
    S = proj.shape[0]
    tb = _tile(S, 512)
    nb = S // tb
    cpb = tb // B_CHUNK
    qb, kb_, vb = 0, B_QK_WIDTH // B_KEY_DIM, 2 * B_QK_WIDTH // B_VAL_DIM
    lb = (B_QK_WIDTH // B_KEY_DIM) if reverse else 0
    blk = (lambda i: i) if reverse else (lambda i: nb - 1 - i)
    scale = B_KEY_DIM ** -0.5

    def body(q_ref, k_ref, v_ref, la_ref, st_ref, do_ref, t_ref, tt_ref, dq_ref, dk_ref, dv_ref, dla_ref, ds_scr):
        @pl.when(pl.program_id(1) == 0)
        def _():
            ds_scr[...] = jnp.zeros_like(ds_scr)

        mask = _chunk_mask(reverse)
        mask_t = _chunk_mask(reverse, transpose=True)
        last = 0 if reverse else B_CHUNK - 1
        is_last = lax.broadcasted_iota(jnp.int32, (B_CHUNK, B_KEY_DIM), 0) == last
        order = range(cpb) if reverse else range(cpb - 1, -1, -1)
        for c in order:
            rows = slice(c * B_CHUNK, (c + 1) * B_CHUNK)
            vf = v_ref[rows, :]
            v = vf.astype(BF16)
            dov = do_ref[rows, :]
            dob = dov.astype(BF16)
            e_nb, e_b, qt, kt, e_end, kend, dec = _gla_terms(q_ref[rows, :], k_ref[rows, :], la_ref[rows, :], t_ref, reverse)
            qtb, ktb, kendb = qt.astype(BF16), kt.astype(BF16), kend.astype(BF16)
            st = st_ref[0, c]
            dst = ds_scr[...]
            dstb = dst.astype(BF16)
            a_t = jnp.where(mask_t, lax.dot_general(ktb, qtb, NT, preferred_element_type=F32), 0.0)
            da = jnp.where(mask, lax.dot_general(dob, v, NT, preferred_element_type=F32), 0.0)
            da_t = jnp.where(mask_t, lax.dot_general(v, dob, NT, preferred_element_type=F32), 0.0)
            dv = jnp.dot(a_t.astype(BF16), dob, preferred_element_type=F32)
            dv = dv + lax.dot_general(kendb, dstb, NT, preferred_element_type=F32)
            dqt = jnp.dot(da.astype(BF16), ktb, preferred_element_type=F32)
            dqt = dqt + jnp.dot(dob, st.astype(BF16), preferred_element_type=F32)
            dkt = jnp.dot(da_t.astype(BF16), qtb, preferred_element_type=F32)
            dkend = jnp.dot(v, dstb, preferred_element_type=F32)
            ddec = jnp.sum(dst * st, axis=0, keepdims=True)
            ds_scr[...] = dst * dec + lax.dot_general(dob, qtb, TN, preferred_element_type=F32)
            ke = dkend * kend
            db = dqt * qt - dkt * kt - ke
            db_last = jnp.sum(ke, axis=0, keepdims=True) + ddec * dec
            db = db + jnp.where(is_last, db_last, 0.0)
            dq_ref[rows, :] = dqt * e_b * scale
            dk_ref[rows, :] = dkt * e_nb + dkend * e_end
            dv_ref[rows, :] = dv
            dla_ref[rows, :] = jnp.dot(tt_ref[...], db, precision=HI, preferred_element_type=F32)

    kspec = lambda cb: pl.BlockSpec((tb, B_KEY_DIM), lambda h, i: (blk(i), cb + h))
    return pl.pallas_call(
        body, name=name, grid=(B_HEADS, nb),
        in_specs=[kspec(qb), kspec(kb_), pl.BlockSpec((tb, B_VAL_DIM), lambda h, i: (blk(i), vb + h)), kspec(lb),
                  pl.BlockSpec((1, cpb, B_VAL_DIM, B_KEY_DIM), lambda h, i: (h, blk(i), 0, 0)),
                  pl.BlockSpec((tb, B_VAL_DIM), lambda h, i: (blk(i), h)),
                  _const_spec((B_CHUNK, B_CHUNK)), _const_spec((B_CHUNK, B_CHUNK))],
        out_specs=[kspec(0), kspec(0), pl.BlockSpec((tb, B_VAL_DIM), lambda h, i: (blk(i), h)), kspec(0)],
        out_shape=[jax.ShapeDtypeStruct((S, B_QK_WIDTH), F32), jax.ShapeDtypeStruct((S, B_QK_WIDTH), F32),
                   jax.ShapeDtypeStruct((S, B_V_WIDTH), F32), jax.ShapeDtypeStruct((S, B_QK_WIDTH), F32)],
        scratch_shapes=[pltpu.VMEM((B_VAL_DIM, B_KEY_DIM), F32)],
        compiler_params=_params(("parallel", "arbitrary")),
    )(proj, proj, proj, loga, states, do, tmat, tmat_t)


def gla_post_fwd(o_f, o_b, gain, proj, *, name):
    S = o_f.shape[0]
    ts = _tile(S, 512)
    rcol = (2 * B_QK_WIDTH + B_V_WIDTH) // B_V_WIDTH

    def body(f_ref, b_ref, g_ref, r_ref, y_ref):
        for h in range(B_HEADS):
            sl = slice(h * B_VAL_DIM, (h + 1) * B_VAL_DIM)
            o = f_ref[:, sl] + b_ref[:, sl]
            n = (o * lax.rsqrt(jnp.mean(o * o, axis=-1, keepdims=True) + RMS_EPS)) * g_ref[:, sl]
            r = r_ref[:, sl]
            y_ref[:, sl] = (n * (r * (1.0 / (1.0 + jnp.exp(-r))))).astype(y_ref.dtype)

    spec = pl.BlockSpec((ts, B_V_WIDTH), lambda i: (i, 0))
    return pl.pallas_call(
        body, name=name, grid=(S // ts,),
        in_specs=[spec, spec, _const_spec((1, B_V_WIDTH)), pl.BlockSpec((ts, B_V_WIDTH), lambda i: (i, rcol))],
        out_specs=spec, out_shape=jax.ShapeDtypeStruct((S, B_V_WIDTH), BF16),
        compiler_params=_params(("parallel",)),
    )(o_f, o_b, gain, proj)


def gla_post_bwd(o_f, o_b, gain, proj, dy, *, name):
    S = o_f.shape[0]
    ts = _tile(S, 512)
    rcol = (2 * B_QK_WIDTH + B_V_WIDTH) // B_V_WIDTH

    def body(f_ref, b_ref, g_ref, r_ref, dy_ref, do_ref, dr_ref, dg_ref):
        @pl.when(pl.program_id(0) == 0)
        def _():
            dg_ref[...] = jnp.zeros_like(dg_ref)

        for h in range(B_HEADS):
            sl = slice(h * B_VAL_DIM, (h + 1) * B_VAL_DIM)
            o = f_ref[:, sl] + b_ref[:, sl]
            rs = lax.rsqrt(jnp.mean(o * o, axis=-1, keepdims=True) + RMS_EPS)
            ohat = o * rs
            g = g_ref[:, sl]
            r = r_ref[:, sl]
            sig = 1.0 / (1.0 + jnp.exp(-r))
            dyv = dy_ref[:, sl].astype(F32)
            dn = dyv * (r * sig)
            dr_ref[:, sl] = dyv * (ohat * g) * (sig * (1.0 + r * (1.0 - sig)))
            dng = dn * g
            do_ref[:, sl] = rs * (dng - ohat * jnp.mean(dng * ohat, axis=-1, keepdims=True))
            dg_ref[0:1, sl] += jnp.sum(dn * ohat, axis=0, keepdims=True)

    spec = pl.BlockSpec((ts, B_V_WIDTH), lambda i: (i, 0))
    return pl.pallas_call(
        body, name=name, grid=(S // ts,),
        in_specs=[spec, spec, _const_spec((1, B_V_WIDTH)), pl.BlockSpec((ts, B_V_WIDTH), lambda i: (i, rcol)), spec],
        out_specs=[spec, spec, _const_spec((8, B_V_WIDTH))],
        out_shape=[jax.ShapeDtypeStruct((S, B_V_WIDTH), F32), jax.ShapeDtypeStruct((S, B_V_WIDTH), F32),
                   jax.ShapeDtypeStruct((8, B_V_WIDTH), F32)],
        compiler_params=_params(("arbitrary",)),
    )(o_f, o_b, gain, proj, dy)


def gla_combine(parts_f, parts_b, dr, dz, *, name):
    S = dr.shape[0]
    ts = _tile(S, 512)

    def body(qf, kf, vf, qb, kb, vb, r_ref, z_ref, o_ref):
        o_ref[:, 0:512] = (qf[...] + qb[...]).astype(o_ref.dtype)
        o_ref[:, 512:1024] = (kf[...] + kb[...]).astype(o_ref.dtype)
        o_ref[:, 1024:2048] = (vf[...] + vb[...]).astype(o_ref.dtype)
        o_ref[:, 2048:3072] = r_ref[...].astype(o_ref.dtype)
        o_ref[:, 3072:3200] = z_ref[...].astype(o_ref.dtype)

    s512 = pl.BlockSpec((ts, B_QK_WIDTH), lambda i: (i, 0))
    s1024 = pl.BlockSpec((ts, B_V_WIDTH), lambda i: (i, 0))
    return pl.pallas_call(
        body, name=name, grid=(S // ts,),
        in_specs=[s512, s512, s1024, s512, s512, s1024, s1024, pl.BlockSpec((ts, LANES), lambda i: (i, 0))],
        out_specs=pl.BlockSpec((ts, B_IN_PAD), lambda i: (i, 0)),
        out_shape=jax.ShapeDtypeStruct((S, B_IN_PAD), BF16),
        compiler_params=_params(("parallel",)),
    )(*parts_f, *parts_b, dr, dz)


def adamw(w, g, m, v, *, name):
    R, C = w.shape
    tr = _tile(R, 512)
    c1 = 1.0 / (1.0 - ADAM_B1 ** ADAM_STEP)
    c2 = 1.0 / (1.0 - ADAM_B2 ** ADAM_STEP)

    def body(w_ref, g_ref, m_ref, v_ref, d_ref, mo_ref, vo_ref):
        gv = g_ref[...]
        mn = ADAM_B1 * m_ref[...] + (1.0 - ADAM_B1) * gv
        vn = ADAM_B2 * v_ref[...] + (1.0 - ADAM_B2) * (gv * gv)
        mo_ref[...] = mn
        vo_ref[...] = vn
        d_ref[...] = -ADAM_LR * ((mn * c1) / (jnp.sqrt(vn * c2) + ADAM_EPS) + ADAM_WD * w_ref[...])

    spec = pl.BlockSpec((tr, C), lambda i: (i, 0))
    return pl.pallas_call(
        body, name=name, grid=(R // tr,), in_specs=[spec] * 4, out_specs=[spec] * 3,
        out_shape=[jax.ShapeDtypeStruct((R, C), F32)] * 3,
        compiler_params=_params(("parallel",)),
    )(w, g, m, v)


def _chip_peers():
    x, y, c = lax.axis_index("x"), lax.axis_index("y"), lax.axis_index("c")
    return x, y, c, [(1 - x, y), (x, 1 - y), (1 - x, 1 - y)]


def chip_exchange(src, per_dest, *, name):
    _, R, C = src.shape

    def body(src_ref, out_ref, send_sems, recv_sems, local_sem):
        x, y, c, chips = _chip_peers()
        me = 2 * x + y

        def block(t):
            return src_ref.at[t if per_dest else 0]

        mine = pltpu.make_async_copy(block(me), out_ref.at[me], local_sem)
        mine.start()
        copies = []
        for k, (px, py) in enumerate(chips):
            cp = pltpu.make_async_remote_copy(
                src_ref=block(2 * px + py), dst_ref=out_ref.at[me], send_sem=send_sems.at[k], recv_sem=recv_sems.at[k],
                device_id=(px, py, c), device_id_type=MESH)
            cp.start()
            copies.append(cp)
        for k, (px, py) in enumerate(chips):
            pltpu.make_async_remote_copy(
                src_ref=block(me), dst_ref=out_ref.at[2 * px + py], send_sem=send_sems.at[k], recv_sem=recv_sems.at[k],
                device_id=(px, py, c), device_id_type=MESH).wait_recv()
        for cp in copies:
            cp.wait_send()
        mine.wait()

    return pl.pallas_call(
        body, name=name,
        in_specs=[pl.BlockSpec(memory_space=pl.ANY)], out_specs=pl.BlockSpec(memory_space=pl.ANY),
        out_shape=jax.ShapeDtypeStruct((N_CHIPS, R, C), src.dtype),
        scratch_shapes=[pltpu.SemaphoreType.DMA((3,)), pltpu.SemaphoreType.DMA((3,)), pltpu.SemaphoreType.DMA],
        compiler_params=pltpu.CompilerParams(has_side_effects=True),
    )(src)


def sibling_exchange(src, *, name):
    def body(src_ref, out_ref, send_sem, recv_sem):
        x, y, c = lax.axis_index("x"), lax.axis_index("y"), lax.axis_index("c")
        cp = pltpu.make_async_remote_copy(
            src_ref=src_ref.at[1 - c], dst_ref=out_ref, send_sem=send_sem, recv_sem=recv_sem,
            device_id=(x, y, 1 - c), device_id_type=MESH)
        cp.start()
        cp.wait()

    return pl.pallas_call(
        body, name=name,
        in_specs=[pl.BlockSpec(memory_space=pl.ANY)], out_specs=pl.BlockSpec(memory_space=pl.ANY),
        out_shape=jax.ShapeDtypeStruct(src.shape[1:], src.dtype),
        scratch_shapes=[pltpu.SemaphoreType.DMA, pltpu.SemaphoreType.DMA],
        compiler_params=pltpu.CompilerParams(has_side_effects=True),
    )(src)


def sibling_share(src, *, name):
    def body(src_ref, out_ref, send_sem, recv_sem, local_sem):
        x, y, c = lax.axis_index("x"), lax.axis_index("y"), lax.axis_index("c")
        mine = pltpu.make_async_copy(src_ref, out_ref.at[c], local_sem)
        mine.start()
        cp = pltpu.make_async_remote_copy(
            src_ref=src_ref, dst_ref=out_ref.at[c], send_sem=send_sem, recv_sem=recv_sem,
            device_id=(x, y, 1 - c), device_id_type=MESH)
        cp.start()
        cp.wait()
        mine.wait()

    return pl.pallas_call(
        body, name=name,
        in_specs=[pl.BlockSpec(memory_space=pl.ANY)], out_specs=pl.BlockSpec(memory_space=pl.ANY),
        out_shape=jax.ShapeDtypeStruct((2,) + src.shape, src.dtype),
        scratch_shapes=[pltpu.SemaphoreType.DMA, pltpu.SemaphoreType.DMA, pltpu.SemaphoreType.DMA],
        compiler_params=pltpu.CompilerParams(has_side_effects=True),
    )(src)


def add_pair(a, b, *, name):
    _, n, R, C = a.shape
    tr = _tile(R, 256)

    def body(c_ref, a_ref, b_ref, o_ref):
        o_ref[...] = a_ref[0] + b_ref[...]

    return pl.pallas_call(
        body, name=name,
        grid_spec=pltpu.PrefetchScalarGridSpec(
            num_scalar_prefetch=1, grid=(n, R // tr),
            in_specs=[pl.BlockSpec((1, 1, tr, C), lambda s, i, c_ref: (c_ref[0], s, i, 0)),
                      pl.BlockSpec((1, tr, C), lambda s, i, c_ref: (s, i, 0))],
            out_specs=pl.BlockSpec((1, tr, C), lambda s, i, c_ref: (s, i, 0))),
        out_shape=jax.ShapeDtypeStruct((n, R, C), a.dtype),
        compiler_params=_params(("parallel", "parallel")),
    )(lax.axis_index("c").reshape(1).astype(jnp.int32), a, b)


def sum_slots(a, *, name):
    n, R, C = a.shape
    tr = _tile(R, 256)

    def body(*refs):
        o_ref = refs[-1]
        acc = refs[0][0]
        for r in refs[1:-1]:
            acc = acc + r[0]
        o_ref[...] = acc

    return pl.pallas_call(
        body, name=name, grid=(R // tr,),
        in_specs=[pl.BlockSpec((1, tr, C), functools.partial(lambda s, i: (s, i, 0), s)) for s in range(n)],
        out_specs=pl.BlockSpec((tr, C), lambda i: (i, 0)),
        out_shape=jax.ShapeDtypeStruct((R, C), a.dtype),
        compiler_params=_params(("parallel",)),
    )(*([a] * n))


def _flat_rows(n_elems, mult):
    rows = -(-n_elems // FLAT_COLS)
    return -(-rows // mult) * mult


def _to_flat(parts, mult):
    v = jnp.concatenate([p.reshape(-1) for p in parts])
    rows = _flat_rows(v.shape[0], mult)
    return jnp.pad(v, (0, rows * FLAT_COLS - v.shape[0])).reshape(rows, FLAT_COLS)


def _from_flat(flat, shapes):
    v = flat.reshape(-1)
    out, off = [], 0
    for s in shapes:
        n = int(np.prod(s))
        out.append(v[off:off + n].reshape(s))
        off += n
    return out


def _unshard(blocks, axis):
    return jnp.concatenate([blocks[s] for s in range(N_CHIPS)], axis=axis)


def _shards(full, axis):
    return jnp.stack(jnp.split(full, N_CHIPS, axis=axis))


def _gather_weights(w):
    full = {}
    for names, dtype, mult, call in ((BIG, BF16, 16, "gather_weights"), (SMALL_SHARDED, F32, 8, "gather_vectors")):
        parts = [w[n].astype(dtype) for n in names]
        got = chip_exchange(_to_flat(parts, mult)[None], False, name=call)
        per_chip = [_from_flat(got[s], [p.shape for p in parts]) for s in range(N_CHIPS)]
        for i, n in enumerate(names):
            full[n] = _unshard([per_chip[s][i] for s in range(N_CHIPS)], SHARD_AXIS[n])
    return full


def _reduce_gradients(grads):
    parts = []
    for s in range(N_CHIPS):
        row = []
        for n in WEIGHTS:
            row.append(grads[n] if n in REPLICATED else _shards(grads[n], SHARD_AXIS[n])[s])
        parts.append(_to_flat(row, 32))
    g = jnp.stack(parts)
    R = g.shape[1]
    halves = jnp.stack([g[:, :R // 2], g[:, R // 2:]])
    other = sibling_exchange(halves, name="grad_pair_exchange")
    pair = add_pair(halves, other, name="grad_pair_add")
    slots = chip_exchange(pair, True, name="grad_chip_exchange")
    mine = sum_slots(slots, name="grad_chip_sum")
    both = sibling_share(mine, name="grad_half_share")
    return both.reshape(R, FLAT_COLS)


def _layer_fwd(i, h, p, aux):
    j = i // 2
    sv = {"h0": h}
    hn = rmsnorm_fwd(h, p["attn_norm"][i][None], name=f"l{i}_norm1")
    sv["hn"] = hn
    if i % 2 == 0:
        qkv = matmul(hn, p["a_w_in"][j], name=f"l{i}_a_in")
        qkvp = qk_prep_fwd(qkv, aux["a_gain"][j], aux["cos"], aux["sin"], aux["ones"], name=f"l{i}_a_prep")
        os_, lses = [], []
        for g in range(len(A_GROUPS)):
            o, l = attn_fwd(qkvp, g, name=f"l{i}_a_attn{g}")
            os_.append(o)
            lses.append(l)
        out, lse = attn_merge(os_, lses, name=f"l{i}_a_merge")
        sv.update(qkv=qkv, qkvp=qkvp, out=out, lse=lse)
        h1 = matmul(out, p["a_w_out"][j], res=h, name=f"l{i}_a_out")
    else:
        proj = matmul(hn, p["b_w_in"][j], tn=640, name=f"l{i}_b_in")
        loga = gate_fwd(proj, aux["b_wblk"][j], aux["b_bias"][j], name=f"l{i}_b_gate")
        o_f, st_f = gla_fwd(proj, loga, aux["tri_f"], False, name=f"l{i}_b_gla_f")
        o_b, st_b = gla_fwd(proj, loga, aux["tri_b"], True, name=f"l{i}_b_gla_b")
        y = gla_post_fwd(o_f, o_b, aux["b_gain"][j], proj, name=f"l{i}_b_post")
        sv.update(proj=proj, loga=loga, o_f=o_f, o_b=o_b, st_f=st_f, st_b=st_b, y=y)
        h1 = matmul(y, p["b_w_out"][j], res=h, name=f"l{i}_b_out")
    sv["h1"] = h1
    hn2 = rmsnorm_fwd(h1, p["ffn_norm"][i][None], name=f"l{i}_norm2")
    gu = matmul(hn2, p["ffn_w_gate_up"][i], out_dtype=BF16, name=f"l{i}_f_up")
    act = swiglu_fwd(gu, name=f"l{i}_f_act")
    h2 = matmul(act, p["ffn_w_down"][i], res=h1, tk=2816, name=f"l{i}_f_down")
    sv.update(hn2=hn2, gu=gu, act=act)
    return h2, sv


def _layer_bwd(i, dh, p, pt, aux, sv, grads):
    j = i // 2
    dhb = dh.astype(BF16)
    grads["ffn_w_down"][i] = matmul(sv["act"], dhb, trans_a=True, tk=512, name=f"l{i}_f_down_dw")
    dact = matmul(dhb, pt["ffn_w_down"][i], out_dtype=BF16, name=f"l{i}_f_down_dx")
    dgu = swiglu_bwd(sv["gu"], dact, name=f"l{i}_f_act_bwd")
    grads["ffn_w_gate_up"][i] = matmul(sv["hn2"], dgu, trans_a=True, tk=512, name=f"l{i}_f_up_dw")
    dhn2 = matmul(dgu, pt["ffn_w_gate_up"][i], tk=512, name=f"l{i}_f_up_dx")
    dh1, dg = rmsnorm_bwd(sv["h1"], p["ffn_norm"][i][None], dhn2, dh, name=f"l{i}_norm2_bwd")
    grads["ffn_norm"][i] = dg[0]
    dh1b = dh1.astype(BF16)
    if i % 2 == 0:
        grads["a_w_out"][j] = matmul(sv["out"], dh1b, trans_a=True, tk=512, name=f"l{i}_a_out_dw")
        dout = matmul(dh1b, pt["a_w_out"][j], out_dtype=BF16, name=f"l{i}_a_out_dx")
        delta = attn_delta(dout, sv["out"], aux["ones"], name=f"l{i}_a_delta")
        gl = []
        for g in range(len(A_GROUPS)):
            dq = attn_bwd_dq(sv["qkvp"], dout, sv["lse"], delta, g, name=f"l{i}_a_dq{g}")
            dk, dv = attn_bwd_dkv(sv["qkvp"], dout, sv["lse"], delta, g, name=f"l{i}_a_dkv{g}")
            gl += [dq, dk, dv]
        dqkv, dgain = qk_prep_bwd(sv["qkv"], aux["a_gain"][j], aux["cos"], aux["sin"], aux["ones"], gl,
                                  name=f"l{i}_a_prep_bwd")
        dgh = dgain[0].reshape(len(A_GROUPS), 3, A_HEADS, A_HEAD_DIM).sum(axis=2)
        grads["a_q_norm"][j] = dgh[:, 0]
        grads["a_k_norm"][j] = dgh[:, 1]
        grads["a_w_in"][j] = matmul(sv["hn"], dqkv, trans_a=True, tk=512, name=f"l{i}_a_in_dw")
        dhn = matmul(dqkv, pt["a_w_in"][j], tk=1024, name=f"l{i}_a_in_dx")
    else:
        grads["b_w_out"][j] = matmul(sv["y"], dh1b, trans_a=True, tk=512, name=f"l{i}_b_out_dw")
        dy = matmul(dh1b, pt["b_w_out"][j], name=f"l{i}_b_out_dx")
        do, dr, dgn = gla_post_bwd(sv["o_f"], sv["o_b"], aux["b_gain"][j], sv["proj"], dy, name=f"l{i}_b_post_bwd")
        grads["b_out_norm"][j] = dgn[0].reshape(B_HEADS, B_VAL_DIM)
        pf = gla_bwd(sv["proj"], sv["loga"], sv["st_f"], do, aux["tri_f"], aux["tri_b"], False, name=f"l{i}_b_gla_f_bwd")
        pb = gla_bwd(sv["proj"], sv["loga"], sv["st_b"], do, aux["tri_b"], aux["tri_f"], True, name=f"l{i}_b_gla_b_bwd")
        dloga = jnp.concatenate([pf[3], pb[3]], axis=1)
        dz, dwblk, dbias = gate_bwd(sv["proj"], aux["b_wblk"][j], aux["b_wblk_t"][j], aux["b_bias"][j], dloga,
                                    name=f"l{i}_b_gate_bwd")
        grads["b_w_gate_f"][j] = dwblk[:B_GATE_RANK, :B_QK_WIDTH]
        grads["b_w_gate_b"][j] = dwblk[B_GATE_RANK:2 * B_GATE_RANK, B_QK_WIDTH:]
        grads["b_gate_bias_f"][j] = dbias[0, :B_QK_WIDTH]
        grads["b_gate_bias_b"][j] = dbias[0, B_QK_WIDTH:]
        dproj = gla_combine(pf[:3], pb[:3], dr, dz, name=f"l{i}_b_combine")
        grads["b_w_in"][j] = matmul(sv["hn"], dproj, trans_a=True, tn=640, tk=512, name=f"l{i}_b_in_dw")[:, :B_IN_WIDTH]
        dhn = matmul(dproj, pt["b_w_in"][j], tk=640, name=f"l{i}_b_in_dx")
    dh0, dg = rmsnorm_bwd(sv["h0"], p["attn_norm"][i][None], dhn, dh1, name=f"l{i}_norm1_bwd")
    grads["attn_norm"][i] = dg[0]
    return dh0


def _local_step(x, target, p, small):
    S = x.shape[0]
    cos, sin = _rope_tables(S)
    ones_v = jnp.ones((A_WIDTH,), F32)
    a_gain = [jnp.concatenate([jnp.concatenate([jnp.tile(small["a_q_norm"][j][g], A_HEADS),
                                                jnp.tile(small["a_k_norm"][j][g], A_HEADS), ones_v])
                               for g in range(len(A_GROUPS))])[None] for j in range(2)]
    b_wblk = [_gate_block_weight(p["b_w_gate_f"][j].astype(F32), p["b_w_gate_b"][j].astype(F32)) for j in range(2)]
    aux = dict(cos=cos, sin=sin, ones=_head_block_ones(), a_gain=a_gain, tri_f=_tri(False), tri_b=_tri(True),
               b_wblk=b_wblk, b_wblk_t=[w.T for w in b_wblk],
               b_bias=[jnp.concatenate([small["b_gate_bias_f"][j], small["b_gate_bias_b"][j]])[None] for j in range(2)],
               b_gain=[small["b_out_norm"][j].reshape(1, B_V_WIDTH) for j in range(2)])
    pw = dict(p)
    pw["b_w_in"] = jnp.pad(p["b_w_in"], ((0, 0), (0, 0), (0, B_IN_PAD - B_IN_WIDTH)))
    pw["attn_norm"], pw["ffn_norm"] = small["attn_norm"], small["ffn_norm"]
    pt = {n: jnp.swapaxes(pw[n], 1, 2) for n in ("a_w_in", "a_w_out", "b_w_in", "b_w_out", "ffn_w_gate_up", "ffn_w_down")}

    h = x
    saved = []
    for i in range(DEPTH):
        h, sv = _layer_fwd(i, h, pw, aux)
        saved.append(sv)
    loss_sq, dh = loss_head(h, target, name="loss_head")
    grads = {n: [None] * (DEPTH if n in ("attn_norm", "ffn_norm", "ffn_w_gate_up", "ffn_w_down") else 2) for n in WEIGHTS}
    for i in reversed(range(DEPTH)):
        dh = _layer_bwd(i, dh, pw, pt, aux, saved[i], grads)
    grads = {n: jnp.stack(v) for n, v in grads.items()}
    return loss_sq[0, 0] * (0.5 / D_MODEL), dh, grads


def kernel(x, attn_norm, ffn_norm, a_w_in, a_q_norm, a_k_norm, a_w_out, b_w_in, b_w_gate_f, b_gate_bias_f, b_w_gate_b, b_gate_bias_b, b_out_norm, b_w_out, ffn_w_gate_up, ffn_w_down, loss_target, m_attn_norm, m_ffn_norm, m_a_w_in, m_a_q_norm, m_a_k_norm, m_a_w_out, m_b_w_in, m_b_w_gate_f, m_b_gate_bias_f, m_b_w_gate_b, m_b_gate_bias_b, m_b_out_norm, m_b_w_out, m_ffn_w_gate_up, m_ffn_w_down, v_attn_norm, v_ffn_norm, v_a_w_in, v_a_q_norm, v_a_k_norm, v_a_w_out, v_b_w_in, v_b_w_gate_f, v_b_gate_bias_f, v_b_w_gate_b, v_b_gate_bias_b, v_b_out_norm, v_b_w_out, v_ffn_w_gate_up, v_ffn_w_down):
    w = dict(attn_norm=attn_norm, ffn_norm=ffn_norm, a_w_in=a_w_in, a_q_norm=a_q_norm, a_k_norm=a_k_norm, a_w_out=a_w_out,
             b_w_in=b_w_in, b_w_gate_f=b_w_gate_f, b_gate_bias_f=b_gate_bias_f, b_w_gate_b=b_w_gate_b,
             b_gate_bias_b=b_gate_bias_b, b_out_norm=b_out_norm, b_w_out=b_w_out, ffn_w_gate_up=ffn_w_gate_up,
             ffn_w_down=ffn_w_down)
    m = dict(attn_norm=m_attn_norm, ffn_norm=m_ffn_norm, a_w_in=m_a_w_in, a_q_norm=m_a_q_norm, a_k_norm=m_a_k_norm,
             a_w_out=m_a_w_out, b_w_in=m_b_w_in, b_w_gate_f=m_b_w_gate_f, b_gate_bias_f=m_b_gate_bias_f,
             b_w_gate_b=m_b_w_gate_b, b_gate_bias_b=m_b_gate_bias_b, b_out_norm=m_b_out_norm, b_w_out=m_b_w_out,
             ffn_w_gate_up=m_ffn_w_gate_up, ffn_w_down=m_ffn_w_down)
    v = dict(attn_norm=v_attn_norm, ffn_norm=v_ffn_norm, a_w_in=v_a_w_in, a_q_norm=v_a_q_norm, a_k_norm=v_a_k_norm,
             a_w_out=v_a_w_out, b_w_in=v_b_w_in, b_w_gate_f=v_b_w_gate_f, b_gate_bias_f=v_b_gate_bias_f,
             b_w_gate_b=v_b_w_gate_b, b_gate_bias_b=v_b_gate_bias_b, b_out_norm=v_b_out_norm, b_w_out=v_b_w_out,
             ffn_w_gate_up=v_ffn_w_gate_up, ffn_w_down=v_ffn_w_down)

    full = _gather_weights(w)
    p = {n: full[n] for n in BIG}
    small = {n: full[n] for n in SMALL_SHARDED}
    small.update({n: w[n] for n in REPLICATED})
    loss_local, dx, grads = _local_step(x[0], loss_target[0], p, small)
    loss = lax.psum(loss_local, ("x", "y", "c"))

    g_flat = _reduce_gradients(grads)
    order = lambda d: _to_flat([d[n] for n in WEIGHTS], 32)
    delta, new_m, new_v = adamw(order(w), g_flat, order(m), order(v), name="adamw")
    shapes = [w[n].shape for n in WEIGHTS]
    outs = [_from_flat(t, shapes) for t in (g_flat, delta, new_m, new_v)]
    return (loss, dx[None], *outs[0], *outs[1], *outs[2], *outs[3])
```

```python
import functools

import numpy as np
import jax
import jax.numpy as jnp
from jax import lax
from jax.experimental import pallas as pl
from jax.experimental.pallas import tpu as pltpu

F32, BF16 = jnp.float32, jnp.bfloat16
HI = lax.Precision.HIGHEST
MESH = pl.DeviceIdType.MESH

D_MODEL = 1024
DEPTH = 4
RMS_EPS = 1e-6
NEG_INF = -1e30
A_GROUPS = ((128, 1), (512, 4), (2048, 16))
DILS = tuple(d for _, d in A_GROUPS)
A_HALF = 64
A_HEADS = 16
A_HEAD_DIM = 64
A_WIDTH = 1024
A_IN_WIDTH = 9216
ROPE_THETA = 10000.0
B_HEADS = 4
B_KEY_DIM = 128
B_VAL_DIM = 256
B_QK_WIDTH = 512
B_V_WIDTH = 1024
B_GATE_RANK = 16
B_GATE_TAU = 16.0
B_CHUNK = 64
B_IN_WIDTH = 3104
B_IN_PAD = 3200
FFN_HIDDEN = 2816
ADAM_LR, ADAM_B1, ADAM_B2, ADAM_EPS, ADAM_WD, ADAM_STEP = 0.001, 0.9, 0.999, 1e-08, 0.01, 10
LANES = 128
VMEM_LIMIT = 48 * 1024 * 1024
FLAT_COLS = 1024
N_CHIPS = 4

WEIGHTS = ['attn_norm', 'ffn_norm', 'a_w_in', 'a_q_norm', 'a_k_norm', 'a_w_out', 'b_w_in', 'b_w_gate_f',
           'b_gate_bias_f', 'b_w_gate_b', 'b_gate_bias_b', 'b_out_norm', 'b_w_out', 'ffn_w_gate_up', 'ffn_w_down']
REPLICATED = ('attn_norm', 'ffn_norm', 'a_q_norm', 'a_k_norm')
SHARD_AXIS = {'a_w_in': 2, 'a_w_out': 1, 'b_w_in': 2, 'b_w_gate_f': 2, 'b_gate_bias_f': 1, 'b_w_gate_b': 2,
              'b_gate_bias_b': 1, 'b_out_norm': 2, 'b_w_out': 1, 'ffn_w_gate_up': 2, 'ffn_w_down': 1}
BIG = ('a_w_in', 'a_w_out', 'b_w_in', 'b_w_gate_f', 'b_w_gate_b', 'b_w_out', 'ffn_w_gate_up', 'ffn_w_down')
SMALL_SHARDED = ('b_gate_bias_f', 'b_gate_bias_b', 'b_out_norm')
MATRICES = ('a_w_in', 'a_w_out', 'b_w_in', 'b_w_out', 'ffn_w_gate_up', 'ffn_w_down')


def _params(sem):
    return pltpu.CompilerParams(dimension_semantics=sem, vmem_limit_bytes=VMEM_LIMIT)


def _tile(n, pref):
    t = min(n, pref)
    while n % t:
        t //= 2
    return t


def _const_spec(shape):
    nd = len(shape)
    return pl.BlockSpec(shape, lambda *_: (0,) * nd)


def matmul(a, b, *, name, trans_a=False, out_dtype=F32, res=None, tm=1024, tn=512, tk=1024):
    if trans_a:
        K, M = a.shape
    else:
        M, K = a.shape
    N = b.shape[1]
    assert b.shape[0] == K
    tm, tn, tk = _tile(M, tm), _tile(N, tn), _tile(K, tk)
    nk = K // tk
    dims = (((0,), (0,)), ((), ())) if trans_a else (((1,), (0,)), ((), ()))

    def body(*refs):
        if res is None:
            a_ref, b_ref, o_ref, acc_ref = refs
        else:
            a_ref, b_ref, r_ref, o_ref, acc_ref = refs
        k = pl.program_id(2)

        @pl.when(k == 0)
        def _():
            acc_ref[...] = jnp.zeros_like(acc_ref)

        acc_ref[...] += lax.dot_general(a_ref[...], b_ref[...], dims, preferred_element_type=F32)

        @pl.when(k == nk - 1)
        def _():
            v = acc_ref[...]
            if res is not None:
                v = v + r_ref[...]
            o_ref[...] = v.astype(o_ref.dtype)

    a_spec = pl.BlockSpec((tk, tm), lambda i, j, k: (k, i)) if trans_a else pl.BlockSpec((tm, tk), lambda i, j, k: (i, k))
    in_specs = [a_spec, pl.BlockSpec((tk, tn), lambda i, j, k: (k, j))]
    args = [a, b]
    if res is not None:
        in_specs.append(pl.BlockSpec((tm, tn), lambda i, j, k: (i, j)))
        args.append(res)
    return pl.pallas_call(
        body, name=name, grid=(M // tm, N // tn, nk), in_specs=in_specs,
        out_specs=pl.BlockSpec((tm, tn), lambda i, j, k: (i, j)),
        out_shape=jax.ShapeDtypeStruct((M, N), out_dtype),
        scratch_shapes=[pltpu.VMEM((tm, tn), F32)],
        compiler_params=_params(("parallel", "parallel", "arbitrary")),
    )(*args)


def _phase_spec(d, ts, W):
    if d == 1:
        return pl.BlockSpec((ts, W), lambda i: (i, 0))
    return pl.BlockSpec((d, ts // d, W), lambda i: (0, i, 0))


def _phase_view(a, d):
    return a if d == 1 else a.reshape(d, a.shape[0] // d, a.shape[1])


def _phase_shape(S, W, d, dtype):
    return jax.ShapeDtypeStruct((S, W) if d == 1 else (d, S // d, W), dtype)


def _nat_scratch(ts, W):
    return pltpu.VMEM((W // LANES, ts, LANES), F32)


def _put_natural(nat_ref, value):
    for c in range(nat_ref.shape[0]):
        nat_ref[c] = value[:, c * LANES:(c + 1) * LANES]


def _get_natural(nat_ref):
    return jnp.concatenate([nat_ref[c] for c in range(nat_ref.shape[0])], axis=1)


def _store_phases(nat_ref, o_ref, d, ts):
    for p in range(d):
        for c in range(nat_ref.shape[0]):
            o_ref[p, :, c * LANES:(c + 1) * LANES] = nat_ref[c, pl.ds(p, ts // d, stride=d), :].astype(o_ref.dtype)


def _load_phases(i_ref, nat_ref, d, ts):
    for p in range(d):
        for c in range(nat_ref.shape[0]):
            nat_ref[c, pl.ds(p, ts // d, stride=d), :] = i_ref[p, :, c * LANES:(c + 1) * LANES].astype(F32)


def rmsnorm_fwd(x, gain, *, name, dils=(1,)):
    S, Dm = x.shape
    ts = _tile(S, 512)

    def body(x_ref, g_ref, *rest):
        o_refs, scr = rest[:len(dils)], rest[len(dils)]
        xv = x_ref[...]
        r = lax.rsqrt(jnp.mean(xv * xv, axis=-1, keepdims=True) + RMS_EPS)
        y = (xv * r) * g_ref[...]
        if any(d > 1 for d in dils):
            _put_natural(scr, y)
        for d, o_ref in zip(dils, o_refs):
            if d == 1:
                o_ref[...] = y.astype(o_ref.dtype)
            else:
                _store_phases(scr, o_ref, d, ts)

    outs = pl.pallas_call(
        body, name=name, grid=(S // ts,),
        in_specs=[pl.BlockSpec((ts, Dm), lambda i: (i, 0)), _const_spec((1, Dm))],
        out_specs=[_phase_spec(d, ts, Dm) for d in dils],
        out_shape=[_phase_shape(S, Dm, d, BF16) for d in dils],
        scratch_shapes=[_nat_scratch(ts, Dm)],
        compiler_params=_params(("parallel",)),
    )(x, gain)
    return [o.reshape(S, Dm) for o in outs]


def rmsnorm_bwd(x, gain, dys, dres, *, name, dils=(1,)):
    S, Dm = x.shape
    ts = _tile(S, 256)
    nd = len(dils)

    def body(x_ref, g_ref, *rest):
        dy_refs, (dr_ref, dx_ref, dg_ref, scr) = rest[:nd], rest[nd:]

        @pl.when(pl.program_id(0) == 0)
        def _():
            dg_ref[...] = jnp.zeros_like(dg_ref)

        dyv = None
        for d, dy_ref in zip(dils, dy_refs):
            if d == 1:
                t = dy_ref[...].astype(F32)
            else:
                _load_phases(dy_ref, scr, d, ts)
                t = _get_natural(scr)
            dyv = t if dyv is None else dyv + t
        xv = x_ref[...]
        r = lax.rsqrt(jnp.mean(xv * xv, axis=-1, keepdims=True) + RMS_EPS)
        xhat = xv * r
        dyg = dyv * g_ref[...]
        dx = r * (dyg - xhat * jnp.mean(dyg * xhat, axis=-1, keepdims=True))
        dx_ref[...] = dr_ref[...] + dx
        dg_ref[0:1, :] += jnp.sum(dyv * xhat, axis=0, keepdims=True)

    row = pl.BlockSpec((ts, Dm), lambda i: (i, 0))
    return pl.pallas_call(
        body, name=name, grid=(S // ts,),
        in_specs=[row, _const_spec((1, Dm))] + [_phase_spec(d, ts, Dm) for d in dils] + [row],
        out_specs=[row, _const_spec((8, Dm))],
        out_shape=[jax.ShapeDtypeStruct((S, Dm), F32), jax.ShapeDtypeStruct((8, Dm), F32)],
        scratch_shapes=[_nat_scratch(ts, Dm)],
        compiler_params=_params(("arbitrary",)),
    )(x, gain, *[_phase_view(dy, d) for dy, d in zip(dys, dils)], dres)


def swiglu_fwd(gu, *, name):
    S, F2 = gu.shape
    Fh = F2 // 2
    ts = _tile(S, 512)

    def body(g_ref, u_ref, o_ref):
        g = g_ref[...].astype(F32)
        u = u_ref[...].astype(F32)
        o_ref[...] = (g * (1.0 / (1.0 + jnp.exp(-g))) * u).astype(o_ref.dtype)

    return pl.pallas_call(
        body, name=name, grid=(S // ts,),
        in_specs=[pl.BlockSpec((ts, Fh), lambda i: (i, 0)), pl.BlockSpec((ts, Fh), lambda i: (i, 1))],
        out_specs=pl.BlockSpec((ts, Fh), lambda i: (i, 0)),
        out_shape=jax.ShapeDtypeStruct((S, Fh), BF16),
        compiler_params=_params(("parallel",)),
    )(gu, gu)


def swiglu_bwd(gu, dact, *, name):
    S, F2 = gu.shape
    Fh = F2 // 2
    ts = _tile(S, 512)

    def body(g_ref, u_ref, d_ref, o_ref):
        g = g_ref[...].astype(F32)
        u = u_ref[...].astype(F32)
        d = d_ref[...].astype(F32)
        sig = 1.0 / (1.0 + jnp.exp(-g))
        j = pl.program_id(1)

        @pl.when(j == 0)
        def _():
            o_ref[...] = (d * u * (sig * (1.0 + g * (1.0 - sig)))).astype(o_ref.dtype)

        @pl.when(j == 1)
        def _():
            o_ref[...] = (d * (g * sig)).astype(o_ref.dtype)

    return pl.pallas_call(
        body, name=name, grid=(S // ts, 2),
        in_specs=[pl.BlockSpec((ts, Fh), lambda i, j: (i, 0)), pl.BlockSpec((ts, Fh), lambda i, j: (i, 1)),
                  pl.BlockSpec((ts, Fh), lambda i, j: (i, 0))],
        out_specs=pl.BlockSpec((ts, Fh), lambda i, j: (i, j)),
        out_shape=jax.ShapeDtypeStruct((S, F2), BF16),
        compiler_params=_params(("parallel", "arbitrary")),
    )(gu, gu, dact)


def loss_head(y, target, *, name):
    S, Dm = y.shape
    ts = _tile(S, 512)

    def body(y_ref, t_ref, l_ref, d_ref):
        @pl.when(pl.program_id(0) == 0)
        def _():
            l_ref[...] = jnp.zeros_like(l_ref)

        e = y_ref[...] - t_ref[...]
        d_ref[...] = e * (1.0 / Dm)
        l_ref[...] += jnp.sum(e * e)

    return pl.pallas_call(
        body, name=name, grid=(S // ts,),
        in_specs=[pl.BlockSpec((ts, Dm), lambda i: (i, 0)), pl.BlockSpec((ts, Dm), lambda i: (i, 0))],
        out_specs=[_const_spec((8, LANES)), pl.BlockSpec((ts, Dm), lambda i: (i, 0))],
        out_shape=[jax.ShapeDtypeStruct((8, LANES), F32), jax.ShapeDtypeStruct((S, Dm), F32)],
        compiler_params=_params(("arbitrary",)),
    )(y, target)


def _head_block_ones():
    i = np.arange(LANES)
    return jnp.asarray((i[:, None] // A_HEAD_DIM == i[None, :] // A_HEAD_DIM).astype(np.float32))


def _rope_tables(S):
    half = A_HEAD_DIM // 2
    inv_freq = ROPE_THETA ** (-jnp.arange(half, dtype=F32) / half)
    ang = jnp.arange(S).astype(F32)[:, None] * inv_freq[None, :]
    cos = jnp.tile(jnp.cos(ang), (1, LANES // half))
    sin = jnp.tile(jnp.sin(ang), (1, LANES // half))
    return cos, sin


def _rot_half(x, lo):
    return jnp.where(lo, -pltpu.roll(x, LANES - 32, 1), pltpu.roll(x, 32, 1))


def _seg_mean(v, ones_ref):
    return jnp.dot(v, ones_ref[...], precision=HI, preferred_element_type=F32) * (1.0 / A_HEAD_DIM)


def qk_prep_fwd(qkv, gain, cos, sin, ones, *, name):
    S, W = qkv.shape
    ts = _tile(S, 256)
    nchunk = A_WIDTH // LANES

    def body(x_ref, g_ref, c_ref, s_ref, ones_ref, o_ref):
        kind = pl.program_id(1) % 3

        @pl.when(kind < 2)
        def _():
            scale = jnp.where(kind == 0, A_HEAD_DIM ** -0.5, 1.0).astype(F32)
            lo = (lax.broadcasted_iota(jnp.int32, (ts, LANES), 1) % A_HEAD_DIM) < (A_HEAD_DIM // 2)
            cv, sv = c_ref[...], s_ref[...]
            for c in range(nchunk):
                sl = slice(c * LANES, (c + 1) * LANES)
                xv = x_ref[:, sl]
                r = lax.rsqrt(_seg_mean(xv * xv, ones_ref) + RMS_EPS)
                y = (xv * r) * g_ref[:, sl]
                y = y * cv + _rot_half(y, lo) * sv
                o_ref[:, sl] = (y * scale).astype(o_ref.dtype)

        @pl.when(kind == 2)
        def _():
            o_ref[...] = x_ref[...].astype(o_ref.dtype)

    return pl.pallas_call(
        body, name=name, grid=(S // ts, W // A_WIDTH),
        in_specs=[pl.BlockSpec((ts, A_WIDTH), lambda i, j: (i, j)), pl.BlockSpec((1, A_WIDTH), lambda i, j: (0, j)),
                  pl.BlockSpec((ts, LANES), lambda i, j: (i, 0)), pl.BlockSpec((ts, LANES), lambda i, j: (i, 0)),
                  _const_spec((LANES, LANES))],
        out_specs=pl.BlockSpec((ts, A_WIDTH), lambda i, j: (i, j)),
        out_shape=jax.ShapeDtypeStruct((S, W), BF16),
        compiler_params=_params(("parallel", "arbitrary")),
    )(qkv, gain, cos, sin, ones)


def qk_prep_bwd(qkv, gain, cos, sin, ones, grads, *, name):
    S, W = qkv.shape
    ts = _tile(S, 256)
    nchunk = A_WIDTH // LANES
    nj = W // A_WIDTH

    def body(x_ref, g_ref, c_ref, s_ref, ones_ref, *rest):
        g_refs, (o_ref, dg_ref) = rest[:nj], rest[nj:]
        j = pl.program_id(0)
        kind = j % 3

        @pl.when(pl.program_id(1) == 0)
        def _():
            dg_ref[...] = jnp.zeros_like(dg_ref)

        for n in range(nj):
            @pl.when(j == n)
            def _(n=n):
                d_ref = g_refs[n]
                if n % 3 == 2:
                    o_ref[...] = d_ref[...].astype(o_ref.dtype)
                    return
                scale = A_HEAD_DIM ** -0.5 if n % 3 == 0 else 1.0
                lo = (lax.broadcasted_iota(jnp.int32, (ts, LANES), 1) % A_HEAD_DIM) < (A_HEAD_DIM // 2)
                cv, sv = c_ref[...], s_ref[...]
                for c in range(nchunk):
                    sl = slice(c * LANES, (c + 1) * LANES)
                    dy = d_ref[:, sl].astype(F32) * scale
                    dn = dy * cv - _rot_half(dy, lo) * sv
                    xv = x_ref[:, sl]
                    r = lax.rsqrt(_seg_mean(xv * xv, ones_ref) + RMS_EPS)
                    xhat = xv * r
                    dyg = dn * g_ref[:, sl]
                    dx = r * (dyg - xhat * _seg_mean(dyg * xhat, ones_ref))
                    o_ref[:, sl] = dx.astype(o_ref.dtype)
                    dg_ref[0:1, sl] += jnp.sum(dn * xhat, axis=0, keepdims=True)

    def gspec(n):
        return pl.BlockSpec((ts, A_WIDTH), lambda j, i: (jnp.where(j == n, i, 0), 0))

    return pl.pallas_call(
        body, name=name, grid=(nj, S // ts),
        in_specs=[pl.BlockSpec((ts, A_WIDTH), lambda j, i: (i, j)), pl.BlockSpec((1, A_WIDTH), lambda j, i: (0, j)),
                  pl.BlockSpec((ts, LANES), lambda j, i: (i, 0)), pl.BlockSpec((ts, LANES), lambda j, i: (i, 0)),
                  _const_spec((LANES, LANES))] + [gspec(n) for n in range(nj)],
        out_specs=[pl.BlockSpec((ts, A_WIDTH), lambda j, i: (i, j)), pl.BlockSpec((8, A_WIDTH), lambda j, i: (0, j))],
        out_shape=[jax.ShapeDtypeStruct((S, W), BF16), jax.ShapeDtypeStruct((8, W), F32)],
        compiler_params=_params(("arbitrary", "arbitrary")),
    )(qkv, gain, cos, sin, ones, *grads)


def _band_specs(kind, tq, nlb):
    nhb = tq // A_HALF
    nb = nlb // nhb
    base = kind * (A_WIDTH // LANES)
    return [pl.BlockSpec((A_HALF, LANES), lambda ph, b, hp: (ph * nlb + jnp.maximum(b * nhb - 1, 0), base + hp)),
            pl.BlockSpec((tq, LANES), lambda ph, b, hp: (ph * nb + b, base + hp)),
            pl.BlockSpec((A_HALF, LANES), lambda ph, b, hp: (ph * nlb + jnp.minimum((b + 1) * nhb, nlb - 1), base + hp))]


def _cat3(a_ref, b_ref, c_ref):
    return jnp.concatenate([a_ref[...], b_ref[...], c_ref[...]], axis=0)


def _lane_lo(rows):
    return lax.broadcasted_iota(jnp.int32, (rows, LANES), 1) < A_HEAD_DIM


NT = (((1,), (1,)), ((), ()))
TN = (((0,), (0,)), ((), ()))


def attn_fwd(qkvp, dil, *, name):
    S = qkvp.shape[0]
    L = S // dil
    tq = _tile(L, 256)
    nlb = L // A_HALF
    tk = tq + 2 * A_HALF

    def body(q_ref, kp_ref, ko_ref, kn_ref, vp_ref, vo_ref, vn_ref, o_ref, l_ref):
        b = pl.program_id(1)
        q = q_ref[...]
        K = _cat3(kp_ref, ko_ref, kn_ref)
        V = _cat3(vp_ref, vo_ref, vn_ref)
        row = lax.broadcasted_iota(jnp.int32, (tq, tk), 0)
        col = lax.broadcasted_iota(jnp.int32, (tq, tk), 1)
        kpos = b * tq - A_HALF + col
        valid = (jnp.abs(col - row - A_HALF) <= A_HALF) & (kpos >= 0) & (kpos < L)
        lo_k = _lane_lo(tk)
        outs, lses = [], []
        for hh in range(2):
            sel = lo_k if hh == 0 else ~lo_k
            Km = jnp.where(sel, K, jnp.zeros_like(K))
            Vm = jnp.where(sel, V, jnp.zeros_like(V))
            s = lax.dot_general(q, Km, NT, preferred_element_type=F32)
            s = jnp.where(valid, s, NEG_INF)
            m = jnp.max(s, axis=1, keepdims=True)
            p = jnp.exp(s - m)
            l = jnp.sum(p, axis=1, keepdims=True)
            outs.append(jnp.dot(p.astype(BF16), Vm, preferred_element_type=F32) / l)
            lses.append(m + jnp.log(l))
        lo_q = _lane_lo(tq)
        o_ref[...] = outs[0] + outs[1]
        l_ref[...] = jnp.where(lo_q, lses[0], lses[1])

    ospec = _band_specs(0, tq, nlb)[1]
    return pl.pallas_call(
        body, name=name, grid=(dil, L // tq, A_WIDTH // LANES),
        in_specs=[_band_specs(0, tq, nlb)[1]] + _band_specs(1, tq, nlb) + _band_specs(2, tq, nlb),
        out_specs=[ospec, ospec],
        out_shape=[jax.ShapeDtypeStruct((S, A_WIDTH), F32)] * 2,
        compiler_params=_params(("parallel", "parallel", "parallel")),
    )(*([qkvp] * 7))


def attn_merge(os_, lses, dils, *, name):
    S = os_[0].shape[0]
    ts = _tile(S, 256)
    ng = len(dils)

    def body(*refs):
        o_refs, l_refs, out_ref = refs[:ng], refs[ng:2 * ng], refs[2 * ng]
        lse_refs, scrs = refs[2 * ng + 1:3 * ng + 1], refs[3 * ng + 1:]
        ov, ls, k = [], [], 0
        for d, o_ref, l_ref in zip(dils, o_refs, l_refs):
            if d == 1:
                ov.append(o_ref[...])
                ls.append(l_ref[...])
            else:
                _load_phases(o_ref, scrs[k], d, ts)
                _load_phases(l_ref, scrs[k + 1], d, ts)
                ov.append(_get_natural(scrs[k]))
                ls.append(_get_natural(scrs[k + 1]))
                k += 2
        m = functools.reduce(jnp.maximum, ls)
        es = [jnp.exp(l - m) for l in ls]
        tot = functools.reduce(jnp.add, es)
        acc = None
        for e, o in zip(es, ov):
            t = (e / tot) * o
            acc = t if acc is None else acc + t
        out_ref[...] = acc.astype(out_ref.dtype)
        total = m + jnp.log(tot)
        _put_natural(scrs[k], total)
        for d, lse_ref in zip(dils, lse_refs):
            if d == 1:
                lse_ref[...] = total
            else:
                _store_phases(scrs[k], lse_ref, d, ts)

    n_scr = 2 * sum(d > 1 for d in dils) + 1
    outs = pl.pallas_call(
        body, name=name, grid=(S // ts,),
        in_specs=[_phase_spec(d, ts, A_WIDTH) for d in dils] * 2,
        out_specs=[pl.BlockSpec((ts, A_WIDTH), lambda i: (i, 0))] + [_phase_spec(d, ts, A_WIDTH) for d in dils],
        out_shape=[jax.ShapeDtypeStruct((S, A_WIDTH), BF16)] + [_phase_shape(S, A_WIDTH, d, F32) for d in dils],
        scratch_shapes=[_nat_scratch(ts, A_WIDTH)] * n_scr,
        compiler_params=_params(("parallel",)),
    )(*[_phase_view(o, d) for o, d in zip(os_, dils)], *[_phase_view(l, d) for l, d in zip(lses, dils)])
    return outs[0], [l.reshape(S, A_WIDTH) for l in outs[1:]]


def attn_delta(dout, out, ones, dils, *, name):
    S = dout.shape[0]
    ts = _tile(S, 256)
    nd = len(dils)

    def body(d_ref, o_ref, ones_ref, *rest):
        do_refs, dl_refs, (scr_do, scr_dl) = rest[:nd], rest[nd:2 * nd], rest[2 * nd:]
        for c in range(A_WIDTH // LANES):
            sl = slice(c * LANES, (c + 1) * LANES)
            dv = d_ref[:, sl].astype(F32)
            scr_do[c] = dv
            scr_dl[c] = jnp.dot(dv * o_ref[:, sl].astype(F32), ones_ref[...], precision=HI, preferred_element_type=F32)
        for d, do_ref, dl_ref in zip(dils, do_refs, dl_refs):
            if d == 1:
                do_ref[...] = d_ref[...]
                dl_ref[...] = _get_natural(scr_dl)
            else:
                _store_phases(scr_do, do_ref, d, ts)
                _store_phases(scr_dl, dl_ref, d, ts)

    spec = pl.BlockSpec((ts, A_WIDTH), lambda i: (i, 0))
    outs = pl.pallas_call(
        body, name=name, grid=(S // ts,), in_specs=[spec, spec, _const_spec((LANES, LANES))],
        out_specs=[_phase_spec(d, ts, A_WIDTH) for d in dils] * 2,
        out_shape=[_phase_shape(S, A_WIDTH, d, BF16) for d in dils] + [_phase_shape(S, A_WIDTH, d, F32) for d in dils],
        scratch_shapes=[_nat_scratch(ts, A_WIDTH)] * 2,
        compiler_params=_params(("parallel",)),
    )(dout, out, ones)
    outs = [o.reshape(S, A_WIDTH) for o in outs]
    return outs[:nd], outs[nd:]


def _head_col(x, hh):
    c = hh * A_HEAD_DIM
    return x[:, c:c + 1]


def attn_bwd_dq(qkvp, dout, lse, delta, dil, *, name):
    S = qkvp.shape[0]
    L = S // dil
    tq = _tile(L, 256)
    nlb = L // A_HALF
    tk = tq + 2 * A_HALF

    def body(q_ref, kp_ref, ko_ref, kn_ref, vp_ref, vo_ref, vn_ref, do_ref, l_ref, d_ref, dq_ref):
        b = pl.program_id(1)
        q = q_ref[...]
        K = _cat3(kp_ref, ko_ref, kn_ref)
        V = _cat3(vp_ref, vo_ref, vn_ref)
        do = do_ref[...]
        lse_v, dl_v = l_ref[...], d_ref[...]
        row = lax.broadcasted_iota(jnp.int32, (tq, tk), 0)
        col = lax.broadcasted_iota(jnp.int32, (tq, tk), 1)
        kpos = b * tq - A_HALF + col
        valid = (jnp.abs(col - row - A_HALF) <= A_HALF) & (kpos >= 0) & (kpos < L)
        lo_k = _lane_lo(tk)
        acc = None
        for hh in range(2):
            sel = lo_k if hh == 0 else ~lo_k
            Km = jnp.where(sel, K, jnp.zeros_like(K))
            Vm = jnp.where(sel, V, jnp.zeros_like(V))
            s = lax.dot_general(q, Km, NT, preferred_element_type=F32)
            p = jnp.where(valid, jnp.exp(s - _head_col(lse_v, hh)), 0.0)
            dp = lax.dot_general(do, Vm, NT, preferred_element_type=F32)
            ds = p * (dp - _head_col(dl_v, hh))
            t = jnp.dot(ds.astype(BF16), Km, preferred_element_type=F32)
            acc = t if acc is None else acc + t
        dq_ref[...] = acc.astype(dq_ref.dtype)

    nspec = _band_specs(0, tq, nlb)[1]
    return pl.pallas_call(
        body, name=name, grid=(dil, L // tq, A_WIDTH // LANES),
        in_specs=[nspec] + _band_specs(1, tq, nlb) + _band_specs(2, tq, nlb) + [nspec, nspec, nspec],
        out_specs=nspec,
        out_shape=jax.ShapeDtypeStruct((S, A_WIDTH), BF16),
        compiler_params=_params(("parallel", "parallel", "parallel")),
    )(*([qkvp] * 7), dout, lse, delta)


def attn_bwd_dkv(qkvp, dout, lse, delta, dil, *, name):
    S = qkvp.shape[0]
    L = S // dil
    tk = _tile(L, 256)
    nlb = L // A_HALF
    tq = tk + 2 * A_HALF

    def body(qp_ref, qo_ref, qn_ref, k_ref, v_ref, dp_ref, do_ref, dn_ref, lp_ref, lo_ref, ln_ref,
             ep_ref, eo_ref, en_ref, dk_ref, dv_ref):
        b = pl.program_id(1)
        Q = _cat3(qp_ref, qo_ref, qn_ref)
        DO = _cat3(dp_ref, do_ref, dn_ref)
        lse_v = _cat3(lp_ref, lo_ref, ln_ref)
        dl_v = _cat3(ep_ref, eo_ref, en_ref)
        K, V = k_ref[...], v_ref[...]
        row = lax.broadcasted_iota(jnp.int32, (tq, tk), 0)
        col = lax.broadcasted_iota(jnp.int32, (tq, tk), 1)
        qpos = b * tk - A_HALF + row
        valid = (jnp.abs(col - row + A_HALF) <= A_HALF) & (qpos >= 0) & (qpos < L)
        lo_q = _lane_lo(tq)
        dk = dv = None
        for hh in range(2):
            sel = lo_q if hh == 0 else ~lo_q
            Qm = jnp.where(sel, Q, jnp.zeros_like(Q))
            DOm = jnp.where(sel, DO, jnp.zeros_like(DO))
            s = lax.dot_general(Qm, K, NT, preferred_element_type=F32)
            p = jnp.where(valid, jnp.exp(s - _head_col(lse_v, hh)), 0.0)
            dp = lax.dot_general(DOm, V, NT, preferred_element_type=F32)
            ds = p * (dp - _head_col(dl_v, hh))
            tv = lax.dot_general(p.astype(BF16), DOm, TN, preferred_element_type=F32)
            tk_ = lax.dot_general(ds.astype(BF16), Qm, TN, preferred_element_type=F32)
            dv = tv if dv is None else dv + tv
            dk = tk_ if dk is None else dk + tk_
        dk_ref[...] = dk.astype(dk_ref.dtype)
        dv_ref[...] = dv.astype(dv_ref.dtype)

    nspec = _band_specs(0, tk, nlb)[1]
    return pl.pallas_call(
        body, name=name, grid=(dil, L // tk, A_WIDTH // LANES),
        in_specs=_band_specs(0, tk, nlb) + [_band_specs(1, tk, nlb)[1], _band_specs(2, tk, nlb)[1]]
        + _band_specs(0, tk, nlb) * 3,
        out_specs=[nspec, nspec],
        out_shape=[jax.ShapeDtypeStruct((S, A_WIDTH), BF16)] * 2,
        compiler_params=_params(("parallel", "parallel", "parallel")),
    )(*([qkvp] * 5), *([dout] * 3), *([lse] * 3), *([delta] * 3))


def _gate_block_weight(wf, wb):
    w = jnp.zeros((LANES, 2 * B_QK_WIDTH), F32)
    w = w.at[:B_GATE_RANK, :B_QK_WIDTH].set(wf)
    w = w.at[B_GATE_RANK:2 * B_GATE_RANK, B_QK_WIDTH:].set(wb)
    return w.astype(BF16)


def gate_fwd(proj, wblk, bias, *, name):
    S = proj.shape[0]
    ts = _tile(S, 512)
    W = 2 * B_QK_WIDTH
    zcol = (2 * B_QK_WIDTH + 2 * B_V_WIDTH) // LANES

    def body(z_ref, w_ref, b_ref, o_ref):
        x = jnp.dot(z_ref[...].astype(BF16), w_ref[...], preferred_element_type=F32) + b_ref[...]
        o_ref[...] = (jnp.minimum(x, 0.0) - jnp.log(1.0 + jnp.exp(-jnp.abs(x)))) * (1.0 / B_GATE_TAU)

    return pl.pallas_call(
        body, name=name, grid=(S // ts,),
        in_specs=[pl.BlockSpec((ts, LANES), lambda i: (i, zcol)), _const_spec((LANES, W)), _const_spec((1, W))],
        out_specs=pl.BlockSpec((ts, W), lambda i: (i, 0)),
        out_shape=jax.ShapeDtypeStruct((S, W), F32),
        compiler_params=_params(("parallel",)),
    )(proj, wblk, bias)


def gate_bwd(proj, wblk, wblk_t, bias, dloga, *, name):
    S = proj.shape[0]
    ts = _tile(S, 512)
    W = 2 * B_QK_WIDTH
    zcol = (2 * B_QK_WIDTH + 2 * B_V_WIDTH) // LANES

    def body(z_ref, w_ref, wt_ref, b_ref, d_ref, dz_ref, dw_ref, db_ref):
        @pl.when(pl.program_id(0) == 0)
        def _():
            dw_ref[...] = jnp.zeros_like(dw_ref)
            db_ref[...] = jnp.zeros_like(db_ref)

        z = z_ref[...].astype(BF16)
        x = jnp.dot(z, w_ref[...], preferred_element_type=F32) + b_ref[...]
        e = jnp.exp(-jnp.abs(x))
        sig_neg = jnp.where(x >= 0, e, 1.0) / (1.0 + e)
        dx = d_ref[...] * (1.0 / B_GATE_TAU) * sig_neg
        dxb = dx.astype(BF16)
        dz_ref[...] = jnp.dot(dxb, wt_ref[...], preferred_element_type=F32)
        dw_ref[...] += lax.dot_general(z, dxb, TN, preferred_element_type=F32)
        db_ref[0:1, :] += jnp.sum(dx, axis=0, keepdims=True)

    return pl.pallas_call(
        body, name=name, grid=(S // ts,),
        in_specs=[pl.BlockSpec((ts, LANES), lambda i: (i, zcol)), _const_spec((LANES, W)), _const_spec((W, LANES)),
                  _const_spec((1, W)), pl.BlockSpec((ts, W), lambda i: (i, 0))],
        out_specs=[pl.BlockSpec((ts, LANES), lambda i: (i, 0)), _const_spec((LANES, W)), _const_spec((8, W))],
        out_shape=[jax.ShapeDtypeStruct((S, LANES), F32), jax.ShapeDtypeStruct((LANES, W), F32),
                   jax.ShapeDtypeStruct((8, W), F32)],
        compiler_params=_params(("arbitrary",)),
    )(proj, wblk, wblk_t, bias, dloga)


def _tri(reverse):
    i = np.arange(B_CHUNK)
    t = (i[None, :] >= i[:, None]) if reverse else (i[None, :] <= i[:, None])
    return jnp.asarray(t.astype(np.float32))


def _gla_terms(q, k, la, t_ref, reverse):
    b = jnp.dot(t_ref[...], la, precision=HI, preferred_element_type=F32)
    b_last = b[0:1, :] if reverse else b[B_CHUNK - 1:B_CHUNK, :]
    e_b = jnp.exp(b)
    qt = (q * (B_KEY_DIM ** -0.5)) * e_b
    e_nb = jnp.exp(-b)
    kt = k * e_nb
    e_end = jnp.exp(b_last - b)
    kend = k * e_end
    dec = jnp.exp(b_last)
    return e_nb, e_b, qt, kt, e_end, kend, dec


def _chunk_mask(reverse, transpose=False):
    r = lax.broadcasted_iota(jnp.int32, (B_CHUNK, B_CHUNK), 0)
    c = lax.broadcasted_iota(jnp.int32, (B_CHUNK, B_CHUNK), 1)
    if transpose:
        r, c = c, r
    return (c > r) if reverse else (c <= r)


def gla_fwd(proj, loga, tmat, reverse, *, name):
    S = proj.shape[0]
    tb = _tile(S, 512)
    nb = S // tb
    cpb = tb // B_CHUNK
    nc = S // B_CHUNK
    qb, kb_, vb = 0, B_QK_WIDTH // B_KEY_DIM, 2 * B_QK_WIDTH // B_VAL_DIM
    lb = (B_QK_WIDTH // B_KEY_DIM) if reverse else 0
    blk = (lambda i: nb - 1 - i) if reverse else (lambda i: i)

    def body(q_ref, k_ref, v_ref, la_ref, t_ref, o_ref, st_ref, s_scr):
        @pl.when(pl.program_id(1) == 0)
        def _():
            s_scr[...] = jnp.zeros_like(s_scr)

        mask = _chunk_mask(reverse)
        order = range(cpb - 1, -1, -1) if reverse else range(cpb)
        for c in order:
            rows = slice(c * B_CHUNK, (c + 1) * B_CHUNK)
            v = v_ref[rows, :].astype(BF16)
            _, _, qt, kt, _, kend, dec = _gla_terms(q_ref[rows, :], k_ref[rows, :], la_ref[rows, :], t_ref, reverse)
            qt, kt, kend = qt.astype(BF16), kt.astype(BF16), kend.astype(BF16)
            st = s_scr[...]
            st_ref[0, c] = st
            a = jnp.where(mask, lax.dot_general(qt, kt, NT, preferred_element_type=F32), 0.0)
            o = jnp.dot(a.astype(BF16), v, preferred_element_type=F32)
            o = o + lax.dot_general(qt, st.astype(BF16), NT, preferred_element_type=F32)
            o_ref[rows, :] = o
            s_scr[...] = st * dec + lax.dot_general(v, kend, TN, preferred_element_type=F32)

    return pl.pallas_call(
        body, name=name, grid=(B_HEADS, nb),
        in_specs=[pl.BlockSpec((tb, B_KEY_DIM), lambda h, i: (blk(i), qb + h)),
                  pl.BlockSpec((tb, B_KEY_DIM), lambda h, i: (blk(i), kb_ + h)),
                  pl.BlockSpec((tb, B_VAL_DIM), lambda h, i: (blk(i), vb + h)),
                  pl.BlockSpec((tb, B_KEY_DIM), lambda h, i: (blk(i), lb + h)),
                  _const_spec((B_CHUNK, B_CHUNK))],
        out_specs=[pl.BlockSpec((tb, B_VAL_DIM), lambda h, i: (blk(i), h)),
                   pl.BlockSpec((1, cpb, B_VAL_DIM, B_KEY_DIM), lambda h, i: (h, blk(i), 0, 0))],
        out_shape=[jax.ShapeDtypeStruct((S, B_V_WIDTH), F32),
                   jax.ShapeDtypeStruct((B_HEADS, nc, B_VAL_DIM, B_KEY_DIM), F32)],
        scratch_shapes=[pltpu.VMEM((B_VAL_DIM, B_KEY_DIM), F32)],
        compiler_params=_params(("parallel", "arbitrary")),
    )(proj, proj, proj, loga, tmat)


def gla_bwd(proj, loga, states, do, tmat, tmat_t, reverse, *, name):
    S = proj.shape[0]
    tb = _tile(S, 512)
    nb = S // tb
    cpb = tb // B_CHUNK
    qb, kb_, vb = 0, B_QK_WIDTH // B_KEY_DIM, 2 * B_QK_WIDTH // B_VAL_DIM
    lb = (B_QK_WIDTH // B_KEY_DIM) if reverse else 0
    blk = (lambda i: i) if reverse else (lambda i: nb - 1 - i)
    scale = B_KEY_DIM ** -0.5

    def body(q_ref, k_ref, v_ref, la_ref, st_ref, do_ref, t_ref, tt_ref, dq_ref, dk_ref, dv_ref, dla_ref, ds_scr):
        @pl.when(pl.program_id(1) == 0)
        def _():
            ds_scr[...] = jnp.zeros_like(ds_scr)

        mask = _chunk_mask(reverse)
        mask_t = _chunk_mask(reverse, transpose=True)
        last = 0 if reverse else B_CHUNK - 1
        is_last = lax.broadcasted_iota(jnp.int32, (B_CHUNK, B_KEY_DIM), 0) == last
        order = range(cpb) if reverse else range(cpb - 1, -1, -1)
        for c in order:
            rows = slice(c * B_CHUNK, (c + 1) * B_CHUNK)
            vf = v_ref[rows, :]
            v = vf.astype(BF16)
            dov = do_ref[rows, :]
            dob = dov.astype(BF16)
            e_nb, e_b, qt, kt, e_end, kend, dec = _gla_terms(q_ref[rows, :], k_ref[rows, :], la_ref[rows, :], t_ref, reverse)
            qtb, ktb, kendb = qt.astype(BF16), kt.astype(BF16), kend.astype(BF16)
            st = st_ref[0, c]
            dst = ds_scr[...]
            dstb = dst.astype(BF16)
            a_t = jnp.where(mask_t, lax.dot_general(ktb, qtb, NT, preferred_element_type=F32), 0.0)
            da = jnp.where(mask, lax.dot_general(dob, v, NT, preferred_element_type=F32), 0.0)
            da_t = jnp.where(mask_t, lax.dot_general(v, dob, NT, preferred_element_type=F32), 0.0)
            dv = jnp.dot(a_t.astype(BF16), dob, preferred_element_type=F32)
            dv = dv + lax.dot_general(kendb, dstb, NT, preferred_element_type=F32)
            dqt = jnp.dot(da.astype(BF16), ktb, preferred_element_type=F32)
            dqt = dqt + jnp.dot(dob, st.astype(BF16), preferred_element_type=F32)
            dkt = jnp.dot(da_t.astype(BF16), qtb, preferred_element_type=F32)
            dkend = jnp.dot(v, dstb, preferred_element_type=F32)
            ddec = jnp.sum(dst * st, axis=0, keepdims=True)
            ds_scr[...] = dst * dec + lax.dot_general(dob, qtb, TN, preferred_element_type=F32)
            ke = dkend * kend
            db = dqt * qt - dkt * kt - ke
            db_last = jnp.sum(ke, axis=0, keepdims=True) + ddec * dec
            db = db + jnp.where(is_last, db_last, 0.0)
            dq_ref[rows, :] = dqt * e_b * scale
            dk_ref[rows, :] = dkt * e_nb + dkend * e_end
            dv_ref[rows, :] = dv
            dla_ref[rows, :] = jnp.dot(tt_ref[...], db, precision=HI, preferred_element_type=F32)

    kspec = lambda cb: pl.BlockSpec((tb, B_KEY_DIM), lambda h, i: (blk(i), cb + h))
    return pl.pallas_call(
        body, name=name, grid=(B_HEADS, nb),
        in_specs=[kspec(qb), kspec(kb_), pl.BlockSpec((tb, B_VAL_DIM), lambda h, i: (blk(i), vb + h)), kspec(lb),
                  pl.BlockSpec((1, cpb, B_VAL_DIM, B_KEY_DIM), lambda h, i: (h, blk(i), 0, 0)),
                  pl.BlockSpec((tb, B_VAL_DIM), lambda h, i: (blk(i), h)),
                  _const_spec((B_CHUNK, B_CHUNK)), _const_spec((B_CHUNK, B_CHUNK))],
        out_specs=[kspec(0), kspec(0), pl.BlockSpec((tb, B_VAL_DIM), lambda h, i: (blk(i), h)), kspec(0)],
        out_shape=[jax.ShapeDtypeStruct((S, B_QK_WIDTH), F32), jax.ShapeDtypeStruct((S, B_QK_WIDTH), F32),
                   jax.ShapeDtypeStruct((S, B_V_WIDTH), F32), jax.ShapeDtypeStruct((S, B_QK_WIDTH), F32)],
        scratch_shapes=[pltpu.VMEM((B_VAL_DIM, B_KEY_DIM), F32)],
        compiler_params=_params(("parallel", "arbitrary")),
    )(proj, proj, proj, loga, states, do, tmat, tmat_t)


def gla_post_fwd(o_f, o_b, gain, proj, *, name):
    S = o_f.shape[0]
    ts = _tile(S, 512)
    rcol = (2 * B_QK_WIDTH + B_V_WIDTH) // B_V_WIDTH

    def body(f_ref, b_ref, g_ref, r_ref, y_ref):
        for h in range(B_HEADS):
            sl = slice(h * B_VAL_DIM, (h + 1) * B_VAL_DIM)
            o = f_ref[:, sl] + b_ref[:, sl]
            n = (o * lax.rsqrt(jnp.mean(o * o, axis=-1, keepdims=True) + RMS_EPS)) * g_ref[:, sl]
            r = r_ref[:, sl]
            y_ref[:, sl] = (n * (r * (1.0 / (1.0 + jnp.exp(-r))))).astype(y_ref.dtype)

    spec = pl.BlockSpec((ts, B_V_WIDTH), lambda i: (i, 0))
    return pl.pallas_call(
        body, name=name, grid=(S // ts,),
        in_specs=[spec, spec, _const_spec((1, B_V_WIDTH)), pl.BlockSpec((ts, B_V_WIDTH), lambda i: (i, rcol))],
        out_specs=spec, out_shape=jax.ShapeDtypeStruct((S, B_V_WIDTH), BF16),
        compiler_params=_params(("parallel",)),
    )(o_f, o_b, gain, proj)


def gla_post_bwd(o_f, o_b, gain, proj, dy, *, name):
    S = o_f.shape[0]
    ts = _tile(S, 512)
    rcol = (2 * B_QK_WIDTH + B_V_WIDTH) // B_V_WIDTH

    def body(f_ref, b_ref, g_ref, r_ref, dy_ref, do_ref, dr_ref, dg_ref):
        @pl.when(pl.program_id(0) == 0)
        def _():
            dg_ref[...] = jnp.zeros_like(dg_ref)

        for h in range(B_HEADS):
            sl = slice(h * B_VAL_DIM, (h + 1) * B_VAL_DIM)
            o = f_ref[:, sl] + b_ref[:, sl]
            rs = lax.rsqrt(jnp.mean(o * o, axis=-1, keepdims=True) + RMS_EPS)
            ohat = o * rs
            g = g_ref[:, sl]
            r = r_ref[:, sl]
            sig = 1.0 / (1.0 + jnp.exp(-r))
            dyv = dy_ref[:, sl].astype(F32)
            dn = dyv * (r * sig)
            dr_ref[:, sl] = dyv * (ohat * g) * (sig * (1.0 + r * (1.0 - sig)))
            dng = dn * g
            do_ref[:, sl] = rs * (dng - ohat * jnp.mean(dng * ohat, axis=-1, keepdims=True))
            dg_ref[0:1, sl] += jnp.sum(dn * ohat, axis=0, keepdims=True)

    spec = pl.BlockSpec((ts, B_V_WIDTH), lambda i: (i, 0))
    return pl.pallas_call(
        body, name=name, grid=(S // ts,),
        in_specs=[spec, spec, _const_spec((1, B_V_WIDTH)), pl.BlockSpec((ts, B_V_WIDTH), lambda i: (i, rcol)), spec],
        out_specs=[spec, spec, _const_spec((8, B_V_WIDTH))],
        out_shape=[jax.ShapeDtypeStruct((S, B_V_WIDTH), F32), jax.ShapeDtypeStruct((S, B_V_WIDTH), F32),
                   jax.ShapeDtypeStruct((8, B_V_WIDTH), F32)],
        compiler_params=_params(("arbitrary",)),
    )(o_f, o_b, gain, proj, dy)


def gla_combine(parts_f, parts_b, dr, dz, *, name):
    S = dr.shape[0]
    ts = _tile(S, 512)

    def body(qf, kf, vf, qb, kb, vb, r_ref, z_ref, o_ref):
        o_ref[:, 0:512] = (qf[...] + qb[...]).astype(o_ref.dtype)
        o_ref[:, 512:1024] = (kf[...] + kb[...]).astype(o_ref.dtype)
        o_ref[:, 1024:2048] = (vf[...] + vb[...]).astype(o_ref.dtype)
        o_ref[:, 2048:3072] = r_ref[...].astype(o_ref.dtype)
        o_ref[:, 3072:3200] = z_ref[...].astype(o_ref.dtype)

    s512 = pl.BlockSpec((ts, B_QK_WIDTH), lambda i: (i, 0))
    s1024 = pl.BlockSpec((ts, B_V_WIDTH), lambda i: (i, 0))
    return pl.pallas_call(
        body, name=name, grid=(S // ts,),
        in_specs=[s512, s512, s1024, s512, s512, s1024, s1024, pl.BlockSpec((ts, LANES), lambda i: (i, 0))],
        out_specs=pl.BlockSpec((ts, B_IN_PAD), lambda i: (i, 0)),
        out_shape=jax.ShapeDtypeStruct((S, B_IN_PAD), BF16),
        compiler_params=_params(("parallel",)),
    )(*parts_f, *parts_b, dr, dz)


def adamw(w, g, m, v, *, name):
    R, C = w.shape
    tr = _tile(R, 256)
    c1 = 1.0 / (1.0 - ADAM_B1 ** ADAM_STEP)
    c2 = 1.0 / (1.0 - ADAM_B2 ** ADAM_STEP)

    def body(w_ref, g_ref, m_ref, v_ref, d_ref, mo_ref, vo_ref):
        gv = g_ref[...]
        mn = ADAM_B1 * m_ref[...] + (1.0 - ADAM_B1) * gv
        vn = ADAM_B2 * v_ref[...] + (1.0 - ADAM_B2) * (gv * gv)
        mo_ref[...] = mn
        vo_ref[...] = vn
        d_ref[...] = -ADAM_LR * ((mn * c1) / (jnp.sqrt(vn * c2) + ADAM_EPS) + ADAM_WD * w_ref[...])

    spec = pl.BlockSpec((tr, C), lambda i: (i, 0))
    return pl.pallas_call(
        body, name=name, grid=(R // tr,), in_specs=[spec] * 4, out_specs=[spec] * 3,
        out_shape=[jax.ShapeDtypeStruct((R, C), F32)] * 3,
        compiler_params=_params(("parallel",)),
    )(w, g, m, v)


def _chip_peers():
    x, y, c = lax.axis_index("x"), lax.axis_index("y"), lax.axis_index("c")
    return x, y, c, [(1 - x, y), (x, 1 - y), (1 - x, 1 - y)]


def chip_exchange(src, per_dest, *, name):
    _, R, C = src.shape

    def body(src_ref, out_ref, send_sems, recv_sems, local_sem):
        x, y, c, chips = _chip_peers()
        me = 2 * x + y

        def block(t):
            return src_ref.at[t if per_dest else 0]

        mine = pltpu.make_async_copy(block(me), out_ref.at[me], local_sem)
        mine.start()
        copies = []
        for k, (px, py) in enumerate(chips):
            cp = pltpu.make_async_remote_copy(
                src_ref=block(2 * px + py), dst_ref=out_ref.at[me], send_sem=send_sems.at[k], recv_sem=recv_sems.at[k],
                device_id=(px, py, c), device_id_type=MESH)
            cp.start()
            copies.append(cp)
        for k, (px, py) in enumerate(chips):
            pltpu.make_async_remote_copy(
                src_ref=block(me), dst_ref=out_ref.at[2 * px + py], send_sem=send_sems.at[k], recv_sem=recv_sems.at[k],
                device_id=(px, py, c), device_id_type=MESH).wait_recv()
        for cp in copies:
            cp.wait_send()
        mine.wait()

    return pl.pallas_call(
        body, name=name,
        in_specs=[pl.BlockSpec(memory_space=pl.ANY)], out_specs=pl.BlockSpec(memory_space=pl.ANY),
        out_shape=jax.ShapeDtypeStruct((N_CHIPS, R, C), src.dtype),
        scratch_shapes=[pltpu.SemaphoreType.DMA((3,)), pltpu.SemaphoreType.DMA((3,)), pltpu.SemaphoreType.DMA],
        compiler_params=pltpu.CompilerParams(has_side_effects=True),
    )(src)


def sibling_exchange(src, *, name):
    n, _, R, C = src.shape

    def body(src_ref, out_ref, send_sem, recv_sem):
        x, y, c = lax.axis_index("x"), lax.axis_index("y"), lax.axis_index("c")
        cp = pltpu.make_async_remote_copy(
            src_ref=src_ref.at[:, 1 - c], dst_ref=out_ref, send_sem=send_sem, recv_sem=recv_sem,
            device_id=(x, y, 1 - c), device_id_type=MESH)
        cp.start()
        cp.wait()

    return pl.pallas_call(
        body, name=name,
        in_specs=[pl.BlockSpec(memory_space=pl.ANY)], out_specs=pl.BlockSpec(memory_space=pl.ANY),
        out_shape=jax.ShapeDtypeStruct((n, R, C), src.dtype),
        scratch_shapes=[pltpu.SemaphoreType.DMA, pltpu.SemaphoreType.DMA],
        compiler_params=pltpu.CompilerParams(has_side_effects=True),
    )(src)


def sibling_share(src, *, name):
    def body(src_ref, out_ref, send_sem, recv_sem, local_sem):
        x, y, c = lax.axis_index("x"), lax.axis_index("y"), lax.axis_index("c")
        mine = pltpu.make_async_copy(src_ref, out_ref.at[c], local_sem)
        mine.start()
        cp = pltpu.make_async_remote_copy(
            src_ref=src_ref, dst_ref=out_ref.at[c], send_sem=send_sem, recv_sem=recv_sem,
            device_id=(x, y, 1 - c), device_id_type=MESH)
        cp.start()
        cp.wait()
        mine.wait()

    return pl.pallas_call(
        body, name=name,
        in_specs=[pl.BlockSpec(memory_space=pl.ANY)], out_specs=pl.BlockSpec(memory_space=pl.ANY),
        out_shape=jax.ShapeDtypeStruct((2,) + src.shape, src.dtype),
        scratch_shapes=[pltpu.SemaphoreType.DMA, pltpu.SemaphoreType.DMA, pltpu.SemaphoreType.DMA],
        compiler_params=pltpu.CompilerParams(has_side_effects=True),
    )(src)


def add_pair(a, b, *, name):
    n, _, R, C = a.shape
    tr = _tile(R, 256)

    def body(c_ref, a_ref, b_ref, o_ref):
        o_ref[...] = a_ref[0] + b_ref[...]

    return pl.pallas_call(
        body, name=name,
        grid_spec=pltpu.PrefetchScalarGridSpec(
            num_scalar_prefetch=1, grid=(n, R // tr),
            in_specs=[pl.BlockSpec((1, 1, tr, C), lambda s, i, c_ref: (s, c_ref[0], i, 0)),
                      pl.BlockSpec((1, tr, C), lambda s, i, c_ref: (s, i, 0))],
            out_specs=pl.BlockSpec((1, tr, C), lambda s, i, c_ref: (s, i, 0))),
        out_shape=jax.ShapeDtypeStruct((n, R, C), a.dtype),
        compiler_params=_params(("parallel", "parallel")),
    )(lax.axis_index("c").reshape(1).astype(jnp.int32), a, b)


def sum_slots(a, *, name):
    n, R, C = a.shape
    tr = _tile(R, 256)

    def body(*refs):
        o_ref = refs[-1]
        acc = refs[0][0]
        for r in refs[1:-1]:
            acc = acc + r[0]
        o_ref[...] = acc

    return pl.pallas_call(
        body, name=name, grid=(R // tr,),
        in_specs=[pl.BlockSpec((1, tr, C), functools.partial(lambda s, i: (s, i, 0), s)) for s in range(n)],
        out_specs=pl.BlockSpec((tr, C), lambda i: (i, 0)),
        out_shape=jax.ShapeDtypeStruct((R, C), a.dtype),
        compiler_params=_params(("parallel",)),
    )(*([a] * n))


def _flat_rows(n_elems, mult):
    rows = -(-n_elems // FLAT_COLS)
    return -(-rows // mult) * mult


def _to_flat(parts, mult):
    v = jnp.concatenate([p.reshape(-1) for p in parts])
    rows = _flat_rows(v.shape[0], mult)
    return jnp.pad(v, (0, rows * FLAT_COLS - v.shape[0])).reshape(rows, FLAT_COLS)


def _from_flat(flat, shapes):
    v = flat.reshape(-1)
    out, off = [], 0
    for s in shapes:
        n = int(np.prod(s))
        out.append(v[off:off + n].reshape(s))
        off += n
    return out


def _unshard(blocks, axis):
    return jnp.concatenate([blocks[s] for s in range(N_CHIPS)], axis=axis)


def _local_shapes(w):
    return {n: (w[n].shape[0], tuple(w[n].shape[1:])) for n in WEIGHTS}


def _gradient_blocks(grads, shapes):
    n_el = sum(nl * int(np.prod(shp)) for nl, shp in shapes.values())
    rows = _flat_rows(n_el, 32)
    pad = jnp.zeros((rows * FLAT_COLS - n_el,), F32)
    pieces = []
    for s in range(N_CHIPS):
        for n in WEIGHTS:
            for gl in grads[n]:
                if n not in REPLICATED:
                    ax = SHARD_AXIS[n] - 1
                    wd = shapes[n][1][ax]
                    gl = lax.slice_in_dim(gl, s * wd, (s + 1) * wd, axis=ax)
                pieces.append(gl.reshape(-1))
        pieces.append(pad)
    return jnp.concatenate(pieces).reshape(N_CHIPS, rows, FLAT_COLS)


def _gather_weights(w):
    full = {}
    for names, dtype, mult, call in ((BIG, BF16, 16, "gather_weights"), (SMALL_SHARDED, F32, 8, "gather_vectors")):
        parts = [w[n].astype(dtype) for n in names]
        got = chip_exchange(_to_flat(parts, mult)[None], False, name=call)
        per_chip = [_from_flat(got[s], [p.shape for p in parts]) for s in range(N_CHIPS)]
        for i, n in enumerate(names):
            full[n] = _unshard([per_chip[s][i] for s in range(N_CHIPS)], SHARD_AXIS[n])
    return full


def _reduce_gradients(grads, shapes):
    g = _gradient_blocks(grads, shapes)
    R = g.shape[1]
    halves = g.reshape(N_CHIPS, 2, R // 2, FLAT_COLS)
    other = sibling_exchange(halves, name="grad_pair_exchange")
    pair = add_pair(halves, other, name="grad_pair_add")
    slots = chip_exchange(pair, True, name="grad_chip_exchange")
    mine = sum_slots(slots, name="grad_chip_sum")
    both = sibling_share(mine, name="grad_half_share")
    return both.reshape(R, FLAT_COLS)


def _layer_fwd(i, h, p, aux):
    j = i // 2
    sv = {"h0": h}
    if i % 2 == 0:
        hns = rmsnorm_fwd(h, p["attn_norm"][i][None], dils=DILS, name=f"l{i}_norm1")
        qkvs, qkvps, os_, lses = [], [], [], []
        for g, d in enumerate(DILS):
            qkv = matmul(hns[g], p["a_w_in_g"][j][g], name=f"l{i}_a_in{g}")
            qkvp = qk_prep_fwd(qkv, aux["a_gain"][j][g], aux["cos"][g], aux["sin"][g], aux["ones"], name=f"l{i}_a_prep{g}")
            o, l = attn_fwd(qkvp, d, name=f"l{i}_a_attn{g}")
            qkvs.append(qkv)
            qkvps.append(qkvp)
            os_.append(o)
            lses.append(l)
        out, lse = attn_merge(os_, lses, DILS, name=f"l{i}_a_merge")
        sv.update(hns=hns, qkv=qkvs, qkvp=qkvps, out=out, lse=lse)
        h1 = matmul(out, p["a_w_out"][j], res=h, name=f"l{i}_a_out")
    else:
        hn = rmsnorm_fwd(h, p["attn_norm"][i][None], name=f"l{i}_norm1")[0]
        sv["hn"] = hn
        proj = matmul(hn, p["b_w_in"][j], tn=640, name=f"l{i}_b_in")
        loga = gate_fwd(proj, aux["b_wblk"][j], aux["b_bias"][j], name=f"l{i}_b_gate")
        o_f, st_f = gla_fwd(proj, loga, aux["tri_f"], False, name=f"l{i}_b_gla_f")
        o_b, st_b = gla_fwd(proj, loga, aux["tri_b"], True, name=f"l{i}_b_gla_b")
        y = gla_post_fwd(o_f, o_b, aux["b_gain"][j], proj, name=f"l{i}_b_post")
        sv.update(proj=proj, loga=loga, o_f=o_f, o_b=o_b, st_f=st_f, st_b=st_b, y=y)
        h1 = matmul(y, p["b_w_out"][j], res=h, name=f"l{i}_b_out")
    sv["h1"] = h1
    hn2 = rmsnorm_fwd(h1, p["ffn_norm"][i][None], name=f"l{i}_norm2")[0]
    gu = matmul(hn2, p["ffn_w_gate_up"][i], out_dtype=BF16, name=f"l{i}_f_up")
    act = swiglu_fwd(gu, name=f"l{i}_f_act")
    h2 = matmul(act, p["ffn_w_down"][i], res=h1, tk=2816, name=f"l{i}_f_down")
    sv.update(hn2=hn2, gu=gu, act=act)
    return h2, sv


def _layer_bwd(i, dh, p, pt, aux, sv, grads):
    j = i // 2
    dhb = dh.astype(BF16)
    grads["ffn_w_down"][i] = matmul(sv["act"].T, dhb, tk=2048, name=f"l{i}_f_down_dw")
    dact = matmul(dhb, pt["ffn_w_down"][i], out_dtype=BF16, name=f"l{i}_f_down_dx")
    dgu = swiglu_bwd(sv["gu"], dact, name=f"l{i}_f_act_bwd")
    grads["ffn_w_gate_up"][i] = matmul(sv["hn2"].T, dgu, tk=2048, name=f"l{i}_f_up_dw")
    dhn2 = matmul(dgu, pt["ffn_w_gate_up"][i], tk=2816, name=f"l{i}_f_up_dx")
    dh1, dg = rmsnorm_bwd(sv["h1"], p["ffn_norm"][i][None], [dhn2], dh, name=f"l{i}_norm2_bwd")
    grads["ffn_norm"][i] = dg[0]
    dh1b = dh1.astype(BF16)
    if i % 2 == 0:
        grads["a_w_out"][j] = matmul(sv["out"].T, dh1b, tk=2048, name=f"l{i}_a_out_dw")
        dout = matmul(dh1b, pt["a_w_out"][j], out_dtype=BF16, name=f"l{i}_a_out_dx")
        douts, deltas = attn_delta(dout, sv["out"], aux["ones"], DILS, name=f"l{i}_a_delta")
        dws, dhns, dgq, dgk = [], [], [], []
        for g, d in enumerate(DILS):
            qkvp, lse = sv["qkvp"][g], sv["lse"][g]
            dq = attn_bwd_dq(qkvp, douts[g], lse, deltas[g], d, name=f"l{i}_a_dq{g}")
            dk, dv = attn_bwd_dkv(qkvp, douts[g], lse, deltas[g], d, name=f"l{i}_a_dkv{g}")
            dqkv, dgain = qk_prep_bwd(sv["qkv"][g], aux["a_gain"][j][g], aux["cos"][g], aux["sin"][g], aux["ones"],
                                      [dq, dk, dv], name=f"l{i}_a_prep_bwd{g}")
            dgh = dgain[0].reshape(3, A_HEADS, A_HEAD_DIM).sum(axis=1)
            dgq.append(dgh[0])
            dgk.append(dgh[1])
            dws.append(matmul(sv["hns"][g].T, dqkv, tk=2048, name=f"l{i}_a_in_dw{g}"))
            dhns.append(matmul(dqkv, pt["a_w_in_g"][j][g], tk=3072, name=f"l{i}_a_in_dx{g}"))
        grads["a_q_norm"][j] = jnp.stack(dgq)
        grads["a_k_norm"][j] = jnp.stack(dgk)
        grads["a_w_in"][j] = jnp.concatenate(dws, axis=1)
        dh0, dg = rmsnorm_bwd(sv["h0"], p["attn_norm"][i][None], dhns, dh1, dils=DILS, name=f"l{i}_norm1_bwd")
    else:
        grads["b_w_out"][j] = matmul(sv["y"].T, dh1b, tk=2048, name=f"l{i}_b_out_dw")
        dy = matmul(dh1b, pt["b_w_out"][j], name=f"l{i}_b_out_dx")
        do, dr, dgn = gla_post_bwd(sv["o_f"], sv["o_b"], aux["b_gain"][j], sv["proj"], dy, name=f"l{i}_b_post_bwd")
        grads["b_out_norm"][j] = dgn[0].reshape(B_HEADS, B_VAL_DIM)
        pf = gla_bwd(sv["proj"], sv["loga"], sv["st_f"], do, aux["tri_f"], aux["tri_b"], False, name=f"l{i}_b_gla_f_bwd")
        pb = gla_bwd(sv["proj"], sv["loga"], sv["st_b"], do, aux["tri_b"], aux["tri_f"], True, name=f"l{i}_b_gla_b_bwd")
        dloga = jnp.concatenate([pf[3], pb[3]], axis=1)
        dz, dwblk, dbias = gate_bwd(sv["proj"], aux["b_wblk"][j], aux["b_wblk_t"][j], aux["b_bias"][j], dloga,
                                    name=f"l{i}_b_gate_bwd")
        grads["b_w_gate_f"][j] = dwblk[:B_GATE_RANK, :B_QK_WIDTH]
        grads["b_w_gate_b"][j] = dwblk[B_GATE_RANK:2 * B_GATE_RANK, B_QK_WIDTH:]
        grads["b_gate_bias_f"][j] = dbias[0, :B_QK_WIDTH]
        grads["b_gate_bias_b"][j] = dbias[0, B_QK_WIDTH:]
        dproj = gla_combine(pf[:3], pb[:3], dr, dz, name=f"l{i}_b_combine")
        grads["b_w_in"][j] = matmul(sv["hn"].T, dproj, tn=640, tk=2048, name=f"l{i}_b_in_dw")[:, :B_IN_WIDTH]
        dhn = matmul(dproj, pt["b_w_in"][j], tk=640, name=f"l{i}_b_in_dx")
        dh0, dg = rmsnorm_bwd(sv["h0"], p["attn_norm"][i][None], [dhn], dh1, name=f"l{i}_norm1_bwd")
    grads["attn_norm"][i] = dg[0]
    return dh0


def _local_step(x, target, p, small):
    S = x.shape[0]
    cos, sin = _rope_tables(S)
    to_phase = lambda t, d: t.reshape(S // d, d, LANES).swapaxes(0, 1).reshape(S, LANES)
    cos, sin = [to_phase(cos, d) for d in DILS], [to_phase(sin, d) for d in DILS]
    ones_v = jnp.ones((A_WIDTH,), F32)
    a_gain = [[jnp.concatenate([jnp.tile(small["a_q_norm"][j][g], A_HEADS), jnp.tile(small["a_k_norm"][j][g], A_HEADS),
                                ones_v])[None] for g in range(len(DILS))] for j in range(2)]
    b_wblk = [_gate_block_weight(p["b_w_gate_f"][j].astype(F32), p["b_w_gate_b"][j].astype(F32)) for j in range(2)]
    aux = dict(cos=cos, sin=sin, ones=_head_block_ones(), a_gain=a_gain, tri_f=_tri(False), tri_b=_tri(True),
               b_wblk=b_wblk, b_wblk_t=[w.T for w in b_wblk],
               b_bias=[jnp.concatenate([small["b_gate_bias_f"][j], small["b_gate_bias_b"][j]])[None] for j in range(2)],
               b_gain=[small["b_out_norm"][j].reshape(1, B_V_WIDTH) for j in range(2)])
    pw = dict(p)
    pw["b_w_in"] = jnp.pad(p["b_w_in"], ((0, 0), (0, 0), (0, B_IN_PAD - B_IN_WIDTH)))
    pw["attn_norm"], pw["ffn_norm"] = small["attn_norm"], small["ffn_norm"]
    gw = 3 * A_WIDTH
    pw["a_w_in_g"] = [[p["a_w_in"][j][:, g * gw:(g + 1) * gw] for g in range(len(DILS))] for j in range(2)]
    pt = {n: jnp.swapaxes(pw[n], 1, 2) for n in ("a_w_out", "b_w_in", "b_w_out", "ffn_w_gate_up", "ffn_w_down")}
    pt["a_w_in_g"] = [[wg.T for wg in row] for row in pw["a_w_in_g"]]

    h = x
    saved = []
    for i in range(DEPTH):
        h, sv = _layer_fwd(i, h, pw, aux)
        saved.append(sv)
    loss_sq, dh = loss_head(h, target, name="loss_head")
    grads = {n: [None] * (DEPTH if n in ("attn_norm", "ffn_norm", "ffn_w_gate_up", "ffn_w_down") else 2) for n in WEIGHTS}
    for i in reversed(range(DEPTH)):
        dh = _layer_bwd(i, dh, pw, pt, aux, saved[i], grads)
    return loss_sq[0, 0] * (0.5 / D_MODEL), dh, grads


def kernel(x, attn_norm, ffn_norm, a_w_in, a_q_norm, a_k_norm, a_w_out, b_w_in, b_w_gate_f, b_gate_bias_f, b_w_gate_b, b_gate_bias_b, b_out_norm, b_w_out, ffn_w_gate_up, ffn_w_down, loss_target, m_attn_norm, m_ffn_norm, m_a_w_in, m_a_q_norm, m_a_k_norm, m_a_w_out, m_b_w_in, m_b_w_gate_f, m_b_gate_bias_f, m_b_w_gate_b, m_b_gate_bias_b, m_b_out_norm, m_b_w_out, m_ffn_w_gate_up, m_ffn_w_down, v_attn_norm, v_ffn_norm, v_a_w_in, v_a_q_norm, v_a_k_norm, v_a_w_out, v_b_w_in, v_b_w_gate_f, v_b_gate_bias_f, v_b_w_gate_b, v_b_gate_bias_b, v_b_out_norm, v_b_w_out, v_ffn_w_gate_up, v_ffn_w_down):
    w = dict(attn_norm=attn_norm, ffn_norm=ffn_norm, a_w_in=a_w_in, a_q_norm=a_q_norm, a_k_norm=a_k_norm, a_w_out=a_w_out,
             b_w_in=b_w_in, b_w_gate_f=b_w_gate_f, b_gate_bias_f=b_gate_bias_f, b_w_gate_b=b_w_gate_b,
             b_gate_bias_b=b_gate_bias_b, b_out_norm=b_out_norm, b_w_out=b_w_out, ffn_w_gate_up=ffn_w_gate_up,
             ffn_w_down=ffn_w_down)
    m = dict(attn_norm=m_attn_norm, ffn_norm=m_ffn_norm, a_w_in=m_a_w_in, a_q_norm=m_a_q_norm, a_k_norm=m_a_k_norm,
             a_w_out=m_a_w_out, b_w_in=m_b_w_in, b_w_gate_f=m_b_w_gate_f, b_gate_bias_f=m_b_gate_bias_f,
             b_w_gate_b=m_b_w_gate_b, b_gate_bias_b=m_b_gate_bias_b, b_out_norm=m_b_out_norm, b_w_out=m_b_w_out,
             ffn_w_gate_up=m_ffn_w_gate_up, ffn_w_down=m_ffn_w_down)
    v = dict(attn_norm=v_attn_norm, ffn_norm=v_ffn_norm, a_w_in=v_a_w_in, a_q_norm=v_a_q_norm, a_k_norm=v_a_k_norm,
             a_w_out=v_a_w_out, b_w_in=v_b_w_in, b_w_gate_f=v_b_w_gate_f, b_gate_bias_f=v_b_gate_bias_f,
             b_w_gate_b=v_b_w_gate_b, b_gate_bias_b=v_b_gate_bias_b, b_out_norm=v_b_out_norm, b_w_out=v_b_w_out,
             ffn_w_gate_up=v_ffn_w_gate_up, ffn_w_down=v_ffn_w_down)

    full = _gather_weights(w)
    p = {n: full[n] for n in BIG}
    small = {n: full[n] for n in SMALL_SHARDED}
    small.update({n: w[n] for n in REPLICATED})
    loss_local, dx, grads = _local_step(x[0], loss_target[0], p, small)
    loss = lax.psum(loss_local, ("x", "y", "c"))

    g_flat = _reduce_gradients(grads, _local_shapes(w))
    g = dict(zip(WEIGHTS, _from_flat(g_flat, [w[n].shape for n in WEIGHTS])))
    delta, new_m, new_v = {}, {}, {}
    rows = lambda t: t.reshape(-1, t.shape[-1])
    for n in MATRICES:
        outs = adamw(rows(w[n]), rows(g[n]), rows(m[n]), rows(v[n]), name=f"adamw_{n}")
        delta[n], new_m[n], new_v[n] = [o.reshape(w[n].shape) for o in outs]
    rest = [n for n in WEIGHTS if n not in MATRICES]
    flat = lambda d: _to_flat([d[n] for n in rest], 8)
    outs = adamw(flat(w), flat(g), flat(m), flat(v), name="adamw_vectors")
    for d, o in zip((delta, new_m, new_v), outs):
        d.update(zip(rest, _from_flat(o, [w[n].shape for n in rest])))
    return (loss, dx[None], *[g[n] for n in WEIGHTS], *[delta[n] for n in WEIGHTS],
            *[new_m[n] for n in WEIGHTS], *[new_v[n] for n in WEIGHTS])
```

```python
import functools

import numpy as np
import jax
import jax.numpy as jnp
from jax import lax
from jax.experimental import pallas as pl
from jax.experimental.pallas import tpu as pltpu

F32, BF16 = jnp.float32, jnp.bfloat16
HI = lax.Precision.HIGHEST
MESH = pl.DeviceIdType.MESH

D_MODEL = 1024
DEPTH = 4
RMS_EPS = 1e-6
NEG_INF = -1e30
A_GROUPS = ((128, 1), (512, 4), (2048, 16))
DILS = tuple(d for _, d in A_GROUPS)
A_HALF = 64
A_HEADS = 16
A_HEAD_DIM = 64
A_WIDTH = 1024
A_IN_WIDTH = 9216
ROPE_THETA = 10000.0
B_HEADS = 4
B_KEY_DIM = 128
B_VAL_DIM = 256
B_QK_WIDTH = 512
B_V_WIDTH = 1024
B_GATE_RANK = 16
B_GATE_TAU = 16.0
B_CHUNK = 64
B_IN_WIDTH = 3104
B_IN_PAD = 3200
FFN_HIDDEN = 2816
ADAM_LR, ADAM_B1, ADAM_B2, ADAM_EPS, ADAM_WD, ADAM_STEP = 0.001, 0.9, 0.999, 1e-08, 0.01, 10
LANES = 128
VMEM_LIMIT = 48 * 1024 * 1024
FLAT_COLS = 1024
N_CHIPS = 4

WEIGHTS = ['attn_norm', 'ffn_norm', 'a_w_in', 'a_q_norm', 'a_k_norm', 'a_w_out', 'b_w_in', 'b_w_gate_f',
           'b_gate_bias_f', 'b_w_gate_b', 'b_gate_bias_b', 'b_out_norm', 'b_w_out', 'ffn_w_gate_up', 'ffn_w_down']
REPLICATED = ('attn_norm', 'ffn_norm', 'a_q_norm', 'a_k_norm')
SHARD_AXIS = {'a_w_in': 2, 'a_w_out': 1, 'b_w_in': 2, 'b_w_gate_f': 2, 'b_gate_bias_f': 1, 'b_w_gate_b': 2,
              'b_gate_bias_b': 1, 'b_out_norm': 2, 'b_w_out': 1, 'ffn_w_gate_up': 2, 'ffn_w_down': 1}
BIG = ('a_w_in', 'a_w_out', 'b_w_in', 'b_w_gate_f', 'b_w_gate_b', 'b_w_out', 'ffn_w_gate_up', 'ffn_w_down')
SMALL_SHARDED = ('b_gate_bias_f', 'b_gate_bias_b', 'b_out_norm')
MATRICES = ('a_w_in', 'a_w_out', 'b_w_in', 'b_w_out', 'ffn_w_gate_up', 'ffn_w_down')


def _params(sem):
    return pltpu.CompilerParams(dimension_semantics=sem, vmem_limit_bytes=VMEM_LIMIT)


def _tile(n, pref):
    t = min(n, pref)
    while n % t:
        t //= 2
    return t


def _const_spec(shape):
    nd = len(shape)
    return pl.BlockSpec(shape, lambda *_: (0,) * nd)


def matmul(a, b, *, name, trans_a=False, out_dtype=F32, res=None, tm=1024, tn=512, tk=1024):
    if trans_a:
        K, M = a.shape
    else:
        M, K = a.shape
    N = b.shape[1]
    assert b.shape[0] == K
    tm, tn, tk = _tile(M, tm), _tile(N, tn), _tile(K, tk)
    nk = K // tk
    dims = (((0,), (0,)), ((), ())) if trans_a else (((1,), (0,)), ((), ()))

    def body(*refs):
        if res is None:
            a_ref, b_ref, o_ref, acc_ref = refs
        else:
            a_ref, b_ref, r_ref, o_ref, acc_ref = refs
        k = pl.program_id(2)

        @pl.when(k == 0)
        def _():
            acc_ref[...] = jnp.zeros_like(acc_ref)

        acc_ref[...] += lax.dot_general(a_ref[...], b_ref[...], dims, preferred_element_type=F32)

        @pl.when(k == nk - 1)
        def _():
            v = acc_ref[...]
            if res is not None:
                v = v + r_ref[...]
            o_ref[...] = v.astype(o_ref.dtype)

    a_spec = pl.BlockSpec((tk, tm), lambda i, j, k: (k, i)) if trans_a else pl.BlockSpec((tm, tk), lambda i, j, k: (i, k))
    in_specs = [a_spec, pl.BlockSpec((tk, tn), lambda i, j, k: (k, j))]
    args = [a, b]
    if res is not None:
        in_specs.append(pl.BlockSpec((tm, tn), lambda i, j, k: (i, j)))
        args.append(res)
    return pl.pallas_call(
        body, name=name, grid=(M // tm, N // tn, nk), in_specs=in_specs,
        out_specs=pl.BlockSpec((tm, tn), lambda i, j, k: (i, j)),
        out_shape=jax.ShapeDtypeStruct((M, N), out_dtype),
        scratch_shapes=[pltpu.VMEM((tm, tn), F32)],
        compiler_params=_params(("parallel", "parallel", "arbitrary")),
    )(*args)


def _phase_spec(d, ts, W):
    if d == 1:
        return pl.BlockSpec((ts, W), lambda i: (i, 0))
    return pl.BlockSpec((d, ts // d, W), lambda i: (0, i, 0))


def _phase_view(a, d):
    return a if d == 1 else a.reshape(d, a.shape[0] // d, a.shape[1])


def _phase_shape(S, W, d, dtype):
    return jax.ShapeDtypeStruct((S, W) if d == 1 else (d, S // d, W), dtype)


def _nat_scratch(ts, W):
    return pltpu.VMEM((W // LANES, ts, LANES), F32)


def _put_natural(nat_ref, value):
    for c in range(nat_ref.shape[0]):
        nat_ref[c] = value[:, c * LANES:(c + 1) * LANES]


def _get_natural(nat_ref):
    return jnp.concatenate([nat_ref[c] for c in range(nat_ref.shape[0])], axis=1)


def _store_phases(nat_ref, o_ref, d, ts):
    for p in range(d):
        for c in range(nat_ref.shape[0]):
            o_ref[p, :, c * LANES:(c + 1) * LANES] = nat_ref[c, pl.ds(p, ts // d, stride=d), :].astype(o_ref.dtype)


def _load_phases(i_ref, nat_ref, d, ts):
    for p in range(d):
        for c in range(nat_ref.shape[0]):
            nat_ref[c, pl.ds(p, ts // d, stride=d), :] = i_ref[p, :, c * LANES:(c + 1) * LANES].astype(F32)


def rmsnorm_fwd(x, gain, *, name, dils=(1,)):
    S, Dm = x.shape
    ts = _tile(S, 512)

    def body(x_ref, g_ref, *rest):
        o_refs, scr = rest[:len(dils)], rest[len(dils)]
        xv = x_ref[...]
        r = lax.rsqrt(jnp.mean(xv * xv, axis=-1, keepdims=True) + RMS_EPS)
        y = (xv * r) * g_ref[...]
        if any(d > 1 for d in dils):
            _put_natural(scr, y)
        for d, o_ref in zip(dils, o_refs):
            if d == 1:
                o_ref[...] = y.astype(o_ref.dtype)
            else:
                _store_phases(scr, o_ref, d, ts)

    outs = pl.pallas_call(
        body, name=name, grid=(S // ts,),
        in_specs=[pl.BlockSpec((ts, Dm), lambda i: (i, 0)), _const_spec((1, Dm))],
        out_specs=[_phase_spec(d, ts, Dm) for d in dils],
        out_shape=[_phase_shape(S, Dm, d, BF16) for d in dils],
        scratch_shapes=[_nat_scratch(ts, Dm)],
        compiler_params=_params(("parallel",)),
    )(x, gain)
    return [o.reshape(S, Dm) for o in outs]


def rmsnorm_bwd(x, gain, dys, dres, *, name, dils=(1,)):
    S, Dm = x.shape
    ts = _tile(S, 256)
    nd = len(dils)

    def body(x_ref, g_ref, *rest):
        dy_refs, (dr_ref, dx_ref, dg_ref, scr) = rest[:nd], rest[nd:]

        @pl.when(pl.program_id(0) == 0)
        def _():
            dg_ref[...] = jnp.zeros_like(dg_ref)

        dyv = None
        for d, dy_ref in zip(dils, dy_refs):
            if d == 1:
                t = dy_ref[...].astype(F32)
            else:
                _load_phases(dy_ref, scr, d, ts)
                t = _get_natural(scr)
            dyv = t if dyv is None else dyv + t
        xv = x_ref[...]
        r = lax.rsqrt(jnp.mean(xv * xv, axis=-1, keepdims=True) + RMS_EPS)
        xhat = xv * r
        dyg = dyv * g_ref[...]
        dx = r * (dyg - xhat * jnp.mean(dyg * xhat, axis=-1, keepdims=True))
        dx_ref[...] = dr_ref[...] + dx
        dg_ref[0:1, :] += jnp.sum(dyv * xhat, axis=0, keepdims=True)

    row = pl.BlockSpec((ts, Dm), lambda i: (i, 0))
    return pl.pallas_call(
        body, name=name, grid=(S // ts,),
        in_specs=[row, _const_spec((1, Dm))] + [_phase_spec(d, ts, Dm) for d in dils] + [row],
        out_specs=[row, _const_spec((8, Dm))],
        out_shape=[jax.ShapeDtypeStruct((S, Dm), F32), jax.ShapeDtypeStruct((8, Dm), F32)],
        scratch_shapes=[_nat_scratch(ts, Dm)],
        compiler_params=_params(("arbitrary",)),
    )(x, gain, *[_phase_view(dy, d) for dy, d in zip(dys, dils)], dres)


def swiglu_fwd(gu, *, name):
    S, F2 = gu.shape
    Fh = F2 // 2
    ts = _tile(S, 512)

    def body(g_ref, u_ref, o_ref):
        g = g_ref[...].astype(F32)
        u = u_ref[...].astype(F32)
        o_ref[...] = (g * (1.0 / (1.0 + jnp.exp(-g))) * u).astype(o_ref.dtype)

    return pl.pallas_call(
        body, name=name, grid=(S // ts,),
        in_specs=[pl.BlockSpec((ts, Fh), lambda i: (i, 0)), pl.BlockSpec((ts, Fh), lambda i: (i, 1))],
        out_specs=pl.BlockSpec((ts, Fh), lambda i: (i, 0)),
        out_shape=jax.ShapeDtypeStruct((S, Fh), BF16),
        compiler_params=_params(("parallel",)),
    )(gu, gu)


def swiglu_bwd(gu, dact, *, name):
    S, F2 = gu.shape
    Fh = F2 // 2
    ts = _tile(S, 256)

    def body(gu_ref, d_ref, o_ref):
        g = gu_ref[:, :Fh].astype(F32)
        u = gu_ref[:, Fh:].astype(F32)
        d = d_ref[...].astype(F32)
        sig = 1.0 / (1.0 + jnp.exp(-g))
        o_ref[:, :Fh] = (d * u * (sig * (1.0 + g * (1.0 - sig)))).astype(o_ref.dtype)
        o_ref[:, Fh:] = (d * (g * sig)).astype(o_ref.dtype)

    return pl.pallas_call(
        body, name=name, grid=(S // ts,),
        in_specs=[pl.BlockSpec((ts, F2), lambda i: (i, 0)), pl.BlockSpec((ts, Fh), lambda i: (i, 0))],
        out_specs=pl.BlockSpec((ts, F2), lambda i: (i, 0)),
        out_shape=jax.ShapeDtypeStruct((S, F2), BF16),
        compiler_params=_params(("parallel",)),
    )(gu, dact)


def loss_head(y, target, *, name):
    S, Dm = y.shape
    ts = _tile(S, 512)

    def body(y_ref, t_ref, l_ref, d_ref):
        @pl.when(pl.program_id(0) == 0)
        def _():
            l_ref[...] = jnp.zeros_like(l_ref)

        e = y_ref[...] - t_ref[...]
        d_ref[...] = e * (1.0 / Dm)
        l_ref[...] += jnp.sum(e * e)

    return pl.pallas_call(
        body, name=name, grid=(S // ts,),
        in_specs=[pl.BlockSpec((ts, Dm), lambda i: (i, 0)), pl.BlockSpec((ts, Dm), lambda i: (i, 0))],
        out_specs=[_const_spec((8, LANES)), pl.BlockSpec((ts, Dm), lambda i: (i, 0))],
        out_shape=[jax.ShapeDtypeStruct((8, LANES), F32), jax.ShapeDtypeStruct((S, Dm), F32)],
        compiler_params=_params(("arbitrary",)),
    )(y, target)


def _head_block_ones():
    i = np.arange(LANES)
    return jnp.asarray((i[:, None] // A_HEAD_DIM == i[None, :] // A_HEAD_DIM).astype(np.float32))


def _rope_tables(S):
    half = A_HEAD_DIM // 2
    inv_freq = ROPE_THETA ** (-jnp.arange(half, dtype=F32) / half)
    ang = jnp.arange(S).astype(F32)[:, None] * inv_freq[None, :]
    cos = jnp.tile(jnp.cos(ang), (1, LANES // half))
    sin = jnp.tile(jnp.sin(ang), (1, LANES // half))
    return cos, sin


def _rot_half(x, lo):
    return jnp.where(lo, -pltpu.roll(x, LANES - 32, 1), pltpu.roll(x, 32, 1))


def _seg_mean(v, ones_ref):
    return jnp.dot(v, ones_ref[...], precision=HI, preferred_element_type=F32) * (1.0 / A_HEAD_DIM)


def qk_prep_fwd(qkv, gain, cos, sin, ones, *, name):
    S, W = qkv.shape
    ts = _tile(S, 256)
    nchunk = A_WIDTH // LANES

    def body(x_ref, g_ref, c_ref, s_ref, ones_ref, o_ref):
        kind = pl.program_id(1) % 3

        @pl.when(kind < 2)
        def _():
            scale = jnp.where(kind == 0, A_HEAD_DIM ** -0.5, 1.0).astype(F32)
            lo = (lax.broadcasted_iota(jnp.int32, (ts, LANES), 1) % A_HEAD_DIM) < (A_HEAD_DIM // 2)
            cv, sv = c_ref[...], s_ref[...]
            for c in range(nchunk):
                sl = slice(c * LANES, (c + 1) * LANES)
                xv = x_ref[:, sl]
                r = lax.rsqrt(_seg_mean(xv * xv, ones_ref) + RMS_EPS)
                y = (xv * r) * g_ref[:, sl]
                y = y * cv + _rot_half(y, lo) * sv
                o_ref[:, sl] = (y * scale).astype(o_ref.dtype)

        @pl.when(kind == 2)
        def _():
            o_ref[...] = x_ref[...].astype(o_ref.dtype)

    return pl.pallas_call(
        body, name=name, grid=(S // ts, W // A_WIDTH),
        in_specs=[pl.BlockSpec((ts, A_WIDTH), lambda i, j: (i, j)), pl.BlockSpec((1, A_WIDTH), lambda i, j: (0, j)),
                  pl.BlockSpec((ts, LANES), lambda i, j: (i, 0)), pl.BlockSpec((ts, LANES), lambda i, j: (i, 0)),
                  _const_spec((LANES, LANES))],
        out_specs=pl.BlockSpec((ts, A_WIDTH), lambda i, j: (i, j)),
        out_shape=jax.ShapeDtypeStruct((S, W), BF16),
        compiler_params=_params(("parallel", "arbitrary")),
    )(qkv, gain, cos, sin, ones)


def qk_prep_bwd(qkv, gain, cos, sin, ones, grads, *, name):
    S, W = qkv.shape
    ts = _tile(S, 256)
    nchunk = A_WIDTH // LANES
    nj = W // A_WIDTH

    def body(x_ref, g_ref, c_ref, s_ref, ones_ref, *rest):
        g_refs, (o_ref, dg_ref) = rest[:nj], rest[nj:]
        j = pl.program_id(0)
        kind = j % 3

        @pl.when(pl.program_id(1) == 0)
        def _():
            dg_ref[...] = jnp.zeros_like(dg_ref)

        for n in range(nj):
            @pl.when(j == n)
            def _(n=n):
                d_ref = g_refs[n]
                if n % 3 == 2:
                    o_ref[...] = d_ref[...].astype(o_ref.dtype)
                    return
                scale = A_HEAD_DIM ** -0.5 if n % 3 == 0 else 1.0
                lo = (lax.broadcasted_iota(jnp.int32, (ts, LANES), 1) % A_HEAD_DIM) < (A_HEAD_DIM // 2)
                cv, sv = c_ref[...], s_ref[...]
                for c in range(nchunk):
                    sl = slice(c * LANES, (c + 1) * LANES)
                    dy = d_ref[:, sl].astype(F32) * scale
                    dn = dy * cv - _rot_half(dy, lo) * sv
                    xv = x_ref[:, sl]
                    r = lax.rsqrt(_seg_mean(xv * xv, ones_ref) + RMS_EPS)
                    xhat = xv * r
                    dyg = dn * g_ref[:, sl]
                    dx = r * (dyg - xhat * _seg_mean(dyg * xhat, ones_ref))
                    o_ref[:, sl] = dx.astype(o_ref.dtype)
                    dg_ref[0:1, sl] += jnp.sum(dn * xhat, axis=0, keepdims=True)

    def gspec(n):
        return pl.BlockSpec((ts, A_WIDTH), lambda j, i: (jnp.where(j == n, i, 0), 0))

    return pl.pallas_call(
        body, name=name, grid=(nj, S // ts),
        in_specs=[pl.BlockSpec((ts, A_WIDTH), lambda j, i: (i, j)), pl.BlockSpec((1, A_WIDTH), lambda j, i: (0, j)),
                  pl.BlockSpec((ts, LANES), lambda j, i: (i, 0)), pl.BlockSpec((ts, LANES), lambda j, i: (i, 0)),
                  _const_spec((LANES, LANES))] + [gspec(n) for n in range(nj)],
        out_specs=[pl.BlockSpec((ts, A_WIDTH), lambda j, i: (i, j)), pl.BlockSpec((8, A_WIDTH), lambda j, i: (0, j))],
        out_shape=[jax.ShapeDtypeStruct((S, W), BF16), jax.ShapeDtypeStruct((8, W), F32)],
        compiler_params=_params(("arbitrary", "arbitrary")),
    )(qkv, gain, cos, sin, ones, *grads)


def _band_specs(kind, tq, nlb):
    nhb = tq // A_HALF
    nb = nlb // nhb
    base = kind * (A_WIDTH // LANES)
    return [pl.BlockSpec((A_HALF, LANES), lambda ph, b, hp: (ph * nlb + jnp.maximum(b * nhb - 1, 0), base + hp)),
            pl.BlockSpec((tq, LANES), lambda ph, b, hp: (ph * nb + b, base + hp)),
            pl.BlockSpec((A_HALF, LANES), lambda ph, b, hp: (ph * nlb + jnp.minimum((b + 1) * nhb, nlb - 1), base + hp))]


A_BLOCK = 512
A_SUB = 128


def _band_bias(sub, key_major):
    i = np.arange(sub)[:, None]
    j = np.arange(sub + 2 * A_HALF)[None, :] - A_HALF
    ok = np.abs(j - i) <= A_HALF
    return jnp.asarray(np.where(ok.T if key_major else ok, 0.0, NEG_INF).astype(np.float32))


def _edge_bias(first, n, L, axis, at_start, at_end):
    if not (at_start or at_end):
        return None
    shape = (1, n) if axis == 1 else (n, 1)
    pos = first - A_HALF + lax.broadcasted_iota(jnp.int32, shape, axis)
    return jnp.where((pos < 0) | (pos >= L), NEG_INF, 0.0).astype(F32)


def _with_edge(bias, edge):
    return bias if edge is None else bias + edge


def _cat3(a_ref, b_ref, c_ref):
    return jnp.concatenate([a_ref[...], b_ref[...], c_ref[...]], axis=0)


def _lane_lo(rows):
    return lax.broadcasted_iota(jnp.int32, (rows, LANES), 1) < A_HEAD_DIM


NT = (((1,), (1,)), ((), ()))
TN = (((0,), (0,)), ((), ()))


def attn_fwd(qkvp, dil, *, name):
    S = qkvp.shape[0]
    L = S // dil
    tq = _tile(L, A_BLOCK)
    sub = min(tq, A_SUB)
    nsub = tq // sub
    nlb = L // A_HALF
    band = _band_bias(sub, False)

    def body(q_ref, kp_ref, ko_ref, kn_ref, vp_ref, vo_ref, vn_ref, band_ref, o_ref, l_ref):
        b = pl.program_id(1)
        K = _cat3(kp_ref, ko_ref, kn_ref)
        V = _cat3(vp_ref, vo_ref, vn_ref)
        lo_k = _lane_lo(tq + 2 * A_HALF)
        lo_q = _lane_lo(sub)
        Km = [jnp.where(sel, K, jnp.zeros_like(K)) for sel in (lo_k, ~lo_k)]
        Vm = [jnp.where(sel, V, jnp.zeros_like(V)) for sel in (lo_k, ~lo_k)]
        for r in range(nsub):
            rows = slice(r * sub, (r + 1) * sub)
            keys = slice(r * sub, (r + 1) * sub + 2 * A_HALF)
            bias = _with_edge(band_ref[...], _edge_bias(b * tq + r * sub, sub + 2 * A_HALF, L, 1, r == 0, r == nsub - 1))
            q = q_ref[rows, :]
            outs, lses = [], []
            for hh in range(2):
                s = lax.dot_general(q, Km[hh][keys], NT, preferred_element_type=F32) + bias
                m = jnp.max(s, axis=1, keepdims=True)
                p = jnp.exp(s - m)
                l = jnp.sum(p, axis=1, keepdims=True)
                outs.append(jnp.dot(p.astype(BF16), Vm[hh][keys], preferred_element_type=F32) * (1.0 / l))
                lses.append(m + jnp.log(l))
            o_ref[rows, :] = outs[0] + outs[1]
            l_ref[rows, :] = jnp.where(lo_q, lses[0], lses[1])

    ospec = _band_specs(0, tq, nlb)[1]
    return pl.pallas_call(
        body, name=name, grid=(dil, L // tq, A_WIDTH // LANES),
        in_specs=[_band_specs(0, tq, nlb)[1]] + _band_specs(1, tq, nlb) + _band_specs(2, tq, nlb)
        + [_const_spec(band.shape)],
        out_specs=[ospec, ospec],
        out_shape=[jax.ShapeDtypeStruct((S, A_WIDTH), F32)] * 2,
        compiler_params=_params(("parallel", "parallel", "parallel")),
    )(*([qkvp] * 7), band)


def attn_merge(os_, lses, dils, *, name):
    S = os_[0].shape[0]
    ts = _tile(S, 256)
    ng = len(dils)

    def body(*refs):
        o_refs, l_refs, out_ref = refs[:ng], refs[ng:2 * ng], refs[2 * ng]
        lse_refs, scrs = refs[2 * ng + 1:3 * ng + 1], refs[3 * ng + 1:]
        ov, ls, k = [], [], 0
        for d, o_ref, l_ref in zip(dils, o_refs, l_refs):
            if d == 1:
                ov.append(o_ref[...])
                ls.append(l_ref[...])
            else:
                _load_phases(o_ref, scrs[k], d, ts)
                _load_phases(l_ref, scrs[k + 1], d, ts)
                ov.append(_get_natural(scrs[k]))
                ls.append(_get_natural(scrs[k + 1]))
                k += 2
        m = functools.reduce(jnp.maximum, ls)
        es = [jnp.exp(l - m) for l in ls]
        tot = functools.reduce(jnp.add, es)
        acc = None
        for e, o in zip(es, ov):
            t = (e / tot) * o
            acc = t if acc is None else acc + t
        out_ref[...] = acc.astype(out_ref.dtype)
        total = m + jnp.log(tot)
        _put_natural(scrs[k], total)
        for d, lse_ref in zip(dils, lse_refs):
            if d == 1:
                lse_ref[...] = total
            else:
                _store_phases(scrs[k], lse_ref, d, ts)

    n_scr = 2 * sum(d > 1 for d in dils) + 1
    outs = pl.pallas_call(
        body, name=name, grid=(S // ts,),
        in_specs=[_phase_spec(d, ts, A_WIDTH) for d in dils] * 2,
        out_specs=[pl.BlockSpec((ts, A_WIDTH), lambda i: (i, 0))] + [_phase_spec(d, ts, A_WIDTH) for d in dils],
        out_shape=[jax.ShapeDtypeStruct((S, A_WIDTH), BF16)] + [_phase_shape(S, A_WIDTH, d, F32) for d in dils],
        scratch_shapes=[_nat_scratch(ts, A_WIDTH)] * n_scr,
        compiler_params=_params(("parallel",)),
    )(*[_phase_view(o, d) for o, d in zip(os_, dils)], *[_phase_view(l, d) for l, d in zip(lses, dils)])
    return outs[0], [l.reshape(S, A_WIDTH) for l in outs[1:]]


def attn_delta(dout, out, ones, dils, *, name):
    S = dout.shape[0]
    ts = _tile(S, 256)
    nd = len(dils)

    def body(d_ref, o_ref, ones_ref, *rest):
        do_refs, dl_refs, (scr_do, scr_dl) = rest[:nd], rest[nd:2 * nd], rest[2 * nd:]
        sums = []
        for c in range(A_WIDTH // LANES):
            sl = slice(c * LANES, (c + 1) * LANES)
            prod = d_ref[:, sl].astype(F32) * o_ref[:, sl].astype(F32)
            sums.append(jnp.dot(prod, ones_ref[...], precision=HI, preferred_element_type=F32))
        _put_natural(scr_do, d_ref[...].astype(F32))
        _put_natural(scr_dl, jnp.concatenate(sums, axis=1))
        for d, do_ref, dl_ref in zip(dils, do_refs, dl_refs):
            if d == 1:
                do_ref[...] = d_ref[...]
                dl_ref[...] = _get_natural(scr_dl)
            else:
                _store_phases(scr_do, do_ref, d, ts)
                _store_phases(scr_dl, dl_ref, d, ts)

    spec = pl.BlockSpec((ts, A_WIDTH), lambda i: (i, 0))
    outs = pl.pallas_call(
        body, name=name, grid=(S // ts,), in_specs=[spec, spec, _const_spec((LANES, LANES))],
        out_specs=[_phase_spec(d, ts, A_WIDTH) for d in dils] * 2,
        out_shape=[_phase_shape(S, A_WIDTH, d, BF16) for d in dils] + [_phase_shape(S, A_WIDTH, d, F32) for d in dils],
        scratch_shapes=[_nat_scratch(ts, A_WIDTH)] * 2,
        compiler_params=_params(("parallel",)),
    )(dout, out, ones)
    outs = [o.reshape(S, A_WIDTH) for o in outs]
    return outs[:nd], outs[nd:]


def _head_col(x, hh):
    c = hh * A_HEAD_DIM
    return x[:, c:c + 1]


def attn_bwd_dq(qkvp, dout, lse, delta, dil, *, name):
    S = qkvp.shape[0]
    L = S // dil
    tq = _tile(L, A_BLOCK)
    sub = min(tq, A_SUB)
    nsub = tq // sub
    nlb = L // A_HALF
    band = _band_bias(sub, False)

    def body(q_ref, kp_ref, ko_ref, kn_ref, vp_ref, vo_ref, vn_ref, do_ref, l_ref, d_ref, band_ref, dq_ref):
        b = pl.program_id(1)
        K = _cat3(kp_ref, ko_ref, kn_ref)
        V = _cat3(vp_ref, vo_ref, vn_ref)
        lo_k = _lane_lo(tq + 2 * A_HALF)
        Km = [jnp.where(sel, K, jnp.zeros_like(K)) for sel in (lo_k, ~lo_k)]
        Vm = [jnp.where(sel, V, jnp.zeros_like(V)) for sel in (lo_k, ~lo_k)]
        for r in range(nsub):
            rows = slice(r * sub, (r + 1) * sub)
            keys = slice(r * sub, (r + 1) * sub + 2 * A_HALF)
            bias = _with_edge(band_ref[...], _edge_bias(b * tq + r * sub, sub + 2 * A_HALF, L, 1, r == 0, r == nsub - 1))
            q, do = q_ref[rows, :], do_ref[rows, :]
            lse_v, dl_v = l_ref[rows, :], d_ref[rows, :]
            acc = None
            for hh in range(2):
                s = lax.dot_general(q, Km[hh][keys], NT, preferred_element_type=F32) + bias
                p = jnp.exp(s - _head_col(lse_v, hh))
                dp = lax.dot_general(do, Vm[hh][keys], NT, preferred_element_type=F32)
                ds = p * (dp - _head_col(dl_v, hh))
                t = jnp.dot(ds.astype(BF16), Km[hh][keys], preferred_element_type=F32)
                acc = t if acc is None else acc + t
            dq_ref[rows, :] = acc.astype(dq_ref.dtype)

    nspec = _band_specs(0, tq, nlb)[1]
    return pl.pallas_call(
        body, name=name, grid=(dil, L // tq, A_WIDTH // LANES),
        in_specs=[nspec] + _band_specs(1, tq, nlb) + _band_specs(2, tq, nlb) + [nspec, nspec, nspec, _const_spec(band.shape)],
        out_specs=nspec,
        out_shape=jax.ShapeDtypeStruct((S, A_WIDTH), BF16),
        compiler_params=_params(("parallel", "parallel", "parallel")),
    )(*([qkvp] * 7), dout, lse, delta, band)


def attn_bwd_dkv(qkvp, dout, lse, delta, dil, *, name):
    S = qkvp.shape[0]
    L = S // dil
    tk = _tile(L, A_BLOCK)
    sub = min(tk, A_SUB)
    nsub = tk // sub
    nlb = L // A_HALF
    band = _band_bias(sub, True)

    def body(qp_ref, qo_ref, qn_ref, k_ref, v_ref, dp_ref, do_ref, dn_ref, lp_ref, lo_ref, ln_ref,
             ep_ref, eo_ref, en_ref, band_ref, dk_ref, dv_ref):
        b = pl.program_id(1)
        Q = _cat3(qp_ref, qo_ref, qn_ref)
        DO = _cat3(dp_ref, do_ref, dn_ref)
        lse_v = _cat3(lp_ref, lo_ref, ln_ref)
        dl_v = _cat3(ep_ref, eo_ref, en_ref)
        lo_q = _lane_lo(tk + 2 * A_HALF)
        Qm = [jnp.where(sel, Q, jnp.zeros_like(Q)) for sel in (lo_q, ~lo_q)]
        DOm = [jnp.where(sel, DO, jnp.zeros_like(DO)) for sel in (lo_q, ~lo_q)]
        for r in range(nsub):
            keys = slice(r * sub, (r + 1) * sub)
            qs = slice(r * sub, (r + 1) * sub + 2 * A_HALF)
            bias = _with_edge(band_ref[...], _edge_bias(b * tk + r * sub, sub + 2 * A_HALF, L, 0, r == 0, r == nsub - 1))
            K, V = k_ref[keys, :], v_ref[keys, :]
            dk = dv = None
            for hh in range(2):
                s = lax.dot_general(Qm[hh][qs], K, NT, preferred_element_type=F32) + bias
                p = jnp.exp(s - _head_col(lse_v[qs], hh))
                dp = lax.dot_general(DOm[hh][qs], V, NT, preferred_element_type=F32)
                ds = p * (dp - _head_col(dl_v[qs], hh))
                tv = lax.dot_general(p.astype(BF16), DOm[hh][qs], TN, preferred_element_type=F32)
                tk_ = lax.dot_general(ds.astype(BF16), Qm[hh][qs], TN, preferred_element_type=F32)
                dv = tv if dv is None else dv + tv
                dk = tk_ if dk is None else dk + tk_
            dk_ref[keys, :] = dk.astype(dk_ref.dtype)
            dv_ref[keys, :] = dv.astype(dv_ref.dtype)

    nspec = _band_specs(0, tk, nlb)[1]
    return pl.pallas_call(
        body, name=name, grid=(dil, L // tk, A_WIDTH // LANES),
        in_specs=_band_specs(0, tk, nlb) + [_band_specs(1, tk, nlb)[1], _band_specs(2, tk, nlb)[1]]
        + _band_specs(0, tk, nlb) * 3 + [_const_spec(band.shape)],
        out_specs=[nspec, nspec],
        out_shape=[jax.ShapeDtypeStruct((S, A_WIDTH), BF16)] * 2,
        compiler_params=_params(("parallel", "parallel", "parallel")),
    )(*([qkvp] * 5), *([dout] * 3), *([lse] * 3), *([delta] * 3), band)


def _gate_block_weight(wf, wb):
    w = jnp.zeros((LANES, 2 * B_QK_WIDTH), F32)
    w = w.at[:B_GATE_RANK, :B_QK_WIDTH].set(wf)
    w = w.at[B_GATE_RANK:2 * B_GATE_RANK, B_QK_WIDTH:].set(wb)
    return w.astype(BF16)


def gate_fwd(proj, wblk, bias, *, name):
    S = proj.shape[0]
    ts = _tile(S, 512)
    W = 2 * B_QK_WIDTH
    zcol = (2 * B_QK_WIDTH + 2 * B_V_WIDTH) // LANES

    def body(z_ref, w_ref, b_ref, o_ref):
        x = jnp.dot(z_ref[...].astype(BF16), w_ref[...], preferred_element_type=F32) + b_ref[...]
        o_ref[...] = (jnp.minimum(x, 0.0) - jnp.log(1.0 + jnp.exp(-jnp.abs(x)))) * (1.0 / B_GATE_TAU)

    return pl.pallas_call(
        body, name=name, grid=(S // ts,),
        in_specs=[pl.BlockSpec((ts, LANES), lambda i: (i, zcol)), _const_spec((LANES, W)), _const_spec((1, W))],
        out_specs=pl.BlockSpec((ts, W), lambda i: (i, 0)),
        out_shape=jax.ShapeDtypeStruct((S, W), F32),
        compiler_params=_params(("parallel",)),
    )(proj, wblk, bias)


def gate_bwd(proj, wblk, wblk_t, bias, dloga, *, name):
    S = proj.shape[0]
    ts = _tile(S, 512)
    W = 2 * B_QK_WIDTH
    zcol = (2 * B_QK_WIDTH + 2 * B_V_WIDTH) // LANES

    def body(z_ref, w_ref, wt_ref, b_ref, d_ref, dz_ref, dw_ref, db_ref):
        @pl.when(pl.program_id(0) == 0)
        def _():
            dw_ref[...] = jnp.zeros_like(dw_ref)
            db_ref[...] = jnp.zeros_like(db_ref)

        z = z_ref[...].astype(BF16)
        x = jnp.dot(z, w_ref[...], preferred_element_type=F32) + b_ref[...]
        e = jnp.exp(-jnp.abs(x))
        sig_neg = jnp.where(x >= 0, e, 1.0) / (1.0 + e)
        dx = d_ref[...] * (1.0 / B_GATE_TAU) * sig_neg
        dxb = dx.astype(BF16)
        dz_ref[...] = jnp.dot(dxb, wt_ref[...], preferred_element_type=F32)
        dw_ref[...] += lax.dot_general(z, dxb, TN, preferred_element_type=F32)
        db_ref[0:1, :] += jnp.sum(dx, axis=0, keepdims=True)

    return pl.pallas_call(
        body, name=name, grid=(S // ts,),
        in_specs=[pl.BlockSpec((ts, LANES), lambda i: (i, zcol)), _const_spec((LANES, W)), _const_spec((W, LANES)),
                  _const_spec((1, W)), pl.BlockSpec((ts, W), lambda i: (i, 0))],
        out_specs=[pl.BlockSpec((ts, LANES), lambda i: (i, 0)), _const_spec((LANES, W)), _const_spec((8, W))],
        out_shape=[jax.ShapeDtypeStruct((S, LANES), F32), jax.ShapeDtypeStruct((LANES, W), F32),
                   jax.ShapeDtypeStruct((8, W), F32)],
        compiler_params=_params(("arbitrary",)),
    )(proj, wblk, wblk_t, bias, dloga)


def _tri(reverse):
    i = np.arange(B_CHUNK)
    t = (i[None, :] >= i[:, None]) if reverse else (i[None, :] <= i[:, None])
    return jnp.asarray(t.astype(np.float32))


def _gla_terms(q, k, la, t_ref, reverse):
    b = jnp.dot(t_ref[...], la, precision=HI, preferred_element_type=F32)
    b_last = b[0:1, :] if reverse else b[B_CHUNK - 1:B_CHUNK, :]
    e_b = jnp.exp(b)
    qt = (q * (B_KEY_DIM ** -0.5)) * e_b
    e_nb = jnp.exp(-b)
    kt = k * e_nb
    e_end = jnp.exp(b_last - b)
    kend = k * e_end
    dec = jnp.exp(b_last)
    return e_nb, e_b, qt, kt, e_end, kend, dec


def _chunk_mask(reverse, transpose=False):
    r = lax.broadcasted_iota(jnp.int32, (B_CHUNK, B_CHUNK), 0)
    c = lax.broadcasted_iota(jnp.int32, (B_CHUNK, B_CHUNK), 1)
    if transpose:
        r, c = c, r
    return (c > r) if reverse else (c <= r)


def gla_fwd(proj, loga, tmat, reverse, *, name):
    S = proj.shape[0]
    tb = _tile(S, 512)
    nb = S // tb
    cpb = tb // B_CHUNK
    nc = S // B_CHUNK
    qb, kb_, vb = 0, B_QK_WIDTH // B_KEY_DIM, 2 * B_QK_WIDTH // B_VAL_DIM
    lb = (B_QK_WIDTH // B_KEY_DIM) if reverse else 0
    blk = (lambda i: nb - 1 - i) if reverse else (lambda i: i)

    def body(q_ref, k_ref, v_ref, la_ref, t_ref, o_ref, st_ref, s_scr):
        @pl.when(pl.program_id(1) == 0)
        def _():
            s_scr[...] = jnp.zeros_like(s_scr)

        mask = _chunk_mask(reverse)
        order = range(cpb - 1, -1, -1) if reverse else range(cpb)
        for c in order:
            rows = slice(c * B_CHUNK, (c + 1) * B_CHUNK)
            v = v_ref[rows, :].astype(BF16)
            _, _, qt, kt, _, kend, dec = _gla_terms(q_ref[rows, :], k_ref[rows, :], la_ref[rows, :], t_ref, reverse)
            qt, kt, kend = qt.astype(BF16), kt.astype(BF16), kend.astype(BF16)
            st = s_scr[...]
            st_ref[0, c] = st
            a = jnp.where(mask, lax.dot_general(qt, kt, NT, preferred_element_type=F32), 0.0)
            o = jnp.dot(a.astype(BF16), v, preferred_element_type=F32)
            o = o + lax.dot_general(qt, st.astype(BF16), NT, preferred_element_type=F32)
            o_ref[rows, :] = o
            s_scr[...] = st * dec + lax.dot_general(v, kend, TN, preferred_element_type=F32)

    return pl.pallas_call(
        body, name=name, grid=(B_HEADS, nb),
        in_specs=[pl.BlockSpec((tb, B_KEY_DIM), lambda h, i: (blk(i), qb + h)),
                  pl.BlockSpec((tb, B_KEY_DIM), lambda h, i: (blk(i), kb_ + h)),
                  pl.BlockSpec((tb, B_VAL_DIM), lambda h, i: (blk(i), vb + h)),
                  pl.BlockSpec((tb, B_KEY_DIM), lambda h, i: (blk(i), lb + h)),
                  _const_spec((B_CHUNK, B_CHUNK))],
        out_specs=[pl.BlockSpec((tb, B_VAL_DIM), lambda h, i: (blk(i), h)),
                   pl.BlockSpec((1, cpb, B_VAL_DIM, B_KEY_DIM), lambda h, i: (h, blk(i), 0, 0))],
        out_shape=[jax.ShapeDtypeStruct((S, B_V_WIDTH), F32),
                   jax.ShapeDtypeStruct((B_HEADS, nc, B_VAL_DIM, B_KEY_DIM), F32)],
        scratch_shapes=[pltpu.VMEM((B_VAL_DIM, B_KEY_DIM), F32)],
        compiler_params=_params(("parallel", "arbitrary")),
    )(proj, proj, proj, loga, tmat)


def gla_bwd(proj, loga, states, do, tmat, tmat_t, reverse, *, name):
    S = proj.shape[0]
    tb = _tile(S, 512)
    nb = S // tb
    cpb = tb // B_CHUNK
    qb, kb_, vb = 0, B_QK_WIDTH // B_KEY_DIM, 2 * B_QK_WIDTH // B_VAL_DIM
    lb = (B_QK_WIDTH // B_KEY_DIM) if reverse else 0
    blk = (lambda i: i) if reverse else (lambda i: nb - 1 - i)
    scale = B_KEY_DIM ** -0.5

    def body(q_ref, k_ref, v_ref, la_ref, st_ref, do_ref, t_ref, tt_ref, dq_ref, dk_ref, dv_ref, dla_ref, ds_scr):
        @pl.when(pl.program_id(1) == 0)
        def _():
            ds_scr[...] = jnp.zeros_like(ds_scr)

        mask = _chunk_mask(reverse)
        mask_t = _chunk_mask(reverse, transpose=True)
        last = 0 if reverse else B_CHUNK - 1
        is_last = lax.broadcasted_iota(jnp.int32, (B_CHUNK, B_KEY_DIM), 0) == last
        order = range(cpb) if reverse else range(cpb - 1, -1, -1)
        for c in order:
            rows = slice(c * B_CHUNK, (c + 1) * B_CHUNK)
            vf = v_ref[rows, :]
            v = vf.astype(BF16)
            dov = do_ref[rows, :]
            dob = dov.astype(BF16)
            e_nb, e_b, qt, kt, e_end, kend, dec = _gla_terms(q_ref[rows, :], k_ref[rows, :], la_ref[rows, :], t_ref, reverse)
            qtb, ktb, kendb = qt.astype(BF16), kt.astype(BF16), kend.astype(BF16)
            st = st_ref[0, c]
            dst = ds_scr[...]
            dstb = dst.astype(BF16)
            a_t = jnp.where(mask_t, lax.dot_general(ktb, qtb, NT, preferred_element_type=F32), 0.0)
            da = jnp.where(mask, lax.dot_general(dob, v, NT, preferred_element_type=F32), 0.0)
            da_t = jnp.where(mask_t, lax.dot_general(v, dob, NT, preferred_element_type=F32), 0.0)
            dv = jnp.dot(a_t.astype(BF16), dob, preferred_element_type=F32)
            dv = dv + lax.dot_general(kendb, dstb, NT, preferred_element_type=F32)
            dqt = jnp.dot(da.astype(BF16), ktb, preferred_element_type=F32)
            dqt = dqt + jnp.dot(dob, st.astype(BF16), preferred_element_type=F32)
            dkt = jnp.dot(da_t.astype(BF16), qtb, preferred_element_type=F32)
            dkend = jnp.dot(v, dstb, preferred_element_type=F32)
            ddec = jnp.sum(dst * st, axis=0, keepdims=True)
            ds_scr[...] = dst * dec + lax.dot_general(dob, qtb, TN, preferred_element_type=F32)
            ke = dkend * kend
            db = dqt * qt - dkt * kt - ke
            db_last = jnp.sum(ke, axis=0, keepdims=True) + ddec * dec
            db = db + jnp.where(is_last, db_last, 0.0)
            dq_ref[rows, :] = dqt * e_b * scale
            dk_ref[rows, :] = dkt * e_nb + dkend * e_end
            dv_ref[rows, :] = dv
            dla_ref[rows, :] = jnp.dot(tt_ref[...], db, precision=HI, preferred_element_type=F32)

    kspec = lambda cb: pl.BlockSpec((tb, B_KEY_DIM), lambda h, i: (blk(i), cb + h))
    return pl.pallas_call(
        body, name=name, grid=(B_HEADS, nb),
        in_specs=[kspec(qb), kspec(kb_), pl.BlockSpec((tb, B_VAL_DIM), lambda h, i: (blk(i), vb + h)), kspec(lb),
                  pl.BlockSpec((1, cpb, B_VAL_DIM, B_KEY_DIM), lambda h, i: (h, blk(i), 0, 0)),
                  pl.BlockSpec((tb, B_VAL_DIM), lambda h, i: (blk(i), h)),
                  _const_spec((B_CHUNK, B_CHUNK)), _const_spec((B_CHUNK, B_CHUNK))],
        out_specs=[kspec(0), kspec(0), pl.BlockSpec((tb, B_VAL_DIM), lambda h, i: (blk(i), h)), kspec(0)],
        out_shape=[jax.ShapeDtypeStruct((S, B_QK_WIDTH), F32), jax.ShapeDtypeStruct((S, B_QK_WIDTH), F32),
                   jax.ShapeDtypeStruct((S, B_V_WIDTH), F32), jax.ShapeDtypeStruct((S, B_QK_WIDTH), F32)],
        scratch_shapes=[pltpu.VMEM((B_VAL_DIM, B_KEY_DIM), F32)],
        compiler_params=_params(("parallel", "arbitrary")),
    )(proj, proj, proj, loga, states, do, tmat, tmat_t)


def gla_post_fwd(o_f, o_b, gain, proj, *, name):
    S = o_f.shape[0]
    ts = _tile(S, 512)
    rcol = (2 * B_QK_WIDTH + B_V_WIDTH) // B_V_WIDTH

    def body(f_ref, b_ref, g_ref, r_ref, y_ref):
        for h in range(B_HEADS):
            sl = slice(h * B_VAL_DIM, (h + 1) * B_VAL_DIM)
            o = f_ref[:, sl] + b_ref[:, sl]
            n = (o * lax.rsqrt(jnp.mean(o * o, axis=-1, keepdims=True) + RMS_EPS)) * g_ref[:, sl]
            r = r_ref[:, sl]
            y_ref[:, sl] = (n * (r * (1.0 / (1.0 + jnp.exp(-r))))).astype(y_ref.dtype)

    spec = pl.BlockSpec((ts, B_V_WIDTH), lambda i: (i, 0))
    return pl.pallas_call(
        body, name=name, grid=(S // ts,),
        in_specs=[spec, spec, _const_spec((1, B_V_WIDTH)), pl.BlockSpec((ts, B_V_WIDTH), lambda i: (i, rcol))],
        out_specs=spec, out_shape=jax.ShapeDtypeStruct((S, B_V_WIDTH), BF16),
        compiler_params=_params(("parallel",)),
    )(o_f, o_b, gain, proj)


def gla_post_bwd(o_f, o_b, gain, proj, dy, *, name):
    S = o_f.shape[0]
    ts = _tile(S, 512)
    rcol = (2 * B_QK_WIDTH + B_V_WIDTH) // B_V_WIDTH

    def body(f_ref, b_ref, g_ref, r_ref, dy_ref, do_ref, dr_ref, dg_ref):
        @pl.when(pl.program_id(0) == 0)
        def _():
            dg_ref[...] = jnp.zeros_like(dg_ref)

        for h in range(B_HEADS):
            sl = slice(h * B_VAL_DIM, (h + 1) * B_VAL_DIM)
            o = f_ref[:, sl] + b_ref[:, sl]
            rs = lax.rsqrt(jnp.mean(o * o, axis=-1, keepdims=True) + RMS_EPS)
            ohat = o * rs
            g = g_ref[:, sl]
            r = r_ref[:, sl]
            sig = 1.0 / (1.0 + jnp.exp(-r))
            dyv = dy_ref[:, sl].astype(F32)
            dn = dyv * (r * sig)
            dr_ref[:, sl] = dyv * (ohat * g) * (sig * (1.0 + r * (1.0 - sig)))
            dng = dn * g
            do_ref[:, sl] = rs * (dng - ohat * jnp.mean(dng * ohat, axis=-1, keepdims=True))
            dg_ref[0:1, sl] += jnp.sum(dn * ohat, axis=0, keepdims=True)

    spec = pl.BlockSpec((ts, B_V_WIDTH), lambda i: (i, 0))
    return pl.pallas_call(
        body, name=name, grid=(S // ts,),
        in_specs=[spec, spec, _const_spec((1, B_V_WIDTH)), pl.BlockSpec((ts, B_V_WIDTH), lambda i: (i, rcol)), spec],
        out_specs=[spec, spec, _const_spec((8, B_V_WIDTH))],
        out_shape=[jax.ShapeDtypeStruct((S, B_V_WIDTH), F32), jax.ShapeDtypeStruct((S, B_V_WIDTH), F32),
                   jax.ShapeDtypeStruct((8, B_V_WIDTH), F32)],
        compiler_params=_params(("arbitrary",)),
    )(o_f, o_b, gain, proj, dy)


def gla_combine(parts_f, parts_b, dr, dz, *, name):
    S = dr.shape[0]
    ts = _tile(S, 512)

    def body(qf, kf, vf, qb, kb, vb, r_ref, z_ref, o_ref):
        o_ref[:, 0:512] = (qf[...] + qb[...]).astype(o_ref.dtype)
        o_ref[:, 512:1024] = (kf[...] + kb[...]).astype(o_ref.dtype)
        o_ref[:, 1024:2048] = (vf[...] + vb[...]).astype(o_ref.dtype)
        o_ref[:, 2048:3072] = r_ref[...].astype(o_ref.dtype)
        o_ref[:, 3072:3200] = z_ref[...].astype(o_ref.dtype)

    s512 = pl.BlockSpec((ts, B_QK_WIDTH), lambda i: (i, 0))
    s1024 = pl.BlockSpec((ts, B_V_WIDTH), lambda i: (i, 0))
    return pl.pallas_call(
        body, name=name, grid=(S // ts,),
        in_specs=[s512, s512, s1024, s512, s512, s1024, s1024, pl.BlockSpec((ts, LANES), lambda i: (i, 0))],
        out_specs=pl.BlockSpec((ts, B_IN_PAD), lambda i: (i, 0)),
        out_shape=jax.ShapeDtypeStruct((S, B_IN_PAD), BF16),
        compiler_params=_params(("parallel",)),
    )(*parts_f, *parts_b, dr, dz)


def adamw(w, g, m, v, *, name):
    R, C = w.shape
    tr = _tile(R, 256)
    c1 = 1.0 / (1.0 - ADAM_B1 ** ADAM_STEP)
    c2 = 1.0 / (1.0 - ADAM_B2 ** ADAM_STEP)

    def body(w_ref, g_ref, m_ref, v_ref, d_ref, mo_ref, vo_ref):
        gv = g_ref[...]
        mn = ADAM_B1 * m_ref[...] + (1.0 - ADAM_B1) * gv
        vn = ADAM_B2 * v_ref[...] + (1.0 - ADAM_B2) * (gv * gv)
        mo_ref[...] = mn
        vo_ref[...] = vn
        d_ref[...] = -ADAM_LR * ((mn * c1) / (jnp.sqrt(vn * c2) + ADAM_EPS) + ADAM_WD * w_ref[...])

    spec = pl.BlockSpec((tr, C), lambda i: (i, 0))
    return pl.pallas_call(
        body, name=name, grid=(R // tr,), in_specs=[spec] * 4, out_specs=[spec] * 3,
        out_shape=[jax.ShapeDtypeStruct((R, C), F32)] * 3,
        compiler_params=_params(("parallel",)),
    )(w, g, m, v)


def _chip_peers():
    x, y, c = lax.axis_index("x"), lax.axis_index("y"), lax.axis_index("c")
    return x, y, c, [(1 - x, y), (x, 1 - y), (1 - x, 1 - y)]


def chip_exchange(src, per_dest, *, name):
    _, R, C = src.shape

    def body(src_ref, out_ref, send_sems, recv_sems, local_sem):
        x, y, c, chips = _chip_peers()
        me = 2 * x + y

        def block(t):
            return src_ref.at[t if per_dest else 0]

        mine = pltpu.make_async_copy(block(me), out_ref.at[me], local_sem)
        mine.start()
        copies = []
        for k, (px, py) in enumerate(chips):
            cp = pltpu.make_async_remote_copy(
                src_ref=block(2 * px + py), dst_ref=out_ref.at[me], send_sem=send_sems.at[k], recv_sem=recv_sems.at[k],
                device_id=(px, py, c), device_id_type=MESH)
            cp.start()
            copies.append(cp)
        for k, (px, py) in enumerate(chips):
            pltpu.make_async_remote_copy(
                src_ref=block(me), dst_ref=out_ref.at[2 * px + py], send_sem=send_sems.at[k], recv_sem=recv_sems.at[k],
                device_id=(px, py, c), device_id_type=MESH).wait_recv()
        for cp in copies:
            cp.wait_send()
        mine.wait()

    return pl.pallas_call(
        body, name=name,
        in_specs=[pl.BlockSpec(memory_space=pl.ANY)], out_specs=pl.BlockSpec(memory_space=pl.ANY),
        out_shape=jax.ShapeDtypeStruct((N_CHIPS, R, C), src.dtype),
        scratch_shapes=[pltpu.SemaphoreType.DMA((3,)), pltpu.SemaphoreType.DMA((3,)), pltpu.SemaphoreType.DMA],
        compiler_params=pltpu.CompilerParams(has_side_effects=True),
    )(src)


def sibling_exchange(src, *, name):
    n, _, R, C = src.shape

    def body(src_ref, out_ref, send_sem, recv_sem):
        x, y, c = lax.axis_index("x"), lax.axis_index("y"), lax.axis_index("c")
        cp = pltpu.make_async_remote_copy(
            src_ref=src_ref.at[:, 1 - c], dst_ref=out_ref, send_sem=send_sem, recv_sem=recv_sem,
            device_id=(x, y, 1 - c), device_id_type=MESH)
        cp.start()
        cp.wait()

    return pl.pallas_call(
        body, name=name,
        in_specs=[pl.BlockSpec(memory_space=pl.ANY)], out_specs=pl.BlockSpec(memory_space=pl.ANY),
        out_shape=jax.ShapeDtypeStruct((n, R, C), src.dtype),
        scratch_shapes=[pltpu.SemaphoreType.DMA, pltpu.SemaphoreType.DMA],
        compiler_params=pltpu.CompilerParams(has_side_effects=True),
    )(src)


def sibling_share(src, *, name):
    def body(src_ref, out_ref, send_sem, recv_sem, local_sem):
        x, y, c = lax.axis_index("x"), lax.axis_index("y"), lax.axis_index("c")
        mine = pltpu.make_async_copy(src_ref, out_ref.at[c], local_sem)
        mine.start()
        cp = pltpu.make_async_remote_copy(
            src_ref=src_ref, dst_ref=out_ref.at[c], send_sem=send_sem, recv_sem=recv_sem,
            device_id=(x, y, 1 - c), device_id_type=MESH)
        cp.start()
        cp.wait()
        mine.wait()

    return pl.pallas_call(
        body, name=name,
        in_specs=[pl.BlockSpec(memory_space=pl.ANY)], out_specs=pl.BlockSpec(memory_space=pl.ANY),
        out_shape=jax.ShapeDtypeStruct((2,) + src.shape, src.dtype),
        scratch_shapes=[pltpu.SemaphoreType.DMA, pltpu.SemaphoreType.DMA, pltpu.SemaphoreType.DMA],
        compiler_params=pltpu.CompilerParams(has_side_effects=True),
    )(src)


def add_pair(a, b, *, name):
    n, _, R, C = a.shape
    tr = _tile(R, 256)

    def body(c_ref, a_ref, b_ref, o_ref):
        o_ref[...] = a_ref[0] + b_ref[...]

    return pl.pallas_call(
        body, name=name,
        grid_spec=pltpu.PrefetchScalarGridSpec(
            num_scalar_prefetch=1, grid=(n, R // tr),
            in_specs=[pl.BlockSpec((1, 1, tr, C), lambda s, i, c_ref: (s, c_ref[0], i, 0)),
                      pl.BlockSpec((1, tr, C), lambda s, i, c_ref: (s, i, 0))],
            out_specs=pl.BlockSpec((1, tr, C), lambda s, i, c_ref: (s, i, 0))),
        out_shape=jax.ShapeDtypeStruct((n, R, C), a.dtype),
        compiler_params=_params(("parallel", "parallel")),
    )(lax.axis_index("c").reshape(1).astype(jnp.int32), a, b)


def sum_slots(a, *, name):
    n, R, C = a.shape
    tr = _tile(R, 256)

    def body(*refs):
        o_ref = refs[-1]
        acc = refs[0][0]
        for r in refs[1:-1]:
            acc = acc + r[0]
        o_ref[...] = acc

    return pl.pallas_call(
        body, name=name, grid=(R // tr,),
        in_specs=[pl.BlockSpec((1, tr, C), functools.partial(lambda s, i: (s, i, 0), s)) for s in range(n)],
        out_specs=pl.BlockSpec((tr, C), lambda i: (i, 0)),
        out_shape=jax.ShapeDtypeStruct((R, C), a.dtype),
        compiler_params=_params(("parallel",)),
    )(*([a] * n))


def _flat_rows(n_elems, mult):
    rows = -(-n_elems // FLAT_COLS)
    return -(-rows // mult) * mult


def _to_flat(parts, mult):
    v = jnp.concatenate([p.reshape(-1) for p in parts])
    rows = _flat_rows(v.shape[0], mult)
    return jnp.pad(v, (0, rows * FLAT_COLS - v.shape[0])).reshape(rows, FLAT_COLS)


def _from_flat(flat, shapes):
    v = flat.reshape(-1)
    out, off = [], 0
    for s in shapes:
        n = int(np.prod(s))
        out.append(v[off:off + n].reshape(s))
        off += n
    return out


def _unshard(blocks, axis):
    return jnp.concatenate([blocks[s] for s in range(N_CHIPS)], axis=axis)


def _local_shapes(w):
    return {n: (w[n].shape[0], tuple(w[n].shape[1:])) for n in WEIGHTS}


def _gradient_blocks(grads, shapes):
    n_el = sum(nl * int(np.prod(shp)) for nl, shp in shapes.values())
    rows = _flat_rows(n_el, 32)
    pad = jnp.zeros((rows * FLAT_COLS - n_el,), F32)
    pieces = []
    for s in range(N_CHIPS):
        for n in WEIGHTS:
            for gl in grads[n]:
                if n not in REPLICATED:
                    ax = SHARD_AXIS[n] - 1
                    wd = shapes[n][1][ax]
                    gl = lax.slice_in_dim(gl, s * wd, (s + 1) * wd, axis=ax)
                pieces.append(gl.reshape(-1))
        pieces.append(pad)
    return jnp.concatenate(pieces).reshape(N_CHIPS, rows, FLAT_COLS)


def _gather_weights(w):
    full = {}
    for names, dtype, mult, call in ((BIG, BF16, 16, "gather_weights"), (SMALL_SHARDED, F32, 8, "gather_vectors")):
        parts = [w[n].astype(dtype) for n in names]
        got = chip_exchange(_to_flat(parts, mult)[None], False, name=call)
        per_chip = [_from_flat(got[s], [p.shape for p in parts]) for s in range(N_CHIPS)]
        for i, n in enumerate(names):
            full[n] = _unshard([per_chip[s][i] for s in range(N_CHIPS)], SHARD_AXIS[n])
    return full


def _reduce_gradients(grads, shapes):
    g = _gradient_blocks(grads, shapes)
    R = g.shape[1]
    halves = g.reshape(N_CHIPS, 2, R // 2, FLAT_COLS)
    other = sibling_exchange(halves, name="grad_pair_exchange")
    pair = add_pair(halves, other, name="grad_pair_add")
    slots = chip_exchange(pair, True, name="grad_chip_exchange")
    mine = sum_slots(slots, name="grad_chip_sum")
    both = sibling_share(mine, name="grad_half_share")
    return both.reshape(R, FLAT_COLS)


def _layer_fwd(i, h, p, aux):
    j = i // 2
    sv = {"h0": h}
    if i % 2 == 0:
        hns = rmsnorm_fwd(h, p["attn_norm"][i][None], dils=DILS, name=f"l{i}_norm1")
        qkvs, qkvps, os_, lses = [], [], [], []
        for g, d in enumerate(DILS):
            qkv = matmul(hns[g], p["a_w_in_g"][j][g], name=f"l{i}_a_in{g}")
            qkvp = qk_prep_fwd(qkv, aux["a_gain"][j][g], aux["cos"][g], aux["sin"][g], aux["ones"], name=f"l{i}_a_prep{g}")
            o, l = attn_fwd(qkvp, d, name=f"l{i}_a_attn{g}")
            qkvs.append(qkv)
            qkvps.append(qkvp)
            os_.append(o)
            lses.append(l)
        out, lse = attn_merge(os_, lses, DILS, name=f"l{i}_a_merge")
        sv.update(hns=hns, qkv=qkvs, qkvp=qkvps, out=out, lse=lse)
        h1 = matmul(out, p["a_w_out"][j], res=h, name=f"l{i}_a_out")
    else:
        hn = rmsnorm_fwd(h, p["attn_norm"][i][None], name=f"l{i}_norm1")[0]
        sv["hn"] = hn
        proj = matmul(hn, p["b_w_in"][j], tn=640, name=f"l{i}_b_in")
        loga = gate_fwd(proj, aux["b_wblk"][j], aux["b_bias"][j], name=f"l{i}_b_gate")
        o_f, st_f = gla_fwd(proj, loga, aux["tri_f"], False, name=f"l{i}_b_gla_f")
        o_b, st_b = gla_fwd(proj, loga, aux["tri_b"], True, name=f"l{i}_b_gla_b")
        y = gla_post_fwd(o_f, o_b, aux["b_gain"][j], proj, name=f"l{i}_b_post")
        sv.update(proj=proj, loga=loga, o_f=o_f, o_b=o_b, st_f=st_f, st_b=st_b, y=y)
        h1 = matmul(y, p["b_w_out"][j], res=h, name=f"l{i}_b_out")
    sv["h1"] = h1
    hn2 = rmsnorm_fwd(h1, p["ffn_norm"][i][None], name=f"l{i}_norm2")[0]
    gu = matmul(hn2, p["ffn_w_gate_up"][i], out_dtype=BF16, name=f"l{i}_f_up")
    act = swiglu_fwd(gu, name=f"l{i}_f_act")
    h2 = matmul(act, p["ffn_w_down"][i], res=h1, tk=2816, name=f"l{i}_f_down")
    sv.update(hn2=hn2, gu=gu, act=act)
    return h2, sv


def _layer_bwd(i, dh, p, pt, aux, sv, grads):
    j = i // 2
    dhb = dh.astype(BF16)
    grads["ffn_w_down"][i] = matmul(sv["act"].T, dhb, tk=2048, name=f"l{i}_f_down_dw")
    dact = matmul(dhb, pt["ffn_w_down"][i], out_dtype=BF16, name=f"l{i}_f_down_dx")
    dgu = swiglu_bwd(sv["gu"], dact, name=f"l{i}_f_act_bwd")
    grads["ffn_w_gate_up"][i] = matmul(sv["hn2"].T, dgu, tk=2048, name=f"l{i}_f_up_dw")
    dhn2 = matmul(dgu, pt["ffn_w_gate_up"][i], tk=2816, name=f"l{i}_f_up_dx")
    dh1, dg = rmsnorm_bwd(sv["h1"], p["ffn_norm"][i][None], [dhn2], dh, name=f"l{i}_norm2_bwd")
    grads["ffn_norm"][i] = dg[0]
    dh1b = dh1.astype(BF16)
    if i % 2 == 0:
        grads["a_w_out"][j] = matmul(sv["out"].T, dh1b, tk=2048, name=f"l{i}_a_out_dw")
        dout = matmul(dh1b, pt["a_w_out"][j], out_dtype=BF16, name=f"l{i}_a_out_dx")
        douts, deltas = attn_delta(dout, sv["out"], aux["ones"], DILS, name=f"l{i}_a_delta")
        dws, dhns, dgq, dgk = [], [], [], []
        for g, d in enumerate(DILS):
            qkvp, lse = sv["qkvp"][g], sv["lse"][g]
            dq = attn_bwd_dq(qkvp, douts[g], lse, deltas[g], d, name=f"l{i}_a_dq{g}")
            dk, dv = attn_bwd_dkv(qkvp, douts[g], lse, deltas[g], d, name=f"l{i}_a_dkv{g}")
            dqkv, dgain = qk_prep_bwd(sv["qkv"][g], aux["a_gain"][j][g], aux["cos"][g], aux["sin"][g], aux["ones"],
                                      [dq, dk, dv], name=f"l{i}_a_prep_bwd{g}")
            dgh = dgain[0].reshape(3, A_HEADS, A_HEAD_DIM).sum(axis=1)
            dgq.append(dgh[0])
            dgk.append(dgh[1])
            dws.append(matmul(sv["hns"][g].T, dqkv, tk=2048, name=f"l{i}_a_in_dw{g}"))
            dhns.append(matmul(dqkv, pt["a_w_in_g"][j][g], tk=3072, name=f"l{i}_a_in_dx{g}"))
        grads["a_q_norm"][j] = jnp.stack(dgq)
        grads["a_k_norm"][j] = jnp.stack(dgk)
        grads["a_w_in"][j] = jnp.concatenate(dws, axis=1)
        dh0, dg = rmsnorm_bwd(sv["h0"], p["attn_norm"][i][None], dhns, dh1, dils=DILS, name=f"l{i}_norm1_bwd")
    else:
        grads["b_w_out"][j] = matmul(sv["y"].T, dh1b, tk=2048, name=f"l{i}_b_out_dw")
        dy = matmul(dh1b, pt["b_w_out"][j], name=f"l{i}_b_out_dx")
        do, dr, dgn = gla_post_bwd(sv["o_f"], sv["o_b"], aux["b_gain"][j], sv["proj"], dy, name=f"l{i}_b_post_bwd")
        grads["b_out_norm"][j] = dgn[0].reshape(B_HEADS, B_VAL_DIM)
        pf = gla_bwd(sv["proj"], sv["loga"], sv["st_f"], do, aux["tri_f"], aux["tri_b"], False, name=f"l{i}_b_gla_f_bwd")
        pb = gla_bwd(sv["proj"], sv["loga"], sv["st_b"], do, aux["tri_b"], aux["tri_f"], True, name=f"l{i}_b_gla_b_bwd")
        dloga = jnp.concatenate([pf[3], pb[3]], axis=1)
        dz, dwblk, dbias = gate_bwd(sv["proj"], aux["b_wblk"][j], aux["b_wblk_t"][j], aux["b_bias"][j], dloga,
                                    name=f"l{i}_b_gate_bwd")
        grads["b_w_gate_f"][j] = dwblk[:B_GATE_RANK, :B_QK_WIDTH]
        grads["b_w_gate_b"][j] = dwblk[B_GATE_RANK:2 * B_GATE_RANK, B_QK_WIDTH:]
        grads["b_gate_bias_f"][j] = dbias[0, :B_QK_WIDTH]
        grads["b_gate_bias_b"][j] = dbias[0, B_QK_WIDTH:]
        dproj = gla_combine(pf[:3], pb[:3], dr, dz, name=f"l{i}_b_combine")
        grads["b_w_in"][j] = matmul(sv["hn"].T, dproj, tn=640, tk=2048, name=f"l{i}_b_in_dw")[:, :B_IN_WIDTH]
        dhn = matmul(dproj, pt["b_w_in"][j], tk=640, name=f"l{i}_b_in_dx")
        dh0, dg = rmsnorm_bwd(sv["h0"], p["attn_norm"][i][None], [dhn], dh1, name=f"l{i}_norm1_bwd")
    grads["attn_norm"][i] = dg[0]
    return dh0


def _local_step(x, target, p, small):
    S = x.shape[0]
    cos, sin = _rope_tables(S)
    to_phase = lambda t, d: t.reshape(S // d, d, LANES).swapaxes(0, 1).reshape(S, LANES)
    cos, sin = [to_phase(cos, d) for d in DILS], [to_phase(sin, d) for d in DILS]
    ones_v = jnp.ones((A_WIDTH,), F32)
    a_gain = [[jnp.concatenate([jnp.tile(small["a_q_norm"][j][g], A_HEADS), jnp.tile(small["a_k_norm"][j][g], A_HEADS),
                                ones_v])[None] for g in range(len(DILS))] for j in range(2)]
    b_wblk = [_gate_block_weight(p["b_w_gate_f"][j].astype(F32), p["b_w_gate_b"][j].astype(F32)) for j in range(2)]
    aux = dict(cos=cos, sin=sin, ones=_head_block_ones(), a_gain=a_gain, tri_f=_tri(False), tri_b=_tri(True),
               b_wblk=b_wblk, b_wblk_t=[w.T for w in b_wblk],
               b_bias=[jnp.concatenate([small["b_gate_bias_f"][j], small["b_gate_bias_b"][j]])[None] for j in range(2)],
               b_gain=[small["b_out_norm"][j].reshape(1, B_V_WIDTH) for j in range(2)])
    pw = dict(p)
    pw["b_w_in"] = jnp.pad(p["b_w_in"], ((0, 0), (0, 0), (0, B_IN_PAD - B_IN_WIDTH)))
    pw["attn_norm"], pw["ffn_norm"] = small["attn_norm"], small["ffn_norm"]
    gw = 3 * A_WIDTH
    pw["a_w_in_g"] = [[p["a_w_in"][j][:, g * gw:(g + 1) * gw] for g in range(len(DILS))] for j in range(2)]
    pt = {n: jnp.swapaxes(pw[n], 1, 2) for n in ("a_w_out", "b_w_in", "b_w_out", "ffn_w_gate_up", "ffn_w_down")}
    pt["a_w_in_g"] = [[wg.T for wg in row] for row in pw["a_w_in_g"]]

    h = x
    saved = []
    for i in range(DEPTH):
        h, sv = _layer_fwd(i, h, pw, aux)
        saved.append(sv)
    loss_sq, dh = loss_head(h, target, name="loss_head")
    grads = {n: [None] * (DEPTH if n in ("attn_norm", "ffn_norm", "ffn_w_gate_up", "ffn_w_down") else 2) for n in WEIGHTS}
    for i in reversed(range(DEPTH)):
        dh = _layer_bwd(i, dh, pw, pt, aux, saved[i], grads)
    return loss_sq[0, 0] * (0.5 / D_MODEL), dh, grads


def kernel(x, attn_norm, ffn_norm, a_w_in, a_q_norm, a_k_norm, a_w_out, b_w_in, b_w_gate_f, b_gate_bias_f, b_w_gate_b, b_gate_bias_b, b_out_norm, b_w_out, ffn_w_gate_up, ffn_w_down, loss_target, m_attn_norm, m_ffn_norm, m_a_w_in, m_a_q_norm, m_a_k_norm, m_a_w_out, m_b_w_in, m_b_w_gate_f, m_b_gate_bias_f, m_b_w_gate_b, m_b_gate_bias_b, m_b_out_norm, m_b_w_out, m_ffn_w_gate_up, m_ffn_w_down, v_attn_norm, v_ffn_norm, v_a_w_in, v_a_q_norm, v_a_k_norm, v_a_w_out, v_b_w_in, v_b_w_gate_f, v_b_gate_bias_f, v_b_w_gate_b, v_b_gate_bias_b, v_b_out_norm, v_b_w_out, v_ffn_w_gate_up, v_ffn_w_down):
    w = dict(attn_norm=attn_norm, ffn_norm=ffn_norm, a_w_in=a_w_in, a_q_norm=a_q_norm, a_k_norm=a_k_norm, a_w_out=a_w_out,
             b_w_in=b_w_in, b_w_gate_f=b_w_gate_f, b_gate_bias_f=b_gate_bias_f, b_w_gate_b=b_w_gate_b,
             b_gate_bias_b=b_gate_bias_b, b_out_norm=b_out_norm, b_w_out=b_w_out, ffn_w_gate_up=ffn_w_gate_up,
             ffn_w_down=ffn_w_down)
    m = dict(attn_norm=m_attn_norm, ffn_norm=m_ffn_norm, a_w_in=m_a_w_in, a_q_norm=m_a_q_norm, a_k_norm=m_a_k_norm,
             a_w_out=m_a_w_out, b_w_in=m_b_w_in, b_w_gate_f=m_b_w_gate_f, b_gate_bias_f=m_b_gate_bias_f,
             b_w_gate_b=m_b_w_gate_b, b_gate_bias_b=m_b_gate_bias_b, b_out_norm=m_b_out_norm, b_w_out=m_b_w_out,
             ffn_w_gate_up=m_ffn_w_gate_up, ffn_w_down=m_ffn_w_down)
    v = dict(attn_norm=v_attn_norm, ffn_norm=v_ffn_norm, a_w_in=v_a_w_in, a_q_norm=v_a_q_norm, a_k_norm=v_a_k_norm,
             a_w_out=v_a_w_out, b_w_in=v_b_w_in, b_w_gate_f=v_b_w_gate_f, b_gate_bias_f=v_b_gate_bias_f,
             b_w_gate_b=v_b_w_gate_b, b_gate_bias_b=v_b_gate_bias_b, b_out_norm=v_b_out_norm, b_w_out=v_b_w_out,
             ffn_w_gate_up=v_ffn_w_gate_up, ffn_w_down=v_ffn_w_down)

    full = _gather_weights(w)
    p = {n: full[n] for n in BIG}
    small = {n: full[n] for n in SMALL_SHARDED}
    small.update({n: w[n] for n in REPLICATED})
    loss_local, dx, grads = _local_step(x[0], loss_target[0], p, small)
    loss = lax.psum(loss_local, ("x", "y", "c"))

    g_flat = _reduce_gradients(grads, _local_shapes(w))
    g = dict(zip(WEIGHTS, _from_flat(g_flat, [w[n].shape for n in WEIGHTS])))
    delta, new_m, new_v = {}, {}, {}
    rows = lambda t: t.reshape(-1, t.shape[-1])
    for n in MATRICES:
        outs = adamw(rows(w[n]), rows(g[n]), rows(m[n]), rows(v[n]), name=f"adamw_{n}")
        delta[n], new_m[n], new_v[n] = [o.reshape(w[n].shape) for o in outs]
    rest = [n for n in WEIGHTS if n not in MATRICES]
    flat = lambda d: _to_flat([d[n] for n in rest], 8)
    outs = adamw(flat(w), flat(g), flat(m), flat(v), name="adamw_vectors")
    for d, o in zip((delta, new_m, new_v), outs):
        d.update(zip(rest, _from_flat(o, [w[n].shape for n in rest])))
    return (loss, dx[None], *[g[n] for n in WEIGHTS], *[delta[n] for n in WEIGHTS],
            *[new_m[n] for n in WEIGHTS], *[new_v[n] for n in WEIGHTS])
```

```python
import functools

import numpy as np
import jax
import jax.numpy as jnp
from jax import lax
from jax.experimental import pallas as pl
from jax.experimental.pallas import tpu as pltpu

F32, BF16 = jnp.float32, jnp.bfloat16
HI = lax.Precision.HIGHEST
MESH = pl.DeviceIdType.MESH

D_MODEL = 1024
DEPTH = 4
RMS_EPS = 1e-6
NEG_INF = -1e30
A_GROUPS = ((128, 1), (512, 4), (2048, 16))
DILS = tuple(d for _, d in A_GROUPS)
A_HALF = 64
A_HEADS = 16
A_HEAD_DIM = 64
A_WIDTH = 1024
A_IN_WIDTH = 9216
ROPE_THETA = 10000.0
B_HEADS = 4
B_KEY_DIM = 128
B_VAL_DIM = 256
B_QK_WIDTH = 512
B_V_WIDTH = 1024
B_GATE_RANK = 16
B_GATE_TAU = 16.0
B_CHUNK = 64
B_IN_WIDTH = 3104
B_IN_PAD = 3200
FFN_HIDDEN = 2816
ADAM_LR, ADAM_B1, ADAM_B2, ADAM_EPS, ADAM_WD, ADAM_STEP = 0.001, 0.9, 0.999, 1e-08, 0.01, 10
LANES = 128
VMEM_LIMIT = 48 * 1024 * 1024
FLAT_COLS = 1024
N_CHIPS = 4

WEIGHTS = ['attn_norm', 'ffn_norm', 'a_w_in', 'a_q_norm', 'a_k_norm', 'a_w_out', 'b_w_in', 'b_w_gate_f',
           'b_gate_bias_f', 'b_w_gate_b', 'b_gate_bias_b', 'b_out_norm', 'b_w_out', 'ffn_w_gate_up', 'ffn_w_down']
REPLICATED = ('attn_norm', 'ffn_norm', 'a_q_norm', 'a_k_norm')
SHARD_AXIS = {'a_w_in': 2, 'a_w_out': 1, 'b_w_in': 2, 'b_w_gate_f': 2, 'b_gate_bias_f': 1, 'b_w_gate_b': 2,
              'b_gate_bias_b': 1, 'b_out_norm': 2, 'b_w_out': 1, 'ffn_w_gate_up': 2, 'ffn_w_down': 1}
BIG = ('a_w_in', 'a_w_out', 'b_w_in', 'b_w_gate_f', 'b_w_gate_b', 'b_w_out', 'ffn_w_gate_up', 'ffn_w_down')
SMALL_SHARDED = ('b_gate_bias_f', 'b_gate_bias_b', 'b_out_norm')
MATRICES = ('a_w_in', 'a_w_out', 'b_w_in', 'b_w_out', 'ffn_w_gate_up', 'ffn_w_down')


def _params(sem):
    return pltpu.CompilerParams(dimension_semantics=sem, vmem_limit_bytes=VMEM_LIMIT)


def _tile(n, pref):
    t = min(n, pref)
    while n % t:
        t //= 2
    return t


def _const_spec(shape):
    nd = len(shape)
    return pl.BlockSpec(shape, lambda *_: (0,) * nd)


def matmul(a, b, *, name, out_dtype=F32, res=None, tm=1024, tn=512, tk=1024):
    M, K = a.shape
    N = b.shape[1]
    assert b.shape[0] == K
    tm, tn, tk = _tile(M, tm), _tile(N, tn), _tile(K, tk)
    nk = K // tk

    def body(*refs):
        a_ref, b_ref = refs[:2]
        r_ref = refs[2] if res is not None else None
        o_ref = refs[3 if res is not None else 2]
        part = jnp.dot(a_ref[...], b_ref[...], preferred_element_type=F32)

        def finish(v):
            if res is not None:
                v = v + r_ref[...]
            o_ref[...] = v.astype(o_ref.dtype)

        if nk == 1:
            finish(part)
            return
        acc_ref = refs[-1]
        k = pl.program_id(2)

        @pl.when(k == 0)
        def _():
            acc_ref[...] = part

        @pl.when((k > 0) & (k < nk - 1))
        def _():
            acc_ref[...] += part

        @pl.when(k == nk - 1)
        def _():
            finish(acc_ref[...] + part)

    in_specs = [pl.BlockSpec((tm, tk), lambda i, j, k: (i, k)), pl.BlockSpec((tk, tn), lambda i, j, k: (k, j))]
    args = [a, b]
    if res is not None:
        in_specs.append(pl.BlockSpec((tm, tn), lambda i, j, k: (i, j)))
        args.append(res)
    return pl.pallas_call(
        body, name=name, grid=(M // tm, N // tn, nk), in_specs=in_specs,
        out_specs=pl.BlockSpec((tm, tn), lambda i, j, k: (i, j)),
        out_shape=jax.ShapeDtypeStruct((M, N), out_dtype),
        scratch_shapes=[pltpu.VMEM((tm, tn), F32)] if nk > 1 else [],
        compiler_params=_params(("parallel", "parallel", "arbitrary")),
    )(*args)


def _phase_spec(d, ts, W):
    if d == 1:
        return pl.BlockSpec((ts, W), lambda i: (i, 0))
    return pl.BlockSpec((d, ts // d, W), lambda i: (0, i, 0))


def _phase_view(a, d):
    return a if d == 1 else a.reshape(d, a.shape[0] // d, a.shape[1])


def _phase_shape(S, W, d, dtype):
    return jax.ShapeDtypeStruct((S, W) if d == 1 else (d, S // d, W), dtype)


def _nat_scratch(ts, W):
    return pltpu.VMEM((W // LANES, ts, LANES), F32)


def _put_natural(nat_ref, value):
    for c in range(nat_ref.shape[0]):
        nat_ref[c] = value[:, c * LANES:(c + 1) * LANES]


def _get_natural(nat_ref):
    return jnp.concatenate([nat_ref[c] for c in range(nat_ref.shape[0])], axis=1)


def _store_phases(nat_ref, o_ref, d, ts):
    for p in range(d):
        for c in range(nat_ref.shape[0]):
            o_ref[p, :, c * LANES:(c + 1) * LANES] = nat_ref[c, pl.ds(p, ts // d, stride=d), :].astype(o_ref.dtype)


def _load_phases(i_ref, nat_ref, d, ts):
    for p in range(d):
        for c in range(nat_ref.shape[0]):
            nat_ref[c, pl.ds(p, ts // d, stride=d), :] = i_ref[p, :, c * LANES:(c + 1) * LANES].astype(F32)


def rmsnorm_fwd(x, gain, *, name, dils=(1,)):
    S, Dm = x.shape
    ts = _tile(S, 512)

    def body(x_ref, g_ref, *rest):
        o_refs, scr = rest[:len(dils)], rest[len(dils)]
        xv = x_ref[...]
        r = lax.rsqrt(jnp.mean(xv * xv, axis=-1, keepdims=True) + RMS_EPS)
        y = (xv * r) * g_ref[...]
        if any(d > 1 for d in dils):
            _put_natural(scr, y)
        for d, o_ref in zip(dils, o_refs):
            if d == 1:
                o_ref[...] = y.astype(o_ref.dtype)
            else:
                _store_phases(scr, o_ref, d, ts)

    outs = pl.pallas_call(
        body, name=name, grid=(S // ts,),
        in_specs=[pl.BlockSpec((ts, Dm), lambda i: (i, 0)), _const_spec((1, Dm))],
        out_specs=[_phase_spec(d, ts, Dm) for d in dils],
        out_shape=[_phase_shape(S, Dm, d, BF16) for d in dils],
        scratch_shapes=[_nat_scratch(ts, Dm)],
        compiler_params=_params(("parallel",)),
    )(x, gain)
    return [o.reshape(S, Dm) for o in outs]


def rmsnorm_bwd(x, gain, dys, dres, *, name, dils=(1,)):
    S, Dm = x.shape
    ts = _tile(S, 256)
    nd = len(dils)

    def body(x_ref, g_ref, *rest):
        dy_refs, (dr_ref, dx_ref, dg_ref, scr) = rest[:nd], rest[nd:]

        @pl.when(pl.program_id(0) == 0)
        def _():
            dg_ref[...] = jnp.zeros_like(dg_ref)

        dyv = None
        for d, dy_ref in zip(dils, dy_refs):
            if d == 1:
                t = dy_ref[...].astype(F32)
            else:
                _load_phases(dy_ref, scr, d, ts)
                t = _get_natural(scr)
            dyv = t if dyv is None else dyv + t
        xv = x_ref[...]
        r = lax.rsqrt(jnp.mean(xv * xv, axis=-1, keepdims=True) + RMS_EPS)
        xhat = xv * r
        dyg = dyv * g_ref[...]
        dx = r * (dyg - xhat * jnp.mean(dyg * xhat, axis=-1, keepdims=True))
        dx_ref[...] = dr_ref[...] + dx
        dg_ref[0:1, :] += jnp.sum(dyv * xhat, axis=0, keepdims=True)

    row = pl.BlockSpec((ts, Dm), lambda i: (i, 0))
    return pl.pallas_call(
        body, name=name, grid=(S // ts,),
        in_specs=[row, _const_spec((1, Dm))] + [_phase_spec(d, ts, Dm) for d in dils] + [row],
        out_specs=[row, _const_spec((8, Dm))],
        out_shape=[jax.ShapeDtypeStruct((S, Dm), F32), jax.ShapeDtypeStruct((8, Dm), F32)],
        scratch_shapes=[_nat_scratch(ts, Dm)],
        compiler_params=_params(("arbitrary",)),
    )(x, gain, *[_phase_view(dy, d) for dy, d in zip(dys, dils)], dres)


def swiglu_fwd(gu, *, name):
    S, F2 = gu.shape
    Fh = F2 // 2
    ts = _tile(S, 512)

    def body(g_ref, u_ref, o_ref):
        g = g_ref[...].astype(F32)
        u = u_ref[...].astype(F32)
        o_ref[...] = (g * (1.0 / (1.0 + jnp.exp(-g))) * u).astype(o_ref.dtype)

    return pl.pallas_call(
        body, name=name, grid=(S // ts,),
        in_specs=[pl.BlockSpec((ts, Fh), lambda i: (i, 0)), pl.BlockSpec((ts, Fh), lambda i: (i, 1))],
        out_specs=pl.BlockSpec((ts, Fh), lambda i: (i, 0)),
        out_shape=jax.ShapeDtypeStruct((S, Fh), BF16),
        compiler_params=_params(("parallel",)),
    )(gu, gu)


def swiglu_bwd(gu, dact, *, name):
    S, F2 = gu.shape
    Fh = F2 // 2
    ts = _tile(S, 256)

    def body(gu_ref, d_ref, o_ref):
        g = gu_ref[:, :Fh].astype(F32)
        u = gu_ref[:, Fh:].astype(F32)
        d = d_ref[...].astype(F32)
        sig = 1.0 / (1.0 + jnp.exp(-g))
        o_ref[:, :Fh] = (d * u * (sig * (1.0 + g * (1.0 - sig)))).astype(o_ref.dtype)
        o_ref[:, Fh:] = (d * (g * sig)).astype(o_ref.dtype)

    return pl.pallas_call(
        body, name=name, grid=(S // ts,),
        in_specs=[pl.BlockSpec((ts, F2), lambda i: (i, 0)), pl.BlockSpec((ts, Fh), lambda i: (i, 0))],
        out_specs=pl.BlockSpec((ts, F2), lambda i: (i, 0)),
        out_shape=jax.ShapeDtypeStruct((S, F2), BF16),
        compiler_params=_params(("parallel",)),
    )(gu, dact)


def loss_head(y, target, *, name):
    S, Dm = y.shape
    ts = _tile(S, 512)

    def body(y_ref, t_ref, l_ref, d_ref):
        @pl.when(pl.program_id(0) == 0)
        def _():
            l_ref[...] = jnp.zeros_like(l_ref)

        e = y_ref[...] - t_ref[...]
        d_ref[...] = e * (1.0 / Dm)
        l_ref[...] += jnp.sum(e * e)

    return pl.pallas_call(
        body, name=name, grid=(S // ts,),
        in_specs=[pl.BlockSpec((ts, Dm), lambda i: (i, 0)), pl.BlockSpec((ts, Dm), lambda i: (i, 0))],
        out_specs=[_const_spec((8, LANES)), pl.BlockSpec((ts, Dm), lambda i: (i, 0))],
        out_shape=[jax.ShapeDtypeStruct((8, LANES), F32), jax.ShapeDtypeStruct((S, Dm), F32)],
        compiler_params=_params(("arbitrary",)),
    )(y, target)


def _head_block_ones():
    i = np.arange(LANES)
    return jnp.asarray((i[:, None] // A_HEAD_DIM == i[None, :] // A_HEAD_DIM).astype(np.float32))


def _rope_tables(S):
    half = A_HEAD_DIM // 2
    inv_freq = ROPE_THETA ** (-jnp.arange(half, dtype=F32) / half)
    ang = jnp.arange(S).astype(F32)[:, None] * inv_freq[None, :]
    cos = jnp.tile(jnp.cos(ang), (1, LANES // half))
    sin = jnp.tile(jnp.sin(ang), (1, LANES // half))
    return cos, sin


def _rot_half(x, lo):
    return jnp.where(lo, -pltpu.roll(x, LANES - 32, 1), pltpu.roll(x, 32, 1))


def _seg_mean(v, ones_ref):
    return jnp.dot(v, ones_ref[...], precision=HI, preferred_element_type=F32) * (1.0 / A_HEAD_DIM)


def qk_prep_fwd(qkv, gain, cos, sin, ones, *, name):
    S, W = qkv.shape
    ts = _tile(S, 256)
    nchunk = A_WIDTH // LANES

    def body(x_ref, g_ref, c_ref, s_ref, ones_ref, o_ref):
        kind = pl.program_id(1) % 3

        @pl.when(kind < 2)
        def _():
            scale = jnp.where(kind == 0, A_HEAD_DIM ** -0.5, 1.0).astype(F32)
            lo = (lax.broadcasted_iota(jnp.int32, (ts, LANES), 1) % A_HEAD_DIM) < (A_HEAD_DIM // 2)
            cv, sv = c_ref[...], s_ref[...]
            for c in range(nchunk):
                sl = slice(c * LANES, (c + 1) * LANES)
                xv = x_ref[:, sl]
                r = lax.rsqrt(_seg_mean(xv * xv, ones_ref) + RMS_EPS)
                y = (xv * r) * g_ref[:, sl]
                y = y * cv + _rot_half(y, lo) * sv
                o_ref[:, sl] = (y * scale).astype(o_ref.dtype)

        @pl.when(kind == 2)
        def _():
            o_ref[...] = x_ref[...].astype(o_ref.dtype)

    return pl.pallas_call(
        body, name=name, grid=(S // ts, W // A_WIDTH),
        in_specs=[pl.BlockSpec((ts, A_WIDTH), lambda i, j: (i, j)), pl.BlockSpec((1, A_WIDTH), lambda i, j: (0, j)),
                  pl.BlockSpec((ts, LANES), lambda i, j: (i, 0)), pl.BlockSpec((ts, LANES), lambda i, j: (i, 0)),
                  _const_spec((LANES, LANES))],
        out_specs=pl.BlockSpec((ts, A_WIDTH), lambda i, j: (i, j)),
        out_shape=jax.ShapeDtypeStruct((S, W), BF16),
        compiler_params=_params(("parallel", "arbitrary")),
    )(qkv, gain, cos, sin, ones)


def qk_prep_bwd(qkv, gain, cos, sin, ones, grads, *, name):
    S, W = qkv.shape
    ts = _tile(S, 256)
    nchunk = A_WIDTH // LANES
    nj = W // A_WIDTH

    def body(x_ref, g_ref, c_ref, s_ref, ones_ref, *rest):
        g_refs, (o_ref, dg_ref) = rest[:nj], rest[nj:]
        j = pl.program_id(0)
        kind = j % 3

        @pl.when(pl.program_id(1) == 0)
        def _():
            dg_ref[...] = jnp.zeros_like(dg_ref)

        for n in range(nj):
            @pl.when(j == n)
            def _(n=n):
                d_ref = g_refs[n]
                if n % 3 == 2:
                    o_ref[...] = d_ref[...].astype(o_ref.dtype)
                    return
                scale = A_HEAD_DIM ** -0.5 if n % 3 == 0 else 1.0
                lo = (lax.broadcasted_iota(jnp.int32, (ts, LANES), 1) % A_HEAD_DIM) < (A_HEAD_DIM // 2)
                cv, sv = c_ref[...], s_ref[...]
                for c in range(nchunk):
                    sl = slice(c * LANES, (c + 1) * LANES)
                    dy = d_ref[:, sl].astype(F32) * scale
                    dn = dy * cv - _rot_half(dy, lo) * sv
                    xv = x_ref[:, sl]
                    r = lax.rsqrt(_seg_mean(xv * xv, ones_ref) + RMS_EPS)
                    xhat = xv * r
                    dyg = dn * g_ref[:, sl]
                    dx = r * (dyg - xhat * _seg_mean(dyg * xhat, ones_ref))
                    o_ref[:, sl] = dx.astype(o_ref.dtype)
                    dg_ref[0:1, sl] += jnp.sum(dn * xhat, axis=0, keepdims=True)

    def gspec(n):
        return pl.BlockSpec((ts, A_WIDTH), lambda j, i: (jnp.where(j == n, i, 0), 0))

    return pl.pallas_call(
        body, name=name, grid=(nj, S // ts),
        in_specs=[pl.BlockSpec((ts, A_WIDTH), lambda j, i: (i, j)), pl.BlockSpec((1, A_WIDTH), lambda j, i: (0, j)),
                  pl.BlockSpec((ts, LANES), lambda j, i: (i, 0)), pl.BlockSpec((ts, LANES), lambda j, i: (i, 0)),
                  _const_spec((LANES, LANES))] + [gspec(n) for n in range(nj)],
        out_specs=[pl.BlockSpec((ts, A_WIDTH), lambda j, i: (i, j)), pl.BlockSpec((8, A_WIDTH), lambda j, i: (0, j))],
        out_shape=[jax.ShapeDtypeStruct((S, W), BF16), jax.ShapeDtypeStruct((8, W), F32)],
        compiler_params=_params(("arbitrary", "arbitrary")),
    )(qkv, gain, cos, sin, ones, *grads)


def _band_specs(kind, tq, nlb):
    nhb = tq // A_HALF
    nb = nlb // nhb
    base = kind * (A_WIDTH // LANES)
    return [pl.BlockSpec((A_HALF, LANES), lambda ph, b, hp: (ph * nlb + jnp.maximum(b * nhb - 1, 0), base + hp)),
            pl.BlockSpec((tq, LANES), lambda ph, b, hp: (ph * nb + b, base + hp)),
            pl.BlockSpec((A_HALF, LANES), lambda ph, b, hp: (ph * nlb + jnp.minimum((b + 1) * nhb, nlb - 1), base + hp))]


A_BLOCK = 512
A_SUB = 128


def _band_bias(sub, key_major):
    i = np.arange(sub)[:, None]
    j = np.arange(sub + 2 * A_HALF)[None, :] - A_HALF
    ok = np.abs(j - i) <= A_HALF
    return jnp.asarray(np.where(ok.T if key_major else ok, 0.0, NEG_INF).astype(np.float32))


def _edge_bias(first, n, L, axis, at_start, at_end):
    if not (at_start or at_end):
        return None
    shape = (1, n) if axis == 1 else (n, 1)
    pos = first - A_HALF + lax.broadcasted_iota(jnp.int32, shape, axis)
    return jnp.where((pos < 0) | (pos >= L), NEG_INF, 0.0).astype(F32)


def _with_edge(bias, edge):
    return bias if edge is None else bias + edge


def _cat3(a_ref, b_ref, c_ref):
    return jnp.concatenate([a_ref[...], b_ref[...], c_ref[...]], axis=0)


def _lane_lo(rows):
    return lax.broadcasted_iota(jnp.int32, (rows, LANES), 1) < A_HEAD_DIM


NT = (((1,), (1,)), ((), ()))
TN = (((0,), (0,)), ((), ()))


def attn_fwd(qkvp, dil, *, name):
    S = qkvp.shape[0]
    L = S // dil
    tq = _tile(L, A_BLOCK)
    sub = min(tq, A_SUB)
    nsub = tq // sub
    nlb = L // A_HALF
    band = _band_bias(sub, False)

    def body(q_ref, kp_ref, ko_ref, kn_ref, vp_ref, vo_ref, vn_ref, band_ref, o_ref, l_ref):
        b = pl.program_id(1)
        K = _cat3(kp_ref, ko_ref, kn_ref)
        V = _cat3(vp_ref, vo_ref, vn_ref)
        lo_k = _lane_lo(tq + 2 * A_HALF)
        lo_q = _lane_lo(sub)
        Km = [jnp.where(sel, K, jnp.zeros_like(K)) for sel in (lo_k, ~lo_k)]
        Vm = [jnp.where(sel, V, jnp.zeros_like(V)) for sel in (lo_k, ~lo_k)]
        for r in range(nsub):
            rows = slice(r * sub, (r + 1) * sub)
            keys = slice(r * sub, (r + 1) * sub + 2 * A_HALF)
            bias = _with_edge(band_ref[...], _edge_bias(b * tq + r * sub, sub + 2 * A_HALF, L, 1, r == 0, r == nsub - 1))
            q = q_ref[rows, :]
            outs, lses = [], []
            for hh in range(2):
                s = lax.dot_general(q, Km[hh][keys], NT, preferred_element_type=F32) + bias
                m = jnp.max(s, axis=1, keepdims=True)
                p = jnp.exp(s - m)
                l = jnp.sum(p, axis=1, keepdims=True)
                outs.append(jnp.dot(p.astype(BF16), Vm[hh][keys], preferred_element_type=F32) * (1.0 / l))
                lses.append(m + jnp.log(l))
            o_ref[rows, :] = outs[0] + outs[1]
            l_ref[rows, :] = jnp.where(lo_q, lses[0], lses[1])

    ospec = _band_specs(0, tq, nlb)[1]
    return pl.pallas_call(
        body, name=name, grid=(dil, L // tq, A_WIDTH // LANES),
        in_specs=[_band_specs(0, tq, nlb)[1]] + _band_specs(1, tq, nlb) + _band_specs(2, tq, nlb)
        + [_const_spec(band.shape)],
        out_specs=[ospec, ospec],
        out_shape=[jax.ShapeDtypeStruct((S, A_WIDTH), F32)] * 2,
        compiler_params=_params(("parallel", "parallel", "parallel")),
    )(*([qkvp] * 7), band)


def attn_merge(os_, lses, dils, *, name):
    S = os_[0].shape[0]
    ts = _tile(S, 256)
    ng = len(dils)

    def body(*refs):
        o_refs, l_refs, out_ref = refs[:ng], refs[ng:2 * ng], refs[2 * ng]
        lse_refs, scrs = refs[2 * ng + 1:3 * ng + 1], refs[3 * ng + 1:]
        ov, ls, k = [], [], 0
        for d, o_ref, l_ref in zip(dils, o_refs, l_refs):
            if d == 1:
                ov.append(o_ref[...])
                ls.append(l_ref[...])
            else:
                _load_phases(o_ref, scrs[k], d, ts)
                _load_phases(l_ref, scrs[k + 1], d, ts)
                ov.append(_get_natural(scrs[k]))
                ls.append(_get_natural(scrs[k + 1]))
                k += 2
        m = functools.reduce(jnp.maximum, ls)
        es = [jnp.exp(l - m) for l in ls]
        tot = functools.reduce(jnp.add, es)
        acc = None
        for e, o in zip(es, ov):
            t = (e / tot) * o
            acc = t if acc is None else acc + t
        out_ref[...] = acc.astype(out_ref.dtype)
        total = m + jnp.log(tot)
        _put_natural(scrs[k], total)
        for d, lse_ref in zip(dils, lse_refs):
            if d == 1:
                lse_ref[...] = total
            else:
                _store_phases(scrs[k], lse_ref, d, ts)

    n_scr = 2 * sum(d > 1 for d in dils) + 1
    outs = pl.pallas_call(
        body, name=name, grid=(S // ts,),
        in_specs=[_phase_spec(d, ts, A_WIDTH) for d in dils] * 2,
        out_specs=[pl.BlockSpec((ts, A_WIDTH), lambda i: (i, 0))] + [_phase_spec(d, ts, A_WIDTH) for d in dils],
        out_shape=[jax.ShapeDtypeStruct((S, A_WIDTH), BF16)] + [_phase_shape(S, A_WIDTH, d, F32) for d in dils],
        scratch_shapes=[_nat_scratch(ts, A_WIDTH)] * n_scr,
        compiler_params=_params(("parallel",)),
    )(*[_phase_view(o, d) for o, d in zip(os_, dils)], *[_phase_view(l, d) for l, d in zip(lses, dils)])
    return outs[0], [l.reshape(S, A_WIDTH) for l in outs[1:]]


def attn_delta(dout, out, ones, dils, *, name):
    S = dout.shape[0]
    ts = _tile(S, 256)
    nd = len(dils)

    def body(d_ref, o_ref, ones_ref, *rest):
        do_refs, dl_refs, (scr_do, scr_dl) = rest[:nd], rest[nd:2 * nd], rest[2 * nd:]
        sums = []
        for c in range(A_WIDTH // LANES):
            sl = slice(c * LANES, (c + 1) * LANES)
            prod = d_ref[:, sl].astype(F32) * o_ref[:, sl].astype(F32)
            sums.append(jnp.dot(prod, ones_ref[...], precision=HI, preferred_element_type=F32))
        _put_natural(scr_do, d_ref[...].astype(F32))
        _put_natural(scr_dl, jnp.concatenate(sums, axis=1))
        for d, do_ref, dl_ref in zip(dils, do_refs, dl_refs):
            if d == 1:
                do_ref[...] = d_ref[...]
                dl_ref[...] = _get_natural(scr_dl)
            else:
                _store_phases(scr_do, do_ref, d, ts)
                _store_phases(scr_dl, dl_ref, d, ts)

    spec = pl.BlockSpec((ts, A_WIDTH), lambda i: (i, 0))
    outs = pl.pallas_call(
        body, name=name, grid=(S // ts,), in_specs=[spec, spec, _const_spec((LANES, LANES))],
        out_specs=[_phase_spec(d, ts, A_WIDTH) for d in dils] * 2,
        out_shape=[_phase_shape(S, A_WIDTH, d, BF16) for d in dils] + [_phase_shape(S, A_WIDTH, d, F32) for d in dils],
        scratch_shapes=[_nat_scratch(ts, A_WIDTH)] * 2,
        compiler_params=_params(("parallel",)),
    )(dout, out, ones)
    outs = [o.reshape(S, A_WIDTH) for o in outs]
    return outs[:nd], outs[nd:]


def _head_col(x, hh):
    c = hh * A_HEAD_DIM
    return x[:, c:c + 1]


def attn_bwd_dq(qkvp, dout, lse, delta, dil, *, name):
    S = qkvp.shape[0]
    L = S // dil
    tq = _tile(L, A_BLOCK)
    sub = min(tq, A_SUB)
    nsub = tq // sub
    nlb = L // A_HALF
    band = _band_bias(sub, False)

    def body(q_ref, kp_ref, ko_ref, kn_ref, vp_ref, vo_ref, vn_ref, do_ref, l_ref, d_ref, band_ref, dq_ref):
        b = pl.program_id(1)
        K = _cat3(kp_ref, ko_ref, kn_ref)
        V = _cat3(vp_ref, vo_ref, vn_ref)
        lo_k = _lane_lo(tq + 2 * A_HALF)
        Km = [jnp.where(sel, K, jnp.zeros_like(K)) for sel in (lo_k, ~lo_k)]
        Vm = [jnp.where(sel, V, jnp.zeros_like(V)) for sel in (lo_k, ~lo_k)]
        for r in range(nsub):
            rows = slice(r * sub, (r + 1) * sub)
            keys = slice(r * sub, (r + 1) * sub + 2 * A_HALF)
            bias = _with_edge(band_ref[...], _edge_bias(b * tq + r * sub, sub + 2 * A_HALF, L, 1, r == 0, r == nsub - 1))
            q, do = q_ref[rows, :], do_ref[rows, :]
            lse_v, dl_v = l_ref[rows, :], d_ref[rows, :]
            acc = None
            for hh in range(2):
                s = lax.dot_general(q, Km[hh][keys], NT, preferred_element_type=F32) + bias
                p = jnp.exp(s - _head_col(lse_v, hh))
                dp = lax.dot_general(do, Vm[hh][keys], NT, preferred_element_type=F32)
                ds = p * (dp - _head_col(dl_v, hh))
                t = jnp.dot(ds.astype(BF16), Km[hh][keys], preferred_element_type=F32)
                acc = t if acc is None else acc + t
            dq_ref[rows, :] = acc.astype(dq_ref.dtype)

    nspec = _band_specs(0, tq, nlb)[1]
    return pl.pallas_call(
        body, name=name, grid=(dil, L // tq, A_WIDTH // LANES),
        in_specs=[nspec] + _band_specs(1, tq, nlb) + _band_specs(2, tq, nlb) + [nspec, nspec, nspec, _const_spec(band.shape)],
        out_specs=nspec,
        out_shape=jax.ShapeDtypeStruct((S, A_WIDTH), BF16),
        compiler_params=_params(("parallel", "parallel", "parallel")),
    )(*([qkvp] * 7), dout, lse, delta, band)


def attn_bwd_dkv(qkvp, dout, lse, delta, dil, *, name):
    S = qkvp.shape[0]
    L = S // dil
    tk = _tile(L, A_BLOCK)
    sub = min(tk, A_SUB)
    nsub = tk // sub
    nlb = L // A_HALF
    band = _band_bias(sub, True)

    def body(qp_ref, qo_ref, qn_ref, k_ref, v_ref, dp_ref, do_ref, dn_ref, lp_ref, lo_ref, ln_ref,
             ep_ref, eo_ref, en_ref, band_ref, dk_ref, dv_ref):
        b = pl.program_id(1)
        Q = _cat3(qp_ref, qo_ref, qn_ref)
        DO = _cat3(dp_ref, do_ref, dn_ref)
        lse_v = _cat3(lp_ref, lo_ref, ln_ref)
        dl_v = _cat3(ep_ref, eo_ref, en_ref)
        lo_q = _lane_lo(tk + 2 * A_HALF)
        Qm = [jnp.where(sel, Q, jnp.zeros_like(Q)) for sel in (lo_q, ~lo_q)]
        DOm = [jnp.where(sel, DO, jnp.zeros_like(DO)) for sel in (lo_q, ~lo_q)]
        for r in range(nsub):
            keys = slice(r * sub, (r + 1) * sub)
            qs = slice(r * sub, (r + 1) * sub + 2 * A_HALF)
            bias = _with_edge(band_ref[...], _edge_bias(b * tk + r * sub, sub + 2 * A_HALF, L, 0, r == 0, r == nsub - 1))
            K, V = k_ref[keys, :], v_ref[keys, :]
            dk = dv = None
            for hh in range(2):
                s = lax.dot_general(Qm[hh][qs], K, NT, preferred_element_type=F32) + bias
                p = jnp.exp(s - _head_col(lse_v[qs], hh))
                dp = lax.dot_general(DOm[hh][qs], V, NT, preferred_element_type=F32)
                ds = p * (dp - _head_col(dl_v[qs], hh))
                tv = lax.dot_general(p.astype(BF16), DOm[hh][qs], TN, preferred_element_type=F32)
                tk_ = lax.dot_general(ds.astype(BF16), Qm[hh][qs], TN, preferred_element_type=F32)
                dv = tv if dv is None else dv + tv
                dk = tk_ if dk is None else dk + tk_
            dk_ref[keys, :] = dk.astype(dk_ref.dtype)
            dv_ref[keys, :] = dv.astype(dv_ref.dtype)

    nspec = _band_specs(0, tk, nlb)[1]
    return pl.pallas_call(
        body, name=name, grid=(dil, L // tk, A_WIDTH // LANES),
        in_specs=_band_specs(0, tk, nlb) + [_band_specs(1, tk, nlb)[1], _band_specs(2, tk, nlb)[1]]
        + _band_specs(0, tk, nlb) * 3 + [_const_spec(band.shape)],
        out_specs=[nspec, nspec],
        out_shape=[jax.ShapeDtypeStruct((S, A_WIDTH), BF16)] * 2,
        compiler_params=_params(("parallel", "parallel", "parallel")),
    )(*([qkvp] * 5), *([dout] * 3), *([lse] * 3), *([delta] * 3), band)


def _gate_block_weight(wf, wb):
    w = jnp.zeros((LANES, 2 * B_QK_WIDTH), F32)
    w = w.at[:B_GATE_RANK, :B_QK_WIDTH].set(wf)
    w = w.at[B_GATE_RANK:2 * B_GATE_RANK, B_QK_WIDTH:].set(wb)
    return w.astype(BF16)


def gate_fwd(proj, wblk, bias, *, name):
    S = proj.shape[0]
    ts = _tile(S, 512)
    W = 2 * B_QK_WIDTH
    zcol = (2 * B_QK_WIDTH + 2 * B_V_WIDTH) // LANES

    def body(z_ref, w_ref, b_ref, o_ref):
        x = jnp.dot(z_ref[...].astype(BF16), w_ref[...], preferred_element_type=F32) + b_ref[...]
        o_ref[...] = (jnp.minimum(x, 0.0) - jnp.log(1.0 + jnp.exp(-jnp.abs(x)))) * (1.0 / B_GATE_TAU)

    return pl.pallas_call(
        body, name=name, grid=(S // ts,),
        in_specs=[pl.BlockSpec((ts, LANES), lambda i: (i, zcol)), _const_spec((LANES, W)), _const_spec((1, W))],
        out_specs=pl.BlockSpec((ts, W), lambda i: (i, 0)),
        out_shape=jax.ShapeDtypeStruct((S, W), F32),
        compiler_params=_params(("parallel",)),
    )(proj, wblk, bias)


def gate_bwd(proj, wblk, wblk_t, bias, dloga, *, name):
    S = proj.shape[0]
    ts = _tile(S, 512)
    W = 2 * B_QK_WIDTH
    zcol = (2 * B_QK_WIDTH + 2 * B_V_WIDTH) // LANES

    def body(z_ref, w_ref, wt_ref, b_ref, d_ref, dz_ref, dw_ref, db_ref):
        @pl.when(pl.program_id(0) == 0)
        def _():
            dw_ref[...] = jnp.zeros_like(dw_ref)
            db_ref[...] = jnp.zeros_like(db_ref)

        z = z_ref[...].astype(BF16)
        x = jnp.dot(z, w_ref[...], preferred_element_type=F32) + b_ref[...]
        e = jnp.exp(-jnp.abs(x))
        sig_neg = jnp.where(x >= 0, e, 1.0) / (1.0 + e)
        dx = d_ref[...] * (1.0 / B_GATE_TAU) * sig_neg
        dxb = dx.astype(BF16)
        dz_ref[...] = jnp.dot(dxb, wt_ref[...], preferred_element_type=F32)
        dw_ref[...] += lax.dot_general(z, dxb, TN, preferred_element_type=F32)
        db_ref[0:1, :] += jnp.sum(dx, axis=0, keepdims=True)

    return pl.pallas_call(
        body, name=name, grid=(S // ts,),
        in_specs=[pl.BlockSpec((ts, LANES), lambda i: (i, zcol)), _const_spec((LANES, W)), _const_spec((W, LANES)),
                  _const_spec((1, W)), pl.BlockSpec((ts, W), lambda i: (i, 0))],
        out_specs=[pl.BlockSpec((ts, LANES), lambda i: (i, 0)), _const_spec((LANES, W)), _const_spec((8, W))],
        out_shape=[jax.ShapeDtypeStruct((S, LANES), F32), jax.ShapeDtypeStruct((LANES, W), F32),
                   jax.ShapeDtypeStruct((8, W), F32)],
        compiler_params=_params(("arbitrary",)),
    )(proj, wblk, wblk_t, bias, dloga)


def _tri(reverse):
    i = np.arange(B_CHUNK)
    t = (i[None, :] >= i[:, None]) if reverse else (i[None, :] <= i[:, None])
    return jnp.asarray(t.astype(np.float32))


def _gla_terms(q, k, la, t_ref, reverse):
    b = jnp.dot(t_ref[...], la, precision=HI, preferred_element_type=F32)
    b_last = b[0:1, :] if reverse else b[B_CHUNK - 1:B_CHUNK, :]
    e_b = jnp.exp(b)
    qt = (q * (B_KEY_DIM ** -0.5)) * e_b
    e_nb = jnp.exp(-b)
    kt = k * e_nb
    e_end = jnp.exp(b_last - b)
    kend = k * e_end
    dec = jnp.exp(b_last)
    return e_nb, e_b, qt, kt, e_end, kend, dec


def _chunk_mask(reverse, transpose=False):
    r = lax.broadcasted_iota(jnp.int32, (B_CHUNK, B_CHUNK), 0)
    c = lax.broadcasted_iota(jnp.int32, (B_CHUNK, B_CHUNK), 1)
    if transpose:
        r, c = c, r
    return (c > r) if reverse else (c <= r)


def gla_fwd(proj, loga, tmat, reverse, *, name):
    S = proj.shape[0]
    tb = _tile(S, 512)
    nb = S // tb
    cpb = tb // B_CHUNK
    nc = S // B_CHUNK
    qb, kb_, vb = 0, B_QK_WIDTH // B_KEY_DIM, 2 * B_QK_WIDTH // B_VAL_DIM
    lb = (B_QK_WIDTH // B_KEY_DIM) if reverse else 0
    blk = (lambda i: nb - 1 - i) if reverse else (lambda i: i)

    def body(q_ref, k_ref, v_ref, la_ref, t_ref, o_ref, st_ref, s_scr):
        @pl.when(pl.program_id(1) == 0)
        def _():
            s_scr[...] = jnp.zeros_like(s_scr)

        mask = _chunk_mask(reverse)
        order = range(cpb - 1, -1, -1) if reverse else range(cpb)
        for c in order:
            rows = slice(c * B_CHUNK, (c + 1) * B_CHUNK)
            v = v_ref[rows, :].astype(BF16)
            _, _, qt, kt, _, kend, dec = _gla_terms(q_ref[rows, :], k_ref[rows, :], la_ref[rows, :], t_ref, reverse)
            qt, kt, kend = qt.astype(BF16), kt.astype(BF16), kend.astype(BF16)
            st = s_scr[...]
            st_ref[0, c] = st
            a = jnp.where(mask, lax.dot_general(qt, kt, NT, preferred_element_type=F32), 0.0)
            o = jnp.dot(a.astype(BF16), v, preferred_element_type=F32)
            o = o + lax.dot_general(qt, st.astype(BF16), NT, preferred_element_type=F32)
            o_ref[rows, :] = o
            s_scr[...] = st * dec + lax.dot_general(v, kend, TN, preferred_element_type=F32)

    return pl.pallas_call(
        body, name=name, grid=(B_HEADS, nb),
        in_specs=[pl.BlockSpec((tb, B_KEY_DIM), lambda h, i: (blk(i), qb + h)),
                  pl.BlockSpec((tb, B_KEY_DIM), lambda h, i: (blk(i), kb_ + h)),
                  pl.BlockSpec((tb, B_VAL_DIM), lambda h, i: (blk(i), vb + h)),
                  pl.BlockSpec((tb, B_KEY_DIM), lambda h, i: (blk(i), lb + h)),
                  _const_spec((B_CHUNK, B_CHUNK))],
        out_specs=[pl.BlockSpec((tb, B_VAL_DIM), lambda h, i: (blk(i), h)),
                   pl.BlockSpec((1, cpb, B_VAL_DIM, B_KEY_DIM), lambda h, i: (h, blk(i), 0, 0))],
        out_shape=[jax.ShapeDtypeStruct((S, B_V_WIDTH), F32),
                   jax.ShapeDtypeStruct((B_HEADS, nc, B_VAL_DIM, B_KEY_DIM), F32)],
        scratch_shapes=[pltpu.VMEM((B_VAL_DIM, B_KEY_DIM), F32)],
        compiler_params=_params(("parallel", "arbitrary")),
    )(proj, proj, proj, loga, tmat)


def gla_bwd(proj, loga, states, do, tmat, tmat_t, reverse, *, name):
    S = proj.shape[0]
    tb = _tile(S, 512)
    nb = S // tb
    cpb = tb // B_CHUNK
    qb, kb_, vb = 0, B_QK_WIDTH // B_KEY_DIM, 2 * B_QK_WIDTH // B_VAL_DIM
    lb = (B_QK_WIDTH // B_KEY_DIM) if reverse else 0
    blk = (lambda i: i) if reverse else (lambda i: nb - 1 - i)
    scale = B_KEY_DIM ** -0.5

    def body(q_ref, k_ref, v_ref, la_ref, st_ref, do_ref, t_ref, tt_ref, dq_ref, dk_ref, dv_ref, dla_ref, ds_scr):
        @pl.when(pl.program_id(1) == 0)
        def _():
            ds_scr[...] = jnp.zeros_like(ds_scr)

        mask = _chunk_mask(reverse)
        mask_t = _chunk_mask(reverse, transpose=True)
        last = 0 if reverse else B_CHUNK - 1
        is_last = lax.broadcasted_iota(jnp.int32, (B_CHUNK, B_KEY_DIM), 0) == last
        order = range(cpb) if reverse else range(cpb - 1, -1, -1)
        for c in order:
            rows = slice(c * B_CHUNK, (c + 1) * B_CHUNK)
            vf = v_ref[rows, :]
            v = vf.astype(BF16)
            dov = do_ref[rows, :]
            dob = dov.astype(BF16)
            e_nb, e_b, qt, kt, e_end, kend, dec = _gla_terms(q_ref[rows, :], k_ref[rows, :], la_ref[rows, :], t_ref, reverse)
            qtb, ktb, kendb = qt.astype(BF16), kt.astype(BF16), kend.astype(BF16)
            st = st_ref[0, c]
            dst = ds_scr[...]
            dstb = dst.astype(BF16)
            a_t = jnp.where(mask_t, lax.dot_general(ktb, qtb, NT, preferred_element_type=F32), 0.0)
            da = jnp.where(mask, lax.dot_general(dob, v, NT, preferred_element_type=F32), 0.0)
            da_t = jnp.where(mask_t, lax.dot_general(v, dob, NT, preferred_element_type=F32), 0.0)
            dv = jnp.dot(a_t.astype(BF16), dob, preferred_element_type=F32)
            dv = dv + lax.dot_general(kendb, dstb, NT, preferred_element_type=F32)
            dqt = jnp.dot(da.astype(BF16), ktb, preferred_element_type=F32)
            dqt = dqt + jnp.dot(dob, st.astype(BF16), preferred_element_type=F32)
            dkt = jnp.dot(da_t.astype(BF16), qtb, preferred_element_type=F32)
            dkend = jnp.dot(v, dstb, preferred_element_type=F32)
            ddec = jnp.sum(dst * st, axis=0, keepdims=True)
            ds_scr[...] = dst * dec + lax.dot_general(dob, qtb, TN, preferred_element_type=F32)
            ke = dkend * kend
            db = dqt * qt - dkt * kt - ke
            db_last = jnp.sum(ke, axis=0, keepdims=True) + ddec * dec
            db = db + jnp.where(is_last, db_last, 0.0)
            dq_ref[rows, :] = dqt * e_b * scale
            dk_ref[rows, :] = dkt * e_nb + dkend * e_end
            dv_ref[rows, :] = dv
            dla_ref[rows, :] = jnp.dot(tt_ref[...], db, precision=HI, preferred_element_type=F32)

    kspec = lambda cb: pl.BlockSpec((tb, B_KEY_DIM), lambda h, i: (blk(i), cb + h))
    return pl.pallas_call(
        body, name=name, grid=(B_HEADS, nb),
        in_specs=[kspec(qb), kspec(kb_), pl.BlockSpec((tb, B_VAL_DIM), lambda h, i: (blk(i), vb + h)), kspec(lb),
                  pl.BlockSpec((1, cpb, B_VAL_DIM, B_KEY_DIM), lambda h, i: (h, blk(i), 0, 0)),
                  pl.BlockSpec((tb, B_VAL_DIM), lambda h, i: (blk(i), h)),
                  _const_spec((B_CHUNK, B_CHUNK)), _const_spec((B_CHUNK, B_CHUNK))],
        out_specs=[kspec(0), kspec(0), pl.BlockSpec((tb, B_VAL_DIM), lambda h, i: (blk(i), h)), kspec(0)],
        out_shape=[jax.ShapeDtypeStruct((S, B_QK_WIDTH), F32), jax.ShapeDtypeStruct((S, B_QK_WIDTH), F32),
                   jax.ShapeDtypeStruct((S, B_V_WIDTH), F32), jax.ShapeDtypeStruct((S, B_QK_WIDTH), F32)],
        scratch_shapes=[pltpu.VMEM((B_VAL_DIM, B_KEY_DIM), F32)],
        compiler_params=_params(("parallel", "arbitrary")),
    )(proj, proj, proj, loga, states, do, tmat, tmat_t)


def gla_post_fwd(o_f, o_b, gain, proj, *, name):
    S = o_f.shape[0]
    ts = _tile(S, 512)
    rcol = (2 * B_QK_WIDTH + B_V_WIDTH) // B_V_WIDTH

    def body(f_ref, b_ref, g_ref, r_ref, y_ref):
        for h in range(B_HEADS):
            sl = slice(h * B_VAL_DIM, (h + 1) * B_VAL_DIM)
            o = f_ref[:, sl] + b_ref[:, sl]
            n = (o * lax.rsqrt(jnp.mean(o * o, axis=-1, keepdims=True) + RMS_EPS)) * g_ref[:, sl]
            r = r_ref[:, sl]
            y_ref[:, sl] = (n * (r * (1.0 / (1.0 + jnp.exp(-r))))).astype(y_ref.dtype)

    spec = pl.BlockSpec((ts, B_V_WIDTH), lambda i: (i, 0))
    return pl.pallas_call(
        body, name=name, grid=(S // ts,),
        in_specs=[spec, spec, _const_spec((1, B_V_WIDTH)), pl.BlockSpec((ts, B_V_WIDTH), lambda i: (i, rcol))],
        out_specs=spec, out_shape=jax.ShapeDtypeStruct((S, B_V_WIDTH), BF16),
        compiler_params=_params(("parallel",)),
    )(o_f, o_b, gain, proj)


def gla_post_bwd(o_f, o_b, gain, proj, dy, *, name):
    S = o_f.shape[0]
    ts = _tile(S, 512)
    rcol = (2 * B_QK_WIDTH + B_V_WIDTH) // B_V_WIDTH

    def body(f_ref, b_ref, g_ref, r_ref, dy_ref, do_ref, dr_ref, dg_ref):
        @pl.when(pl.program_id(0) == 0)
        def _():
            dg_ref[...] = jnp.zeros_like(dg_ref)

        for h in range(B_HEADS):
            sl = slice(h * B_VAL_DIM, (h + 1) * B_VAL_DIM)
            o = f_ref[:, sl] + b_ref[:, sl]
            rs = lax.rsqrt(jnp.mean(o * o, axis=-1, keepdims=True) + RMS_EPS)
            ohat = o * rs
            g = g_ref[:, sl]
            r = r_ref[:, sl]
            sig = 1.0 / (1.0 + jnp.exp(-r))
            dyv = dy_ref[:, sl].astype(F32)
            dn = dyv * (r * sig)
            dr_ref[:, sl] = dyv * (ohat * g) * (sig * (1.0 + r * (1.0 - sig)))
            dng = dn * g
            do_ref[:, sl] = rs * (dng - ohat * jnp.mean(dng * ohat, axis=-1, keepdims=True))
            dg_ref[0:1, sl] += jnp.sum(dn * ohat, axis=0, keepdims=True)

    spec = pl.BlockSpec((ts, B_V_WIDTH), lambda i: (i, 0))
    return pl.pallas_call(
        body, name=name, grid=(S // ts,),
        in_specs=[spec, spec, _const_spec((1, B_V_WIDTH)), pl.BlockSpec((ts, B_V_WIDTH), lambda i: (i, rcol)), spec],
        out_specs=[spec, spec, _const_spec((8, B_V_WIDTH))],
        out_shape=[jax.ShapeDtypeStruct((S, B_V_WIDTH), F32), jax.ShapeDtypeStruct((S, B_V_WIDTH), F32),
                   jax.ShapeDtypeStruct((8, B_V_WIDTH), F32)],
        compiler_params=_params(("arbitrary",)),
    )(o_f, o_b, gain, proj, dy)


def gla_combine(parts_f, parts_b, dr, dz, *, name):
    S = dr.shape[0]
    ts = _tile(S, 512)

    def body(qf, kf, vf, qb, kb, vb, r_ref, z_ref, o_ref):
        o_ref[:, 0:512] = (qf[...] + qb[...]).astype(o_ref.dtype)
        o_ref[:, 512:1024] = (kf[...] + kb[...]).astype(o_ref.dtype)
        o_ref[:, 1024:2048] = (vf[...] + vb[...]).astype(o_ref.dtype)
        o_ref[:, 2048:3072] = r_ref[...].astype(o_ref.dtype)
        o_ref[:, 3072:3200] = z_ref[...].astype(o_ref.dtype)

    s512 = pl.BlockSpec((ts, B_QK_WIDTH), lambda i: (i, 0))
    s1024 = pl.BlockSpec((ts, B_V_WIDTH), lambda i: (i, 0))
    return pl.pallas_call(
        body, name=name, grid=(S // ts,),
        in_specs=[s512, s512, s1024, s512, s512, s1024, s1024, pl.BlockSpec((ts, LANES), lambda i: (i, 0))],
        out_specs=pl.BlockSpec((ts, B_IN_PAD), lambda i: (i, 0)),
        out_shape=jax.ShapeDtypeStruct((S, B_IN_PAD), BF16),
        compiler_params=_params(("parallel",)),
    )(*parts_f, *parts_b, dr, dz)


def adamw(w, g, m, v, *, name):
    R, C = w.shape
    tr = _tile(R, 256)
    c1 = 1.0 / (1.0 - ADAM_B1 ** ADAM_STEP)
    c2 = 1.0 / (1.0 - ADAM_B2 ** ADAM_STEP)

    def body(w_ref, g_ref, m_ref, v_ref, d_ref, mo_ref, vo_ref):
        gv = g_ref[...]
        mn = ADAM_B1 * m_ref[...] + (1.0 - ADAM_B1) * gv
        vn = ADAM_B2 * v_ref[...] + (1.0 - ADAM_B2) * (gv * gv)
        mo_ref[...] = mn
        vo_ref[...] = vn
        d_ref[...] = -ADAM_LR * ((mn * c1) / (jnp.sqrt(vn * c2) + ADAM_EPS) + ADAM_WD * w_ref[...])

    spec = pl.BlockSpec((tr, C), lambda i: (i, 0))
    return pl.pallas_call(
        body, name=name, grid=(R // tr,), in_specs=[spec] * 4, out_specs=[spec] * 3,
        out_shape=[jax.ShapeDtypeStruct((R, C), F32)] * 3,
        compiler_params=_params(("parallel",)),
    )(w, g, m, v)


def _chip_peers():
    x, y, c = lax.axis_index("x"), lax.axis_index("y"), lax.axis_index("c")
    return x, y, c, [(1 - x, y), (x, 1 - y), (1 - x, 1 - y)]


_ANY = pl.BlockSpec(memory_space=pl.ANY)


def gather_shards(src, *, name):
    _, R, C = src.shape

    def body(src_ref, out_ref, ici_send, ici_recv, d2d_send, d2d_recv, local_sem):
        x, y, c, chips = _chip_peers()
        me = 2 * x + y
        sibling = (x, y, 1 - c)
        mine = pltpu.make_async_copy(src_ref, out_ref.at[me], local_sem)
        mine.start()
        sends = []
        for k, (px, py) in enumerate(chips):
            cp = pltpu.make_async_remote_copy(
                src_ref=src_ref.at[c], dst_ref=out_ref.at[me, c], send_sem=ici_send.at[k], recv_sem=ici_recv.at[k],
                device_id=(px, py, c), device_id_type=MESH)
            cp.start()
            sends.append(cp)
        for k, (px, py) in enumerate(chips):
            landed = out_ref.at[2 * px + py, c]
            pltpu.make_async_remote_copy(
                src_ref=src_ref.at[c], dst_ref=landed, send_sem=ici_send.at[k], recv_sem=ici_recv.at[k],
                device_id=(px, py, c), device_id_type=MESH).wait_recv()
            cp = pltpu.make_async_remote_copy(
                src_ref=landed, dst_ref=landed, send_sem=d2d_send.at[k], recv_sem=d2d_recv.at[k],
                device_id=sibling, device_id_type=MESH)
            cp.start()
            sends.append(cp)
        for k, (px, py) in enumerate(chips):
            other_half = out_ref.at[2 * px + py, 1 - c]
            pltpu.make_async_remote_copy(
                src_ref=other_half, dst_ref=other_half, send_sem=d2d_send.at[k], recv_sem=d2d_recv.at[k],
                device_id=sibling, device_id_type=MESH).wait_recv()
        for cp in sends:
            cp.wait_send()
        mine.wait()

    return pl.pallas_call(
        body, name=name, in_specs=[_ANY], out_specs=_ANY,
        out_shape=jax.ShapeDtypeStruct((N_CHIPS, 2, R, C), src.dtype),
        scratch_shapes=[pltpu.SemaphoreType.DMA((3,))] * 4 + [pltpu.SemaphoreType.DMA],
        compiler_params=pltpu.CompilerParams(has_side_effects=True),
    )(src)


def chip_exchange(srcs, *, name):
    n = len(srcs)

    def body(*refs):
        src_refs, out_refs = refs[:n], refs[n:2 * n]
        send_sems, recv_sems, local_sems = refs[2 * n:]
        x, y, c, chips = _chip_peers()
        me = 2 * x + y
        copies, local = [], []
        for i, (src_ref, out_ref) in enumerate(zip(src_refs, out_refs)):
            mine = pltpu.make_async_copy(src_ref.at[me], out_ref.at[me], local_sems.at[i])
            mine.start()
            local.append(mine)
            for k, (px, py) in enumerate(chips):
                cp = pltpu.make_async_remote_copy(
                    src_ref=src_ref.at[2 * px + py], dst_ref=out_ref.at[me],
                    send_sem=send_sems.at[i, k], recv_sem=recv_sems.at[i, k],
                    device_id=(px, py, c), device_id_type=MESH)
                cp.start()
                copies.append(cp)
        for i, (src_ref, out_ref) in enumerate(zip(src_refs, out_refs)):
            for k, (px, py) in enumerate(chips):
                pltpu.make_async_remote_copy(
                    src_ref=src_ref.at[me], dst_ref=out_ref.at[2 * px + py],
                    send_sem=send_sems.at[i, k], recv_sem=recv_sems.at[i, k],
                    device_id=(px, py, c), device_id_type=MESH).wait_recv()
        for cp in copies:
            cp.wait_send()
        for mine in local:
            mine.wait()

    return pl.pallas_call(
        body, name=name, in_specs=[_ANY] * n, out_specs=[_ANY] * n,
        out_shape=[jax.ShapeDtypeStruct(s.shape, s.dtype) for s in srcs],
        scratch_shapes=[pltpu.SemaphoreType.DMA((n, 3)), pltpu.SemaphoreType.DMA((n, 3)), pltpu.SemaphoreType.DMA((n,))],
        compiler_params=pltpu.CompilerParams(has_side_effects=True),
    )(*srcs)


def sibling_exchange(srcs, *, name):
    n = len(srcs)

    def body(*refs):
        src_refs, out_refs, (send_sems, recv_sems) = refs[:n], refs[n:2 * n], refs[2 * n:]
        x, y, c = lax.axis_index("x"), lax.axis_index("y"), lax.axis_index("c")
        copies = []
        for i, (src_ref, out_ref) in enumerate(zip(src_refs, out_refs)):
            cp = pltpu.make_async_remote_copy(
                src_ref=src_ref.at[:, 1 - c], dst_ref=out_ref, send_sem=send_sems.at[i], recv_sem=recv_sems.at[i],
                device_id=(x, y, 1 - c), device_id_type=MESH)
            cp.start()
            copies.append(cp)
        for cp in copies:
            cp.wait()

    return pl.pallas_call(
        body, name=name, in_specs=[_ANY] * n, out_specs=[_ANY] * n,
        out_shape=[jax.ShapeDtypeStruct((s.shape[0],) + s.shape[2:], s.dtype) for s in srcs],
        scratch_shapes=[pltpu.SemaphoreType.DMA((n,)), pltpu.SemaphoreType.DMA((n,))],
        compiler_params=pltpu.CompilerParams(has_side_effects=True),
    )(*srcs)


def sibling_share(srcs, *, name):
    n = len(srcs)

    def body(*refs):
        src_refs, out_refs, (send_sems, recv_sems, local_sems) = refs[:n], refs[n:2 * n], refs[2 * n:]
        x, y, c = lax.axis_index("x"), lax.axis_index("y"), lax.axis_index("c")
        copies = []
        for i, (src_ref, out_ref) in enumerate(zip(src_refs, out_refs)):
            mine = pltpu.make_async_copy(src_ref, out_ref.at[c], local_sems.at[i])
            mine.start()
            cp = pltpu.make_async_remote_copy(
                src_ref=src_ref, dst_ref=out_ref.at[c], send_sem=send_sems.at[i], recv_sem=recv_sems.at[i],
                device_id=(x, y, 1 - c), device_id_type=MESH)
            cp.start()
            copies += [mine, cp]
        for cp in copies:
            cp.wait()

    return pl.pallas_call(
        body, name=name, in_specs=[_ANY] * n, out_specs=[_ANY] * n,
        out_shape=[jax.ShapeDtypeStruct((2,) + s.shape, s.dtype) for s in srcs],
        scratch_shapes=[pltpu.SemaphoreType.DMA((n,))] * 3,
        compiler_params=pltpu.CompilerParams(has_side_effects=True),
    )(*srcs)


def add_pair(a, b, *, name):
    n, _, R, C = a.shape
    tr = _tile(R, 256)

    def body(c_ref, a_ref, b_ref, o_ref):
        o_ref[...] = a_ref[0] + b_ref[...]

    return pl.pallas_call(
        body, name=name,
        grid_spec=pltpu.PrefetchScalarGridSpec(
            num_scalar_prefetch=1, grid=(n, R // tr),
            in_specs=[pl.BlockSpec((1, 1, tr, C), lambda s, i, c_ref: (s, c_ref[0], i, 0)),
                      pl.BlockSpec((1, tr, C), lambda s, i, c_ref: (s, i, 0))],
            out_specs=pl.BlockSpec((1, tr, C), lambda s, i, c_ref: (s, i, 0))),
        out_shape=jax.ShapeDtypeStruct((n, R, C), a.dtype),
        compiler_params=_params(("parallel", "parallel")),
    )(lax.axis_index("c").reshape(1).astype(jnp.int32), a, b)


def sum_slots(a, *, name):
    n, R, C = a.shape
    tr = _tile(R, 256)

    def body(*refs):
        o_ref = refs[-1]
        acc = refs[0][0]
        for r in refs[1:-1]:
            acc = acc + r[0]
        o_ref[...] = acc

    return pl.pallas_call(
        body, name=name, grid=(R // tr,),
        in_specs=[pl.BlockSpec((1, tr, C), functools.partial(lambda s, i: (s, i, 0), s)) for s in range(n)],
        out_specs=pl.BlockSpec((tr, C), lambda i: (i, 0)),
        out_shape=jax.ShapeDtypeStruct((R, C), a.dtype),
        compiler_params=_params(("parallel",)),
    )(*([a] * n))


def _flat_rows(n_elems, mult):
    rows = -(-n_elems // FLAT_COLS)
    return -(-rows // mult) * mult


def _to_flat(parts, mult):
    v = jnp.concatenate([p.reshape(-1) for p in parts])
    rows = _flat_rows(v.shape[0], mult)
    return jnp.pad(v, (0, rows * FLAT_COLS - v.shape[0])).reshape(rows, FLAT_COLS)


def _from_flat(flat, shapes):
    v = flat.reshape(-1)
    out, off = [], 0
    for s in shapes:
        n = int(np.prod(s))
        out.append(v[off:off + n].reshape(s))
        off += n
    return out


def _unshard(blocks, axis):
    return jnp.concatenate([blocks[s] for s in range(N_CHIPS)], axis=axis)


def _by_shard(full, axis):
    shp = full.shape
    cut = full.reshape(shp[:axis] + (N_CHIPS, shp[axis] // N_CHIPS) + shp[axis + 1:])
    return jnp.moveaxis(cut, axis, 0)


def _gradient_blocks(grads):
    blocks = []
    for n in MATRICES:
        t = _by_shard(jnp.stack(grads[n]), SHARD_AXIS[n])
        blocks.append(t.reshape(N_CHIPS, 2, -1, t.shape[-1]))
    rest = []
    for s in range(N_CHIPS):
        parts = [jnp.stack(grads[n]) if n in REPLICATED else _by_shard(jnp.stack(grads[n]), SHARD_AXIS[n])[s]
                 for n in WEIGHTS if n not in MATRICES]
        rest.append(_to_flat(parts, 16))
    rest = jnp.stack(rest)
    blocks.append(rest.reshape(N_CHIPS, 2, rest.shape[1] // 2, FLAT_COLS))
    return blocks


def _gather_weights(w):
    full = {}
    for names, dtype, mult, call in ((BIG, BF16, 32, "gather_weights"), (SMALL_SHARDED, F32, 16, "gather_vectors")):
        parts = [w[n].astype(dtype) for n in names]
        flat = _to_flat(parts, mult)
        got = gather_shards(flat.reshape(2, flat.shape[0] // 2, FLAT_COLS), name=call)
        got = got.reshape(N_CHIPS, flat.shape[0], FLAT_COLS)
        per_chip = [_from_flat(got[s], [p.shape for p in parts]) for s in range(N_CHIPS)]
        for i, n in enumerate(names):
            full[n] = _unshard([per_chip[s][i] for s in range(N_CHIPS)], SHARD_AXIS[n])
    return full


def _reduce_gradients(grads, w):
    halves = _gradient_blocks(grads)
    other = sibling_exchange(halves, name="grad_pair_exchange")
    pair = [add_pair(a, b, name=f"grad_pair_add{i}") for i, (a, b) in enumerate(zip(halves, other))]
    slots = chip_exchange(pair, name="grad_chip_exchange")
    mine = [sum_slots(a, name=f"grad_chip_sum{i}") for i, a in enumerate(slots)]
    both = sibling_share(mine, name="grad_half_share")
    g = {n: t.reshape(w[n].shape) for n, t in zip(MATRICES, both)}
    rest = [n for n in WEIGHTS if n not in MATRICES]
    g.update(zip(rest, _from_flat(both[-1], [w[n].shape for n in rest])))
    return g


def _layer_fwd(i, h, p, aux):
    j = i // 2
    sv = {"h0": h}
    if i % 2 == 0:
        hns = rmsnorm_fwd(h, p["attn_norm"][i][None], dils=DILS, name=f"l{i}_norm1")
        qkvs, qkvps, os_, lses = [], [], [], []
        for g, d in enumerate(DILS):
            qkv = matmul(hns[g], p["a_w_in_g"][j][g], name=f"l{i}_a_in{g}")
            qkvp = qk_prep_fwd(qkv, aux["a_gain"][j][g], aux["cos"][g], aux["sin"][g], aux["ones"], name=f"l{i}_a_prep{g}")
            o, l = attn_fwd(qkvp, d, name=f"l{i}_a_attn{g}")
            qkvs.append(qkv)
            qkvps.append(qkvp)
            os_.append(o)
            lses.append(l)
        out, lse = attn_merge(os_, lses, DILS, name=f"l{i}_a_merge")
        sv.update(hns=hns, qkv=qkvs, qkvp=qkvps, out=out, lse=lse)
        h1 = matmul(out, p["a_w_out"][j], res=h, name=f"l{i}_a_out")
    else:
        hn = rmsnorm_fwd(h, p["attn_norm"][i][None], name=f"l{i}_norm1")[0]
        sv["hn"] = hn
        proj = matmul(hn, p["b_w_in"][j], tn=640, name=f"l{i}_b_in")
        loga = gate_fwd(proj, aux["b_wblk"][j], aux["b_bias"][j], name=f"l{i}_b_gate")
        o_f, st_f = gla_fwd(proj, loga, aux["tri_f"], False, name=f"l{i}_b_gla_f")
        o_b, st_b = gla_fwd(proj, loga, aux["tri_b"], True, name=f"l{i}_b_gla_b")
        y = gla_post_fwd(o_f, o_b, aux["b_gain"][j], proj, name=f"l{i}_b_post")
        sv.update(proj=proj, loga=loga, o_f=o_f, o_b=o_b, st_f=st_f, st_b=st_b, y=y)
        h1 = matmul(y, p["b_w_out"][j], res=h, name=f"l{i}_b_out")
    sv["h1"] = h1
    hn2 = rmsnorm_fwd(h1, p["ffn_norm"][i][None], name=f"l{i}_norm2")[0]
    gu = matmul(hn2, p["ffn_w_gate_up"][i], out_dtype=BF16, tn=FFN_HIDDEN // 2, name=f"l{i}_f_up")
    act = swiglu_fwd(gu, name=f"l{i}_f_act")
    h2 = matmul(act, p["ffn_w_down"][i], res=h1, tk=2816, name=f"l{i}_f_down")
    sv.update(hn2=hn2, gu=gu, act=act)
    return h2, sv


def _layer_bwd(i, dh, p, pt, aux, sv, grads):
    j = i // 2
    dhb = dh.astype(BF16)
    grads["ffn_w_down"][i] = matmul(sv["act"].T, dhb, tm=FFN_HIDDEN // 2, tk=2048, name=f"l{i}_f_down_dw")
    dact = matmul(dhb, pt["ffn_w_down"][i], out_dtype=BF16, tn=FFN_HIDDEN // 2, name=f"l{i}_f_down_dx")
    dgu = swiglu_bwd(sv["gu"], dact, name=f"l{i}_f_act_bwd")
    grads["ffn_w_gate_up"][i] = matmul(sv["hn2"].T, dgu, tk=2048, name=f"l{i}_f_up_dw")
    dhn2 = matmul(dgu, pt["ffn_w_gate_up"][i], tk=2816, name=f"l{i}_f_up_dx")
    dh1, dg = rmsnorm_bwd(sv["h1"], p["ffn_norm"][i][None], [dhn2], dh, name=f"l{i}_norm2_bwd")
    grads["ffn_norm"][i] = dg[0]
    dh1b = dh1.astype(BF16)
    if i % 2 == 0:
        grads["a_w_out"][j] = matmul(sv["out"].T, dh1b, tk=2048, name=f"l{i}_a_out_dw")
        dout = matmul(dh1b, pt["a_w_out"][j], out_dtype=BF16, name=f"l{i}_a_out_dx")
        douts, deltas = attn_delta(dout, sv["out"], aux["ones"], DILS, name=f"l{i}_a_delta")
        dws, dhns, dgq, dgk = [], [], [], []
        for g, d in enumerate(DILS):
            qkvp, lse = sv["qkvp"][g], sv["lse"][g]
            dq = attn_bwd_dq(qkvp, douts[g], lse, deltas[g], d, name=f"l{i}_a_dq{g}")
            dk, dv = attn_bwd_dkv(qkvp, douts[g], lse, deltas[g], d, name=f"l{i}_a_dkv{g}")
            dqkv, dgain = qk_prep_bwd(sv["qkv"][g], aux["a_gain"][j][g], aux["cos"][g], aux["sin"][g], aux["ones"],
                                      [dq, dk, dv], name=f"l{i}_a_prep_bwd{g}")
            dgh = dgain[0].reshape(3, A_HEADS, A_HEAD_DIM).sum(axis=1)
            dgq.append(dgh[0])
            dgk.append(dgh[1])
            dws.append(matmul(sv["hns"][g].T, dqkv, tk=2048, name=f"l{i}_a_in_dw{g}"))
            dhns.append(matmul(dqkv, pt["a_w_in_g"][j][g], tk=3072, name=f"l{i}_a_in_dx{g}"))
        grads["a_q_norm"][j] = jnp.stack(dgq)
        grads["a_k_norm"][j] = jnp.stack(dgk)
        grads["a_w_in"][j] = jnp.concatenate(dws, axis=1)
        dh0, dg = rmsnorm_bwd(sv["h0"], p["attn_norm"][i][None], dhns, dh1, dils=DILS, name=f"l{i}_norm1_bwd")
    else:
        grads["b_w_out"][j] = matmul(sv["y"].T, dh1b, tk=2048, name=f"l{i}_b_out_dw")
        dy = matmul(dh1b, pt["b_w_out"][j], name=f"l{i}_b_out_dx")
        do, dr, dgn = gla_post_bwd(sv["o_f"], sv["o_b"], aux["b_gain"][j], sv["proj"], dy, name=f"l{i}_b_post_bwd")
        grads["b_out_norm"][j] = dgn[0].reshape(B_HEADS, B_VAL_DIM)
        pf = gla_bwd(sv["proj"], sv["loga"], sv["st_f"], do, aux["tri_f"], aux["tri_b"], False, name=f"l{i}_b_gla_f_bwd")
        pb = gla_bwd(sv["proj"], sv["loga"], sv["st_b"], do, aux["tri_b"], aux["tri_f"], True, name=f"l{i}_b_gla_b_bwd")
        dloga = jnp.concatenate([pf[3], pb[3]], axis=1)
        dz, dwblk, dbias = gate_bwd(sv["proj"], aux["b_wblk"][j], aux["b_wblk_t"][j], aux["b_bias"][j], dloga,
                                    name=f"l{i}_b_gate_bwd")
        grads["b_w_gate_f"][j] = dwblk[:B_GATE_RANK, :B_QK_WIDTH]
        grads["b_w_gate_b"][j] = dwblk[B_GATE_RANK:2 * B_GATE_RANK, B_QK_WIDTH:]
        grads["b_gate_bias_f"][j] = dbias[0, :B_QK_WIDTH]
        grads["b_gate_bias_b"][j] = dbias[0, B_QK_WIDTH:]
        dproj = gla_combine(pf[:3], pb[:3], dr, dz, name=f"l{i}_b_combine")
        grads["b_w_in"][j] = matmul(sv["hn"].T, dproj, tn=640, tk=2048, name=f"l{i}_b_in_dw")[:, :B_IN_WIDTH]
        dhn = matmul(dproj, pt["b_w_in"][j], tk=640, name=f"l{i}_b_in_dx")
        dh0, dg = rmsnorm_bwd(sv["h0"], p["attn_norm"][i][None], [dhn], dh1, name=f"l{i}_norm1_bwd")
    grads["attn_norm"][i] = dg[0]
    return dh0


def _local_step(x, target, p, small):
    S = x.shape[0]
    cos, sin = _rope_tables(S)
    to_phase = lambda t, d: t.reshape(S // d, d, LANES).swapaxes(0, 1).reshape(S, LANES)
    cos, sin = [to_phase(cos, d) for d in DILS], [to_phase(sin, d) for d in DILS]
    ones_v = jnp.ones((A_WIDTH,), F32)
    a_gain = [[jnp.concatenate([jnp.tile(small["a_q_norm"][j][g], A_HEADS), jnp.tile(small["a_k_norm"][j][g], A_HEADS),
                                ones_v])[None] for g in range(len(DILS))] for j in range(2)]
    b_wblk = [_gate_block_weight(p["b_w_gate_f"][j].astype(F32), p["b_w_gate_b"][j].astype(F32)) for j in range(2)]
    aux = dict(cos=cos, sin=sin, ones=_head_block_ones(), a_gain=a_gain, tri_f=_tri(False), tri_b=_tri(True),
               b_wblk=b_wblk, b_wblk_t=[w.T for w in b_wblk],
               b_bias=[jnp.concatenate([small["b_gate_bias_f"][j], small["b_gate_bias_b"][j]])[None] for j in range(2)],
               b_gain=[small["b_out_norm"][j].reshape(1, B_V_WIDTH) for j in range(2)])
    pw = dict(p)
    pw["b_w_in"] = jnp.pad(p["b_w_in"], ((0, 0), (0, 0), (0, B_IN_PAD - B_IN_WIDTH)))
    pw["attn_norm"], pw["ffn_norm"] = small["attn_norm"], small["ffn_norm"]
    gw = 3 * A_WIDTH
    pw["a_w_in_g"] = [[p["a_w_in"][j][:, g * gw:(g + 1) * gw] for g in range(len(DILS))] for j in range(2)]
    pt = {n: jnp.swapaxes(pw[n], 1, 2) for n in ("a_w_out", "b_w_in", "b_w_out", "ffn_w_gate_up", "ffn_w_down")}
    pt["a_w_in_g"] = [[wg.T for wg in row] for row in pw["a_w_in_g"]]

    h = x
    saved = []
    for i in range(DEPTH):
        h, sv = _layer_fwd(i, h, pw, aux)
        saved.append(sv)
    loss_sq, dh = loss_head(h, target, name="loss_head")
    grads = {n: [None] * (DEPTH if n in ("attn_norm", "ffn_norm", "ffn_w_gate_up", "ffn_w_down") else 2) for n in WEIGHTS}
    for i in reversed(range(DEPTH)):
        dh = _layer_bwd(i, dh, pw, pt, aux, saved[i], grads)
    return loss_sq[0, 0] * (0.5 / D_MODEL), dh, grads


def kernel(x, attn_norm, ffn_norm, a_w_in, a_q_norm, a_k_norm, a_w_out, b_w_in, b_w_gate_f, b_gate_bias_f, b_w_gate_b, b_gate_bias_b, b_out_norm, b_w_out, ffn_w_gate_up, ffn_w_down, loss_target, m_attn_norm, m_ffn_norm, m_a_w_in, m_a_q_norm, m_a_k_norm, m_a_w_out, m_b_w_in, m_b_w_gate_f, m_b_gate_bias_f, m_b_w_gate_b, m_b_gate_bias_b, m_b_out_norm, m_b_w_out, m_ffn_w_gate_up, m_ffn_w_down, v_attn_norm, v_ffn_norm, v_a_w_in, v_a_q_norm, v_a_k_norm, v_a_w_out, v_b_w_in, v_b_w_gate_f, v_b_gate_bias_f, v_b_w_gate_b, v_b_gate_bias_b, v_b_out_norm, v_b_w_out, v_ffn_w_gate_up, v_ffn_w_down):
    w = dict(attn_norm=attn_norm, ffn_norm=ffn_norm, a_w_in=a_w_in, a_q_norm=a_q_norm, a_k_norm=a_k_norm, a_w_out=a_w_out,
             b_w_in=b_w_in, b_w_gate_f=b_w_gate_f, b_gate_bias_f=b_gate_bias_f, b_w_gate_b=b_w_gate_b,
             b_gate_bias_b=b_gate_bias_b, b_out_norm=b_out_norm, b_w_out=b_w_out, ffn_w_gate_up=ffn_w_gate_up,
             ffn_w_down=ffn_w_down)
    m = dict(attn_norm=m_attn_norm, ffn_norm=m_ffn_norm, a_w_in=m_a_w_in, a_q_norm=m_a_q_norm, a_k_norm=m_a_k_norm,
             a_w_out=m_a_w_out, b_w_in=m_b_w_in, b_w_gate_f=m_b_w_gate_f, b_gate_bias_f=m_b_gate_bias_f,
             b_w_gate_b=m_b_w_gate_b, b_gate_bias_b=m_b_gate_bias_b, b_out_norm=m_b_out_norm, b_w_out=m_b_w_out,
             ffn_w_gate_up=m_ffn_w_gate_up, ffn_w_down=m_ffn_w_down)
    v = dict(attn_norm=v_attn_norm, ffn_norm=v_ffn_norm, a_w_in=v_a_w_in, a_q_norm=v_a_q_norm, a_k_norm=v_a_k_norm,
             a_w_out=v_a_w_out, b_w_in=v_b_w_in, b_w_gate_f=v_b_w_gate_f, b_gate_bias_f=v_b_gate_bias_f,
             b_w_gate_b=v_b_w_gate_b, b_gate_bias_b=v_b_gate_bias_b, b_out_norm=v_b_out_norm, b_w_out=v_b_w_out,
             ffn_w_gate_up=v_ffn_w_gate_up, ffn_w_down=v_ffn_w_down)

    full = _gather_weights(w)
    p = {n: full[n] for n in BIG}
    small = {n: full[n] for n in SMALL_SHARDED}
    small.update({n: w[n] for n in REPLICATED})
    loss_local, dx, grads = _local_step(x[0], loss_target[0], p, small)
    loss = lax.psum(loss_local, ("x", "y", "c"))

    g = _reduce_gradients(grads, w)
    delta, new_m, new_v = {}, {}, {}
    rows = lambda t: t.reshape(-1, t.shape[-1])
    for n in MATRICES:
        outs = adamw(rows(w[n]), rows(g[n]), rows(m[n]), rows(v[n]), name=f"adamw_{n}")
        delta[n], new_m[n], new_v[n] = [o.reshape(w[n].shape) for o in outs]
    rest = [n for n in WEIGHTS if n not in MATRICES]
    flat = lambda d: _to_flat([d[n] for n in rest], 8)
    outs = adamw(flat(w), flat(g), flat(m), flat(v), name="adamw_vectors")
    for d, o in zip((delta, new_m, new_v), outs):
        d.update(zip(rest, _from_flat(o, [w[n].shape for n in rest])))
    return (loss, dx[None], *[g[n] for n in WEIGHTS], *[delta[n] for n in WEIGHTS],
            *[new_m[n] for n in WEIGHTS], *[new_v[n] for n in WEIGHTS])
```

```python
import functools

import numpy as np
import jax
import jax.numpy as jnp
from jax import lax
from jax.experimental import pallas as pl
from jax.experimental.pallas import tpu as pltpu

F32, BF16 = jnp.float32, jnp.bfloat16
HI = lax.Precision.HIGHEST
MESH = pl.DeviceIdType.MESH

D_MODEL = 1024
DEPTH = 4
RMS_EPS = 1e-6
NEG_INF = -1e30
A_GROUPS = ((128, 1), (512, 4), (2048, 16))
DILS = tuple(d for _, d in A_GROUPS)
A_HALF = 64
A_HEADS = 16
A_HEAD_DIM = 64
A_WIDTH = 1024
A_IN_WIDTH = 9216
ROPE_THETA = 10000.0
B_HEADS = 4
B_KEY_DIM = 128
B_VAL_DIM = 256
B_QK_WIDTH = 512
B_V_WIDTH = 1024
B_GATE_RANK = 16
B_GATE_TAU = 16.0
B_CHUNK = 64
B_IN_WIDTH = 3104
B_IN_PAD = 3200
FFN_HIDDEN = 2816
ADAM_LR, ADAM_B1, ADAM_B2, ADAM_EPS, ADAM_WD, ADAM_STEP = 0.001, 0.9, 0.999, 1e-08, 0.01, 10
LANES = 128
VMEM_LIMIT = 48 * 1024 * 1024
FLAT_COLS = 1024
N_CHIPS = 4

WEIGHTS = ['attn_norm', 'ffn_norm', 'a_w_in', 'a_q_norm', 'a_k_norm', 'a_w_out', 'b_w_in', 'b_w_gate_f',
           'b_gate_bias_f', 'b_w_gate_b', 'b_gate_bias_b', 'b_out_norm', 'b_w_out', 'ffn_w_gate_up', 'ffn_w_down']
REPLICATED = ('attn_norm', 'ffn_norm', 'a_q_norm', 'a_k_norm')
SHARD_AXIS = {'a_w_in': 2, 'a_w_out': 1, 'b_w_in': 2, 'b_w_gate_f': 2, 'b_gate_bias_f': 1, 'b_w_gate_b': 2,
              'b_gate_bias_b': 1, 'b_out_norm': 2, 'b_w_out': 1, 'ffn_w_gate_up': 2, 'ffn_w_down': 1}
BIG = ('a_w_in', 'a_w_out', 'b_w_in', 'b_w_gate_f', 'b_w_gate_b', 'b_w_out', 'ffn_w_gate_up', 'ffn_w_down')
SMALL_SHARDED = ('b_gate_bias_f', 'b_gate_bias_b', 'b_out_norm')
MATRICES = ('a_w_in', 'a_w_out', 'b_w_in', 'b_w_out', 'ffn_w_gate_up', 'ffn_w_down')


def _params(sem):
    return pltpu.CompilerParams(dimension_semantics=sem, vmem_limit_bytes=VMEM_LIMIT)


def _tile(n, pref):
    t = min(n, pref)
    while n % t:
        t //= 2
    return t


def _const_spec(shape):
    nd = len(shape)
    return pl.BlockSpec(shape, lambda *_: (0,) * nd)


def matmul(a, b, *, name, out_dtype=F32, res=None, tm=1024, tn=512, tk=1024):
    M, K = a.shape
    N = b.shape[1]
    assert b.shape[0] == K
    tm, tn, tk = _tile(M, tm), _tile(N, tn), _tile(K, tk)
    nk = K // tk

    def body(*refs):
        a_ref, b_ref = refs[:2]
        r_ref = refs[2] if res is not None else None
        o_ref = refs[3 if res is not None else 2]
        part = jnp.dot(a_ref[...], b_ref[...], preferred_element_type=F32)

        def finish(v):
            if res is not None:
                v = v + r_ref[...]
            o_ref[...] = v.astype(o_ref.dtype)

        if nk == 1:
            finish(part)
            return
        acc_ref = refs[-1]
        k = pl.program_id(2)

        @pl.when(k == 0)
        def _():
            acc_ref[...] = part

        @pl.when((k > 0) & (k < nk - 1))
        def _():
            acc_ref[...] += part

        @pl.when(k == nk - 1)
        def _():
            finish(acc_ref[...] + part)

    in_specs = [pl.BlockSpec((tm, tk), lambda i, j, k: (i, k)), pl.BlockSpec((tk, tn), lambda i, j, k: (k, j))]
    args = [a, b]
    if res is not None:
        in_specs.append(pl.BlockSpec((tm, tn), lambda i, j, k: (i, j)))
        args.append(res)
    return pl.pallas_call(
        body, name=name, grid=(M // tm, N // tn, nk), in_specs=in_specs,
        out_specs=pl.BlockSpec((tm, tn), lambda i, j, k: (i, j)),
        out_shape=jax.ShapeDtypeStruct((M, N), out_dtype),
        scratch_shapes=[pltpu.VMEM((tm, tn), F32)] if nk > 1 else [],
        compiler_params=_params(("parallel", "parallel", "arbitrary")),
    )(*args)


def _phase_spec(d, ts, W):
    if d == 1:
        return pl.BlockSpec((ts, W), lambda i: (i, 0))
    return pl.BlockSpec((d, ts // d, W), lambda i: (0, i, 0))


def _phase_view(a, d):
    return a if d == 1 else a.reshape(d, a.shape[0] // d, a.shape[1])


def _phase_shape(S, W, d, dtype):
    return jax.ShapeDtypeStruct((S, W) if d == 1 else (d, S // d, W), dtype)


def _nat_scratch(ts, W):
    return pltpu.VMEM((W // LANES, ts, LANES), F32)


def _put_natural(nat_ref, value):
    for c in range(nat_ref.shape[0]):
        nat_ref[c] = value[:, c * LANES:(c + 1) * LANES]


def _get_natural(nat_ref):
    return jnp.concatenate([nat_ref[c] for c in range(nat_ref.shape[0])], axis=1)


def _store_phases(nat_ref, o_ref, d, ts):
    for p in range(d):
        for c in range(nat_ref.shape[0]):
            o_ref[p, :, c * LANES:(c + 1) * LANES] = nat_ref[c, pl.ds(p, ts // d, stride=d), :].astype(o_ref.dtype)


def _load_phases(i_ref, nat_ref, d, ts):
    for p in range(d):
        for c in range(nat_ref.shape[0]):
            nat_ref[c, pl.ds(p, ts // d, stride=d), :] = i_ref[p, :, c * LANES:(c + 1) * LANES].astype(F32)


def rmsnorm_fwd(x, gain, *, name, dils=(1,)):
    S, Dm = x.shape
    ts = _tile(S, 512)

    def body(x_ref, g_ref, *rest):
        o_refs, scr = rest[:len(dils)], rest[len(dils)]
        xv = x_ref[...]
        r = lax.rsqrt(jnp.mean(xv * xv, axis=-1, keepdims=True) + RMS_EPS)
        y = (xv * r) * g_ref[...]
        if any(d > 1 for d in dils):
            _put_natural(scr, y)
        for d, o_ref in zip(dils, o_refs):
            if d == 1:
                o_ref[...] = y.astype(o_ref.dtype)
            else:
                _store_phases(scr, o_ref, d, ts)

    outs = pl.pallas_call(
        body, name=name, grid=(S // ts,),
        in_specs=[pl.BlockSpec((ts, Dm), lambda i: (i, 0)), _const_spec((1, Dm))],
        out_specs=[_phase_spec(d, ts, Dm) for d in dils],
        out_shape=[_phase_shape(S, Dm, d, BF16) for d in dils],
        scratch_shapes=[_nat_scratch(ts, Dm)],
        compiler_params=_params(("parallel",)),
    )(x, gain)
    return [o.reshape(S, Dm) for o in outs]


def rmsnorm_bwd(x, gain, dys, dres, *, name, dils=(1,)):
    S, Dm = x.shape
    ts = _tile(S, 256)
    nd = len(dils)

    def body(x_ref, g_ref, *rest):
        dy_refs, (dr_ref, dx_ref, dg_ref, scr) = rest[:nd], rest[nd:]

        @pl.when(pl.program_id(0) == 0)
        def _():
            dg_ref[...] = jnp.zeros_like(dg_ref)

        dyv = None
        for d, dy_ref in zip(dils, dy_refs):
            if d == 1:
                t = dy_ref[...].astype(F32)
            else:
                _load_phases(dy_ref, scr, d, ts)
                t = _get_natural(scr)
            dyv = t if dyv is None else dyv + t
        xv = x_ref[...]
        r = lax.rsqrt(jnp.mean(xv * xv, axis=-1, keepdims=True) + RMS_EPS)
        xhat = xv * r
        dyg = dyv * g_ref[...]
        dx = r * (dyg - xhat * jnp.mean(dyg * xhat, axis=-1, keepdims=True))
        dx_ref[...] = dr_ref[...] + dx
        dg_ref[0:1, :] += jnp.sum(dyv * xhat, axis=0, keepdims=True)

    row = pl.BlockSpec((ts, Dm), lambda i: (i, 0))
    return pl.pallas_call(
        body, name=name, grid=(S // ts,),
        in_specs=[row, _const_spec((1, Dm))] + [_phase_spec(d, ts, Dm) for d in dils] + [row],
        out_specs=[row, _const_spec((8, Dm))],
        out_shape=[jax.ShapeDtypeStruct((S, Dm), F32), jax.ShapeDtypeStruct((8, Dm), F32)],
        scratch_shapes=[_nat_scratch(ts, Dm)],
        compiler_params=_params(("arbitrary",)),
    )(x, gain, *[_phase_view(dy, d) for dy, d in zip(dys, dils)], dres)


def swiglu_fwd(gu, *, name):
    S, F2 = gu.shape
    Fh = F2 // 2
    ts = _tile(S, 512)

    def body(g_ref, u_ref, o_ref):
        g = g_ref[...].astype(F32)
        u = u_ref[...].astype(F32)
        o_ref[...] = (g * (1.0 / (1.0 + jnp.exp(-g))) * u).astype(o_ref.dtype)

    return pl.pallas_call(
        body, name=name, grid=(S // ts,),
        in_specs=[pl.BlockSpec((ts, Fh), lambda i: (i, 0)), pl.BlockSpec((ts, Fh), lambda i: (i, 1))],
        out_specs=pl.BlockSpec((ts, Fh), lambda i: (i, 0)),
        out_shape=jax.ShapeDtypeStruct((S, Fh), BF16),
        compiler_params=_params(("parallel",)),
    )(gu, gu)


def swiglu_bwd(gu, dact, *, name):
    S, F2 = gu.shape
    Fh = F2 // 2
    ts = _tile(S, 256)

    def body(gu_ref, d_ref, o_ref):
        g = gu_ref[:, :Fh].astype(F32)
        u = gu_ref[:, Fh:].astype(F32)
        d = d_ref[...].astype(F32)
        sig = 1.0 / (1.0 + jnp.exp(-g))
        o_ref[:, :Fh] = (d * u * (sig * (1.0 + g * (1.0 - sig)))).astype(o_ref.dtype)
        o_ref[:, Fh:] = (d * (g * sig)).astype(o_ref.dtype)

    return pl.pallas_call(
        body, name=name, grid=(S // ts,),
        in_specs=[pl.BlockSpec((ts, F2), lambda i: (i, 0)), pl.BlockSpec((ts, Fh), lambda i: (i, 0))],
        out_specs=pl.BlockSpec((ts, F2), lambda i: (i, 0)),
        out_shape=jax.ShapeDtypeStruct((S, F2), BF16),
        compiler_params=_params(("parallel",)),
    )(gu, dact)


def loss_head(y, target, *, name):
    S, Dm = y.shape
    ts = _tile(S, 512)

    def body(y_ref, t_ref, l_ref, d_ref):
        @pl.when(pl.program_id(0) == 0)
        def _():
            l_ref[...] = jnp.zeros_like(l_ref)

        e = y_ref[...] - t_ref[...]
        d_ref[...] = e * (1.0 / Dm)
        l_ref[...] += jnp.sum(e * e)

    return pl.pallas_call(
        body, name=name, grid=(S // ts,),
        in_specs=[pl.BlockSpec((ts, Dm), lambda i: (i, 0)), pl.BlockSpec((ts, Dm), lambda i: (i, 0))],
        out_specs=[_const_spec((8, LANES)), pl.BlockSpec((ts, Dm), lambda i: (i, 0))],
        out_shape=[jax.ShapeDtypeStruct((8, LANES), F32), jax.ShapeDtypeStruct((S, Dm), F32)],
        compiler_params=_params(("arbitrary",)),
    )(y, target)


def _head_block_ones():
    i = np.arange(LANES)
    return jnp.asarray((i[:, None] // A_HEAD_DIM == i[None, :] // A_HEAD_DIM).astype(np.float32)).astype(BF16)


def _rope_tables(S):
    half = A_HEAD_DIM // 2
    inv_freq = ROPE_THETA ** (-jnp.arange(half, dtype=F32) / half)
    ang = jnp.arange(S).astype(F32)[:, None] * inv_freq[None, :]
    cos = jnp.tile(jnp.cos(ang), (1, LANES // half))
    sin = jnp.tile(jnp.sin(ang), (1, LANES // half))
    return cos, sin


def _rot_half(x, lo):
    return jnp.where(lo, -pltpu.roll(x, LANES - 32, 1), pltpu.roll(x, 32, 1))


def _seg_sum(v, ones_ref):
    hi = v.astype(BF16)
    lo = (v - hi.astype(F32)).astype(BF16)
    ones = ones_ref[...]
    return jnp.dot(hi, ones, preferred_element_type=F32) + jnp.dot(lo, ones, preferred_element_type=F32)


def _seg_mean(v, ones_ref):
    return _seg_sum(v, ones_ref) * (1.0 / A_HEAD_DIM)


def qk_prep_fwd(qkv, gain, cos, sin, ones, *, name):
    S, W = qkv.shape
    ts = _tile(S, 256)
    nchunk = A_WIDTH // LANES

    def body(x_ref, g_ref, c_ref, s_ref, ones_ref, o_ref):
        kind = pl.program_id(1) % 3

        @pl.when(kind < 2)
        def _():
            scale = jnp.where(kind == 0, A_HEAD_DIM ** -0.5, 1.0).astype(F32)
            lo = (lax.broadcasted_iota(jnp.int32, (ts, LANES), 1) % A_HEAD_DIM) < (A_HEAD_DIM // 2)
            cv, sv = c_ref[...], s_ref[...]
            for c in range(nchunk):
                sl = slice(c * LANES, (c + 1) * LANES)
                xv = x_ref[:, sl]
                r = lax.rsqrt(_seg_mean(xv * xv, ones_ref) + RMS_EPS)
                y = (xv * r) * g_ref[:, sl]
                y = y * cv + _rot_half(y, lo) * sv
                o_ref[:, sl] = (y * scale).astype(o_ref.dtype)

        @pl.when(kind == 2)
        def _():
            o_ref[...] = x_ref[...].astype(o_ref.dtype)

    return pl.pallas_call(
        body, name=name, grid=(S // ts, W // A_WIDTH),
        in_specs=[pl.BlockSpec((ts, A_WIDTH), lambda i, j: (i, j)), pl.BlockSpec((1, A_WIDTH), lambda i, j: (0, j)),
                  pl.BlockSpec((ts, LANES), lambda i, j: (i, 0)), pl.BlockSpec((ts, LANES), lambda i, j: (i, 0)),
                  _const_spec((LANES, LANES))],
        out_specs=pl.BlockSpec((ts, A_WIDTH), lambda i, j: (i, j)),
        out_shape=jax.ShapeDtypeStruct((S, W), BF16),
        compiler_params=_params(("parallel", "arbitrary")),
    )(qkv, gain, cos, sin, ones)


def qk_prep_bwd(qkv, gain, cos, sin, ones, grads, *, name):
    S, W = qkv.shape
    ts = _tile(S, 256)
    nchunk = A_WIDTH // LANES
    nj = W // A_WIDTH

    def body(x_ref, g_ref, c_ref, s_ref, ones_ref, *rest):
        g_refs, (o_ref, dg_ref) = rest[:nj], rest[nj:]
        j = pl.program_id(0)
        kind = j % 3

        @pl.when(pl.program_id(1) == 0)
        def _():
            dg_ref[...] = jnp.zeros_like(dg_ref)

        for n in range(nj):
            @pl.when(j == n)
            def _(n=n):
                d_ref = g_refs[n]
                if n % 3 == 2:
                    o_ref[...] = d_ref[...].astype(o_ref.dtype)
                    return
                scale = A_HEAD_DIM ** -0.5 if n % 3 == 0 else 1.0
                lo = (lax.broadcasted_iota(jnp.int32, (ts, LANES), 1) % A_HEAD_DIM) < (A_HEAD_DIM // 2)
                cv, sv = c_ref[...], s_ref[...]
                for c in range(nchunk):
                    sl = slice(c * LANES, (c + 1) * LANES)
                    dy = d_ref[:, sl].astype(F32) * scale
                    dn = dy * cv - _rot_half(dy, lo) * sv
                    xv = x_ref[:, sl]
                    r = lax.rsqrt(_seg_mean(xv * xv, ones_ref) + RMS_EPS)
                    xhat = xv * r
                    dyg = dn * g_ref[:, sl]
                    dx = r * (dyg - xhat * _seg_mean(dyg * xhat, ones_ref))
                    o_ref[:, sl] = dx.astype(o_ref.dtype)
                    dg_ref[0:1, sl] += jnp.sum(dn * xhat, axis=0, keepdims=True)

    def gspec(n):
        return pl.BlockSpec((ts, A_WIDTH), lambda j, i: (jnp.where(j == n, i, 0), 0))

    return pl.pallas_call(
        body, name=name, grid=(nj, S // ts),
        in_specs=[pl.BlockSpec((ts, A_WIDTH), lambda j, i: (i, j)), pl.BlockSpec((1, A_WIDTH), lambda j, i: (0, j)),
                  pl.BlockSpec((ts, LANES), lambda j, i: (i, 0)), pl.BlockSpec((ts, LANES), lambda j, i: (i, 0)),
                  _const_spec((LANES, LANES))] + [gspec(n) for n in range(nj)],
        out_specs=[pl.BlockSpec((ts, A_WIDTH), lambda j, i: (i, j)), pl.BlockSpec((8, A_WIDTH), lambda j, i: (0, j))],
        out_shape=[jax.ShapeDtypeStruct((S, W), BF16), jax.ShapeDtypeStruct((8, W), F32)],
        compiler_params=_params(("arbitrary", "arbitrary")),
    )(qkv, gain, cos, sin, ones, *grads)


def _band_specs(kind, tq, nlb):
    nhb = tq // A_HALF
    nb = nlb // nhb
    base = kind * (A_WIDTH // LANES)
    return [pl.BlockSpec((A_HALF, LANES), lambda ph, b, hp: (ph * nlb + jnp.maximum(b * nhb - 1, 0), base + hp)),
            pl.BlockSpec((tq, LANES), lambda ph, b, hp: (ph * nb + b, base + hp)),
            pl.BlockSpec((A_HALF, LANES), lambda ph, b, hp: (ph * nlb + jnp.minimum((b + 1) * nhb, nlb - 1), base + hp))]


A_BLOCK = 512
A_SUB = 128


def _band_bias(sub, key_major):
    i = np.arange(sub)[:, None]
    j = np.arange(sub + 2 * A_HALF)[None, :] - A_HALF
    ok = np.abs(j - i) <= A_HALF
    return jnp.asarray(np.where(ok.T if key_major else ok, 0.0, NEG_INF).astype(np.float32))


def _edge_bias(first, n, L, axis, at_start, at_end):
    if not (at_start or at_end):
        return None
    shape = (1, n) if axis == 1 else (n, 1)
    pos = first - A_HALF + lax.broadcasted_iota(jnp.int32, shape, axis)
    return jnp.where((pos < 0) | (pos >= L), NEG_INF, 0.0).astype(F32)


def _with_edge(bias, edge):
    return bias if edge is None else bias + edge


def _cat3(a_ref, b_ref, c_ref):
    return jnp.concatenate([a_ref[...], b_ref[...], c_ref[...]], axis=0)


def _lane_lo(rows):
    return lax.broadcasted_iota(jnp.int32, (rows, LANES), 1) < A_HEAD_DIM


NT = (((1,), (1,)), ((), ()))
TN = (((0,), (0,)), ((), ()))


def attn_fwd(qkvp, dil, *, name):
    S = qkvp.shape[0]
    L = S // dil
    tq = _tile(L, A_BLOCK)
    sub = min(tq, A_SUB)
    nsub = tq // sub
    nlb = L // A_HALF
    band = _band_bias(sub, False)

    def body(q_ref, kp_ref, ko_ref, kn_ref, vp_ref, vo_ref, vn_ref, band_ref, o_ref, l_ref):
        b = pl.program_id(1)
        K = _cat3(kp_ref, ko_ref, kn_ref)
        V = _cat3(vp_ref, vo_ref, vn_ref)
        lo_k = _lane_lo(tq + 2 * A_HALF)
        lo_q = _lane_lo(sub)
        Km = [jnp.where(sel, K, jnp.zeros_like(K)) for sel in (lo_k, ~lo_k)]
        Vm = [jnp.where(sel, V, jnp.zeros_like(V)) for sel in (lo_k, ~lo_k)]
        for r in range(nsub):
            rows = slice(r * sub, (r + 1) * sub)
            keys = slice(r * sub, (r + 1) * sub + 2 * A_HALF)
            bias = _with_edge(band_ref[...], _edge_bias(b * tq + r * sub, sub + 2 * A_HALF, L, 1, r == 0, r == nsub - 1))
            q = q_ref[rows, :]
            outs, lses = [], []
            for hh in range(2):
                s = lax.dot_general(q, Km[hh][keys], NT, preferred_element_type=F32) + bias
                m = jnp.max(s, axis=1, keepdims=True)
                p = jnp.exp(s - m)
                l = jnp.sum(p, axis=1, keepdims=True)
                outs.append(jnp.dot(p.astype(BF16), Vm[hh][keys], preferred_element_type=F32) * (1.0 / l))
                lses.append(m + jnp.log(l))
            o_ref[rows, :] = outs[0] + outs[1]
            l_ref[rows, :] = jnp.where(lo_q, lses[0], lses[1])

    ospec = _band_specs(0, tq, nlb)[1]
    return pl.pallas_call(
        body, name=name, grid=(dil, L // tq, A_WIDTH // LANES),
        in_specs=[_band_specs(0, tq, nlb)[1]] + _band_specs(1, tq, nlb) + _band_specs(2, tq, nlb)
        + [_const_spec(band.shape)],
        out_specs=[ospec, ospec],
        out_shape=[jax.ShapeDtypeStruct((S, A_WIDTH), F32)] * 2,
        compiler_params=_params(("parallel", "parallel", "parallel")),
    )(*([qkvp] * 7), band)


def attn_merge(os_, lses, dils, *, name):
    S = os_[0].shape[0]
    ts = _tile(S, 256)
    ng = len(dils)

    def body(*refs):
        o_refs, l_refs, out_ref = refs[:ng], refs[ng:2 * ng], refs[2 * ng]
        lse_refs, scrs = refs[2 * ng + 1:3 * ng + 1], refs[3 * ng + 1:]
        ov, ls, k = [], [], 0
        for d, o_ref, l_ref in zip(dils, o_refs, l_refs):
            if d == 1:
                ov.append(o_ref[...])
                ls.append(l_ref[...])
            else:
                _load_phases(o_ref, scrs[k], d, ts)
                _load_phases(l_ref, scrs[k + 1], d, ts)
                ov.append(_get_natural(scrs[k]))
                ls.append(_get_natural(scrs[k + 1]))
                k += 2
        m = functools.reduce(jnp.maximum, ls)
        es = [jnp.exp(l - m) for l in ls]
        tot = functools.reduce(jnp.add, es)
        acc = None
        for e, o in zip(es, ov):
            t = (e / tot) * o
            acc = t if acc is None else acc + t
        out_ref[...] = acc.astype(out_ref.dtype)
        total = m + jnp.log(tot)
        _put_natural(scrs[k], total)
        for d, lse_ref in zip(dils, lse_refs):
            if d == 1:
                lse_ref[...] = total
            else:
                _store_phases(scrs[k], lse_ref, d, ts)

    n_scr = 2 * sum(d > 1 for d in dils) + 1
    outs = pl.pallas_call(
        body, name=name, grid=(S // ts,),
        in_specs=[_phase_spec(d, ts, A_WIDTH) for d in dils] * 2,
        out_specs=[pl.BlockSpec((ts, A_WIDTH), lambda i: (i, 0))] + [_phase_spec(d, ts, A_WIDTH) for d in dils],
        out_shape=[jax.ShapeDtypeStruct((S, A_WIDTH), BF16)] + [_phase_shape(S, A_WIDTH, d, F32) for d in dils],
        scratch_shapes=[_nat_scratch(ts, A_WIDTH)] * n_scr,
        compiler_params=_params(("parallel",)),
    )(*[_phase_view(o, d) for o, d in zip(os_, dils)], *[_phase_view(l, d) for l, d in zip(lses, dils)])
    return outs[0], [l.reshape(S, A_WIDTH) for l in outs[1:]]


def attn_delta(dout, out, ones, dils, *, name):
    S = dout.shape[0]
    ts = _tile(S, 256)
    nd = len(dils)

    def body(d_ref, o_ref, ones_ref, *rest):
        do_refs, dl_refs, (scr_do, scr_dl) = rest[:nd], rest[nd:2 * nd], rest[2 * nd:]
        sums = []
        for c in range(A_WIDTH // LANES):
            sl = slice(c * LANES, (c + 1) * LANES)
            prod = d_ref[:, sl].astype(F32) * o_ref[:, sl].astype(F32)
            sums.append(_seg_sum(prod, ones_ref))
        _put_natural(scr_do, d_ref[...].astype(F32))
        _put_natural(scr_dl, jnp.concatenate(sums, axis=1))
        for d, do_ref, dl_ref in zip(dils, do_refs, dl_refs):
            if d == 1:
                do_ref[...] = d_ref[...]
                dl_ref[...] = _get_natural(scr_dl)
            else:
                _store_phases(scr_do, do_ref, d, ts)
                _store_phases(scr_dl, dl_ref, d, ts)

    spec = pl.BlockSpec((ts, A_WIDTH), lambda i: (i, 0))
    outs = pl.pallas_call(
        body, name=name, grid=(S // ts,), in_specs=[spec, spec, _const_spec((LANES, LANES))],
        out_specs=[_phase_spec(d, ts, A_WIDTH) for d in dils] * 2,
        out_shape=[_phase_shape(S, A_WIDTH, d, BF16) for d in dils] + [_phase_shape(S, A_WIDTH, d, F32) for d in dils],
        scratch_shapes=[_nat_scratch(ts, A_WIDTH)] * 2,
        compiler_params=_params(("parallel",)),
    )(dout, out, ones)
    outs = [o.reshape(S, A_WIDTH) for o in outs]
    return outs[:nd], outs[nd:]


def _head_col(x, hh):
    c = hh * A_HEAD_DIM
    return x[:, c:c + 1]


def attn_bwd_dq(qkvp, dout, lse, delta, dil, *, name):
    S = qkvp.shape[0]
    L = S // dil
    tq = _tile(L, A_BLOCK)
    sub = min(tq, A_SUB)
    nsub = tq // sub
    nlb = L // A_HALF
    band = _band_bias(sub, False)

    def body(q_ref, kp_ref, ko_ref, kn_ref, vp_ref, vo_ref, vn_ref, do_ref, l_ref, d_ref, band_ref, dq_ref):
        b = pl.program_id(1)
        K = _cat3(kp_ref, ko_ref, kn_ref)
        V = _cat3(vp_ref, vo_ref, vn_ref)
        lo_k = _lane_lo(tq + 2 * A_HALF)
        Km = [jnp.where(sel, K, jnp.zeros_like(K)) for sel in (lo_k, ~lo_k)]
        Vm = [jnp.where(sel, V, jnp.zeros_like(V)) for sel in (lo_k, ~lo_k)]
        for r in range(nsub):
            rows = slice(r * sub, (r + 1) * sub)
            keys = slice(r * sub, (r + 1) * sub + 2 * A_HALF)
            bias = _with_edge(band_ref[...], _edge_bias(b * tq + r * sub, sub + 2 * A_HALF, L, 1, r == 0, r == nsub - 1))
            q, do = q_ref[rows, :], do_ref[rows, :]
            lse_v, dl_v = l_ref[rows, :], d_ref[rows, :]
            acc = None
            for hh in range(2):
                s = lax.dot_general(q, Km[hh][keys], NT, preferred_element_type=F32) + bias
                p = jnp.exp(s - _head_col(lse_v, hh))
                dp = lax.dot_general(do, Vm[hh][keys], NT, preferred_element_type=F32)
                ds = p * (dp - _head_col(dl_v, hh))
                t = jnp.dot(ds.astype(BF16), Km[hh][keys], preferred_element_type=F32)
                acc = t if acc is None else acc + t
            dq_ref[rows, :] = acc.astype(dq_ref.dtype)

    nspec = _band_specs(0, tq, nlb)[1]
    return pl.pallas_call(
        body, name=name, grid=(dil, L // tq, A_WIDTH // LANES),
        in_specs=[nspec] + _band_specs(1, tq, nlb) + _band_specs(2, tq, nlb) + [nspec, nspec, nspec, _const_spec(band.shape)],
        out_specs=nspec,
        out_shape=jax.ShapeDtypeStruct((S, A_WIDTH), BF16),
        compiler_params=_params(("parallel", "parallel", "parallel")),
    )(*([qkvp] * 7), dout, lse, delta, band)


def attn_bwd_dkv(qkvp, dout, lse, delta, dil, *, name):
    S = qkvp.shape[0]
    L = S // dil
    tk = _tile(L, A_BLOCK)
    sub = min(tk, A_SUB)
    nsub = tk // sub
    nlb = L // A_HALF
    band = _band_bias(sub, False)

    def body(qp_ref, qo_ref, qn_ref, k_ref, v_ref, dp_ref, do_ref, dn_ref, lp_ref, lo_ref, ln_ref,
             ep_ref, eo_ref, en_ref, band_ref, dk_ref, dv_ref):
        b = pl.program_id(1)
        Q = _cat3(qp_ref, qo_ref, qn_ref)
        DO = _cat3(dp_ref, do_ref, dn_ref)
        lse_v = _cat3(lp_ref, lo_ref, ln_ref)
        dl_v = _cat3(ep_ref, eo_ref, en_ref)
        lo_q = _lane_lo(tk + 2 * A_HALF)
        Qm = [jnp.where(sel, Q, jnp.zeros_like(Q)) for sel in (lo_q, ~lo_q)]
        DOm = [jnp.where(sel, DO, jnp.zeros_like(DO)) for sel in (lo_q, ~lo_q)]
        for r in range(nsub):
            keys = slice(r * sub, (r + 1) * sub)
            qs = slice(r * sub, (r + 1) * sub + 2 * A_HALF)
            bias = _with_edge(band_ref[...], _edge_bias(b * tk + r * sub, sub + 2 * A_HALF, L, 1, r == 0, r == nsub - 1))
            K, V = k_ref[keys, :], v_ref[keys, :]
            lse_t, dl_t = lse_v[qs].T, dl_v[qs].T
            dk = dv = None
            for hh in range(2):
                hr = slice(hh * A_HEAD_DIM, hh * A_HEAD_DIM + 1)
                st = lax.dot_general(K, Qm[hh][qs], NT, preferred_element_type=F32) + bias
                pt = jnp.exp(st - lse_t[hr, :])
                dpt = lax.dot_general(V, DOm[hh][qs], NT, preferred_element_type=F32)
                dst = pt * (dpt - dl_t[hr, :])
                tv = jnp.dot(pt.astype(BF16), DOm[hh][qs], preferred_element_type=F32)
                tk_ = jnp.dot(dst.astype(BF16), Qm[hh][qs], preferred_element_type=F32)
                dv = tv if dv is None else dv + tv
                dk = tk_ if dk is None else dk + tk_
            dk_ref[keys, :] = dk.astype(dk_ref.dtype)
            dv_ref[keys, :] = dv.astype(dv_ref.dtype)

    nspec = _band_specs(0, tk, nlb)[1]
    return pl.pallas_call(
        body, name=name, grid=(dil, L // tk, A_WIDTH // LANES),
        in_specs=_band_specs(0, tk, nlb) + [_band_specs(1, tk, nlb)[1], _band_specs(2, tk, nlb)[1]]
        + _band_specs(0, tk, nlb) * 3 + [_const_spec(band.shape)],
        out_specs=[nspec, nspec],
        out_shape=[jax.ShapeDtypeStruct((S, A_WIDTH), BF16)] * 2,
        compiler_params=_params(("parallel", "parallel", "parallel")),
    )(*([qkvp] * 5), *([dout] * 3), *([lse] * 3), *([delta] * 3), band)


def _gate_block_weight(wf, wb):
    w = jnp.zeros((LANES, 2 * B_QK_WIDTH), F32)
    w = w.at[:B_GATE_RANK, :B_QK_WIDTH].set(wf)
    w = w.at[B_GATE_RANK:2 * B_GATE_RANK, B_QK_WIDTH:].set(wb)
    return w.astype(BF16)


def gate_fwd(proj, wblk, bias, *, name):
    S = proj.shape[0]
    ts = _tile(S, 512)
    W = 2 * B_QK_WIDTH
    zcol = (2 * B_QK_WIDTH + 2 * B_V_WIDTH) // LANES

    def body(z_ref, w_ref, b_ref, o_ref):
        x = jnp.dot(z_ref[...].astype(BF16), w_ref[...], preferred_element_type=F32) + b_ref[...]
        o_ref[...] = (jnp.minimum(x, 0.0) - jnp.log(1.0 + jnp.exp(-jnp.abs(x)))) * (1.0 / B_GATE_TAU)

    return pl.pallas_call(
        body, name=name, grid=(S // ts,),
        in_specs=[pl.BlockSpec((ts, LANES), lambda i: (i, zcol)), _const_spec((LANES, W)), _const_spec((1, W))],
        out_specs=pl.BlockSpec((ts, W), lambda i: (i, 0)),
        out_shape=jax.ShapeDtypeStruct((S, W), F32),
        compiler_params=_params(("parallel",)),
    )(proj, wblk, bias)


def gate_bwd(proj, wblk, wblk_t, bias, dloga, *, name):
    S = proj.shape[0]
    ts = _tile(S, 512)
    W = 2 * B_QK_WIDTH
    zcol = (2 * B_QK_WIDTH + 2 * B_V_WIDTH) // LANES

    def body(z_ref, w_ref, wt_ref, b_ref, d_ref, dz_ref, dw_ref, db_ref):
        @pl.when(pl.program_id(0) == 0)
        def _():
            dw_ref[...] = jnp.zeros_like(dw_ref)
            db_ref[...] = jnp.zeros_like(db_ref)

        z = z_ref[...].astype(BF16)
        x = jnp.dot(z, w_ref[...], preferred_element_type=F32) + b_ref[...]
        e = jnp.exp(-jnp.abs(x))
        sig_neg = jnp.where(x >= 0, e, 1.0) / (1.0 + e)
        dx = d_ref[...] * (1.0 / B_GATE_TAU) * sig_neg
        dxb = dx.astype(BF16)
        dz_ref[...] = jnp.dot(dxb, wt_ref[...], preferred_element_type=F32)
        dw_ref[...] += lax.dot_general(z, dxb, TN, preferred_element_type=F32)
        db_ref[0:1, :] += jnp.sum(dx, axis=0, keepdims=True)

    return pl.pallas_call(
        body, name=name, grid=(S // ts,),
        in_specs=[pl.BlockSpec((ts, LANES), lambda i: (i, zcol)), _const_spec((LANES, W)), _const_spec((W, LANES)),
                  _const_spec((1, W)), pl.BlockSpec((ts, W), lambda i: (i, 0))],
        out_specs=[pl.BlockSpec((ts, LANES), lambda i: (i, 0)), _const_spec((LANES, W)), _const_spec((8, W))],
        out_shape=[jax.ShapeDtypeStruct((S, LANES), F32), jax.ShapeDtypeStruct((LANES, W), F32),
                   jax.ShapeDtypeStruct((8, W), F32)],
        compiler_params=_params(("arbitrary",)),
    )(proj, wblk, wblk_t, bias, dloga)


def _tri(reverse):
    i = np.arange(B_CHUNK)
    t = (i[None, :] >= i[:, None]) if reverse else (i[None, :] <= i[:, None])
    return jnp.asarray(t.astype(np.float32))


def _gla_terms(q, k, la, t_ref, reverse):
    b = jnp.dot(t_ref[...], la, precision=HI, preferred_element_type=F32)
    b_last = b[0:1, :] if reverse else b[B_CHUNK - 1:B_CHUNK, :]
    e_b = jnp.exp(b)
    qt = (q * (B_KEY_DIM ** -0.5)) * e_b
    e_nb = jnp.exp(-b)
    kt = k * e_nb
    e_end = jnp.exp(b_last - b)
    kend = k * e_end
    dec = jnp.exp(b_last)
    return e_nb, e_b, qt, kt, e_end, kend, dec


def _chunk_mask(reverse, transpose=False):
    r = lax.broadcasted_iota(jnp.int32, (B_CHUNK, B_CHUNK), 0)
    c = lax.broadcasted_iota(jnp.int32, (B_CHUNK, B_CHUNK), 1)
    if transpose:
        r, c = c, r
    return (c > r) if reverse else (c <= r)


def gla_fwd(proj, loga, tmat, reverse, *, name):
    S = proj.shape[0]
    tb = _tile(S, 512)
    nb = S // tb
    cpb = tb // B_CHUNK
    nc = S // B_CHUNK
    qb, kb_, vb = 0, B_QK_WIDTH // B_KEY_DIM, 2 * B_QK_WIDTH // B_VAL_DIM
    lb = (B_QK_WIDTH // B_KEY_DIM) if reverse else 0
    blk = (lambda i: nb - 1 - i) if reverse else (lambda i: i)

    def body(q_ref, k_ref, v_ref, la_ref, t_ref, o_ref, st_ref, s_scr):
        @pl.when(pl.program_id(1) == 0)
        def _():
            s_scr[...] = jnp.zeros_like(s_scr)

        mask = _chunk_mask(reverse)
        order = range(cpb - 1, -1, -1) if reverse else range(cpb)
        for c in order:
            rows = slice(c * B_CHUNK, (c + 1) * B_CHUNK)
            v = v_ref[rows, :].astype(BF16)
            _, _, qt, kt, _, kend, dec = _gla_terms(q_ref[rows, :], k_ref[rows, :], la_ref[rows, :], t_ref, reverse)
            qt, kt, kend = qt.astype(BF16), kt.astype(BF16), kend.astype(BF16)
            st = s_scr[...]
            st_ref[0, c] = st
            a = jnp.where(mask, lax.dot_general(qt, kt, NT, preferred_element_type=F32), 0.0)
            o = jnp.dot(a.astype(BF16), v, preferred_element_type=F32)
            o = o + lax.dot_general(qt, st.astype(BF16), NT, preferred_element_type=F32)
            o_ref[rows, :] = o
            s_scr[...] = st * dec + lax.dot_general(v, kend, TN, preferred_element_type=F32)

    return pl.pallas_call(
        body, name=name, grid=(B_HEADS, nb),
        in_specs=[pl.BlockSpec((tb, B_KEY_DIM), lambda h, i: (blk(i), qb + h)),
                  pl.BlockSpec((tb, B_KEY_DIM), lambda h, i: (blk(i), kb_ + h)),
                  pl.BlockSpec((tb, B_VAL_DIM), lambda h, i: (blk(i), vb + h)),
                  pl.BlockSpec((tb, B_KEY_DIM), lambda h, i: (blk(i), lb + h)),
                  _const_spec((B_CHUNK, B_CHUNK))],
        out_specs=[pl.BlockSpec((tb, B_VAL_DIM), lambda h, i: (blk(i), h)),
                   pl.BlockSpec((1, cpb, B_VAL_DIM, B_KEY_DIM), lambda h, i: (h, blk(i), 0, 0))],
        out_shape=[jax.ShapeDtypeStruct((S, B_V_WIDTH), F32),
                   jax.ShapeDtypeStruct((B_HEADS, nc, B_VAL_DIM, B_KEY_DIM), F32)],
        scratch_shapes=[pltpu.VMEM((B_VAL_DIM, B_KEY_DIM), F32)],
        compiler_params=_params(("parallel", "arbitrary")),
    )(proj, proj, proj, loga, tmat)


def gla_bwd(proj, loga, states, do, tmat, tmat_t, reverse, *, name):
    S = proj.shape[0]
    tb = _tile(S, 512)
    nb = S // tb
    cpb = tb // B_CHUNK
    qb, kb_, vb = 0, B_QK_WIDTH // B_KEY_DIM, 2 * B_QK_WIDTH // B_VAL_DIM
    lb = (B_QK_WIDTH // B_KEY_DIM) if reverse else 0
    blk = (lambda i: i) if reverse else (lambda i: nb - 1 - i)
    scale = B_KEY_DIM ** -0.5

    def body(q_ref, k_ref, v_ref, la_ref, st_ref, do_ref, t_ref, tt_ref, dq_ref, dk_ref, dv_ref, dla_ref, ds_scr):
        @pl.when(pl.program_id(1) == 0)
        def _():
            ds_scr[...] = jnp.zeros_like(ds_scr)

        mask = _chunk_mask(reverse)
        mask_t = _chunk_mask(reverse, transpose=True)
        last = 0 if reverse else B_CHUNK - 1
        is_last = lax.broadcasted_iota(jnp.int32, (B_CHUNK, B_KEY_DIM), 0) == last
        order = range(cpb) if reverse else range(cpb - 1, -1, -1)
        for c in order:
            rows = slice(c * B_CHUNK, (c + 1) * B_CHUNK)
            vf = v_ref[rows, :]
            v = vf.astype(BF16)
            dov = do_ref[rows, :]
            dob = dov.astype(BF16)
            e_nb, e_b, qt, kt, e_end, kend, dec = _gla_terms(q_ref[rows, :], k_ref[rows, :], la_ref[rows, :], t_ref, reverse)
            qtb, ktb, kendb = qt.astype(BF16), kt.astype(BF16), kend.astype(BF16)
            st = st_ref[0, c]
            dst = ds_scr[...]
            dstb = dst.astype(BF16)
            a_t = jnp.where(mask_t, lax.dot_general(ktb, qtb, NT, preferred_element_type=F32), 0.0)
            da = jnp.where(mask, lax.dot_general(dob, v, NT, preferred_element_type=F32), 0.0)
            da_t = jnp.where(mask_t, lax.dot_general(v, dob, NT, preferred_element_type=F32), 0.0)
            dv = jnp.dot(a_t.astype(BF16), dob, preferred_element_type=F32)
            dv = dv + lax.dot_general(kendb, dstb, NT, preferred_element_type=F32)
            dqt = jnp.dot(da.astype(BF16), ktb, preferred_element_type=F32)
            dqt = dqt + jnp.dot(dob, st.astype(BF16), preferred_element_type=F32)
            dkt = jnp.dot(da_t.astype(BF16), qtb, preferred_element_type=F32)
            dkend = jnp.dot(v, dstb, preferred_element_type=F32)
            ddec = jnp.sum(dst * st, axis=0, keepdims=True)
            ds_scr[...] = dst * dec + lax.dot_general(dob, qtb, TN, preferred_element_type=F32)
            ke = dkend * kend
            db = dqt * qt - dkt * kt - ke
            db_last = jnp.sum(ke, axis=0, keepdims=True) + ddec * dec
            db = db + jnp.where(is_last, db_last, 0.0)
            dq_ref[rows, :] = dqt * e_b * scale
            dk_ref[rows, :] = dkt * e_nb + dkend * e_end
            dv_ref[rows, :] = dv
            dla_ref[rows, :] = jnp.dot(tt_ref[...], db, precision=HI, preferred_element_type=F32)

    kspec = lambda cb: pl.BlockSpec((tb, B_KEY_DIM), lambda h, i: (blk(i), cb + h))
    return pl.pallas_call(
        body, name=name, grid=(B_HEADS, nb),
        in_specs=[kspec(qb), kspec(kb_), pl.BlockSpec((tb, B_VAL_DIM), lambda h, i: (blk(i), vb + h)), kspec(lb),
                  pl.BlockSpec((1, cpb, B_VAL_DIM, B_KEY_DIM), lambda h, i: (h, blk(i), 0, 0)),
                  pl.BlockSpec((tb, B_VAL_DIM), lambda h, i: (blk(i), h)),
                  _const_spec((B_CHUNK, B_CHUNK)), _const_spec((B_CHUNK, B_CHUNK))],
        out_specs=[kspec(0), kspec(0), pl.BlockSpec((tb, B_VAL_DIM), lambda h, i: (blk(i), h)), kspec(0)],
        out_shape=[jax.ShapeDtypeStruct((S, B_QK_WIDTH), F32), jax.ShapeDtypeStruct((S, B_QK_WIDTH), F32),
                   jax.ShapeDtypeStruct((S, B_V_WIDTH), F32), jax.ShapeDtypeStruct((S, B_QK_WIDTH), F32)],
        scratch_shapes=[pltpu.VMEM((B_VAL_DIM, B_KEY_DIM), F32)],
        compiler_params=_params(("parallel", "arbitrary")),
    )(proj, proj, proj, loga, states, do, tmat, tmat_t)


def gla_post_fwd(o_f, o_b, gain, proj, *, name):
    S = o_f.shape[0]
    ts = _tile(S, 512)
    rcol = (2 * B_QK_WIDTH + B_V_WIDTH) // B_V_WIDTH

    def body(f_ref, b_ref, g_ref, r_ref, y_ref):
        for h in range(B_HEADS):
            sl = slice(h * B_VAL_DIM, (h + 1) * B_VAL_DIM)
            o = f_ref[:, sl] + b_ref[:, sl]
            n = (o * lax.rsqrt(jnp.mean(o * o, axis=-1, keepdims=True) + RMS_EPS)) * g_ref[:, sl]
            r = r_ref[:, sl]
            y_ref[:, sl] = (n * (r * (1.0 / (1.0 + jnp.exp(-r))))).astype(y_ref.dtype)

    spec = pl.BlockSpec((ts, B_V_WIDTH), lambda i: (i, 0))
    return pl.pallas_call(
        body, name=name, grid=(S // ts,),
        in_specs=[spec, spec, _const_spec((1, B_V_WIDTH)), pl.BlockSpec((ts, B_V_WIDTH), lambda i: (i, rcol))],
        out_specs=spec, out_shape=jax.ShapeDtypeStruct((S, B_V_WIDTH), BF16),
        compiler_params=_params(("parallel",)),
    )(o_f, o_b, gain, proj)


def gla_post_bwd(o_f, o_b, gain, proj, dy, *, name):
    S = o_f.shape[0]
    ts = _tile(S, 512)
    rcol = (2 * B_QK_WIDTH + B_V_WIDTH) // B_V_WIDTH

    def body(f_ref, b_ref, g_ref, r_ref, dy_ref, do_ref, dr_ref, dg_ref):
        @pl.when(pl.program_id(0) == 0)
        def _():
            dg_ref[...] = jnp.zeros_like(dg_ref)

        for h in range(B_HEADS):
            sl = slice(h * B_VAL_DIM, (h + 1) * B_VAL_DIM)
            o = f_ref[:, sl] + b_ref[:, sl]
            rs = lax.rsqrt(jnp.mean(o * o, axis=-1, keepdims=True) + RMS_EPS)
            ohat = o * rs
            g = g_ref[:, sl]
            r = r_ref[:, sl]
            sig = 1.0 / (1.0 + jnp.exp(-r))
            dyv = dy_ref[:, sl].astype(F32)
            dn = dyv * (r * sig)
            dr_ref[:, sl] = dyv * (ohat * g) * (sig * (1.0 + r * (1.0 - sig)))
            dng = dn * g
            do_ref[:, sl] = rs * (dng - ohat * jnp.mean(dng * ohat, axis=-1, keepdims=True))
            dg_ref[0:1, sl] += jnp.sum(dn * ohat, axis=0, keepdims=True)

    spec = pl.BlockSpec((ts, B_V_WIDTH), lambda i: (i, 0))
    return pl.pallas_call(
        body, name=name, grid=(S // ts,),
        in_specs=[spec, spec, _const_spec((1, B_V_WIDTH)), pl.BlockSpec((ts, B_V_WIDTH), lambda i: (i, rcol)), spec],
        out_specs=[spec, spec, _const_spec((8, B_V_WIDTH))],
        out_shape=[jax.ShapeDtypeStruct((S, B_V_WIDTH), F32), jax.ShapeDtypeStruct((S, B_V_WIDTH), F32),
                   jax.ShapeDtypeStruct((8, B_V_WIDTH), F32)],
        compiler_params=_params(("arbitrary",)),
    )(o_f, o_b, gain, proj, dy)


def gla_combine(parts_f, parts_b, dr, dz, *, name):
    S = dr.shape[0]
    ts = _tile(S, 512)

    def body(qf, kf, vf, qb, kb, vb, r_ref, z_ref, o_ref):
        o_ref[:, 0:512] = (qf[...] + qb[...]).astype(o_ref.dtype)
        o_ref[:, 512:1024] = (kf[...] + kb[...]).astype(o_ref.dtype)
        o_ref[:, 1024:2048] = (vf[...] + vb[...]).astype(o_ref.dtype)
        o_ref[:, 2048:3072] = r_ref[...].astype(o_ref.dtype)
        o_ref[:, 3072:3200] = z_ref[...].astype(o_ref.dtype)

    s512 = pl.BlockSpec((ts, B_QK_WIDTH), lambda i: (i, 0))
    s1024 = pl.BlockSpec((ts, B_V_WIDTH), lambda i: (i, 0))
    return pl.pallas_call(
        body, name=name, grid=(S // ts,),
        in_specs=[s512, s512, s1024, s512, s512, s1024, s1024, pl.BlockSpec((ts, LANES), lambda i: (i, 0))],
        out_specs=pl.BlockSpec((ts, B_IN_PAD), lambda i: (i, 0)),
        out_shape=jax.ShapeDtypeStruct((S, B_IN_PAD), BF16),
        compiler_params=_params(("parallel",)),
    )(*parts_f, *parts_b, dr, dz)


def adamw(w, g, m, v, *, name):
    R, C = w.shape
    tr = _tile(R, 256)
    c1 = 1.0 / (1.0 - ADAM_B1 ** ADAM_STEP)
    c2 = 1.0 / (1.0 - ADAM_B2 ** ADAM_STEP)

    def body(w_ref, g_ref, m_ref, v_ref, d_ref, mo_ref, vo_ref):
        gv = g_ref[...]
        mn = ADAM_B1 * m_ref[...] + (1.0 - ADAM_B1) * gv
        vn = ADAM_B2 * v_ref[...] + (1.0 - ADAM_B2) * (gv * gv)
        mo_ref[...] = mn
        vo_ref[...] = vn
        d_ref[...] = -ADAM_LR * ((mn * c1) / (jnp.sqrt(vn * c2) + ADAM_EPS) + ADAM_WD * w_ref[...])

    spec = pl.BlockSpec((tr, C), lambda i: (i, 0))
    return pl.pallas_call(
        body, name=name, grid=(R // tr,), in_specs=[spec] * 4, out_specs=[spec] * 3,
        out_shape=[jax.ShapeDtypeStruct((R, C), F32)] * 3,
        compiler_params=_params(("parallel",)),
    )(w, g, m, v)


def _chip_peers():
    x, y, c = lax.axis_index("x"), lax.axis_index("y"), lax.axis_index("c")
    return x, y, c, [(1 - x, y), (x, 1 - y), (1 - x, 1 - y)]


_ANY = pl.BlockSpec(memory_space=pl.ANY)


def gather_shards(src, *, name):
    _, R, C = src.shape

    def body(src_ref, out_ref, ici_send, ici_recv, d2d_send, d2d_recv, local_sem):
        x, y, c, chips = _chip_peers()
        me = 2 * x + y
        sibling = (x, y, 1 - c)
        mine = pltpu.make_async_copy(src_ref, out_ref.at[me], local_sem)
        mine.start()
        sends = []
        for k, (px, py) in enumerate(chips):
            cp = pltpu.make_async_remote_copy(
                src_ref=src_ref.at[c], dst_ref=out_ref.at[me, c], send_sem=ici_send.at[k], recv_sem=ici_recv.at[k],
                device_id=(px, py, c), device_id_type=MESH)
            cp.start()
            sends.append(cp)
        for k, (px, py) in enumerate(chips):
            landed = out_ref.at[2 * px + py, c]
            pltpu.make_async_remote_copy(
                src_ref=src_ref.at[c], dst_ref=landed, send_sem=ici_send.at[k], recv_sem=ici_recv.at[k],
                device_id=(px, py, c), device_id_type=MESH).wait_recv()
            cp = pltpu.make_async_remote_copy(
                src_ref=landed, dst_ref=landed, send_sem=d2d_send.at[k], recv_sem=d2d_recv.at[k],
                device_id=sibling, device_id_type=MESH)
            cp.start()
            sends.append(cp)
        for k, (px, py) in enumerate(chips):
            other_half = out_ref.at[2 * px + py, 1 - c]
            pltpu.make_async_remote_copy(
                src_ref=other_half, dst_ref=other_half, send_sem=d2d_send.at[k], recv_sem=d2d_recv.at[k],
                device_id=sibling, device_id_type=MESH).wait_recv()
        for cp in sends:
            cp.wait_send()
        mine.wait()

    return pl.pallas_call(
        body, name=name, in_specs=[_ANY], out_specs=_ANY,
        out_shape=jax.ShapeDtypeStruct((N_CHIPS, 2, R, C), src.dtype),
        scratch_shapes=[pltpu.SemaphoreType.DMA((3,))] * 4 + [pltpu.SemaphoreType.DMA],
        compiler_params=pltpu.CompilerParams(has_side_effects=True),
    )(src)


def chip_exchange(srcs, *, name):
    n = len(srcs)

    def body(*refs):
        src_refs, out_refs = refs[:n], refs[n:2 * n]
        send_sems, recv_sems, local_sems = refs[2 * n:]
        x, y, c, chips = _chip_peers()
        me = 2 * x + y
        copies, local = [], []
        for i, (src_ref, out_ref) in enumerate(zip(src_refs, out_refs)):
            mine = pltpu.make_async_copy(src_ref.at[me], out_ref.at[me], local_sems.at[i])
            mine.start()
            local.append(mine)
            for k, (px, py) in enumerate(chips):
                cp = pltpu.make_async_remote_copy(
                    src_ref=src_ref.at[2 * px + py], dst_ref=out_ref.at[me],
                    send_sem=send_sems.at[i, k], recv_sem=recv_sems.at[i, k],
                    device_id=(px, py, c), device_id_type=MESH)
                cp.start()
                copies.append(cp)
        for i, (src_ref, out_ref) in enumerate(zip(src_refs, out_refs)):
            for k, (px, py) in enumerate(chips):
                pltpu.make_async_remote_copy(
                    src_ref=src_ref.at[me], dst_ref=out_ref.at[2 * px + py],
                    send_sem=send_sems.at[i, k], recv_sem=recv_sems.at[i, k],
                    device_id=(px, py, c), device_id_type=MESH).wait_recv()
        for cp in copies:
            cp.wait_send()
        for mine in local:
            mine.wait()

    return pl.pallas_call(
        body, name=name, in_specs=[_ANY] * n, out_specs=[_ANY] * n,
        out_shape=[jax.ShapeDtypeStruct(s.shape, s.dtype) for s in srcs],
        scratch_shapes=[pltpu.SemaphoreType.DMA((n, 3)), pltpu.SemaphoreType.DMA((n, 3)), pltpu.SemaphoreType.DMA((n,))],
        compiler_params=pltpu.CompilerParams(has_side_effects=True),
    )(*srcs)


def sibling_exchange(srcs, *, name):
    n = len(srcs)

    def body(*refs):
        src_refs, out_refs, (send_sems, recv_sems) = refs[:n], refs[n:2 * n], refs[2 * n:]
        x, y, c = lax.axis_index("x"), lax.axis_index("y"), lax.axis_index("c")
        copies = []
        for i, (src_ref, out_ref) in enumerate(zip(src_refs, out_refs)):
            cp = pltpu.make_async_remote_copy(
                src_ref=src_ref.at[:, 1 - c], dst_ref=out_ref, send_sem=send_sems.at[i], recv_sem=recv_sems.at[i],
                device_id=(x, y, 1 - c), device_id_type=MESH)
            cp.start()
            copies.append(cp)
        for cp in copies:
            cp.wait()

    return pl.pallas_call(
        body, name=name, in_specs=[_ANY] * n, out_specs=[_ANY] * n,
        out_shape=[jax.ShapeDtypeStruct((s.shape[0],) + s.shape[2:], s.dtype) for s in srcs],
        scratch_shapes=[pltpu.SemaphoreType.DMA((n,)), pltpu.SemaphoreType.DMA((n,))],
        compiler_params=pltpu.CompilerParams(has_side_effects=True),
    )(*srcs)


def sibling_share(srcs, *, name):
    n = len(srcs)

    def body(*refs):
        src_refs, out_refs, (send_sems, recv_sems, local_sems) = refs[:n], refs[n:2 * n], refs[2 * n:]
        x, y, c = lax.axis_index("x"), lax.axis_index("y"), lax.axis_index("c")
        copies = []
        for i, (src_ref, out_ref) in enumerate(zip(src_refs, out_refs)):
            mine = pltpu.make_async_copy(src_ref, out_ref.at[c], local_sems.at[i])
            mine.start()
            cp = pltpu.make_async_remote_copy(
                src_ref=src_ref, dst_ref=out_ref.at[c], send_sem=send_sems.at[i], recv_sem=recv_sems.at[i],
                device_id=(x, y, 1 - c), device_id_type=MESH)
            cp.start()
            copies += [mine, cp]
        for cp in copies:
            cp.wait()

    return pl.pallas_call(
        body, name=name, in_specs=[_ANY] * n, out_specs=[_ANY] * n,
        out_shape=[jax.ShapeDtypeStruct((2,) + s.shape, s.dtype) for s in srcs],
        scratch_shapes=[pltpu.SemaphoreType.DMA((n,))] * 3,
        compiler_params=pltpu.CompilerParams(has_side_effects=True),
    )(*srcs)


def add_pair(a, b, out_dtype, *, name):
    n, _, R, C = a.shape
    tr = _tile(R, 256)

    def body(c_ref, a_ref, b_ref, o_ref):
        o_ref[...] = (a_ref[0] + b_ref[...]).astype(o_ref.dtype)

    return pl.pallas_call(
        body, name=name,
        grid_spec=pltpu.PrefetchScalarGridSpec(
            num_scalar_prefetch=1, grid=(n, R // tr),
            in_specs=[pl.BlockSpec((1, 1, tr, C), lambda s, i, c_ref: (s, c_ref[0], i, 0)),
                      pl.BlockSpec((1, tr, C), lambda s, i, c_ref: (s, i, 0))],
            out_specs=pl.BlockSpec((1, tr, C), lambda s, i, c_ref: (s, i, 0))),
        out_shape=jax.ShapeDtypeStruct((n, R, C), out_dtype),
        compiler_params=_params(("parallel", "parallel")),
    )(lax.axis_index("c").reshape(1).astype(jnp.int32), a, b)


def sum_slots(a, *, name):
    n, R, C = a.shape
    tr = _tile(R, 256)

    def body(*refs):
        o_ref = refs[-1]
        acc = refs[0][0].astype(F32)
        for r in refs[1:-1]:
            acc = acc + r[0].astype(F32)
        o_ref[...] = acc

    return pl.pallas_call(
        body, name=name, grid=(R // tr,),
        in_specs=[pl.BlockSpec((1, tr, C), functools.partial(lambda s, i: (s, i, 0), s)) for s in range(n)],
        out_specs=pl.BlockSpec((tr, C), lambda i: (i, 0)),
        out_shape=jax.ShapeDtypeStruct((R, C), F32),
        compiler_params=_params(("parallel",)),
    )(*([a] * n))


def _flat_rows(n_elems, mult):
    rows = -(-n_elems // FLAT_COLS)
    return -(-rows // mult) * mult


def _to_flat(parts, mult):
    v = jnp.concatenate([p.reshape(-1) for p in parts])
    rows = _flat_rows(v.shape[0], mult)
    return jnp.pad(v, (0, rows * FLAT_COLS - v.shape[0])).reshape(rows, FLAT_COLS)


def _from_flat(flat, shapes):
    v = flat.reshape(-1)
    out, off = [], 0
    for s in shapes:
        n = int(np.prod(s))
        out.append(v[off:off + n].reshape(s))
        off += n
    return out


def _unshard(blocks, axis):
    return jnp.concatenate([blocks[s] for s in range(N_CHIPS)], axis=axis)


def _by_shard(full, axis):
    shp = full.shape
    cut = full.reshape(shp[:axis] + (N_CHIPS, shp[axis] // N_CHIPS) + shp[axis + 1:])
    return jnp.moveaxis(cut, axis, 0)


def _gradient_blocks(grads):
    blocks = []
    for n in MATRICES:
        t = _by_shard(jnp.stack(grads[n]), SHARD_AXIS[n])
        blocks.append(t.reshape(N_CHIPS, 2, -1, t.shape[-1]))
    rest = []
    for s in range(N_CHIPS):
        parts = [jnp.stack(grads[n]) if n in REPLICATED else _by_shard(jnp.stack(grads[n]), SHARD_AXIS[n])[s]
                 for n in WEIGHTS if n not in MATRICES]
        rest.append(_to_flat(parts, 16))
    rest = jnp.stack(rest)
    blocks.append(rest.reshape(N_CHIPS, 2, rest.shape[1] // 2, FLAT_COLS))
    return blocks


def _gather_weights(w):
    full = {}
    for names, dtype, mult, call in ((BIG, BF16, 32, "gather_weights"), (SMALL_SHARDED, F32, 16, "gather_vectors")):
        parts = [w[n].astype(dtype) for n in names]
        flat = _to_flat(parts, mult)
        got = gather_shards(flat.reshape(2, flat.shape[0] // 2, FLAT_COLS), name=call)
        got = got.reshape(N_CHIPS, flat.shape[0], FLAT_COLS)
        per_chip = [_from_flat(got[s], [p.shape for p in parts]) for s in range(N_CHIPS)]
        for i, n in enumerate(names):
            full[n] = _unshard([per_chip[s][i] for s in range(N_CHIPS)], SHARD_AXIS[n])
    return full


def _reduce_gradients(grads, w):
    halves = _gradient_blocks(grads)
    other = sibling_exchange(halves, name="grad_pair_exchange")
    pair = [add_pair(a, b, BF16 if i < len(MATRICES) else F32, name=f"grad_pair_add{i}")
            for i, (a, b) in enumerate(zip(halves, other))]
    slots = chip_exchange(pair, name="grad_chip_exchange")
    mine = [sum_slots(a, name=f"grad_chip_sum{i}") for i, a in enumerate(slots)]
    both = sibling_share(mine, name="grad_half_share")
    g = {n: t.reshape(w[n].shape) for n, t in zip(MATRICES, both)}
    rest = [n for n in WEIGHTS if n not in MATRICES]
    g.update(zip(rest, _from_flat(both[-1], [w[n].shape for n in rest])))
    return g


def _layer_fwd(i, h, p, aux):
    j = i // 2
    sv = {"h0": h}
    if i % 2 == 0:
        hns = rmsnorm_fwd(h, p["attn_norm"][i][None], dils=DILS, name=f"l{i}_norm1")
        qkvs, qkvps, os_, lses = [], [], [], []
        for g, d in enumerate(DILS):
            qkv = matmul(hns[g], p["a_w_in_g"][j][g], name=f"l{i}_a_in{g}")
            qkvp = qk_prep_fwd(qkv, aux["a_gain"][j][g], aux["cos"][g], aux["sin"][g], aux["ones"], name=f"l{i}_a_prep{g}")
            o, l = attn_fwd(qkvp, d, name=f"l{i}_a_attn{g}")
            qkvs.append(qkv)
            qkvps.append(qkvp)
            os_.append(o)
            lses.append(l)
        out, lse = attn_merge(os_, lses, DILS, name=f"l{i}_a_merge")
        sv.update(hns=hns, qkv=qkvs, qkvp=qkvps, out=out, lse=lse)
        h1 = matmul(out, p["a_w_out"][j], res=h, name=f"l{i}_a_out")
    else:
        hn = rmsnorm_fwd(h, p["attn_norm"][i][None], name=f"l{i}_norm1")[0]
        sv["hn"] = hn
        proj = matmul(hn, p["b_w_in"][j], tn=640, name=f"l{i}_b_in")
        loga = gate_fwd(proj, aux["b_wblk"][j], aux["b_bias"][j], name=f"l{i}_b_gate")
        o_f, st_f = gla_fwd(proj, loga, aux["tri_f"], False, name=f"l{i}_b_gla_f")
        o_b, st_b = gla_fwd(proj, loga, aux["tri_b"], True, name=f"l{i}_b_gla_b")
        y = gla_post_fwd(o_f, o_b, aux["b_gain"][j], proj, name=f"l{i}_b_post")
        sv.update(proj=proj, loga=loga, o_f=o_f, o_b=o_b, st_f=st_f, st_b=st_b, y=y)
        h1 = matmul(y, p["b_w_out"][j], res=h, name=f"l{i}_b_out")
    sv["h1"] = h1
    hn2 = rmsnorm_fwd(h1, p["ffn_norm"][i][None], name=f"l{i}_norm2")[0]
    gu = matmul(hn2, p["ffn_w_gate_up"][i], out_dtype=BF16, tn=FFN_HIDDEN // 2, name=f"l{i}_f_up")
    act = swiglu_fwd(gu, name=f"l{i}_f_act")
    h2 = matmul(act, p["ffn_w_down"][i], res=h1, tk=2816, name=f"l{i}_f_down")
    sv.update(hn2=hn2, gu=gu, act=act)
    return h2, sv


def _layer_bwd(i, dh, p, pt, aux, sv, grads):
    j = i // 2
    dhb = dh.astype(BF16)
    grads["ffn_w_down"][i] = matmul(sv["act"].T, dhb, tm=FFN_HIDDEN // 2, tk=2048, name=f"l{i}_f_down_dw")
    dact = matmul(dhb, pt["ffn_w_down"][i], out_dtype=BF16, tn=FFN_HIDDEN // 2, name=f"l{i}_f_down_dx")
    dgu = swiglu_bwd(sv["gu"], dact, name=f"l{i}_f_act_bwd")
    grads["ffn_w_gate_up"][i] = matmul(sv["hn2"].T, dgu, tk=2048, name=f"l{i}_f_up_dw")
    dhn2 = matmul(dgu, pt["ffn_w_gate_up"][i], tk=2816, name=f"l{i}_f_up_dx")
    dh1, dg = rmsnorm_bwd(sv["h1"], p["ffn_norm"][i][None], [dhn2], dh, name=f"l{i}_norm2_bwd")
    grads["ffn_norm"][i] = dg[0]
    dh1b = dh1.astype(BF16)
    if i % 2 == 0:
        grads["a_w_out"][j] = matmul(sv["out"].T, dh1b, tk=2048, name=f"l{i}_a_out_dw")
        dout = matmul(dh1b, pt["a_w_out"][j], out_dtype=BF16, name=f"l{i}_a_out_dx")
        douts, deltas = attn_delta(dout, sv["out"], aux["ones"], DILS, name=f"l{i}_a_delta")
        dws, dhns, dgq, dgk = [], [], [], []
        for g, d in enumerate(DILS):
            qkvp, lse = sv["qkvp"][g], sv["lse"][g]
            dq = attn_bwd_dq(qkvp, douts[g], lse, deltas[g], d, name=f"l{i}_a_dq{g}")
            dk, dv = attn_bwd_dkv(qkvp, douts[g], lse, deltas[g], d, name=f"l{i}_a_dkv{g}")
            dqkv, dgain = qk_prep_bwd(sv["qkv"][g], aux["a_gain"][j][g], aux["cos"][g], aux["sin"][g], aux["ones"],
                                      [dq, dk, dv], name=f"l{i}_a_prep_bwd{g}")
            dgh = dgain[0].reshape(3, A_HEADS, A_HEAD_DIM).sum(axis=1)
            dgq.append(dgh[0])
            dgk.append(dgh[1])
            dws.append(matmul(sv["hns"][g].T, dqkv, tk=2048, name=f"l{i}_a_in_dw{g}"))
            dhns.append(matmul(dqkv, pt["a_w_in_g"][j][g], tk=3072, name=f"l{i}_a_in_dx{g}"))
        grads["a_q_norm"][j] = jnp.stack(dgq)
        grads["a_k_norm"][j] = jnp.stack(dgk)
        grads["a_w_in"][j] = jnp.concatenate(dws, axis=1)
        dh0, dg = rmsnorm_bwd(sv["h0"], p["attn_norm"][i][None], dhns, dh1, dils=DILS, name=f"l{i}_norm1_bwd")
    else:
        grads["b_w_out"][j] = matmul(sv["y"].T, dh1b, tk=2048, name=f"l{i}_b_out_dw")
        dy = matmul(dh1b, pt["b_w_out"][j], name=f"l{i}_b_out_dx")
        do, dr, dgn = gla_post_bwd(sv["o_f"], sv["o_b"], aux["b_gain"][j], sv["proj"], dy, name=f"l{i}_b_post_bwd")
        grads["b_out_norm"][j] = dgn[0].reshape(B_HEADS, B_VAL_DIM)
        pf = gla_bwd(sv["proj"], sv["loga"], sv["st_f"], do, aux["tri_f"], aux["tri_b"], False, name=f"l{i}_b_gla_f_bwd")
        pb = gla_bwd(sv["proj"], sv["loga"], sv["st_b"], do, aux["tri_b"], aux["tri_f"], True, name=f"l{i}_b_gla_b_bwd")
        dloga = jnp.concatenate([pf[3], pb[3]], axis=1)
        dz, dwblk, dbias = gate_bwd(sv["proj"], aux["b_wblk"][j], aux["b_wblk_t"][j], aux["b_bias"][j], dloga,
                                    name=f"l{i}_b_gate_bwd")
        grads["b_w_gate_f"][j] = dwblk[:B_GATE_RANK, :B_QK_WIDTH]
        grads["b_w_gate_b"][j] = dwblk[B_GATE_RANK:2 * B_GATE_RANK, B_QK_WIDTH:]
        grads["b_gate_bias_f"][j] = dbias[0, :B_QK_WIDTH]
        grads["b_gate_bias_b"][j] = dbias[0, B_QK_WIDTH:]
        dproj = gla_combine(pf[:3], pb[:3], dr, dz, name=f"l{i}_b_combine")
        grads["b_w_in"][j] = matmul(sv["hn"].T, dproj, tn=640, tk=2048, name=f"l{i}_b_in_dw")[:, :B_IN_WIDTH]
        dhn = matmul(dproj, pt["b_w_in"][j], tk=640, name=f"l{i}_b_in_dx")
        dh0, dg = rmsnorm_bwd(sv["h0"], p["attn_norm"][i][None], [dhn], dh1, name=f"l{i}_norm1_bwd")
    grads["attn_norm"][i] = dg[0]
    return dh0


def _local_step(x, target, p, small):
    S = x.shape[0]
    cos, sin = _rope_tables(S)
    to_phase = lambda t, d: t.reshape(S // d, d, LANES).swapaxes(0, 1).reshape(S, LANES)
    cos, sin = [to_phase(cos, d) for d in DILS], [to_phase(sin, d) for d in DILS]
    ones_v = jnp.ones((A_WIDTH,), F32)
    a_gain = [[jnp.concatenate([jnp.tile(small["a_q_norm"][j][g], A_HEADS), jnp.tile(small["a_k_norm"][j][g], A_HEADS),
                                ones_v])[None] for g in range(len(DILS))] for j in range(2)]
    b_wblk = [_gate_block_weight(p["b_w_gate_f"][j].astype(F32), p["b_w_gate_b"][j].astype(F32)) for j in range(2)]
    aux = dict(cos=cos, sin=sin, ones=_head_block_ones(), a_gain=a_gain, tri_f=_tri(False), tri_b=_tri(True),
               b_wblk=b_wblk, b_wblk_t=[w.T for w in b_wblk],
               b_bias=[jnp.concatenate([small["b_gate_bias_f"][j], small["b_gate_bias_b"][j]])[None] for j in range(2)],
               b_gain=[small["b_out_norm"][j].reshape(1, B_V_WIDTH) for j in range(2)])
    pw = dict(p)
    pw["b_w_in"] = jnp.pad(p["b_w_in"], ((0, 0), (0, 0), (0, B_IN_PAD - B_IN_WIDTH)))
    pw["attn_norm"], pw["ffn_norm"] = small["attn_norm"], small["ffn_norm"]
    gw = 3 * A_WIDTH
    pw["a_w_in_g"] = [[p["a_w_in"][j][:, g * gw:(g + 1) * gw] for g in range(len(DILS))] for j in range(2)]
    pt = {n: jnp.swapaxes(pw[n], 1, 2) for n in ("a_w_out", "b_w_in", "b_w_out", "ffn_w_gate_up", "ffn_w_down")}
    pt["a_w_in_g"] = [[wg.T for wg in row] for row in pw["a_w_in_g"]]

    h = x
    saved = []
    for i in range(DEPTH):
        h, sv = _layer_fwd(i, h, pw, aux)
        saved.append(sv)
    loss_sq, dh = loss_head(h, target, name="loss_head")
    grads = {n: [None] * (DEPTH if n in ("attn_norm", "ffn_norm", "ffn_w_gate_up", "ffn_w_down") else 2) for n in WEIGHTS}
    for i in reversed(range(DEPTH)):
        dh = _layer_bwd(i, dh, pw, pt, aux, saved[i], grads)
    return loss_sq[0, 0] * (0.5 / D_MODEL), dh, grads


def kernel(x, attn_norm, ffn_norm, a_w_in, a_q_norm, a_k_norm, a_w_out, b_w_in, b_w_gate_f, b_gate_bias_f, b_w_gate_b, b_gate_bias_b, b_out_norm, b_w_out, ffn_w_gate_up, ffn_w_down, loss_target, m_attn_norm, m_ffn_norm, m_a_w_in, m_a_q_norm, m_a_k_norm, m_a_w_out, m_b_w_in, m_b_w_gate_f, m_b_gate_bias_f, m_b_w_gate_b, m_b_gate_bias_b, m_b_out_norm, m_b_w_out, m_ffn_w_gate_up, m_ffn_w_down, v_attn_norm, v_ffn_norm, v_a_w_in, v_a_q_norm, v_a_k_norm, v_a_w_out, v_b_w_in, v_b_w_gate_f, v_b_gate_bias_f, v_b_w_gate_b, v_b_gate_bias_b, v_b_out_norm, v_b_w_out, v_ffn_w_gate_up, v_ffn_w_down):
    w = dict(attn_norm=attn_norm, ffn_norm=ffn_norm, a_w_in=a_w_in, a_q_norm=a_q_norm, a_k_norm=a_k_norm, a_w_out=a_w_out,
             b_w_in=b_w_in, b_w_gate_f=b_w_gate_f, b_gate_bias_f=b_gate_bias_f, b_w_gate_b=b_w_gate_b,
             b_gate_bias_b=b_gate_bias_b, b_out_norm=b_out_norm, b_w_out=b_w_out, ffn_w_gate_up=ffn_w_gate_up,
             ffn_w_down=ffn_w_down)
    m = dict(attn_norm=m_attn_norm, ffn_norm=m_ffn_norm, a_w_in=m_a_w_in, a_q_norm=m_a_q_norm, a_k_norm=m_a_k_norm,
             a_w_out=m_a_w_out, b_w_in=m_b_w_in, b_w_gate_f=m_b_w_gate_f, b_gate_bias_f=m_b_gate_bias_f,
             b_w_gate_b=m_b_w_gate_b, b_gate_bias_b=m_b_gate_bias_b, b_out_norm=m_b_out_norm, b_w_out=m_b_w_out,
             ffn_w_gate_up=m_ffn_w_gate_up, ffn_w_down=m_ffn_w_down)
    v = dict(attn_norm=v_attn_norm, ffn_norm=v_ffn_norm, a_w_in=v_a_w_in, a_q_norm=v_a_q_norm, a_k_norm=v_a_k_norm,
             a_w_out=v_a_w_out, b_w_in=v_b_w_in, b_w_gate_f=v_b_w_gate_f, b_gate_bias_f=v_b_gate_bias_f,
             b_w_gate_b=v_b_w_gate_b, b_gate_bias_b=v_b_gate_bias_b, b_out_norm=v_b_out_norm, b_w_out=v_b_w_out,
             ffn_w_gate_up=v_ffn_w_gate_up, ffn_w_down=v_ffn_w_down)

    full = _gather_weights(w)
    p = {n: full[n] for n in BIG}
    small = {n: full[n] for n in SMALL_SHARDED}
    small.update({n: w[n] for n in REPLICATED})
    loss_local, dx, grads = _local_step(x[0], loss_target[0], p, small)
    loss = lax.psum(loss_local, ("x", "y", "c"))

    g = _reduce_gradients(grads, w)
    delta, new_m, new_v = {}, {}, {}
    rows = lambda t: t.reshape(-1, t.shape[-1])
    for n in MATRICES:
        outs = adamw(rows(w[n]), rows(g[n]), rows(m[n]), rows(v[n]), name=f"adamw_{n}")
        delta[n], new_m[n], new_v[n] = [o.reshape(w[n].shape) for o in outs]
    rest = [n for n in WEIGHTS if n not in MATRICES]
    flat = lambda d: _to_flat([d[n] for n in rest], 8)
    outs = adamw(flat(w), flat(g), flat(m), flat(v), name="adamw_vectors")
    for d, o in zip((delta, new_m, new_v), outs):
        d.update(zip(rest, _from_flat(o, [w[n].shape for n in rest])))
    return (loss, dx[None], *[g[n] for n in WEIGHTS], *[delta[n] for n in WEIGHTS],
            *[new_m[n] for n in WEIGHTS], *[new_v[n] for n in WEIGHTS])
```

```python
import functools

import numpy as np
import jax
import jax.numpy as jnp
from jax import lax
from jax.experimental import pallas as pl
from jax.experimental.pallas import tpu as pltpu

F32, BF16 = jnp.float32, jnp.bfloat16
HI = lax.Precision.HIGHEST
MESH = pl.DeviceIdType.MESH

D_MODEL = 1024
DEPTH = 4
RMS_EPS = 1e-6
NEG_INF = -1e30
A_GROUPS = ((128, 1), (512, 4), (2048, 16))
DILS = tuple(d for _, d in A_GROUPS)
A_HALF = 64
A_HEADS = 16
A_HEAD_DIM = 64
A_WIDTH = 1024
A_IN_WIDTH = 9216
ROPE_THETA = 10000.0
B_HEADS = 4
B_KEY_DIM = 128
B_VAL_DIM = 256
B_QK_WIDTH = 512
B_V_WIDTH = 1024
B_GATE_RANK = 16
B_GATE_TAU = 16.0
B_CHUNK = 64
B_IN_WIDTH = 3104
B_IN_PAD = 3200
FFN_HIDDEN = 2816
ADAM_LR, ADAM_B1, ADAM_B2, ADAM_EPS, ADAM_WD, ADAM_STEP = 0.001, 0.9, 0.999, 1e-08, 0.01, 10
LANES = 128
VMEM_LIMIT = 48 * 1024 * 1024
FLAT_COLS = 1024
N_CHIPS = 4

WEIGHTS = ['attn_norm', 'ffn_norm', 'a_w_in', 'a_q_norm', 'a_k_norm', 'a_w_out', 'b_w_in', 'b_w_gate_f',
           'b_gate_bias_f', 'b_w_gate_b', 'b_gate_bias_b', 'b_out_norm', 'b_w_out', 'ffn_w_gate_up', 'ffn_w_down']
REPLICATED = ('attn_norm', 'ffn_norm', 'a_q_norm', 'a_k_norm')
SHARD_AXIS = {'a_w_in': 2, 'a_w_out': 1, 'b_w_in': 2, 'b_w_gate_f': 2, 'b_gate_bias_f': 1, 'b_w_gate_b': 2,
              'b_gate_bias_b': 1, 'b_out_norm': 2, 'b_w_out': 1, 'ffn_w_gate_up': 2, 'ffn_w_down': 1}
BIG = ('a_w_in', 'a_w_out', 'b_w_in', 'b_w_gate_f', 'b_w_gate_b', 'b_w_out', 'ffn_w_gate_up', 'ffn_w_down')
SMALL_SHARDED = ('b_gate_bias_f', 'b_gate_bias_b', 'b_out_norm')
MATRICES = ('a_w_in', 'a_w_out', 'b_w_in', 'b_w_out', 'ffn_w_gate_up', 'ffn_w_down')


def _params(sem):
    return pltpu.CompilerParams(dimension_semantics=sem, vmem_limit_bytes=VMEM_LIMIT)


def _tile(n, pref):
    t = min(n, pref)
    while n % t:
        t //= 2
    return t


def _const_spec(shape):
    nd = len(shape)
    return pl.BlockSpec(shape, lambda *_: (0,) * nd)


def matmul(a, b, *, name, out_dtype=F32, res=None, tm=1024, tn=512, tk=1024):
    M, K = a.shape
    N = b.shape[1]
    assert b.shape[0] == K
    tm, tn, tk = _tile(M, tm), _tile(N, tn), _tile(K, tk)
    nk = K // tk

    def body(*refs):
        a_ref, b_ref = refs[:2]
        r_ref = refs[2] if res is not None else None
        o_ref = refs[3 if res is not None else 2]
        part = jnp.dot(a_ref[...], b_ref[...], preferred_element_type=F32)

        def finish(v):
            if res is not None:
                v = v + r_ref[...]
            o_ref[...] = v.astype(o_ref.dtype)

        if nk == 1:
            finish(part)
            return
        acc_ref = refs[-1]
        k = pl.program_id(2)

        @pl.when(k == 0)
        def _():
            acc_ref[...] = part

        @pl.when((k > 0) & (k < nk - 1))
        def _():
            acc_ref[...] += part

        @pl.when(k == nk - 1)
        def _():
            finish(acc_ref[...] + part)

    in_specs = [pl.BlockSpec((tm, tk), lambda i, j, k: (i, k)), pl.BlockSpec((tk, tn), lambda i, j, k: (k, j))]
    args = [a, b]
    if res is not None:
        in_specs.append(pl.BlockSpec((tm, tn), lambda i, j, k: (i, j)))
        args.append(res)
    return pl.pallas_call(
        body, name=name, grid=(M // tm, N // tn, nk), in_specs=in_specs,
        out_specs=pl.BlockSpec((tm, tn), lambda i, j, k: (i, j)),
        out_shape=jax.ShapeDtypeStruct((M, N), out_dtype),
        scratch_shapes=[pltpu.VMEM((tm, tn), F32)] if nk > 1 else [],
        compiler_params=_params(("parallel", "parallel", "arbitrary")),
    )(*args)


def _phase_spec(d, ts, W):
    if d == 1:
        return pl.BlockSpec((ts, W), lambda i: (i, 0))
    return pl.BlockSpec((d, ts // d, W), lambda i: (0, i, 0))


def _phase_view(a, d):
    return a if d == 1 else a.reshape(d, a.shape[0] // d, a.shape[1])


def _phase_shape(S, W, d, dtype):
    return jax.ShapeDtypeStruct((S, W) if d == 1 else (d, S // d, W), dtype)


def _nat_scratch(ts, W):
    return pltpu.VMEM((W // LANES, ts, LANES), F32)


def _put_natural(nat_ref, value):
    for c in range(nat_ref.shape[0]):
        nat_ref[c] = value[:, c * LANES:(c + 1) * LANES]


def _get_natural(nat_ref):
    return jnp.concatenate([nat_ref[c] for c in range(nat_ref.shape[0])], axis=1)


def _store_phases(nat_ref, o_ref, d, ts):
    for p in range(d):
        for c in range(nat_ref.shape[0]):
            o_ref[p, :, c * LANES:(c + 1) * LANES] = nat_ref[c, pl.ds(p, ts // d, stride=d), :].astype(o_ref.dtype)


def _load_phases(i_ref, nat_ref, d, ts):
    for p in range(d):
        for c in range(nat_ref.shape[0]):
            nat_ref[c, pl.ds(p, ts // d, stride=d), :] = i_ref[p, :, c * LANES:(c + 1) * LANES].astype(F32)


def rmsnorm_fwd(x, gain, *, name, dils=(1,)):
    S, Dm = x.shape
    ts = _tile(S, 512)

    def body(x_ref, g_ref, *rest):
        o_refs, scr = rest[:len(dils)], rest[len(dils)]
        xv = x_ref[...]
        r = lax.rsqrt(jnp.mean(xv * xv, axis=-1, keepdims=True) + RMS_EPS)
        y = (xv * r) * g_ref[...]
        if any(d > 1 for d in dils):
            _put_natural(scr, y)
        for d, o_ref in zip(dils, o_refs):
            if d == 1:
                o_ref[...] = y.astype(o_ref.dtype)
            else:
                _store_phases(scr, o_ref, d, ts)

    outs = pl.pallas_call(
        body, name=name, grid=(S // ts,),
        in_specs=[pl.BlockSpec((ts, Dm), lambda i: (i, 0)), _const_spec((1, Dm))],
        out_specs=[_phase_spec(d, ts, Dm) for d in dils],
        out_shape=[_phase_shape(S, Dm, d, BF16) for d in dils],
        scratch_shapes=[_nat_scratch(ts, Dm)],
        compiler_params=_params(("parallel",)),
    )(x, gain)
    return [o.reshape(S, Dm) for o in outs]


def rmsnorm_bwd(x, gain, dys, dres, *, name, dils=(1,)):
    S, Dm = x.shape
    ts = _tile(S, 256)
    nd = len(dils)

    def body(x_ref, g_ref, *rest):
        dy_refs, (dr_ref, dx_ref, dg_ref, scr) = rest[:nd], rest[nd:]

        @pl.when(pl.program_id(0) == 0)
        def _():
            dg_ref[...] = jnp.zeros_like(dg_ref)

        dyv = None
        for d, dy_ref in zip(dils, dy_refs):
            if d == 1:
                t = dy_ref[...].astype(F32)
            else:
                _load_phases(dy_ref, scr, d, ts)
                t = _get_natural(scr)
            dyv = t if dyv is None else dyv + t
        xv = x_ref[...]
        r = lax.rsqrt(jnp.mean(xv * xv, axis=-1, keepdims=True) + RMS_EPS)
        xhat = xv * r
        dyg = dyv * g_ref[...]
        dx = r * (dyg - xhat * jnp.mean(dyg * xhat, axis=-1, keepdims=True))
        dx_ref[...] = dr_ref[...] + dx
        dg_ref[0:1, :] += jnp.sum(dyv * xhat, axis=0, keepdims=True)

    row = pl.BlockSpec((ts, Dm), lambda i: (i, 0))
    return pl.pallas_call(
        body, name=name, grid=(S // ts,),
        in_specs=[row, _const_spec((1, Dm))] + [_phase_spec(d, ts, Dm) for d in dils] + [row],
        out_specs=[row, _const_spec((8, Dm))],
        out_shape=[jax.ShapeDtypeStruct((S, Dm), F32), jax.ShapeDtypeStruct((8, Dm), F32)],
        scratch_shapes=[_nat_scratch(ts, Dm)],
        compiler_params=_params(("arbitrary",)),
    )(x, gain, *[_phase_view(dy, d) for dy, d in zip(dys, dils)], dres)


def swiglu_fwd(gu, *, name):
    S, F2 = gu.shape
    Fh = F2 // 2
    ts = _tile(S, 512)

    def body(g_ref, u_ref, o_ref):
        g = g_ref[...].astype(F32)
        u = u_ref[...].astype(F32)
        o_ref[...] = (g * (1.0 / (1.0 + jnp.exp(-g))) * u).astype(o_ref.dtype)

    return pl.pallas_call(
        body, name=name, grid=(S // ts,),
        in_specs=[pl.BlockSpec((ts, Fh), lambda i: (i, 0)), pl.BlockSpec((ts, Fh), lambda i: (i, 1))],
        out_specs=pl.BlockSpec((ts, Fh), lambda i: (i, 0)),
        out_shape=jax.ShapeDtypeStruct((S, Fh), BF16),
        compiler_params=_params(("parallel",)),
    )(gu, gu)


def swiglu_bwd(gu, dact, *, name):
    S, F2 = gu.shape
    Fh = F2 // 2
    ts = _tile(S, 256)

    def body(gu_ref, d_ref, o_ref):
        g = gu_ref[:, :Fh].astype(F32)
        u = gu_ref[:, Fh:].astype(F32)
        d = d_ref[...].astype(F32)
        sig = 1.0 / (1.0 + jnp.exp(-g))
        o_ref[:, :Fh] = (d * u * (sig * (1.0 + g * (1.0 - sig)))).astype(o_ref.dtype)
        o_ref[:, Fh:] = (d * (g * sig)).astype(o_ref.dtype)

    return pl.pallas_call(
        body, name=name, grid=(S // ts,),
        in_specs=[pl.BlockSpec((ts, F2), lambda i: (i, 0)), pl.BlockSpec((ts, Fh), lambda i: (i, 0))],
        out_specs=pl.BlockSpec((ts, F2), lambda i: (i, 0)),
        out_shape=jax.ShapeDtypeStruct((S, F2), BF16),
        compiler_params=_params(("parallel",)),
    )(gu, dact)


def loss_head(y, target, *, name):
    S, Dm = y.shape
    ts = _tile(S, 512)

    def body(y_ref, t_ref, l_ref, d_ref):
        @pl.when(pl.program_id(0) == 0)
        def _():
            l_ref[...] = jnp.zeros_like(l_ref)

        e = y_ref[...] - t_ref[...]
        d_ref[...] = e * (1.0 / Dm)
        l_ref[...] += jnp.sum(e * e)

    return pl.pallas_call(
        body, name=name, grid=(S // ts,),
        in_specs=[pl.BlockSpec((ts, Dm), lambda i: (i, 0)), pl.BlockSpec((ts, Dm), lambda i: (i, 0))],
        out_specs=[_const_spec((8, LANES)), pl.BlockSpec((ts, Dm), lambda i: (i, 0))],
        out_shape=[jax.ShapeDtypeStruct((8, LANES), F32), jax.ShapeDtypeStruct((S, Dm), F32)],
        compiler_params=_params(("arbitrary",)),
    )(y, target)


def _head_block_ones():
    i = np.arange(LANES)
    return jnp.asarray((i[:, None] // A_HEAD_DIM == i[None, :] // A_HEAD_DIM).astype(np.float32)).astype(BF16)


def _rope_tables(S):
    half = A_HEAD_DIM // 2
    inv_freq = ROPE_THETA ** (-jnp.arange(half, dtype=F32) / half)
    ang = jnp.arange(S).astype(F32)[:, None] * inv_freq[None, :]
    cos = jnp.tile(jnp.cos(ang), (1, LANES // half))
    sin = jnp.tile(jnp.sin(ang), (1, LANES // half))
    return cos, sin


def _rot_half(x, lo):
    return jnp.where(lo, -pltpu.roll(x, LANES - 32, 1), pltpu.roll(x, 32, 1))


def _seg_sum(v, ones_ref):
    hi = v.astype(BF16)
    lo = (v - hi.astype(F32)).astype(BF16)
    ones = ones_ref[...]
    return jnp.dot(hi, ones, preferred_element_type=F32) + jnp.dot(lo, ones, preferred_element_type=F32)


def _seg_mean(v, ones_ref):
    return _seg_sum(v, ones_ref) * (1.0 / A_HEAD_DIM)


def qk_prep_fwd(qkv, gain, cos, sin, ones, *, name):
    S, W = qkv.shape
    ts = _tile(S, 256)
    nchunk = A_WIDTH // LANES

    def body(x_ref, g_ref, c_ref, s_ref, ones_ref, o_ref):
        kind = pl.program_id(1) % 3

        @pl.when(kind < 2)
        def _():
            scale = jnp.where(kind == 0, A_HEAD_DIM ** -0.5, 1.0).astype(F32)
            lo = (lax.broadcasted_iota(jnp.int32, (ts, LANES), 1) % A_HEAD_DIM) < (A_HEAD_DIM // 2)
            cv, sv = c_ref[...], s_ref[...]
            for c in range(nchunk):
                sl = slice(c * LANES, (c + 1) * LANES)
                xv = x_ref[:, sl]
                r = lax.rsqrt(_seg_mean(xv * xv, ones_ref) + RMS_EPS)
                y = (xv * r) * g_ref[:, sl]
                y = y * cv + _rot_half(y, lo) * sv
                o_ref[:, sl] = (y * scale).astype(o_ref.dtype)

        @pl.when(kind == 2)
        def _():
            o_ref[...] = x_ref[...].astype(o_ref.dtype)

    return pl.pallas_call(
        body, name=name, grid=(S // ts, W // A_WIDTH),
        in_specs=[pl.BlockSpec((ts, A_WIDTH), lambda i, j: (i, j)), pl.BlockSpec((1, A_WIDTH), lambda i, j: (0, j)),
                  pl.BlockSpec((ts, LANES), lambda i, j: (i, 0)), pl.BlockSpec((ts, LANES), lambda i, j: (i, 0)),
                  _const_spec((LANES, LANES))],
        out_specs=pl.BlockSpec((ts, A_WIDTH), lambda i, j: (i, j)),
        out_shape=jax.ShapeDtypeStruct((S, W), BF16),
        compiler_params=_params(("parallel", "arbitrary")),
    )(qkv, gain, cos, sin, ones)


def qk_prep_bwd(qkv, gain, cos, sin, ones, grads, *, name):
    S, W = qkv.shape
    ts = _tile(S, 256)
    nchunk = A_WIDTH // LANES
    nj = W // A_WIDTH

    def body(x_ref, g_ref, c_ref, s_ref, ones_ref, *rest):
        g_refs, (o_ref, dg_ref) = rest[:nj], rest[nj:]
        j = pl.program_id(0)
        kind = j % 3

        @pl.when(pl.program_id(1) == 0)
        def _():
            dg_ref[...] = jnp.zeros_like(dg_ref)

        for n in range(nj):
            @pl.when(j == n)
            def _(n=n):
                d_ref = g_refs[n]
                if n % 3 == 2:
                    o_ref[...] = d_ref[...].astype(o_ref.dtype)
                    return
                scale = A_HEAD_DIM ** -0.5 if n % 3 == 0 else 1.0
                lo = (lax.broadcasted_iota(jnp.int32, (ts, LANES), 1) % A_HEAD_DIM) < (A_HEAD_DIM // 2)
                cv, sv = c_ref[...], s_ref[...]
                for c in range(nchunk):
                    sl = slice(c * LANES, (c + 1) * LANES)
                    dy = d_ref[:, sl].astype(F32) * scale
                    dn = dy * cv - _rot_half(dy, lo) * sv
                    xv = x_ref[:, sl]
                    r = lax.rsqrt(_seg_mean(xv * xv, ones_ref) + RMS_EPS)
                    xhat = xv * r
                    dyg = dn * g_ref[:, sl]
                    dx = r * (dyg - xhat * _seg_mean(dyg * xhat, ones_ref))
                    o_ref[:, sl] = dx.astype(o_ref.dtype)
                    dg_ref[0:1, sl] += jnp.sum(dn * xhat, axis=0, keepdims=True)

    def gspec(n):
        return pl.BlockSpec((ts, A_WIDTH), lambda j, i: (jnp.where(j == n, i, 0), 0))

    return pl.pallas_call(
        body, name=name, grid=(nj, S // ts),
        in_specs=[pl.BlockSpec((ts, A_WIDTH), lambda j, i: (i, j)), pl.BlockSpec((1, A_WIDTH), lambda j, i: (0, j)),
                  pl.BlockSpec((ts, LANES), lambda j, i: (i, 0)), pl.BlockSpec((ts, LANES), lambda j, i: (i, 0)),
                  _const_spec((LANES, LANES))] + [gspec(n) for n in range(nj)],
        out_specs=[pl.BlockSpec((ts, A_WIDTH), lambda j, i: (i, j)), pl.BlockSpec((8, A_WIDTH), lambda j, i: (0, j))],
        out_shape=[jax.ShapeDtypeStruct((S, W), BF16), jax.ShapeDtypeStruct((8, W), F32)],
        compiler_params=_params(("arbitrary", "arbitrary")),
    )(qkv, gain, cos, sin, ones, *grads)


def _band_specs(kind, tq, nlb):
    nhb = tq // A_HALF
    nb = nlb // nhb
    base = kind * (A_WIDTH // LANES)
    return [pl.BlockSpec((A_HALF, LANES), lambda ph, b, hp: (ph * nlb + jnp.maximum(b * nhb - 1, 0), base + hp)),
            pl.BlockSpec((tq, LANES), lambda ph, b, hp: (ph * nb + b, base + hp)),
            pl.BlockSpec((A_HALF, LANES), lambda ph, b, hp: (ph * nlb + jnp.minimum((b + 1) * nhb, nlb - 1), base + hp))]


A_BLOCK = 512
A_SUB = 128


def _band_bias(sub, key_major):
    i = np.arange(sub)[:, None]
    j = np.arange(sub + 2 * A_HALF)[None, :] - A_HALF
    ok = np.abs(j - i) <= A_HALF
    return jnp.asarray(np.where(ok.T if key_major else ok, 0.0, NEG_INF).astype(np.float32))


def _edge_bias(first, n, L, axis, at_start, at_end):
    if not (at_start or at_end):
        return None
    shape = (1, n) if axis == 1 else (n, 1)
    pos = first - A_HALF + lax.broadcasted_iota(jnp.int32, shape, axis)
    return jnp.where((pos < 0) | (pos >= L), NEG_INF, 0.0).astype(F32)


def _with_edge(bias, edge):
    return bias if edge is None else bias + edge


def _cat3(a_ref, b_ref, c_ref):
    return jnp.concatenate([a_ref[...], b_ref[...], c_ref[...]], axis=0)


def _lane_lo(rows):
    return lax.broadcasted_iota(jnp.int32, (rows, LANES), 1) < A_HEAD_DIM


NT = (((1,), (1,)), ((), ()))
TN = (((0,), (0,)), ((), ()))


def attn_fwd(qkvp, dil, *, name):
    S = qkvp.shape[0]
    L = S // dil
    tq = _tile(L, A_BLOCK)
    sub = min(tq, A_SUB)
    nsub = tq // sub
    nlb = L // A_HALF
    band = _band_bias(sub, False)

    def body(q_ref, kp_ref, ko_ref, kn_ref, vp_ref, vo_ref, vn_ref, band_ref, o_ref, l_ref):
        b = pl.program_id(1)
        K = _cat3(kp_ref, ko_ref, kn_ref)
        V = _cat3(vp_ref, vo_ref, vn_ref)
        lo_k = _lane_lo(tq + 2 * A_HALF)
        lo_q = _lane_lo(sub)
        Km = [jnp.where(sel, K, jnp.zeros_like(K)) for sel in (lo_k, ~lo_k)]
        Vm = [jnp.where(sel, V, jnp.zeros_like(V)) for sel in (lo_k, ~lo_k)]
        for r in range(nsub):
            rows = slice(r * sub, (r + 1) * sub)
            keys = slice(r * sub, (r + 1) * sub + 2 * A_HALF)
            bias = _with_edge(band_ref[...], _edge_bias(b * tq + r * sub, sub + 2 * A_HALF, L, 1, r == 0, r == nsub - 1))
            q = q_ref[rows, :]
            outs, lses = [], []
            for hh in range(2):
                s = lax.dot_general(q, Km[hh][keys], NT, preferred_element_type=F32) + bias
                m = jnp.max(s, axis=1, keepdims=True)
                p = jnp.exp(s - m)
                l = jnp.sum(p, axis=1, keepdims=True)
                outs.append(jnp.dot(p.astype(BF16), Vm[hh][keys], preferred_element_type=F32) * (1.0 / l))
                lses.append(m + jnp.log(l))
            o_ref[rows, :] = outs[0] + outs[1]
            l_ref[rows, :] = jnp.where(lo_q, lses[0], lses[1])

    ospec = _band_specs(0, tq, nlb)[1]
    return pl.pallas_call(
        body, name=name, grid=(dil, L // tq, A_WIDTH // LANES),
        in_specs=[_band_specs(0, tq, nlb)[1]] + _band_specs(1, tq, nlb) + _band_specs(2, tq, nlb)
        + [_const_spec(band.shape)],
        out_specs=[ospec, ospec],
        out_shape=[jax.ShapeDtypeStruct((S, A_WIDTH), F32)] * 2,
        compiler_params=_params(("parallel", "parallel", "parallel")),
    )(*([qkvp] * 7), band)


def attn_merge(os_, lses, dils, *, name):
    S = os_[0].shape[0]
    ts = _tile(S, 256)
    ng = len(dils)

    def body(*refs):
        o_refs, l_refs, out_ref = refs[:ng], refs[ng:2 * ng], refs[2 * ng]
        lse_refs, scrs = refs[2 * ng + 1:3 * ng + 1], refs[3 * ng + 1:]
        ov, ls, k = [], [], 0
        for d, o_ref, l_ref in zip(dils, o_refs, l_refs):
            if d == 1:
                ov.append(o_ref[...])
                ls.append(l_ref[...])
            else:
                _load_phases(o_ref, scrs[k], d, ts)
                _load_phases(l_ref, scrs[k + 1], d, ts)
                ov.append(_get_natural(scrs[k]))
                ls.append(_get_natural(scrs[k + 1]))
                k += 2
        m = functools.reduce(jnp.maximum, ls)
        es = [jnp.exp(l - m) for l in ls]
        tot = functools.reduce(jnp.add, es)
        acc = None
        for e, o in zip(es, ov):
            t = (e / tot) * o
            acc = t if acc is None else acc + t
        out_ref[...] = acc.astype(out_ref.dtype)
        total = m + jnp.log(tot)
        _put_natural(scrs[k], total)
        for d, lse_ref in zip(dils, lse_refs):
            if d == 1:
                lse_ref[...] = total
            else:
                _store_phases(scrs[k], lse_ref, d, ts)

    n_scr = 2 * sum(d > 1 for d in dils) + 1
    outs = pl.pallas_call(
        body, name=name, grid=(S // ts,),
        in_specs=[_phase_spec(d, ts, A_WIDTH) for d in dils] * 2,
        out_specs=[pl.BlockSpec((ts, A_WIDTH), lambda i: (i, 0))] + [_phase_spec(d, ts, A_WIDTH) for d in dils],
        out_shape=[jax.ShapeDtypeStruct((S, A_WIDTH), BF16)] + [_phase_shape(S, A_WIDTH, d, F32) for d in dils],
        scratch_shapes=[_nat_scratch(ts, A_WIDTH)] * n_scr,
        compiler_params=_params(("parallel",)),
    )(*[_phase_view(o, d) for o, d in zip(os_, dils)], *[_phase_view(l, d) for l, d in zip(lses, dils)])
    return outs[0], [l.reshape(S, A_WIDTH) for l in outs[1:]]


def attn_delta(dout, out, ones, dils, *, name):
    S = dout.shape[0]
    ts = _tile(S, 256)
    nd = len(dils)

    def body(d_ref, o_ref, ones_ref, *rest):
        do_refs, dl_refs, (scr_do, scr_dl) = rest[:nd], rest[nd:2 * nd], rest[2 * nd:]
        sums = []
        for c in range(A_WIDTH // LANES):
            sl = slice(c * LANES, (c + 1) * LANES)
            prod = d_ref[:, sl].astype(F32) * o_ref[:, sl].astype(F32)
            sums.append(_seg_sum(prod, ones_ref))
        _put_natural(scr_do, d_ref[...].astype(F32))
        _put_natural(scr_dl, jnp.concatenate(sums, axis=1))
        for d, do_ref, dl_ref in zip(dils, do_refs, dl_refs):
            if d == 1:
                do_ref[...] = d_ref[...]
                dl_ref[...] = _get_natural(scr_dl)
            else:
                _store_phases(scr_do, do_ref, d, ts)
                _store_phases(scr_dl, dl_ref, d, ts)

    spec = pl.BlockSpec((ts, A_WIDTH), lambda i: (i, 0))
    outs = pl.pallas_call(
        body, name=name, grid=(S // ts,), in_specs=[spec, spec, _const_spec((LANES, LANES))],
        out_specs=[_phase_spec(d, ts, A_WIDTH) for d in dils] * 2,
        out_shape=[_phase_shape(S, A_WIDTH, d, BF16) for d in dils] + [_phase_shape(S, A_WIDTH, d, F32) for d in dils],
        scratch_shapes=[_nat_scratch(ts, A_WIDTH)] * 2,
        compiler_params=_params(("parallel",)),
    )(dout, out, ones)
    outs = [o.reshape(S, A_WIDTH) for o in outs]
    return outs[:nd], outs[nd:]


def _head_col(x, hh):
    c = hh * A_HEAD_DIM
    return x[:, c:c + 1]


def attn_bwd_dq(qkvp, dout, lse, delta, dil, *, name):
    S = qkvp.shape[0]
    L = S // dil
    tq = _tile(L, A_BLOCK)
    sub = min(tq, A_SUB)
    nsub = tq // sub
    nlb = L // A_HALF
    band = _band_bias(sub, False)

    def body(q_ref, kp_ref, ko_ref, kn_ref, vp_ref, vo_ref, vn_ref, do_ref, l_ref, d_ref, band_ref, dq_ref):
        b = pl.program_id(1)
        K = _cat3(kp_ref, ko_ref, kn_ref)
        V = _cat3(vp_ref, vo_ref, vn_ref)
        lo_k = _lane_lo(tq + 2 * A_HALF)
        Km = [jnp.where(sel, K, jnp.zeros_like(K)) for sel in (lo_k, ~lo_k)]
        Vm = [jnp.where(sel, V, jnp.zeros_like(V)) for sel in (lo_k, ~lo_k)]
        for r in range(nsub):
            rows = slice(r * sub, (r + 1) * sub)
            keys = slice(r * sub, (r + 1) * sub + 2 * A_HALF)
            bias = _with_edge(band_ref[...], _edge_bias(b * tq + r * sub, sub + 2 * A_HALF, L, 1, r == 0, r == nsub - 1))
            q, do = q_ref[rows, :], do_ref[rows, :]
            lse_v, dl_v = l_ref[rows, :], d_ref[rows, :]
            acc = None
            for hh in range(2):
                s = lax.dot_general(q, Km[hh][keys], NT, preferred_element_type=F32) + bias
                p = jnp.exp(s - _head_col(lse_v, hh))
                dp = lax.dot_general(do, Vm[hh][keys], NT, preferred_element_type=F32)
                ds = p * (dp - _head_col(dl_v, hh))
                t = jnp.dot(ds.astype(BF16), Km[hh][keys], preferred_element_type=F32)
                acc = t if acc is None else acc + t
            dq_ref[rows, :] = acc.astype(dq_ref.dtype)

    nspec = _band_specs(0, tq, nlb)[1]
    return pl.pallas_call(
        body, name=name, grid=(dil, L // tq, A_WIDTH // LANES),
        in_specs=[nspec] + _band_specs(1, tq, nlb) + _band_specs(2, tq, nlb) + [nspec, nspec, nspec, _const_spec(band.shape)],
        out_specs=nspec,
        out_shape=jax.ShapeDtypeStruct((S, A_WIDTH), BF16),
        compiler_params=_params(("parallel", "parallel", "parallel")),
    )(*([qkvp] * 7), dout, lse, delta, band)


def attn_bwd_dkv(qkvp, dout, lse, delta, dil, *, name):
    S = qkvp.shape[0]
    L = S // dil
    tk = _tile(L, A_BLOCK)
    sub = min(tk, A_SUB)
    nsub = tk // sub
    nlb = L // A_HALF
    band = _band_bias(sub, False)

    def body(qp_ref, qo_ref, qn_ref, k_ref, v_ref, dp_ref, do_ref, dn_ref, lp_ref, lo_ref, ln_ref,
             ep_ref, eo_ref, en_ref, band_ref, dk_ref, dv_ref):
        b = pl.program_id(1)
        Q = _cat3(qp_ref, qo_ref, qn_ref)
        DO = _cat3(dp_ref, do_ref, dn_ref)
        lse_v = _cat3(lp_ref, lo_ref, ln_ref)
        dl_v = _cat3(ep_ref, eo_ref, en_ref)
        lo_q = _lane_lo(tk + 2 * A_HALF)
        Qm = [jnp.where(sel, Q, jnp.zeros_like(Q)) for sel in (lo_q, ~lo_q)]
        DOm = [jnp.where(sel, DO, jnp.zeros_like(DO)) for sel in (lo_q, ~lo_q)]
        for r in range(nsub):
            keys = slice(r * sub, (r + 1) * sub)
            qs = slice(r * sub, (r + 1) * sub + 2 * A_HALF)
            bias = _with_edge(band_ref[...], _edge_bias(b * tk + r * sub, sub + 2 * A_HALF, L, 1, r == 0, r == nsub - 1))
            K, V = k_ref[keys, :], v_ref[keys, :]
            lse_t, dl_t = lse_v[qs].T, dl_v[qs].T
            dk = dv = None
            for hh in range(2):
                hr = slice(hh * A_HEAD_DIM, hh * A_HEAD_DIM + 1)
                st = lax.dot_general(K, Qm[hh][qs], NT, preferred_element_type=F32) + bias
                pt = jnp.exp(st - lse_t[hr, :])
                dpt = lax.dot_general(V, DOm[hh][qs], NT, preferred_element_type=F32)
                dst = pt * (dpt - dl_t[hr, :])
                tv = jnp.dot(pt.astype(BF16), DOm[hh][qs], preferred_element_type=F32)
                tk_ = jnp.dot(dst.astype(BF16), Qm[hh][qs], preferred_element_type=F32)
                dv = tv if dv is None else dv + tv
                dk = tk_ if dk is None else dk + tk_
            dk_ref[keys, :] = dk.astype(dk_ref.dtype)
            dv_ref[keys, :] = dv.astype(dv_ref.dtype)

    nspec = _band_specs(0, tk, nlb)[1]
    return pl.pallas_call(
        body, name=name, grid=(dil, L // tk, A_WIDTH // LANES),
        in_specs=_band_specs(0, tk, nlb) + [_band_specs(1, tk, nlb)[1], _band_specs(2, tk, nlb)[1]]
        + _band_specs(0, tk, nlb) * 3 + [_const_spec(band.shape)],
        out_specs=[nspec, nspec],
        out_shape=[jax.ShapeDtypeStruct((S, A_WIDTH), BF16)] * 2,
        compiler_params=_params(("parallel", "parallel", "parallel")),
    )(*([qkvp] * 5), *([dout] * 3), *([lse] * 3), *([delta] * 3), band)


def _gate_block_weight(wf, wb):
    w = jnp.zeros((LANES, 2 * B_QK_WIDTH), F32)
    w = w.at[:B_GATE_RANK, :B_QK_WIDTH].set(wf)
    w = w.at[B_GATE_RANK:2 * B_GATE_RANK, B_QK_WIDTH:].set(wb)
    return w.astype(BF16)


def gate_fwd(proj, wblk, bias, *, name):
    S = proj.shape[0]
    ts = _tile(S, 512)
    W = 2 * B_QK_WIDTH
    zcol = (2 * B_QK_WIDTH + 2 * B_V_WIDTH) // LANES

    def body(z_ref, w_ref, b_ref, o_ref):
        x = jnp.dot(z_ref[...].astype(BF16), w_ref[...], preferred_element_type=F32) + b_ref[...]
        o_ref[...] = (jnp.minimum(x, 0.0) - jnp.log(1.0 + jnp.exp(-jnp.abs(x)))) * (1.0 / B_GATE_TAU)

    return pl.pallas_call(
        body, name=name, grid=(S // ts,),
        in_specs=[pl.BlockSpec((ts, LANES), lambda i: (i, zcol)), _const_spec((LANES, W)), _const_spec((1, W))],
        out_specs=pl.BlockSpec((ts, W), lambda i: (i, 0)),
        out_shape=jax.ShapeDtypeStruct((S, W), F32),
        compiler_params=_params(("parallel",)),
    )(proj, wblk, bias)


def gate_bwd(proj, wblk, wblk_t, bias, dloga, *, name):
    S = proj.shape[0]
    ts = _tile(S, 512)
    W = 2 * B_QK_WIDTH
    zcol = (2 * B_QK_WIDTH + 2 * B_V_WIDTH) // LANES

    def body(z_ref, w_ref, wt_ref, b_ref, d_ref, dz_ref, dw_ref, db_ref):
        @pl.when(pl.program_id(0) == 0)
        def _():
            dw_ref[...] = jnp.zeros_like(dw_ref)
            db_ref[...] = jnp.zeros_like(db_ref)

        z = z_ref[...].astype(BF16)
        x = jnp.dot(z, w_ref[...], preferred_element_type=F32) + b_ref[...]
        e = jnp.exp(-jnp.abs(x))
        sig_neg = jnp.where(x >= 0, e, 1.0) / (1.0 + e)
        dx = d_ref[...] * (1.0 / B_GATE_TAU) * sig_neg
        dxb = dx.astype(BF16)
        dz_ref[...] = jnp.dot(dxb, wt_ref[...], preferred_element_type=F32)
        dw_ref[...] += lax.dot_general(z, dxb, TN, preferred_element_type=F32)
        db_ref[0:1, :] += jnp.sum(dx, axis=0, keepdims=True)

    return pl.pallas_call(
        body, name=name, grid=(S // ts,),
        in_specs=[pl.BlockSpec((ts, LANES), lambda i: (i, zcol)), _const_spec((LANES, W)), _const_spec((W, LANES)),
                  _const_spec((1, W)), pl.BlockSpec((ts, W), lambda i: (i, 0))],
        out_specs=[pl.BlockSpec((ts, LANES), lambda i: (i, 0)), _const_spec((LANES, W)), _const_spec((8, W))],
        out_shape=[jax.ShapeDtypeStruct((S, LANES), F32), jax.ShapeDtypeStruct((LANES, W), F32),
                   jax.ShapeDtypeStruct((8, W), F32)],
        compiler_params=_params(("arbitrary",)),
    )(proj, wblk, wblk_t, bias, dloga)


def _tri(reverse):
    i = np.arange(B_CHUNK)
    t = (i[None, :] >= i[:, None]) if reverse else (i[None, :] <= i[:, None])
    return jnp.asarray(t.astype(np.float32))


def _gla_terms(q, k, la, t_ref, reverse):
    b = jnp.dot(t_ref[...], la, precision=HI, preferred_element_type=F32)
    b_last = b[0:1, :] if reverse else b[B_CHUNK - 1:B_CHUNK, :]
    e_b = jnp.exp(b)
    qt = (q * (B_KEY_DIM ** -0.5)) * e_b
    e_nb = jnp.exp(-b)
    kt = k * e_nb
    e_end = jnp.exp(b_last - b)
    kend = k * e_end
    dec = jnp.exp(b_last)
    return e_nb, e_b, qt, kt, e_end, kend, dec


def _chunk_mask(reverse, transpose=False):
    r = lax.broadcasted_iota(jnp.int32, (B_CHUNK, B_CHUNK), 0)
    c = lax.broadcasted_iota(jnp.int32, (B_CHUNK, B_CHUNK), 1)
    if transpose:
        r, c = c, r
    return (c > r) if reverse else (c <= r)


def gla_fwd(proj, loga, tmat, reverse, *, name):
    S = proj.shape[0]
    tb = _tile(S, 512)
    nb = S // tb
    cpb = tb // B_CHUNK
    nc = S // B_CHUNK
    qb, kb_, vb = 0, B_QK_WIDTH // B_KEY_DIM, 2 * B_QK_WIDTH // B_VAL_DIM
    lb = (B_QK_WIDTH // B_KEY_DIM) if reverse else 0
    blk = (lambda i: nb - 1 - i) if reverse else (lambda i: i)

    def body(q_ref, k_ref, v_ref, la_ref, t_ref, o_ref, st_ref, s_scr):
        @pl.when(pl.program_id(1) == 0)
        def _():
            s_scr[...] = jnp.zeros_like(s_scr)

        mask = _chunk_mask(reverse)
        order = range(cpb - 1, -1, -1) if reverse else range(cpb)
        for c in order:
            rows = slice(c * B_CHUNK, (c + 1) * B_CHUNK)
            v = v_ref[rows, :].astype(BF16)
            _, _, qt, kt, _, kend, dec = _gla_terms(q_ref[rows, :], k_ref[rows, :], la_ref[rows, :], t_ref, reverse)
            qt, kt, kend = qt.astype(BF16), kt.astype(BF16), kend.astype(BF16)
            st = s_scr[...]
            st_ref[0, c] = st
            a = jnp.where(mask, lax.dot_general(qt, kt, NT, preferred_element_type=F32), 0.0)
            o = jnp.dot(a.astype(BF16), v, preferred_element_type=F32)
            o = o + lax.dot_general(qt, st.astype(BF16), NT, preferred_element_type=F32)
            o_ref[rows, :] = o
            s_scr[...] = st * dec + lax.dot_general(v, kend, TN, preferred_element_type=F32)

    return pl.pallas_call(
        body, name=name, grid=(B_HEADS, nb),
        in_specs=[pl.BlockSpec((tb, B_KEY_DIM), lambda h, i: (blk(i), qb + h)),
                  pl.BlockSpec((tb, B_KEY_DIM), lambda h, i: (blk(i), kb_ + h)),
                  pl.BlockSpec((tb, B_VAL_DIM), lambda h, i: (blk(i), vb + h)),
                  pl.BlockSpec((tb, B_KEY_DIM), lambda h, i: (blk(i), lb + h)),
                  _const_spec((B_CHUNK, B_CHUNK))],
        out_specs=[pl.BlockSpec((tb, B_VAL_DIM), lambda h, i: (blk(i), h)),
                   pl.BlockSpec((1, cpb, B_VAL_DIM, B_KEY_DIM), lambda h, i: (h, blk(i), 0, 0))],
        out_shape=[jax.ShapeDtypeStruct((S, B_V_WIDTH), F32),
                   jax.ShapeDtypeStruct((B_HEADS, nc, B_VAL_DIM, B_KEY_DIM), F32)],
        scratch_shapes=[pltpu.VMEM((B_VAL_DIM, B_KEY_DIM), F32)],
        compiler_params=_params(("parallel", "arbitrary")),
    )(proj, proj, proj, loga, tmat)


def gla_bwd(proj, loga, states, do, tmat, tmat_t, reverse, *, name):
    S = proj.shape[0]
    tb = _tile(S, 512)
    nb = S // tb
    cpb = tb // B_CHUNK
    qb, kb_, vb = 0, B_QK_WIDTH // B_KEY_DIM, 2 * B_QK_WIDTH // B_VAL_DIM
    lb = (B_QK_WIDTH // B_KEY_DIM) if reverse else 0
    blk = (lambda i: i) if reverse else (lambda i: nb - 1 - i)
    scale = B_KEY_DIM ** -0.5

    def body(q_ref, k_ref, v_ref, la_ref, st_ref, do_ref, t_ref, tt_ref, dq_ref, dk_ref, dv_ref, dla_ref, ds_scr):
        @pl.when(pl.program_id(1) == 0)
        def _():
            ds_scr[...] = jnp.zeros_like(ds_scr)

        mask = _chunk_mask(reverse)
        mask_t = _chunk_mask(reverse, transpose=True)
        last = 0 if reverse else B_CHUNK - 1
        is_last = lax.broadcasted_iota(jnp.int32, (B_CHUNK, B_KEY_DIM), 0) == last
        order = range(cpb) if reverse else range(cpb - 1, -1, -1)
        for c in order:
            rows = slice(c * B_CHUNK, (c + 1) * B_CHUNK)
            vf = v_ref[rows, :]
            v = vf.astype(BF16)
            dov = do_ref[rows, :]
            dob = dov.astype(BF16)
            e_nb, e_b, qt, kt, e_end, kend, dec = _gla_terms(q_ref[rows, :], k_ref[rows, :], la_ref[rows, :], t_ref, reverse)
            qtb, ktb, kendb = qt.astype(BF16), kt.astype(BF16), kend.astype(BF16)
            st = st_ref[0, c]
            dst = ds_scr[...]
            dstb = dst.astype(BF16)
            a_t = jnp.where(mask_t, lax.dot_general(ktb, qtb, NT, preferred_element_type=F32), 0.0)
            da = jnp.where(mask, lax.dot_general(dob, v, NT, preferred_element_type=F32), 0.0)
            da_t = jnp.where(mask_t, lax.dot_general(v, dob, NT, preferred_element_type=F32), 0.0)
            dv = jnp.dot(a_t.astype(BF16), dob, preferred_element_type=F32)
            dv = dv + lax.dot_general(kendb, dstb, NT, preferred_element_type=F32)
            dqt = jnp.dot(da.astype(BF16), ktb, preferred_element_type=F32)
            dqt = dqt + jnp.dot(dob, st.astype(BF16), preferred_element_type=F32)
            dkt = jnp.dot(da_t.astype(BF16), qtb, preferred_element_type=F32)
            dkend = jnp.dot(v, dstb, preferred_element_type=F32)
            ddec = jnp.sum(dst * st, axis=0, keepdims=True)
            ds_scr[...] = dst * dec + lax.dot_general(dob, qtb, TN, preferred_element_type=F32)
            ke = dkend * kend
            db = dqt * qt - dkt * kt - ke
            db_last = jnp.sum(ke, axis=0, keepdims=True) + ddec * dec
            db = db + jnp.where(is_last, db_last, 0.0)
            dq_ref[rows, :] = dqt * e_b * scale
            dk_ref[rows, :] = dkt * e_nb + dkend * e_end
            dv_ref[rows, :] = dv
            dla_ref[rows, :] = jnp.dot(tt_ref[...], db, precision=HI, preferred_element_type=F32)

    kspec = lambda cb: pl.BlockSpec((tb, B_KEY_DIM), lambda h, i: (blk(i), cb + h))
    return pl.pallas_call(
        body, name=name, grid=(B_HEADS, nb),
        in_specs=[kspec(qb), kspec(kb_), pl.BlockSpec((tb, B_VAL_DIM), lambda h, i: (blk(i), vb + h)), kspec(lb),
                  pl.BlockSpec((1, cpb, B_VAL_DIM, B_KEY_DIM), lambda h, i: (h, blk(i), 0, 0)),
                  pl.BlockSpec((tb, B_VAL_DIM), lambda h, i: (blk(i), h)),
                  _const_spec((B_CHUNK, B_CHUNK)), _const_spec((B_CHUNK, B_CHUNK))],
        out_specs=[kspec(0), kspec(0), pl.BlockSpec((tb, B_VAL_DIM), lambda h, i: (blk(i), h)), kspec(0)],
        out_shape=[jax.ShapeDtypeStruct((S, B_QK_WIDTH), F32), jax.ShapeDtypeStruct((S, B_QK_WIDTH), F32),
                   jax.ShapeDtypeStruct((S, B_V_WIDTH), F32), jax.ShapeDtypeStruct((S, B_QK_WIDTH), F32)],
        scratch_shapes=[pltpu.VMEM((B_VAL_DIM, B_KEY_DIM), F32)],
        compiler_params=_params(("parallel", "arbitrary")),
    )(proj, proj, proj, loga, states, do, tmat, tmat_t)


def gla_post_fwd(o_f, o_b, gain, proj, *, name):
    S = o_f.shape[0]
    ts = _tile(S, 512)
    rcol = (2 * B_QK_WIDTH + B_V_WIDTH) // B_V_WIDTH

    def body(f_ref, b_ref, g_ref, r_ref, y_ref):
        for h in range(B_HEADS):
            sl = slice(h * B_VAL_DIM, (h + 1) * B_VAL_DIM)
            o = f_ref[:, sl] + b_ref[:, sl]
            n = (o * lax.rsqrt(jnp.mean(o * o, axis=-1, keepdims=True) + RMS_EPS)) * g_ref[:, sl]
            r = r_ref[:, sl]
            y_ref[:, sl] = (n * (r * (1.0 / (1.0 + jnp.exp(-r))))).astype(y_ref.dtype)

    spec = pl.BlockSpec((ts, B_V_WIDTH), lambda i: (i, 0))
    return pl.pallas_call(
        body, name=name, grid=(S // ts,),
        in_specs=[spec, spec, _const_spec((1, B_V_WIDTH)), pl.BlockSpec((ts, B_V_WIDTH), lambda i: (i, rcol))],
        out_specs=spec, out_shape=jax.ShapeDtypeStruct((S, B_V_WIDTH), BF16),
        compiler_params=_params(("parallel",)),
    )(o_f, o_b, gain, proj)


def gla_post_bwd(o_f, o_b, gain, proj, dy, *, name):
    S = o_f.shape[0]
    ts = _tile(S, 512)
    rcol = (2 * B_QK_WIDTH + B_V_WIDTH) // B_V_WIDTH

    def body(f_ref, b_ref, g_ref, r_ref, dy_ref, do_ref, dr_ref, dg_ref):
        @pl.when(pl.program_id(0) == 0)
        def _():
            dg_ref[...] = jnp.zeros_like(dg_ref)

        for h in range(B_HEADS):
            sl = slice(h * B_VAL_DIM, (h + 1) * B_VAL_DIM)
            o = f_ref[:, sl] + b_ref[:, sl]
            rs = lax.rsqrt(jnp.mean(o * o, axis=-1, keepdims=True) + RMS_EPS)
            ohat = o * rs
            g = g_ref[:, sl]
            r = r_ref[:, sl]
            sig = 1.0 / (1.0 + jnp.exp(-r))
            dyv = dy_ref[:, sl].astype(F32)
            dn = dyv * (r * sig)
            dr_ref[:, sl] = dyv * (ohat * g) * (sig * (1.0 + r * (1.0 - sig)))
            dng = dn * g
            do_ref[:, sl] = rs * (dng - ohat * jnp.mean(dng * ohat, axis=-1, keepdims=True))
            dg_ref[0:1, sl] += jnp.sum(dn * ohat, axis=0, keepdims=True)

    spec = pl.BlockSpec((ts, B_V_WIDTH), lambda i: (i, 0))
    return pl.pallas_call(
        body, name=name, grid=(S // ts,),
        in_specs=[spec, spec, _const_spec((1, B_V_WIDTH)), pl.BlockSpec((ts, B_V_WIDTH), lambda i: (i, rcol)), spec],
        out_specs=[spec, spec, _const_spec((8, B_V_WIDTH))],
        out_shape=[jax.ShapeDtypeStruct((S, B_V_WIDTH), F32), jax.ShapeDtypeStruct((S, B_V_WIDTH), F32),
                   jax.ShapeDtypeStruct((8, B_V_WIDTH), F32)],
        compiler_params=_params(("arbitrary",)),
    )(o_f, o_b, gain, proj, dy)


def gla_combine(parts_f, parts_b, dr, dz, *, name):
    S = dr.shape[0]
    ts = _tile(S, 512)

    def body(qf, kf, vf, qb, kb, vb, r_ref, z_ref, o_ref):
        o_ref[:, 0:512] = (qf[...] + qb[...]).astype(o_ref.dtype)
        o_ref[:, 512:1024] = (kf[...] + kb[...]).astype(o_ref.dtype)
        o_ref[:, 1024:2048] = (vf[...] + vb[...]).astype(o_ref.dtype)
        o_ref[:, 2048:3072] = r_ref[...].astype(o_ref.dtype)
        o_ref[:, 3072:3200] = z_ref[...].astype(o_ref.dtype)

    s512 = pl.BlockSpec((ts, B_QK_WIDTH), lambda i: (i, 0))
    s1024 = pl.BlockSpec((ts, B_V_WIDTH), lambda i: (i, 0))
    return pl.pallas_call(
        body, name=name, grid=(S // ts,),
        in_specs=[s512, s512, s1024, s512, s512, s1024, s1024, pl.BlockSpec((ts, LANES), lambda i: (i, 0))],
        out_specs=pl.BlockSpec((ts, B_IN_PAD), lambda i: (i, 0)),
        out_shape=jax.ShapeDtypeStruct((S, B_IN_PAD), BF16),
        compiler_params=_params(("parallel",)),
    )(*parts_f, *parts_b, dr, dz)


def adamw(w, g, m, v, *, name):
    R, C = w.shape
    tr = _tile(R, 256)
    c1 = 1.0 / (1.0 - ADAM_B1 ** ADAM_STEP)
    c2 = 1.0 / (1.0 - ADAM_B2 ** ADAM_STEP)

    def body(w_ref, g_ref, m_ref, v_ref, d_ref, mo_ref, vo_ref):
        gv = g_ref[...]
        mn = ADAM_B1 * m_ref[...] + (1.0 - ADAM_B1) * gv
        vn = ADAM_B2 * v_ref[...] + (1.0 - ADAM_B2) * (gv * gv)
        mo_ref[...] = mn
        vo_ref[...] = vn
        d_ref[...] = -ADAM_LR * ((mn * c1) / (jnp.sqrt(vn * c2) + ADAM_EPS) + ADAM_WD * w_ref[...])

    spec = pl.BlockSpec((tr, C), lambda i: (i, 0))
    return pl.pallas_call(
        body, name=name, grid=(R // tr,), in_specs=[spec] * 4, out_specs=[spec] * 3,
        out_shape=[jax.ShapeDtypeStruct((R, C), F32)] * 3,
        compiler_params=_params(("parallel",)),
    )(w, g, m, v)


def _chip_peers():
    x, y, c = lax.axis_index("x"), lax.axis_index("y"), lax.axis_index("c")
    return x, y, c, [(1 - x, y), (x, 1 - y), (1 - x, 1 - y)]


_ANY = pl.BlockSpec(memory_space=pl.ANY)


def gather_shards(src, *, name):
    _, R, C = src.shape

    def body(src_ref, out_ref, ici_send, ici_recv, d2d_send, d2d_recv):
        x, y, c, chips = _chip_peers()
        me = 2 * x + y
        sibling = (x, y, 1 - c)
        sends = []
        for k, (px, py) in enumerate(chips):
            cp = pltpu.make_async_remote_copy(
                src_ref=src_ref.at[c], dst_ref=out_ref.at[me, c], send_sem=ici_send.at[k], recv_sem=ici_recv.at[k],
                device_id=(px, py, c), device_id_type=MESH)
            cp.start()
            sends.append(cp)
        for k, (px, py) in enumerate(chips):
            landed = out_ref.at[2 * px + py, c]
            pltpu.make_async_remote_copy(
                src_ref=src_ref.at[c], dst_ref=landed, send_sem=ici_send.at[k], recv_sem=ici_recv.at[k],
                device_id=(px, py, c), device_id_type=MESH).wait_recv()
            cp = pltpu.make_async_remote_copy(
                src_ref=landed, dst_ref=landed, send_sem=d2d_send.at[k], recv_sem=d2d_recv.at[k],
                device_id=sibling, device_id_type=MESH)
            cp.start()
            sends.append(cp)
        for k, (px, py) in enumerate(chips):
            other_half = out_ref.at[2 * px + py, 1 - c]
            pltpu.make_async_remote_copy(
                src_ref=other_half, dst_ref=other_half, send_sem=d2d_send.at[k], recv_sem=d2d_recv.at[k],
                device_id=sibling, device_id_type=MESH).wait_recv()
        for cp in sends:
            cp.wait_send()

    return pl.pallas_call(
        body, name=name, in_specs=[_ANY], out_specs=_ANY,
        out_shape=jax.ShapeDtypeStruct((N_CHIPS, 2, R, C), src.dtype),
        scratch_shapes=[pltpu.SemaphoreType.DMA((3,))] * 4,
        compiler_params=pltpu.CompilerParams(has_side_effects=True),
    )(src)


def chip_exchange(srcs, *, name):
    n = len(srcs)

    def body(*refs):
        src_refs, out_refs = refs[:n], refs[n:2 * n]
        send_sems, recv_sems = refs[2 * n:]
        x, y, c, chips = _chip_peers()
        me = 2 * x + y
        copies = []
        for i, (src_ref, out_ref) in enumerate(zip(src_refs, out_refs)):
            for k, (px, py) in enumerate(chips):
                cp = pltpu.make_async_remote_copy(
                    src_ref=src_ref.at[2 * px + py], dst_ref=out_ref.at[me],
                    send_sem=send_sems.at[i, k], recv_sem=recv_sems.at[i, k],
                    device_id=(px, py, c), device_id_type=MESH)
                cp.start()
                copies.append(cp)
        for i, (src_ref, out_ref) in enumerate(zip(src_refs, out_refs)):
            for k, (px, py) in enumerate(chips):
                pltpu.make_async_remote_copy(
                    src_ref=src_ref.at[me], dst_ref=out_ref.at[2 * px + py],
                    send_sem=send_sems.at[i, k], recv_sem=recv_sems.at[i, k],
                    device_id=(px, py, c), device_id_type=MESH).wait_recv()
        for cp in copies:
            cp.wait_send()

    return pl.pallas_call(
        body, name=name, in_specs=[_ANY] * n, out_specs=[_ANY] * n,
        out_shape=[jax.ShapeDtypeStruct(s.shape, s.dtype) for s in srcs],
        scratch_shapes=[pltpu.SemaphoreType.DMA((n, 3)), pltpu.SemaphoreType.DMA((n, 3))],
        compiler_params=pltpu.CompilerParams(has_side_effects=True),
    )(*srcs)


def sibling_exchange(srcs, *, name):
    n = len(srcs)

    def body(*refs):
        src_refs, out_refs, (send_sems, recv_sems) = refs[:n], refs[n:2 * n], refs[2 * n:]
        x, y, c = lax.axis_index("x"), lax.axis_index("y"), lax.axis_index("c")
        copies = []
        for i, (src_ref, out_ref) in enumerate(zip(src_refs, out_refs)):
            cp = pltpu.make_async_remote_copy(
                src_ref=src_ref.at[:, 1 - c], dst_ref=out_ref, send_sem=send_sems.at[i], recv_sem=recv_sems.at[i],
                device_id=(x, y, 1 - c), device_id_type=MESH)
            cp.start()
            copies.append(cp)
        for cp in copies:
            cp.wait()

    return pl.pallas_call(
        body, name=name, in_specs=[_ANY] * n, out_specs=[_ANY] * n,
        out_shape=[jax.ShapeDtypeStruct((s.shape[0],) + s.shape[2:], s.dtype) for s in srcs],
        scratch_shapes=[pltpu.SemaphoreType.DMA((n,)), pltpu.SemaphoreType.DMA((n,))],
        compiler_params=pltpu.CompilerParams(has_side_effects=True),
    )(*srcs)


def sibling_share(bufs, *, name):
    n = len(bufs)

    def body(*refs):
        out_refs, (send_sems, recv_sems) = refs[n:2 * n], refs[2 * n:]
        x, y, c = lax.axis_index("x"), lax.axis_index("y"), lax.axis_index("c")
        copies = []
        for i, out_ref in enumerate(out_refs):
            cp = pltpu.make_async_remote_copy(
                src_ref=out_ref.at[c], dst_ref=out_ref.at[c], send_sem=send_sems.at[i], recv_sem=recv_sems.at[i],
                device_id=(x, y, 1 - c), device_id_type=MESH)
            cp.start()
            copies.append(cp)
        for cp in copies:
            cp.wait()

    return pl.pallas_call(
        body, name=name, in_specs=[_ANY] * n, out_specs=[_ANY] * n,
        out_shape=[jax.ShapeDtypeStruct(b.shape, b.dtype) for b in bufs],
        input_output_aliases={i: i for i in range(n)},
        scratch_shapes=[pltpu.SemaphoreType.DMA((n,))] * 2,
        compiler_params=pltpu.CompilerParams(has_side_effects=True),
    )(*bufs)


def add_pair(a, b, out_dtype, *, name):
    n, _, R, C = a.shape
    tr = _tile(R, 256)

    def body(c_ref, a_ref, b_ref, o_ref):
        o_ref[...] = (a_ref[0] + b_ref[...]).astype(o_ref.dtype)

    return pl.pallas_call(
        body, name=name,
        grid_spec=pltpu.PrefetchScalarGridSpec(
            num_scalar_prefetch=1, grid=(n, R // tr),
            in_specs=[pl.BlockSpec((1, 1, tr, C), lambda s, i, c_ref: (s, c_ref[0], i, 0)),
                      pl.BlockSpec((1, tr, C), lambda s, i, c_ref: (s, i, 0))],
            out_specs=pl.BlockSpec((1, tr, C), lambda s, i, c_ref: (s, i, 0))),
        out_shape=jax.ShapeDtypeStruct((n, R, C), out_dtype),
        compiler_params=_params(("parallel", "parallel")),
    )(lax.axis_index("c").reshape(1).astype(jnp.int32), a, b)


def sum_slots(slots, own, *, name):
    n, R, C = slots.shape
    tr = _tile(R, 256)

    def body(ids_ref, *refs):
        slot_refs, own_ref, o_ref = refs[:n], refs[n], refs[n + 1]
        me = ids_ref[0]
        acc = None
        for s, r in enumerate(slot_refs):
            t = jnp.where(me == s, own_ref[0], r[0]).astype(F32)
            acc = t if acc is None else acc + t
        o_ref[0] = acc

    def slot_spec(s):
        return pl.BlockSpec((1, tr, C), lambda i, ids: (jnp.where(ids[0] == s, (s + 1) % n, s), i, 0))

    x, y, c = lax.axis_index("x"), lax.axis_index("y"), lax.axis_index("c")
    return pl.pallas_call(
        body, name=name,
        grid_spec=pltpu.PrefetchScalarGridSpec(
            num_scalar_prefetch=1, grid=(R // tr,),
            in_specs=[slot_spec(s) for s in range(n)] + [pl.BlockSpec((1, tr, C), lambda i, ids: (ids[0], i, 0))],
            out_specs=pl.BlockSpec((1, tr, C), lambda i, ids: (ids[1], i, 0))),
        out_shape=jax.ShapeDtypeStruct((2, R, C), F32),
        compiler_params=_params(("parallel",)),
    )(jnp.stack([2 * x + y, c]).astype(jnp.int32), *([slots] * n), own)


def _flat_rows(n_elems, mult):
    rows = -(-n_elems // FLAT_COLS)
    return -(-rows // mult) * mult


def _to_flat(parts, mult):
    v = jnp.concatenate([p.reshape(-1) for p in parts])
    rows = _flat_rows(v.shape[0], mult)
    return jnp.pad(v, (0, rows * FLAT_COLS - v.shape[0])).reshape(rows, FLAT_COLS)


def _from_flat(flat, shapes):
    v = flat.reshape(-1)
    out, off = [], 0
    for s in shapes:
        n = int(np.prod(s))
        out.append(v[off:off + n].reshape(s))
        off += n
    return out


def _unshard(blocks, axis):
    return jnp.concatenate([blocks[s] for s in range(N_CHIPS)], axis=axis)


def _by_shard(full, axis):
    shp = full.shape
    cut = full.reshape(shp[:axis] + (N_CHIPS, shp[axis] // N_CHIPS) + shp[axis + 1:])
    return jnp.moveaxis(cut, axis, 0)


def _gradient_blocks(grads):
    blocks = []
    for n in MATRICES:
        t = _by_shard(jnp.stack(grads[n]), SHARD_AXIS[n])
        blocks.append(t.reshape(N_CHIPS, 2, -1, t.shape[-1]))
    rest = []
    for s in range(N_CHIPS):
        parts = [jnp.stack(grads[n]) if n in REPLICATED else _by_shard(jnp.stack(grads[n]), SHARD_AXIS[n])[s]
                 for n in WEIGHTS if n not in MATRICES]
        rest.append(_to_flat(parts, 16))
    rest = jnp.stack(rest)
    blocks.append(rest.reshape(N_CHIPS, 2, rest.shape[1] // 2, FLAT_COLS))
    return blocks


def _gather_weights(w):
    full = {}
    for names, dtype, mult, call in ((BIG, BF16, 32, "gather_weights"), (SMALL_SHARDED, F32, 16, "gather_vectors")):
        parts = [w[n].astype(dtype) for n in names]
        flat = _to_flat(parts, mult)
        got = gather_shards(flat.reshape(2, flat.shape[0] // 2, FLAT_COLS), name=call)
        got = got.reshape(N_CHIPS, flat.shape[0], FLAT_COLS)
        me = 2 * lax.axis_index("x") + lax.axis_index("y")
        got = lax.dynamic_update_slice(got, flat[None], (me, 0, 0))
        per_chip = [_from_flat(got[s], [p.shape for p in parts]) for s in range(N_CHIPS)]
        for i, n in enumerate(names):
            full[n] = _unshard([per_chip[s][i] for s in range(N_CHIPS)], SHARD_AXIS[n])
    return full


def _reduce_gradients(grads, w):
    halves = _gradient_blocks(grads)
    other = sibling_exchange(halves, name="grad_pair_exchange")
    pair = [add_pair(a, b, BF16 if i < len(MATRICES) else F32, name=f"grad_pair_add{i}")
            for i, (a, b) in enumerate(zip(halves, other))]
    slots = chip_exchange(pair, name="grad_chip_exchange")
    mine = [sum_slots(a, p, name=f"grad_chip_sum{i}") for i, (a, p) in enumerate(zip(slots, pair))]
    both = sibling_share(mine, name="grad_half_share")
    g = {n: t.reshape(w[n].shape) for n, t in zip(MATRICES, both)}
    rest = [n for n in WEIGHTS if n not in MATRICES]
    g.update(zip(rest, _from_flat(both[-1], [w[n].shape for n in rest])))
    return g


def _layer_fwd(i, h, p, aux):
    j = i // 2
    sv = {"h0": h}
    if i % 2 == 0:
        hns = rmsnorm_fwd(h, p["attn_norm"][i][None], dils=DILS, name=f"l{i}_norm1")
        qkvs, qkvps, os_, lses = [], [], [], []
        for g, d in enumerate(DILS):
            qkv = matmul(hns[g], p["a_w_in_g"][j][g], tn=3 * A_WIDTH // 2, name=f"l{i}_a_in{g}")
            qkvp = qk_prep_fwd(qkv, aux["a_gain"][j][g], aux["cos"][g], aux["sin"][g], aux["ones"], name=f"l{i}_a_prep{g}")
            o, l = attn_fwd(qkvp, d, name=f"l{i}_a_attn{g}")
            qkvs.append(qkv)
            qkvps.append(qkvp)
            os_.append(o)
            lses.append(l)
        out, lse = attn_merge(os_, lses, DILS, name=f"l{i}_a_merge")
        sv.update(hns=hns, qkv=qkvs, qkvp=qkvps, out=out, lse=lse)
        h1 = matmul(out, p["a_w_out"][j], res=h, name=f"l{i}_a_out")
    else:
        hn = rmsnorm_fwd(h, p["attn_norm"][i][None], name=f"l{i}_norm1")[0]
        sv["hn"] = hn
        proj = matmul(hn, p["b_w_in"][j], tn=640, name=f"l{i}_b_in")
        loga = gate_fwd(proj, aux["b_wblk"][j], aux["b_bias"][j], name=f"l{i}_b_gate")
        o_f, st_f = gla_fwd(proj, loga, aux["tri_f"], False, name=f"l{i}_b_gla_f")
        o_b, st_b = gla_fwd(proj, loga, aux["tri_b"], True, name=f"l{i}_b_gla_b")
        y = gla_post_fwd(o_f, o_b, aux["b_gain"][j], proj, name=f"l{i}_b_post")
        sv.update(proj=proj, loga=loga, o_f=o_f, o_b=o_b, st_f=st_f, st_b=st_b, y=y)
        h1 = matmul(y, p["b_w_out"][j], res=h, name=f"l{i}_b_out")
    sv["h1"] = h1
    hn2 = rmsnorm_fwd(h1, p["ffn_norm"][i][None], name=f"l{i}_norm2")[0]
    gu = matmul(hn2, p["ffn_w_gate_up"][i], out_dtype=BF16, tn=FFN_HIDDEN // 2, name=f"l{i}_f_up")
    act = swiglu_fwd(gu, name=f"l{i}_f_act")
    h2 = matmul(act, p["ffn_w_down"][i], res=h1, tk=2816, name=f"l{i}_f_down")
    sv.update(hn2=hn2, gu=gu, act=act)
    return h2, sv


def _layer_bwd(i, dh, p, pt, aux, sv, grads):
    j = i // 2
    dhb = dh.astype(BF16)
    grads["ffn_w_down"][i] = matmul(sv["act"].T, dhb, tm=FFN_HIDDEN // 2, tk=2048, name=f"l{i}_f_down_dw")
    dact = matmul(dhb, pt["ffn_w_down"][i], out_dtype=BF16, tn=FFN_HIDDEN // 2, name=f"l{i}_f_down_dx")
    dgu = swiglu_bwd(sv["gu"], dact, name=f"l{i}_f_act_bwd")
    grads["ffn_w_gate_up"][i] = matmul(sv["hn2"].T, dgu, tk=2048, name=f"l{i}_f_up_dw")
    dhn2 = matmul(dgu, pt["ffn_w_gate_up"][i], tk=2816, name=f"l{i}_f_up_dx")
    dh1, dg = rmsnorm_bwd(sv["h1"], p["ffn_norm"][i][None], [dhn2], dh, name=f"l{i}_norm2_bwd")
    grads["ffn_norm"][i] = dg[0]
    dh1b = dh1.astype(BF16)
    if i % 2 == 0:
        grads["a_w_out"][j] = matmul(sv["out"].T, dh1b, tk=2048, name=f"l{i}_a_out_dw")
        dout = matmul(dh1b, pt["a_w_out"][j], out_dtype=BF16, name=f"l{i}_a_out_dx")
        douts, deltas = attn_delta(dout, sv["out"], aux["ones"], DILS, name=f"l{i}_a_delta")
        dws, dhns, dgq, dgk = [], [], [], []
        for g, d in enumerate(DILS):
            qkvp, lse = sv["qkvp"][g], sv["lse"][g]
            dq = attn_bwd_dq(qkvp, douts[g], lse, deltas[g], d, name=f"l{i}_a_dq{g}")
            dk, dv = attn_bwd_dkv(qkvp, douts[g], lse, deltas[g], d, name=f"l{i}_a_dkv{g}")
            dqkv, dgain = qk_prep_bwd(sv["qkv"][g], aux["a_gain"][j][g], aux["cos"][g], aux["sin"][g], aux["ones"],
                                      [dq, dk, dv], name=f"l{i}_a_prep_bwd{g}")
            dgh = dgain[0].reshape(3, A_HEADS, A_HEAD_DIM).sum(axis=1)
            dgq.append(dgh[0])
            dgk.append(dgh[1])
            dws.append(matmul(sv["hns"][g].T, dqkv, tk=2048, name=f"l{i}_a_in_dw{g}"))
            dhns.append(matmul(dqkv, pt["a_w_in_g"][j][g], tk=3072, name=f"l{i}_a_in_dx{g}"))
        grads["a_q_norm"][j] = jnp.stack(dgq)
        grads["a_k_norm"][j] = jnp.stack(dgk)
        grads["a_w_in"][j] = jnp.concatenate(dws, axis=1)
        dh0, dg = rmsnorm_bwd(sv["h0"], p["attn_norm"][i][None], dhns, dh1, dils=DILS, name=f"l{i}_norm1_bwd")
    else:
        grads["b_w_out"][j] = matmul(sv["y"].T, dh1b, tk=2048, name=f"l{i}_b_out_dw")
        dy = matmul(dh1b, pt["b_w_out"][j], name=f"l{i}_b_out_dx")
        do, dr, dgn = gla_post_bwd(sv["o_f"], sv["o_b"], aux["b_gain"][j], sv["proj"], dy, name=f"l{i}_b_post_bwd")
        grads["b_out_norm"][j] = dgn[0].reshape(B_HEADS, B_VAL_DIM)
        pf = gla_bwd(sv["proj"], sv["loga"], sv["st_f"], do, aux["tri_f"], aux["tri_b"], False, name=f"l{i}_b_gla_f_bwd")
        pb = gla_bwd(sv["proj"], sv["loga"], sv["st_b"], do, aux["tri_b"], aux["tri_f"], True, name=f"l{i}_b_gla_b_bwd")
        dloga = jnp.concatenate([pf[3], pb[3]], axis=1)
        dz, dwblk, dbias = gate_bwd(sv["proj"], aux["b_wblk"][j], aux["b_wblk_t"][j], aux["b_bias"][j], dloga,
                                    name=f"l{i}_b_gate_bwd")
        grads["b_w_gate_f"][j] = dwblk[:B_GATE_RANK, :B_QK_WIDTH]
        grads["b_w_gate_b"][j] = dwblk[B_GATE_RANK:2 * B_GATE_RANK, B_QK_WIDTH:]
        grads["b_gate_bias_f"][j] = dbias[0, :B_QK_WIDTH]
        grads["b_gate_bias_b"][j] = dbias[0, B_QK_WIDTH:]
        dproj = gla_combine(pf[:3], pb[:3], dr, dz, name=f"l{i}_b_combine")
        grads["b_w_in"][j] = matmul(sv["hn"].T, dproj, tn=640, tk=2048, name=f"l{i}_b_in_dw")[:, :B_IN_WIDTH]
        dhn = matmul(dproj, pt["b_w_in"][j], tk=B_IN_PAD, name=f"l{i}_b_in_dx")
        dh0, dg = rmsnorm_bwd(sv["h0"], p["attn_norm"][i][None], [dhn], dh1, name=f"l{i}_norm1_bwd")
    grads["attn_norm"][i] = dg[0]
    return dh0


def _local_step(x, target, p, small):
    S = x.shape[0]
    cos, sin = _rope_tables(S)
    to_phase = lambda t, d: t.reshape(S // d, d, LANES).swapaxes(0, 1).reshape(S, LANES)
    cos, sin = [to_phase(cos, d) for d in DILS], [to_phase(sin, d) for d in DILS]
    ones_v = jnp.ones((A_WIDTH,), F32)
    a_gain = [[jnp.concatenate([jnp.tile(small["a_q_norm"][j][g], A_HEADS), jnp.tile(small["a_k_norm"][j][g], A_HEADS),
                                ones_v])[None] for g in range(len(DILS))] for j in range(2)]
    b_wblk = [_gate_block_weight(p["b_w_gate_f"][j].astype(F32), p["b_w_gate_b"][j].astype(F32)) for j in range(2)]
    aux = dict(cos=cos, sin=sin, ones=_head_block_ones(), a_gain=a_gain, tri_f=_tri(False), tri_b=_tri(True),
               b_wblk=b_wblk, b_wblk_t=[w.T for w in b_wblk],
               b_bias=[jnp.concatenate([small["b_gate_bias_f"][j], small["b_gate_bias_b"][j]])[None] for j in range(2)],
               b_gain=[small["b_out_norm"][j].reshape(1, B_V_WIDTH) for j in range(2)])
    pw = dict(p)
    pw["b_w_in"] = jnp.pad(p["b_w_in"], ((0, 0), (0, 0), (0, B_IN_PAD - B_IN_WIDTH)))
    pw["attn_norm"], pw["ffn_norm"] = small["attn_norm"], small["ffn_norm"]
    gw = 3 * A_WIDTH
    pw["a_w_in_g"] = [[p["a_w_in"][j][:, g * gw:(g + 1) * gw] for g in range(len(DILS))] for j in range(2)]
    pt = {n: jnp.swapaxes(pw[n], 1, 2) for n in ("a_w_out", "b_w_in", "b_w_out", "ffn_w_gate_up", "ffn_w_down")}
    pt["a_w_in_g"] = [[wg.T for wg in row] for row in pw["a_w_in_g"]]

    h = x
    saved = []
    for i in range(DEPTH):
        h, sv = _layer_fwd(i, h, pw, aux)
        saved.append(sv)
    loss_sq, dh = loss_head(h, target, name="loss_head")
    grads = {n: [None] * (DEPTH if n in ("attn_norm", "ffn_norm", "ffn_w_gate_up", "ffn_w_down") else 2) for n in WEIGHTS}
    for i in reversed(range(DEPTH)):
        dh = _layer_bwd(i, dh, pw, pt, aux, saved[i], grads)
    return loss_sq[0, 0] * (0.5 / D_MODEL), dh, grads


def kernel(x, attn_norm, ffn_norm, a_w_in, a_q_norm, a_k_norm, a_w_out, b_w_in, b_w_gate_f, b_gate_bias_f, b_w_gate_b, b_gate_bias_b, b_out_norm, b_w_out, ffn_w_gate_up, ffn_w_down, loss_target, m_attn_norm, m_ffn_norm, m_a_w_in, m_a_q_norm, m_a_k_norm, m_a_w_out, m_b_w_in, m_b_w_gate_f, m_b_gate_bias_f, m_b_w_gate_b, m_b_gate_bias_b, m_b_out_norm, m_b_w_out, m_ffn_w_gate_up, m_ffn_w_down, v_attn_norm, v_ffn_norm, v_a_w_in, v_a_q_norm, v_a_k_norm, v_a_w_out, v_b_w_in, v_b_w_gate_f, v_b_gate_bias_f, v_b_w_gate_b, v_b_gate_bias_b, v_b_out_norm, v_b_w_out, v_ffn_w_gate_up, v_ffn_w_down):
    w = dict(attn_norm=attn_norm, ffn_norm=ffn_norm, a_w_in=a_w_in, a_q_norm=a_q_norm, a_k_norm=a_k_norm, a_w_out=a_w_out,
             b_w_in=b_w_in, b_w_gate_f=b_w_gate_f, b_gate_bias_f=b_gate_bias_f, b_w_gate_b=b_w_gate_b,
             b_gate_bias_b=b_gate_bias_b, b_out_norm=b_out_norm, b_w_out=b_w_out, ffn_w_gate_up=ffn_w_gate_up,
             ffn_w_down=ffn_w_down)
    m = dict(attn_norm=m_attn_norm, ffn_norm=m_ffn_norm, a_w_in=m_a_w_in, a_q_norm=m_a_q_norm, a_k_norm=m_a_k_norm,
             a_w_out=m_a_w_out, b_w_in=m_b_w_in, b_w_gate_f=m_b_w_gate_f, b_gate_bias_f=m_b_gate_bias_f,
             b_w_gate_b=m_b_w_gate_b, b_gate_bias_b=m_b_gate_bias_b, b_out_norm=m_b_out_norm, b_w_out=m_b_w_out,
             ffn_w_gate_up=m_ffn_w_gate_up, ffn_w_down=m_ffn_w_down)
    v = dict(attn_norm=v_attn_norm, ffn_norm=v_ffn_norm, a_w_in=v_a_w_in, a_q_norm=v_a_q_norm, a_k_norm=v_a_k_norm,
             a_w_out=v_a_w_out, b_w_in=v_b_w_in, b_w_gate_f=v_b_w_gate_f, b_gate_bias_f=v_b_gate_bias_f,
             b_w_gate_b=v_b_w_gate_b, b_gate_bias_b=v_b_gate_bias_b, b_out_norm=v_b_out_norm, b_w_out=v_b_w_out,
             ffn_w_gate_up=v_ffn_w_gate_up, ffn_w_down=v_ffn_w_down)

    full = _gather_weights(w)
    p = {n: full[n] for n in BIG}
    small = {n: full[n] for n in SMALL_SHARDED}
    small.update({n: w[n] for n in REPLICATED})
    loss_local, dx, grads = _local_step(x[0], loss_target[0], p, small)
    loss = lax.psum(loss_local, ("x", "y", "c"))

    g = _reduce_gradients(grads, w)
    delta, new_m, new_v = {}, {}, {}
    rows = lambda t: t.reshape(-1, t.shape[-1])
    for n in MATRICES:
        outs = adamw(rows(w[n]), rows(g[n]), rows(m[n]), rows(v[n]), name=f"adamw_{n}")
        delta[n], new_m[n], new_v[n] = [o.reshape(w[n].shape) for o in outs]
    rest = [n for n in WEIGHTS if n not in MATRICES]
    flat = lambda d: _to_flat([d[n] for n in rest], 8)
    outs = adamw(flat(w), flat(g), flat(m), flat(v), name="adamw_vectors")
    for d, o in zip((delta, new_m, new_v), outs):
        d.update(zip(rest, _from_flat(o, [w[n].shape for n in rest])))
    return (loss, dx[None], *[g[n] for n in WEIGHTS], *[delta[n] for n in WEIGHTS],
            *[new_m[n] for n in WEIGHTS], *[new_v[n] for n in WEIGHTS])
```

```python
import functools

import numpy as np
import jax
import jax.numpy as jnp
from jax import lax
from jax.experimental import pallas as pl
from jax.experimental.pallas import tpu as pltpu

F32, BF16 = jnp.float32, jnp.bfloat16
HI = lax.Precision.HIGHEST
MESH = pl.DeviceIdType.MESH

D_MODEL = 1024
DEPTH = 4
RMS_EPS = 1e-6
NEG_INF = -1e30
A_GROUPS = ((128, 1), (512, 4), (2048, 16))
DILS = tuple(d for _, d in A_GROUPS)
A_HALF = 64
A_HEADS = 16
A_HEAD_DIM = 64
A_WIDTH = 1024
A_IN_WIDTH = 9216
ROPE_THETA = 10000.0
B_HEADS = 4
B_KEY_DIM = 128
B_VAL_DIM = 256
B_QK_WIDTH = 512
B_V_WIDTH = 1024
B_GATE_RANK = 16
B_GATE_TAU = 16.0
B_CHUNK = 64
B_IN_WIDTH = 3104
B_IN_PAD = 3200
FFN_HIDDEN = 2816
ADAM_LR, ADAM_B1, ADAM_B2, ADAM_EPS, ADAM_WD, ADAM_STEP = 0.001, 0.9, 0.999, 1e-08, 0.01, 10
LANES = 128
VMEM_LIMIT = 48 * 1024 * 1024
FLAT_COLS = 1024
N_CHIPS = 4

WEIGHTS = ['attn_norm', 'ffn_norm', 'a_w_in', 'a_q_norm', 'a_k_norm', 'a_w_out', 'b_w_in', 'b_w_gate_f',
           'b_gate_bias_f', 'b_w_gate_b', 'b_gate_bias_b', 'b_out_norm', 'b_w_out', 'ffn_w_gate_up', 'ffn_w_down']
REPLICATED = ('attn_norm', 'ffn_norm', 'a_q_norm', 'a_k_norm')
SHARD_AXIS = {'a_w_in': 2, 'a_w_out': 1, 'b_w_in': 2, 'b_w_gate_f': 2, 'b_gate_bias_f': 1, 'b_w_gate_b': 2,
              'b_gate_bias_b': 1, 'b_out_norm': 2, 'b_w_out': 1, 'ffn_w_gate_up': 2, 'ffn_w_down': 1}
BIG = ('a_w_in', 'a_w_out', 'b_w_in', 'b_w_gate_f', 'b_w_gate_b', 'b_w_out', 'ffn_w_gate_up', 'ffn_w_down')
SMALL_SHARDED = ('b_gate_bias_f', 'b_gate_bias_b', 'b_out_norm')
MATRICES = ('a_w_in', 'a_w_out', 'b_w_in', 'b_w_out', 'ffn_w_gate_up', 'ffn_w_down')


def _params(sem):
    return pltpu.CompilerParams(dimension_semantics=sem, vmem_limit_bytes=VMEM_LIMIT)


def _tile(n, pref):
    t = min(n, pref)
    while n % t:
        t //= 2
    return t


def _const_spec(shape):
    nd = len(shape)
    return pl.BlockSpec(shape, lambda *_: (0,) * nd)


def matmul(a, b, *, name, out_dtype=F32, res=None, tm=1024, tn=512, tk=1024):
    M, K = a.shape
    N = b.shape[1]
    assert b.shape[0] == K
    tm, tn, tk = _tile(M, tm), _tile(N, tn), _tile(K, tk)
    nk = K // tk

    def body(*refs):
        a_ref, b_ref = refs[:2]
        r_ref = refs[2] if res is not None else None
        o_ref = refs[3 if res is not None else 2]
        part = jnp.dot(a_ref[...], b_ref[...], preferred_element_type=F32)

        def finish(v):
            if res is not None:
                v = v + r_ref[...]
            o_ref[...] = v.astype(o_ref.dtype)

        if nk == 1:
            finish(part)
            return
        acc_ref = refs[-1]
        k = pl.program_id(2)

        @pl.when(k == 0)
        def _():
            acc_ref[...] = part

        @pl.when((k > 0) & (k < nk - 1))
        def _():
            acc_ref[...] += part

        @pl.when(k == nk - 1)
        def _():
            finish(acc_ref[...] + part)

    in_specs = [pl.BlockSpec((tm, tk), lambda i, j, k: (i, k)), pl.BlockSpec((tk, tn), lambda i, j, k: (k, j))]
    args = [a, b]
    if res is not None:
        in_specs.append(pl.BlockSpec((tm, tn), lambda i, j, k: (i, j)))
        args.append(res)
    return pl.pallas_call(
        body, name=name, grid=(M // tm, N // tn, nk), in_specs=in_specs,
        out_specs=pl.BlockSpec((tm, tn), lambda i, j, k: (i, j)),
        out_shape=jax.ShapeDtypeStruct((M, N), out_dtype),
        scratch_shapes=[pltpu.VMEM((tm, tn), F32)] if nk > 1 else [],
        compiler_params=_params(("parallel", "parallel", "arbitrary")),
    )(*args)


def _phase_spec(d, ts, W):
    if d == 1:
        return pl.BlockSpec((ts, W), lambda i: (i, 0))
    return pl.BlockSpec((d, ts // d, W), lambda i: (0, i, 0))


def _phase_view(a, d):
    return a if d == 1 else a.reshape(d, a.shape[0] // d, a.shape[1])


def _phase_shape(S, W, d, dtype):
    return jax.ShapeDtypeStruct((S, W) if d == 1 else (d, S // d, W), dtype)


def _nat_scratch(ts, W):
    return pltpu.VMEM((W // LANES, ts, LANES), F32)


def _put_natural(nat_ref, value):
    for c in range(nat_ref.shape[0]):
        nat_ref[c] = value[:, c * LANES:(c + 1) * LANES]


def _get_natural(nat_ref):
    return jnp.concatenate([nat_ref[c] for c in range(nat_ref.shape[0])], axis=1)


def _store_phases(nat_ref, o_ref, d, ts):
    for p in range(d):
        for c in range(nat_ref.shape[0]):
            o_ref[p, :, c * LANES:(c + 1) * LANES] = nat_ref[c, pl.ds(p, ts // d, stride=d), :].astype(o_ref.dtype)


def _load_phases(i_ref, nat_ref, d, ts):
    for p in range(d):
        for c in range(nat_ref.shape[0]):
            nat_ref[c, pl.ds(p, ts // d, stride=d), :] = i_ref[p, :, c * LANES:(c + 1) * LANES].astype(F32)


def rmsnorm_fwd(x, gain, *, name, dils=(1,)):
    S, Dm = x.shape
    ts = _tile(S, 512)

    def body(x_ref, g_ref, *rest):
        o_refs, scr = rest[:len(dils)], rest[len(dils)]
        xv = x_ref[...]
        r = lax.rsqrt(jnp.mean(xv * xv, axis=-1, keepdims=True) + RMS_EPS)
        y = (xv * r) * g_ref[...]
        if any(d > 1 for d in dils):
            _put_natural(scr, y)
        for d, o_ref in zip(dils, o_refs):
            if d == 1:
                o_ref[...] = y.astype(o_ref.dtype)
            else:
                _store_phases(scr, o_ref, d, ts)

    outs = pl.pallas_call(
        body, name=name, grid=(S // ts,),
        in_specs=[pl.BlockSpec((ts, Dm), lambda i: (i, 0)), _const_spec((1, Dm))],
        out_specs=[_phase_spec(d, ts, Dm) for d in dils],
        out_shape=[_phase_shape(S, Dm, d, BF16) for d in dils],
        scratch_shapes=[_nat_scratch(ts, Dm)],
        compiler_params=_params(("parallel",)),
    )(x, gain)
    return [o.reshape(S, Dm) for o in outs]


def rmsnorm_bwd(x, gain, dys, dres, *, name, dils=(1,)):
    S, Dm = x.shape
    ts = _tile(S, 256)
    nd = len(dils)

    def body(x_ref, g_ref, *rest):
        dy_refs, (dr_ref, dx_ref, dg_ref, scr) = rest[:nd], rest[nd:]

        @pl.when(pl.program_id(0) == 0)
        def _():
            dg_ref[...] = jnp.zeros_like(dg_ref)

        dyv = None
        for d, dy_ref in zip(dils, dy_refs):
            if d == 1:
                t = dy_ref[...].astype(F32)
            else:
                _load_phases(dy_ref, scr, d, ts)
                t = _get_natural(scr)
            dyv = t if dyv is None else dyv + t
        xv = x_ref[...]
        r = lax.rsqrt(jnp.mean(xv * xv, axis=-1, keepdims=True) + RMS_EPS)
        xhat = xv * r
        dyg = dyv * g_ref[...]
        dx = r * (dyg - xhat * jnp.mean(dyg * xhat, axis=-1, keepdims=True))
        dx_ref[...] = dr_ref[...] + dx
        dg_ref[0:1, :] += jnp.sum(dyv * xhat, axis=0, keepdims=True)

    row = pl.BlockSpec((ts, Dm), lambda i: (i, 0))
    return pl.pallas_call(
        body, name=name, grid=(S // ts,),
        in_specs=[row, _const_spec((1, Dm))] + [_phase_spec(d, ts, Dm) for d in dils] + [row],
        out_specs=[row, _const_spec((8, Dm))],
        out_shape=[jax.ShapeDtypeStruct((S, Dm), F32), jax.ShapeDtypeStruct((8, Dm), F32)],
        scratch_shapes=[_nat_scratch(ts, Dm)],
        compiler_params=_params(("arbitrary",)),
    )(x, gain, *[_phase_view(dy, d) for dy, d in zip(dys, dils)], dres)


def swiglu_fwd(gu, *, name):
    S, F2 = gu.shape
    Fh = F2 // 2
    ts = _tile(S, 512)

    def body(g_ref, u_ref, o_ref):
        g = g_ref[...].astype(F32)
        u = u_ref[...].astype(F32)
        o_ref[...] = (g * (1.0 / (1.0 + jnp.exp(-g))) * u).astype(o_ref.dtype)

    return pl.pallas_call(
        body, name=name, grid=(S // ts,),
        in_specs=[pl.BlockSpec((ts, Fh), lambda i: (i, 0)), pl.BlockSpec((ts, Fh), lambda i: (i, 1))],
        out_specs=pl.BlockSpec((ts, Fh), lambda i: (i, 0)),
        out_shape=jax.ShapeDtypeStruct((S, Fh), BF16),
        compiler_params=_params(("parallel",)),
    )(gu, gu)


def swiglu_bwd(gu, dact, *, name):
    S, F2 = gu.shape
    Fh = F2 // 2
    ts = _tile(S, 256)

    def body(gu_ref, d_ref, o_ref):
        g = gu_ref[:, :Fh].astype(F32)
        u = gu_ref[:, Fh:].astype(F32)
        d = d_ref[...].astype(F32)
        sig = 1.0 / (1.0 + jnp.exp(-g))
        o_ref[:, :Fh] = (d * u * (sig * (1.0 + g * (1.0 - sig)))).astype(o_ref.dtype)
        o_ref[:, Fh:] = (d * (g * sig)).astype(o_ref.dtype)

    return pl.pallas_call(
        body, name=name, grid=(S // ts,),
        in_specs=[pl.BlockSpec((ts, F2), lambda i: (i, 0)), pl.BlockSpec((ts, Fh), lambda i: (i, 0))],
        out_specs=pl.BlockSpec((ts, F2), lambda i: (i, 0)),
        out_shape=jax.ShapeDtypeStruct((S, F2), BF16),
        compiler_params=_params(("parallel",)),
    )(gu, dact)


def loss_head(y, target, *, name):
    S, Dm = y.shape
    ts = _tile(S, 512)

    def body(y_ref, t_ref, l_ref, d_ref):
        @pl.when(pl.program_id(0) == 0)
        def _():
            l_ref[...] = jnp.zeros_like(l_ref)

        e = y_ref[...] - t_ref[...]
        d_ref[...] = e * (1.0 / Dm)
        l_ref[...] += jnp.sum(e * e)

    return pl.pallas_call(
        body, name=name, grid=(S // ts,),
        in_specs=[pl.BlockSpec((ts, Dm), lambda i: (i, 0)), pl.BlockSpec((ts, Dm), lambda i: (i, 0))],
        out_specs=[_const_spec((8, LANES)), pl.BlockSpec((ts, Dm), lambda i: (i, 0))],
        out_shape=[jax.ShapeDtypeStruct((8, LANES), F32), jax.ShapeDtypeStruct((S, Dm), F32)],
        compiler_params=_params(("arbitrary",)),
    )(y, target)


def _head_block_ones():
    i = np.arange(LANES)
    return jnp.asarray((i[:, None] // A_HEAD_DIM == i[None, :] // A_HEAD_DIM).astype(np.float32)).astype(BF16)


def _rope_tables(S):
    half = A_HEAD_DIM // 2
    inv_freq = ROPE_THETA ** (-jnp.arange(half, dtype=F32) / half)
    ang = jnp.arange(S).astype(F32)[:, None] * inv_freq[None, :]
    cos = jnp.tile(jnp.cos(ang), (1, LANES // half))
    sin = jnp.tile(jnp.sin(ang), (1, LANES // half))
    return cos, sin


def _rot_half(x, lo):
    return jnp.where(lo, -pltpu.roll(x, LANES - 32, 1), pltpu.roll(x, 32, 1))


def _seg_sum(v, ones_ref):
    hi = v.astype(BF16)
    lo = (v - hi.astype(F32)).astype(BF16)
    ones = ones_ref[...]
    return jnp.dot(hi, ones, preferred_element_type=F32) + jnp.dot(lo, ones, preferred_element_type=F32)


def _seg_mean(v, ones_ref):
    return _seg_sum(v, ones_ref) * (1.0 / A_HEAD_DIM)


def qk_prep_fwd(qkv, gain, cos, sin, ones, *, name):
    S, W = qkv.shape
    ts = _tile(S, 256)
    nchunk = A_WIDTH // LANES

    def body(x_ref, g_ref, c_ref, s_ref, ones_ref, o_ref):
        kind = pl.program_id(1) % 3

        @pl.when(kind < 2)
        def _():
            scale = jnp.where(kind == 0, A_HEAD_DIM ** -0.5, 1.0).astype(F32)
            lo = (lax.broadcasted_iota(jnp.int32, (ts, LANES), 1) % A_HEAD_DIM) < (A_HEAD_DIM // 2)
            cv, sv = c_ref[...], s_ref[...]
            for c in range(nchunk):
                sl = slice(c * LANES, (c + 1) * LANES)
                xv = x_ref[:, sl]
                r = lax.rsqrt(_seg_mean(xv * xv, ones_ref) + RMS_EPS)
                y = (xv * r) * g_ref[:, sl]
                y = y * cv + _rot_half(y, lo) * sv
                o_ref[:, sl] = (y * scale).astype(o_ref.dtype)

        @pl.when(kind == 2)
        def _():
            o_ref[...] = x_ref[...].astype(o_ref.dtype)

    return pl.pallas_call(
        body, name=name, grid=(S // ts, W // A_WIDTH),
        in_specs=[pl.BlockSpec((ts, A_WIDTH), lambda i, j: (i, j)), pl.BlockSpec((1, A_WIDTH), lambda i, j: (0, j)),
                  pl.BlockSpec((ts, LANES), lambda i, j: (i, 0)), pl.BlockSpec((ts, LANES), lambda i, j: (i, 0)),
                  _const_spec((LANES, LANES))],
        out_specs=pl.BlockSpec((ts, A_WIDTH), lambda i, j: (i, j)),
        out_shape=jax.ShapeDtypeStruct((S, W), BF16),
        compiler_params=_params(("parallel", "arbitrary")),
    )(qkv, gain, cos, sin, ones)


def qk_prep_bwd(qkv, gain, cos, sin, ones, grads, *, name):
    S, W = qkv.shape
    ts = _tile(S, 256)
    nchunk = A_WIDTH // LANES
    nj = W // A_WIDTH

    def body(x_ref, g_ref, c_ref, s_ref, ones_ref, *rest):
        g_refs, (o_ref, dg_ref) = rest[:nj], rest[nj:]
        j = pl.program_id(0)
        kind = j % 3

        @pl.when(pl.program_id(1) == 0)
        def _():
            dg_ref[...] = jnp.zeros_like(dg_ref)

        for n in range(nj):
            @pl.when(j == n)
            def _(n=n):
                d_ref = g_refs[n]
                if n % 3 == 2:
                    o_ref[...] = d_ref[...].astype(o_ref.dtype)
                    return
                scale = A_HEAD_DIM ** -0.5 if n % 3 == 0 else 1.0
                lo = (lax.broadcasted_iota(jnp.int32, (ts, LANES), 1) % A_HEAD_DIM) < (A_HEAD_DIM // 2)
                cv, sv = c_ref[...], s_ref[...]
                for c in range(nchunk):
                    sl = slice(c * LANES, (c + 1) * LANES)
                    dy = d_ref[:, sl].astype(F32) * scale
                    dn = dy * cv - _rot_half(dy, lo) * sv
                    xv = x_ref[:, sl]
                    r = lax.rsqrt(_seg_mean(xv * xv, ones_ref) + RMS_EPS)
                    xhat = xv * r
                    dyg = dn * g_ref[:, sl]
                    dx = r * (dyg - xhat * _seg_mean(dyg * xhat, ones_ref))
                    o_ref[:, sl] = dx.astype(o_ref.dtype)
                    dg_ref[0:1, sl] += jnp.sum(dn * xhat, axis=0, keepdims=True)

    def gspec(n):
        return pl.BlockSpec((ts, A_WIDTH), lambda j, i: (jnp.where(j == n, i, 0), 0))

    return pl.pallas_call(
        body, name=name, grid=(nj, S // ts),
        in_specs=[pl.BlockSpec((ts, A_WIDTH), lambda j, i: (i, j)), pl.BlockSpec((1, A_WIDTH), lambda j, i: (0, j)),
                  pl.BlockSpec((ts, LANES), lambda j, i: (i, 0)), pl.BlockSpec((ts, LANES), lambda j, i: (i, 0)),
                  _const_spec((LANES, LANES))] + [gspec(n) for n in range(nj)],
        out_specs=[pl.BlockSpec((ts, A_WIDTH), lambda j, i: (i, j)), pl.BlockSpec((8, A_WIDTH), lambda j, i: (0, j))],
        out_shape=[jax.ShapeDtypeStruct((S, W), BF16), jax.ShapeDtypeStruct((8, W), F32)],
        compiler_params=_params(("arbitrary", "arbitrary")),
    )(qkv, gain, cos, sin, ones, *grads)


def _band_specs(kind, tq, nlb):
    nhb = tq // A_HALF
    nb = nlb // nhb
    base = kind * (A_WIDTH // LANES)
    return [pl.BlockSpec((A_HALF, LANES), lambda ph, b, hp: (ph * nlb + jnp.maximum(b * nhb - 1, 0), base + hp)),
            pl.BlockSpec((tq, LANES), lambda ph, b, hp: (ph * nb + b, base + hp)),
            pl.BlockSpec((A_HALF, LANES), lambda ph, b, hp: (ph * nlb + jnp.minimum((b + 1) * nhb, nlb - 1), base + hp))]


A_BLOCK = 512
A_SUB = 128


def _band_bias(sub, key_major):
    i = np.arange(sub)[:, None]
    j = np.arange(sub + 2 * A_HALF)[None, :] - A_HALF
    ok = np.abs(j - i) <= A_HALF
    return jnp.asarray(np.where(ok.T if key_major else ok, 0.0, NEG_INF).astype(np.float32))


def _edge_bias(first, n, L, axis, at_start, at_end):
    if not (at_start or at_end):
        return None
    shape = (1, n) if axis == 1 else (n, 1)
    pos = first - A_HALF + lax.broadcasted_iota(jnp.int32, shape, axis)
    return jnp.where((pos < 0) | (pos >= L), NEG_INF, 0.0).astype(F32)


def _with_edge(bias, edge):
    return bias if edge is None else bias + edge


def _cat3(a_ref, b_ref, c_ref):
    return jnp.concatenate([a_ref[...], b_ref[...], c_ref[...]], axis=0)


def _lane_lo(rows):
    return lax.broadcasted_iota(jnp.int32, (rows, LANES), 1) < A_HEAD_DIM


NT = (((1,), (1,)), ((), ()))
TN = (((0,), (0,)), ((), ()))


def attn_fwd(qkvp, dil, *, name):
    S = qkvp.shape[0]
    L = S // dil
    tq = _tile(L, A_BLOCK)
    sub = min(tq, A_SUB)
    nsub = tq // sub
    nlb = L // A_HALF
    band = _band_bias(sub, False)

    def body(q_ref, kp_ref, ko_ref, kn_ref, vp_ref, vo_ref, vn_ref, band_ref, o_ref, l_ref):
        b = pl.program_id(1)
        K = _cat3(kp_ref, ko_ref, kn_ref)
        V = _cat3(vp_ref, vo_ref, vn_ref)
        lo_k = _lane_lo(tq + 2 * A_HALF)
        lo_q = _lane_lo(sub)
        Km = [jnp.where(sel, K, jnp.zeros_like(K)) for sel in (lo_k, ~lo_k)]
        Vm = [jnp.where(sel, V, jnp.zeros_like(V)) for sel in (lo_k, ~lo_k)]
        for r in range(nsub):
            rows = slice(r * sub, (r + 1) * sub)
            keys = slice(r * sub, (r + 1) * sub + 2 * A_HALF)
            bias = _with_edge(band_ref[...], _edge_bias(b * tq + r * sub, sub + 2 * A_HALF, L, 1, r == 0, r == nsub - 1))
            q = q_ref[rows, :]
            outs, lses = [], []
            for hh in range(2):
                s = lax.dot_general(q, Km[hh][keys], NT, preferred_element_type=F32) + bias
                m = jnp.max(s, axis=1, keepdims=True)
                p = jnp.exp(s - m)
                l = jnp.sum(p, axis=1, keepdims=True)
                outs.append(jnp.dot(p.astype(BF16), Vm[hh][keys], preferred_element_type=F32) * (1.0 / l))
                lses.append(m + jnp.log(l))
            o_ref[rows, :] = outs[0] + outs[1]
            l_ref[rows, :] = jnp.where(lo_q, lses[0], lses[1])

    ospec = _band_specs(0, tq, nlb)[1]
    return pl.pallas_call(
        body, name=name, grid=(dil, L // tq, A_WIDTH // LANES),
        in_specs=[_band_specs(0, tq, nlb)[1]] + _band_specs(1, tq, nlb) + _band_specs(2, tq, nlb)
        + [_const_spec(band.shape)],
        out_specs=[ospec, ospec],
        out_shape=[jax.ShapeDtypeStruct((S, A_WIDTH), F32)] * 2,
        compiler_params=_params(("parallel", "parallel", "parallel")),
    )(*([qkvp] * 7), band)


def attn_merge(os_, lses, dils, *, name):
    S = os_[0].shape[0]
    ts = _tile(S, 256)
    ng = len(dils)

    def body(*refs):
        o_refs, l_refs, out_ref = refs[:ng], refs[ng:2 * ng], refs[2 * ng]
        lse_refs, scrs = refs[2 * ng + 1:3 * ng + 1], refs[3 * ng + 1:]
        ov, ls, k = [], [], 0
        for d, o_ref, l_ref in zip(dils, o_refs, l_refs):
            if d == 1:
                ov.append(o_ref[...])
                ls.append(l_ref[...])
            else:
                _load_phases(o_ref, scrs[k], d, ts)
                _load_phases(l_ref, scrs[k + 1], d, ts)
                ov.append(_get_natural(scrs[k]))
                ls.append(_get_natural(scrs[k + 1]))
                k += 2
        m = functools.reduce(jnp.maximum, ls)
        es = [jnp.exp(l - m) for l in ls]
        tot = functools.reduce(jnp.add, es)
        acc = None
        for e, o in zip(es, ov):
            t = (e / tot) * o
            acc = t if acc is None else acc + t
        out_ref[...] = acc.astype(out_ref.dtype)
        total = m + jnp.log(tot)
        _put_natural(scrs[k], total)
        for d, lse_ref in zip(dils, lse_refs):
            if d == 1:
                lse_ref[...] = total
            else:
                _store_phases(scrs[k], lse_ref, d, ts)

    n_scr = 2 * sum(d > 1 for d in dils) + 1
    outs = pl.pallas_call(
        body, name=name, grid=(S // ts,),
        in_specs=[_phase_spec(d, ts, A_WIDTH) for d in dils] * 2,
        out_specs=[pl.BlockSpec((ts, A_WIDTH), lambda i: (i, 0))] + [_phase_spec(d, ts, A_WIDTH) for d in dils],
        out_shape=[jax.ShapeDtypeStruct((S, A_WIDTH), BF16)] + [_phase_shape(S, A_WIDTH, d, F32) for d in dils],
        scratch_shapes=[_nat_scratch(ts, A_WIDTH)] * n_scr,
        compiler_params=_params(("parallel",)),
    )(*[_phase_view(o, d) for o, d in zip(os_, dils)], *[_phase_view(l, d) for l, d in zip(lses, dils)])
    return outs[0], [l.reshape(S, A_WIDTH) for l in outs[1:]]


def attn_delta(dout, out, ones, dils, *, name):
    S = dout.shape[0]
    ts = _tile(S, 256)
    nd = len(dils)

    def body(d_ref, o_ref, ones_ref, *rest):
        do_refs, dl_refs, (scr_do, scr_dl) = rest[:nd], rest[nd:2 * nd], rest[2 * nd:]
        sums = []
        for c in range(A_WIDTH // LANES):
            sl = slice(c * LANES, (c + 1) * LANES)
            prod = d_ref[:, sl].astype(F32) * o_ref[:, sl].astype(F32)
            sums.append(_seg_sum(prod, ones_ref))
        _put_natural(scr_do, d_ref[...].astype(F32))
        _put_natural(scr_dl, jnp.concatenate(sums, axis=1))
        for d, do_ref, dl_ref in zip(dils, do_refs, dl_refs):
            if d == 1:
                do_ref[...] = d_ref[...]
                dl_ref[...] = _get_natural(scr_dl)
            else:
                _store_phases(scr_do, do_ref, d, ts)
                _store_phases(scr_dl, dl_ref, d, ts)

    spec = pl.BlockSpec((ts, A_WIDTH), lambda i: (i, 0))
    outs = pl.pallas_call(
        body, name=name, grid=(S // ts,), in_specs=[spec, spec, _const_spec((LANES, LANES))],
        out_specs=[_phase_spec(d, ts, A_WIDTH) for d in dils] * 2,
        out_shape=[_phase_shape(S, A_WIDTH, d, BF16) for d in dils] + [_phase_shape(S, A_WIDTH, d, F32) for d in dils],
        scratch_shapes=[_nat_scratch(ts, A_WIDTH)] * 2,
        compiler_params=_params(("parallel",)),
    )(dout, out, ones)
    outs = [o.reshape(S, A_WIDTH) for o in outs]
    return outs[:nd], outs[nd:]


def _head_col(x, hh):
    c = hh * A_HEAD_DIM
    return x[:, c:c + 1]


def attn_bwd_dq(qkvp, dout, lse, delta, dil, *, name):
    S = qkvp.shape[0]
    L = S // dil
    tq = _tile(L, A_BLOCK)
    sub = min(tq, A_SUB)
    nsub = tq // sub
    nlb = L // A_HALF
    band = _band_bias(sub, False)

    def body(q_ref, kp_ref, ko_ref, kn_ref, vp_ref, vo_ref, vn_ref, do_ref, l_ref, d_ref, band_ref, dq_ref):
        b = pl.program_id(1)
        K = _cat3(kp_ref, ko_ref, kn_ref)
        V = _cat3(vp_ref, vo_ref, vn_ref)
        lo_k = _lane_lo(tq + 2 * A_HALF)
        Km = [jnp.where(sel, K, jnp.zeros_like(K)) for sel in (lo_k, ~lo_k)]
        Vm = [jnp.where(sel, V, jnp.zeros_like(V)) for sel in (lo_k, ~lo_k)]
        for r in range(nsub):
            rows = slice(r * sub, (r + 1) * sub)
            keys = slice(r * sub, (r + 1) * sub + 2 * A_HALF)
            bias = _with_edge(band_ref[...], _edge_bias(b * tq + r * sub, sub + 2 * A_HALF, L, 1, r == 0, r == nsub - 1))
            q, do = q_ref[rows, :], do_ref[rows, :]
            lse_v, dl_v = l_ref[rows, :], d_ref[rows, :]
            acc = None
            for hh in range(2):
                s = lax.dot_general(q, Km[hh][keys], NT, preferred_element_type=F32) + bias
                p = jnp.exp(s - _head_col(lse_v, hh))
                dp = lax.dot_general(do, Vm[hh][keys], NT, preferred_element_type=F32)
                ds = p * (dp - _head_col(dl_v, hh))
                t = jnp.dot(ds.astype(BF16), Km[hh][keys], preferred_element_type=F32)
                acc = t if acc is None else acc + t
            dq_ref[rows, :] = acc.astype(dq_ref.dtype)

    nspec = _band_specs(0, tq, nlb)[1]
    return pl.pallas_call(
        body, name=name, grid=(dil, L // tq, A_WIDTH // LANES),
        in_specs=[nspec] + _band_specs(1, tq, nlb) + _band_specs(2, tq, nlb) + [nspec, nspec, nspec, _const_spec(band.shape)],
        out_specs=nspec,
        out_shape=jax.ShapeDtypeStruct((S, A_WIDTH), BF16),
        compiler_params=_params(("parallel", "parallel", "parallel")),
    )(*([qkvp] * 7), dout, lse, delta, band)


def attn_bwd_dkv(qkvp, dout, lse, delta, dil, *, name):
    S = qkvp.shape[0]
    L = S // dil
    tk = _tile(L, A_BLOCK)
    sub = min(tk, A_SUB)
    nsub = tk // sub
    nlb = L // A_HALF
    band = _band_bias(sub, False)

    def body(qp_ref, qo_ref, qn_ref, k_ref, v_ref, dp_ref, do_ref, dn_ref, lp_ref, lo_ref, ln_ref,
             ep_ref, eo_ref, en_ref, band_ref, dk_ref, dv_ref):
        b = pl.program_id(1)
        Q = _cat3(qp_ref, qo_ref, qn_ref)
        DO = _cat3(dp_ref, do_ref, dn_ref)
        lse_v = _cat3(lp_ref, lo_ref, ln_ref)
        dl_v = _cat3(ep_ref, eo_ref, en_ref)
        lo_q = _lane_lo(tk + 2 * A_HALF)
        Qm = [jnp.where(sel, Q, jnp.zeros_like(Q)) for sel in (lo_q, ~lo_q)]
        DOm = [jnp.where(sel, DO, jnp.zeros_like(DO)) for sel in (lo_q, ~lo_q)]
        for r in range(nsub):
            keys = slice(r * sub, (r + 1) * sub)
            qs = slice(r * sub, (r + 1) * sub + 2 * A_HALF)
            bias = _with_edge(band_ref[...], _edge_bias(b * tk + r * sub, sub + 2 * A_HALF, L, 1, r == 0, r == nsub - 1))
            K, V = k_ref[keys, :], v_ref[keys, :]
            lse_t, dl_t = lse_v[qs].T, dl_v[qs].T
            dk = dv = None
            for hh in range(2):
                hr = slice(hh * A_HEAD_DIM, hh * A_HEAD_DIM + 1)
                st = lax.dot_general(K, Qm[hh][qs], NT, preferred_element_type=F32) + bias
                pt = jnp.exp(st - lse_t[hr, :])
                dpt = lax.dot_general(V, DOm[hh][qs], NT, preferred_element_type=F32)
                dst = pt * (dpt - dl_t[hr, :])
                tv = jnp.dot(pt.astype(BF16), DOm[hh][qs], preferred_element_type=F32)
                tk_ = jnp.dot(dst.astype(BF16), Qm[hh][qs], preferred_element_type=F32)
                dv = tv if dv is None else dv + tv
                dk = tk_ if dk is None else dk + tk_
            dk_ref[keys, :] = dk.astype(dk_ref.dtype)
            dv_ref[keys, :] = dv.astype(dv_ref.dtype)

    nspec = _band_specs(0, tk, nlb)[1]
    return pl.pallas_call(
        body, name=name, grid=(dil, L // tk, A_WIDTH // LANES),
        in_specs=_band_specs(0, tk, nlb) + [_band_specs(1, tk, nlb)[1], _band_specs(2, tk, nlb)[1]]
        + _band_specs(0, tk, nlb) * 3 + [_const_spec(band.shape)],
        out_specs=[nspec, nspec],
        out_shape=[jax.ShapeDtypeStruct((S, A_WIDTH), BF16)] * 2,
        compiler_params=_params(("parallel", "parallel", "parallel")),
    )(*([qkvp] * 5), *([dout] * 3), *([lse] * 3), *([delta] * 3), band)


def _gate_block_weight(wf, wb):
    w = jnp.zeros((LANES, 2 * B_QK_WIDTH), F32)
    w = w.at[:B_GATE_RANK, :B_QK_WIDTH].set(wf)
    w = w.at[B_GATE_RANK:2 * B_GATE_RANK, B_QK_WIDTH:].set(wb)
    return w.astype(BF16)


def gate_fwd(proj, wblk, bias, *, name):
    S = proj.shape[0]
    ts = _tile(S, 512)
    W = 2 * B_QK_WIDTH
    zcol = (2 * B_QK_WIDTH + 2 * B_V_WIDTH) // LANES

    def body(z_ref, w_ref, b_ref, o_ref):
        x = jnp.dot(z_ref[...].astype(BF16), w_ref[...], preferred_element_type=F32) + b_ref[...]
        o_ref[...] = (jnp.minimum(x, 0.0) - jnp.log(1.0 + jnp.exp(-jnp.abs(x)))) * (1.0 / B_GATE_TAU)

    return pl.pallas_call(
        body, name=name, grid=(S // ts,),
        in_specs=[pl.BlockSpec((ts, LANES), lambda i: (i, zcol)), _const_spec((LANES, W)), _const_spec((1, W))],
        out_specs=pl.BlockSpec((ts, W), lambda i: (i, 0)),
        out_shape=jax.ShapeDtypeStruct((S, W), F32),
        compiler_params=_params(("parallel",)),
    )(proj, wblk, bias)


def gate_bwd(proj, wblk, wblk_t, bias, dloga, *, name):
    S = proj.shape[0]
    ts = _tile(S, 512)
    W = 2 * B_QK_WIDTH
    zcol = (2 * B_QK_WIDTH + 2 * B_V_WIDTH) // LANES

    def body(z_ref, w_ref, wt_ref, b_ref, d_ref, dz_ref, dw_ref, db_ref):
        @pl.when(pl.program_id(0) == 0)
        def _():
            dw_ref[...] = jnp.zeros_like(dw_ref)
            db_ref[...] = jnp.zeros_like(db_ref)

        z = z_ref[...].astype(BF16)
        x = jnp.dot(z, w_ref[...], preferred_element_type=F32) + b_ref[...]
        e = jnp.exp(-jnp.abs(x))
        sig_neg = jnp.where(x >= 0, e, 1.0) / (1.0 + e)
        dx = d_ref[...] * (1.0 / B_GATE_TAU) * sig_neg
        dxb = dx.astype(BF16)
        dz_ref[...] = jnp.dot(dxb, wt_ref[...], preferred_element_type=F32)
        dw_ref[...] += lax.dot_general(z, dxb, TN, preferred_element_type=F32)
        db_ref[0:1, :] += jnp.sum(dx, axis=0, keepdims=True)

    return pl.pallas_call(
        body, name=name, grid=(S // ts,),
        in_specs=[pl.BlockSpec((ts, LANES), lambda i: (i, zcol)), _const_spec((LANES, W)), _const_spec((W, LANES)),
                  _const_spec((1, W)), pl.BlockSpec((ts, W), lambda i: (i, 0))],
        out_specs=[pl.BlockSpec((ts, LANES), lambda i: (i, 0)), _const_spec((LANES, W)), _const_spec((8, W))],
        out_shape=[jax.ShapeDtypeStruct((S, LANES), F32), jax.ShapeDtypeStruct((LANES, W), F32),
                   jax.ShapeDtypeStruct((8, W), F32)],
        compiler_params=_params(("arbitrary",)),
    )(proj, wblk, wblk_t, bias, dloga)


def _tri(reverse):
    i = np.arange(B_CHUNK)
    t = (i[None, :] >= i[:, None]) if reverse else (i[None, :] <= i[:, None])
    return jnp.asarray(t.astype(np.float32))


def _gla_terms(q, k, la, t_ref, reverse):
    b = jnp.dot(t_ref[...], la, precision=HI, preferred_element_type=F32)
    b_last = b[0:1, :] if reverse else b[B_CHUNK - 1:B_CHUNK, :]
    e_b = jnp.exp(b)
    qt = (q * (B_KEY_DIM ** -0.5)) * e_b
    e_nb = jnp.exp(-b)
    kt = k * e_nb
    e_end = jnp.exp(b_last - b)
    kend = k * e_end
    dec = jnp.exp(b_last)
    return e_nb, e_b, qt, kt, e_end, kend, dec


def _chunk_mask(reverse, transpose=False):
    r = lax.broadcasted_iota(jnp.int32, (B_CHUNK, B_CHUNK), 0)
    c = lax.broadcasted_iota(jnp.int32, (B_CHUNK, B_CHUNK), 1)
    if transpose:
        r, c = c, r
    return (c > r) if reverse else (c <= r)


def gla_fwd(proj, loga, tmat, reverse, *, name):
    S = proj.shape[0]
    tb = _tile(S, 512)
    nb = S // tb
    cpb = tb // B_CHUNK
    nc = S // B_CHUNK
    lb = 1 if reverse else 0
    blk = (lambda i: nb - 1 - i) if reverse else (lambda i: i)

    def body(qk_ref, v_ref, la_ref, t_ref, o_ref, st_ref, s_scr):
        @pl.when(pl.program_id(0) == 0)
        def _():
            s_scr[...] = jnp.zeros_like(s_scr)

        mask = _chunk_mask(reverse)
        order = range(cpb - 1, -1, -1) if reverse else range(cpb)
        for c in order:
            rows = slice(c * B_CHUNK, (c + 1) * B_CHUNK)
            _, _, qt, kt, _, kend, dec = _gla_terms(qk_ref[rows, :B_QK_WIDTH], qk_ref[rows, B_QK_WIDTH:],
                                                   la_ref[rows, :], t_ref, reverse)
            qt, kt, kend = qt.astype(BF16), kt.astype(BF16), kend.astype(BF16)
            for h in range(B_HEADS):
                kc = slice(h * B_KEY_DIM, (h + 1) * B_KEY_DIM)
                vc = slice(h * B_VAL_DIM, (h + 1) * B_VAL_DIM)
                v = v_ref[rows, vc].astype(BF16)
                st = s_scr[h]
                st_ref[h, c] = st
                a = jnp.where(mask, lax.dot_general(qt[:, kc], kt[:, kc], NT, preferred_element_type=F32), 0.0)
                o = jnp.dot(a.astype(BF16), v, preferred_element_type=F32)
                o = o + lax.dot_general(qt[:, kc], st.astype(BF16), NT, preferred_element_type=F32)
                o_ref[rows, vc] = o
                s_scr[h] = st * dec[:, kc] + lax.dot_general(v, kend[:, kc], TN, preferred_element_type=F32)

    return pl.pallas_call(
        body, name=name, grid=(nb,),
        in_specs=[pl.BlockSpec((tb, 2 * B_QK_WIDTH), lambda i: (blk(i), 0)),
                  pl.BlockSpec((tb, B_V_WIDTH), lambda i: (blk(i), 1)),
                  pl.BlockSpec((tb, B_QK_WIDTH), lambda i: (blk(i), lb)),
                  _const_spec((B_CHUNK, B_CHUNK))],
        out_specs=[pl.BlockSpec((tb, B_V_WIDTH), lambda i: (blk(i), 0)),
                   pl.BlockSpec((B_HEADS, cpb, B_VAL_DIM, B_KEY_DIM), lambda i: (0, blk(i), 0, 0))],
        out_shape=[jax.ShapeDtypeStruct((S, B_V_WIDTH), F32),
                   jax.ShapeDtypeStruct((B_HEADS, nc, B_VAL_DIM, B_KEY_DIM), F32)],
        scratch_shapes=[pltpu.VMEM((B_HEADS, B_VAL_DIM, B_KEY_DIM), F32)],
        compiler_params=_params(("arbitrary",)),
    )(proj, proj, loga, tmat)


def gla_bwd(proj, loga, states, do, tmat, tmat_t, reverse, *, name):
    S = proj.shape[0]
    tb = _tile(S, 256)
    nb = S // tb
    cpb = tb // B_CHUNK
    lb = 1 if reverse else 0
    blk = (lambda i: i) if reverse else (lambda i: nb - 1 - i)
    scale = B_KEY_DIM ** -0.5

    def body(qk_ref, v_ref, la_ref, st_ref, do_ref, t_ref, tt_ref, dq_ref, dk_ref, dv_ref, dla_ref, ds_scr):
        @pl.when(pl.program_id(0) == 0)
        def _():
            ds_scr[...] = jnp.zeros_like(ds_scr)

        mask = _chunk_mask(reverse)
        mask_t = _chunk_mask(reverse, transpose=True)
        last = 0 if reverse else B_CHUNK - 1
        is_last = lax.broadcasted_iota(jnp.int32, (B_CHUNK, B_QK_WIDTH), 0) == last
        order = range(cpb) if reverse else range(cpb - 1, -1, -1)
        for c in order:
            rows = slice(c * B_CHUNK, (c + 1) * B_CHUNK)
            e_nb, e_b, qt, kt, e_end, kend, dec = _gla_terms(qk_ref[rows, :B_QK_WIDTH], qk_ref[rows, B_QK_WIDTH:],
                                                             la_ref[rows, :], t_ref, reverse)
            qtb, ktb, kendb = qt.astype(BF16), kt.astype(BF16), kend.astype(BF16)
            dqt_h, dkt_h, dkend_h, ddec_h = [], [], [], []
            for h in range(B_HEADS):
                kc = slice(h * B_KEY_DIM, (h + 1) * B_KEY_DIM)
                vc = slice(h * B_VAL_DIM, (h + 1) * B_VAL_DIM)
                v = v_ref[rows, vc].astype(BF16)
                dob = do_ref[rows, vc].astype(BF16)
                st = st_ref[h, c]
                dst = ds_scr[h]
                dstb = dst.astype(BF16)
                a_t = jnp.where(mask_t, lax.dot_general(ktb[:, kc], qtb[:, kc], NT, preferred_element_type=F32), 0.0)
                da = jnp.where(mask, lax.dot_general(dob, v, NT, preferred_element_type=F32), 0.0)
                da_t = jnp.where(mask_t, lax.dot_general(v, dob, NT, preferred_element_type=F32), 0.0)
                dv = jnp.dot(a_t.astype(BF16), dob, preferred_element_type=F32)
                dv_ref[rows, vc] = dv + lax.dot_general(kendb[:, kc], dstb, NT, preferred_element_type=F32)
                dqt = jnp.dot(da.astype(BF16), ktb[:, kc], preferred_element_type=F32)
                dqt_h.append(dqt + jnp.dot(dob, st.astype(BF16), preferred_element_type=F32))
                dkt_h.append(jnp.dot(da_t.astype(BF16), qtb[:, kc], preferred_element_type=F32))
                dkend_h.append(jnp.dot(v, dstb, preferred_element_type=F32))
                ddec_h.append(jnp.sum(dst * st, axis=0, keepdims=True))
                ds_scr[h] = dst * dec[:, kc] + lax.dot_general(dob, qtb[:, kc], TN, preferred_element_type=F32)
            dqt, dkt = jnp.concatenate(dqt_h, axis=1), jnp.concatenate(dkt_h, axis=1)
            dkend, ddec = jnp.concatenate(dkend_h, axis=1), jnp.concatenate(ddec_h, axis=1)
            ke = dkend * kend
            db = dqt * qt - dkt * kt - ke
            db_last = jnp.sum(ke, axis=0, keepdims=True) + ddec * dec
            db = db + jnp.where(is_last, db_last, 0.0)
            dq_ref[rows, :] = dqt * e_b * scale
            dk_ref[rows, :] = dkt * e_nb + dkend * e_end
            dla_ref[rows, :] = jnp.dot(tt_ref[...], db, precision=HI, preferred_element_type=F32)

    qspec = pl.BlockSpec((tb, B_QK_WIDTH), lambda i: (blk(i), 0))
    vspec = pl.BlockSpec((tb, B_V_WIDTH), lambda i: (blk(i), 0))
    return pl.pallas_call(
        body, name=name, grid=(nb,),
        in_specs=[pl.BlockSpec((tb, 2 * B_QK_WIDTH), lambda i: (blk(i), 0)),
                  pl.BlockSpec((tb, B_V_WIDTH), lambda i: (blk(i), 1)),
                  pl.BlockSpec((tb, B_QK_WIDTH), lambda i: (blk(i), lb)),
                  pl.BlockSpec((B_HEADS, cpb, B_VAL_DIM, B_KEY_DIM), lambda i: (0, blk(i), 0, 0)),
                  vspec, _const_spec((B_CHUNK, B_CHUNK)), _const_spec((B_CHUNK, B_CHUNK))],
        out_specs=[qspec, qspec, vspec, qspec],
        out_shape=[jax.ShapeDtypeStruct((S, B_QK_WIDTH), F32), jax.ShapeDtypeStruct((S, B_QK_WIDTH), F32),
                   jax.ShapeDtypeStruct((S, B_V_WIDTH), F32), jax.ShapeDtypeStruct((S, B_QK_WIDTH), F32)],
        scratch_shapes=[pltpu.VMEM((B_HEADS, B_VAL_DIM, B_KEY_DIM), F32)],
        compiler_params=_params(("arbitrary",)),
    )(proj, proj, loga, states, do, tmat, tmat_t)


def gla_post_fwd(o_f, o_b, gain, proj, *, name):
    S = o_f.shape[0]
    ts = _tile(S, 512)
    rcol = (2 * B_QK_WIDTH + B_V_WIDTH) // B_V_WIDTH

    def body(f_ref, b_ref, g_ref, r_ref, y_ref):
        for h in range(B_HEADS):
            sl = slice(h * B_VAL_DIM, (h + 1) * B_VAL_DIM)
            o = f_ref[:, sl] + b_ref[:, sl]
            n = (o * lax.rsqrt(jnp.mean(o * o, axis=-1, keepdims=True) + RMS_EPS)) * g_ref[:, sl]
            r = r_ref[:, sl]
            y_ref[:, sl] = (n * (r * (1.0 / (1.0 + jnp.exp(-r))))).astype(y_ref.dtype)

    spec = pl.BlockSpec((ts, B_V_WIDTH), lambda i: (i, 0))
    return pl.pallas_call(
        body, name=name, grid=(S // ts,),
        in_specs=[spec, spec, _const_spec((1, B_V_WIDTH)), pl.BlockSpec((ts, B_V_WIDTH), lambda i: (i, rcol))],
        out_specs=spec, out_shape=jax.ShapeDtypeStruct((S, B_V_WIDTH), BF16),
        compiler_params=_params(("parallel",)),
    )(o_f, o_b, gain, proj)


def gla_post_bwd(o_f, o_b, gain, proj, dy, *, name):
    S = o_f.shape[0]
    ts = _tile(S, 512)
    rcol = (2 * B_QK_WIDTH + B_V_WIDTH) // B_V_WIDTH

    def body(f_ref, b_ref, g_ref, r_ref, dy_ref, do_ref, dr_ref, dg_ref):
        @pl.when(pl.program_id(0) == 0)
        def _():
            dg_ref[...] = jnp.zeros_like(dg_ref)

        for h in range(B_HEADS):
            sl = slice(h * B_VAL_DIM, (h + 1) * B_VAL_DIM)
            o = f_ref[:, sl] + b_ref[:, sl]
            rs = lax.rsqrt(jnp.mean(o * o, axis=-1, keepdims=True) + RMS_EPS)
            ohat = o * rs
            g = g_ref[:, sl]
            r = r_ref[:, sl]
            sig = 1.0 / (1.0 + jnp.exp(-r))
            dyv = dy_ref[:, sl].astype(F32)
            dn = dyv * (r * sig)
            dr_ref[:, sl] = dyv * (ohat * g) * (sig * (1.0 + r * (1.0 - sig)))
            dng = dn * g
            do_ref[:, sl] = rs * (dng - ohat * jnp.mean(dng * ohat, axis=-1, keepdims=True))
            dg_ref[0:1, sl] += jnp.sum(dn * ohat, axis=0, keepdims=True)

    spec = pl.BlockSpec((ts, B_V_WIDTH), lambda i: (i, 0))
    return pl.pallas_call(
        body, name=name, grid=(S // ts,),
        in_specs=[spec, spec, _const_spec((1, B_V_WIDTH)), pl.BlockSpec((ts, B_V_WIDTH), lambda i: (i, rcol)), spec],
        out_specs=[spec, spec, _const_spec((8, B_V_WIDTH))],
        out_shape=[jax.ShapeDtypeStruct((S, B_V_WIDTH), F32), jax.ShapeDtypeStruct((S, B_V_WIDTH), F32),
                   jax.ShapeDtypeStruct((8, B_V_WIDTH), F32)],
        compiler_params=_params(("arbitrary",)),
    )(o_f, o_b, gain, proj, dy)


def gla_combine(parts_f, parts_b, dr, dz, *, name):
    S = dr.shape[0]
    ts = _tile(S, 512)

    def body(qf, kf, vf, qb, kb, vb, r_ref, z_ref, o_ref):
        o_ref[:, 0:512] = (qf[...] + qb[...]).astype(o_ref.dtype)
        o_ref[:, 512:1024] = (kf[...] + kb[...]).astype(o_ref.dtype)
        o_ref[:, 1024:2048] = (vf[...] + vb[...]).astype(o_ref.dtype)
        o_ref[:, 2048:3072] = r_ref[...].astype(o_ref.dtype)
        o_ref[:, 3072:3200] = z_ref[...].astype(o_ref.dtype)

    s512 = pl.BlockSpec((ts, B_QK_WIDTH), lambda i: (i, 0))
    s1024 = pl.BlockSpec((ts, B_V_WIDTH), lambda i: (i, 0))
    return pl.pallas_call(
        body, name=name, grid=(S // ts,),
        in_specs=[s512, s512, s1024, s512, s512, s1024, s1024, pl.BlockSpec((ts, LANES), lambda i: (i, 0))],
        out_specs=pl.BlockSpec((ts, B_IN_PAD), lambda i: (i, 0)),
        out_shape=jax.ShapeDtypeStruct((S, B_IN_PAD), BF16),
        compiler_params=_params(("parallel",)),
    )(*parts_f, *parts_b, dr, dz)


def adamw(w, g, m, v, *, name):
    R, C = w.shape
    tr = _tile(R, 256)
    c1 = 1.0 / (1.0 - ADAM_B1 ** ADAM_STEP)
    c2 = 1.0 / (1.0 - ADAM_B2 ** ADAM_STEP)

    def body(w_ref, g_ref, m_ref, v_ref, d_ref, mo_ref, vo_ref):
        gv = g_ref[...]
        mn = ADAM_B1 * m_ref[...] + (1.0 - ADAM_B1) * gv
        vn = ADAM_B2 * v_ref[...] + (1.0 - ADAM_B2) * (gv * gv)
        mo_ref[...] = mn
        vo_ref[...] = vn
        d_ref[...] = -ADAM_LR * ((mn * c1) / (jnp.sqrt(vn * c2) + ADAM_EPS) + ADAM_WD * w_ref[...])

    spec = pl.BlockSpec((tr, C), lambda i: (i, 0))
    return pl.pallas_call(
        body, name=name, grid=(R // tr,), in_specs=[spec] * 4, out_specs=[spec] * 3,
        out_shape=[jax.ShapeDtypeStruct((R, C), F32)] * 3,
        compiler_params=_params(("parallel",)),
    )(w, g, m, v)


def _chip_peers():
    x, y, c = lax.axis_index("x"), lax.axis_index("y"), lax.axis_index("c")
    return x, y, c, [(1 - x, y), (x, 1 - y), (1 - x, 1 - y)]


_ANY = pl.BlockSpec(memory_space=pl.ANY)


def gather_shards(src, *, name):
    _, R, C = src.shape

    def body(src_ref, out_ref, ici_send, ici_recv, d2d_send, d2d_recv):
        x, y, c, chips = _chip_peers()
        me = 2 * x + y
        sibling = (x, y, 1 - c)
        sends = []
        for k, (px, py) in enumerate(chips):
            cp = pltpu.make_async_remote_copy(
                src_ref=src_ref.at[c], dst_ref=out_ref.at[me, c], send_sem=ici_send.at[k], recv_sem=ici_recv.at[k],
                device_id=(px, py, c), device_id_type=MESH)
            cp.start()
            sends.append(cp)
        for k, (px, py) in enumerate(chips):
            landed = out_ref.at[2 * px + py, c]
            pltpu.make_async_remote_copy(
                src_ref=src_ref.at[c], dst_ref=landed, send_sem=ici_send.at[k], recv_sem=ici_recv.at[k],
                device_id=(px, py, c), device_id_type=MESH).wait_recv()
            cp = pltpu.make_async_remote_copy(
                src_ref=landed, dst_ref=landed, send_sem=d2d_send.at[k], recv_sem=d2d_recv.at[k],
                device_id=sibling, device_id_type=MESH)
            cp.start()
            sends.append(cp)
        for k, (px, py) in enumerate(chips):
            other_half = out_ref.at[2 * px + py, 1 - c]
            pltpu.make_async_remote_copy(
                src_ref=other_half, dst_ref=other_half, send_sem=d2d_send.at[k], recv_sem=d2d_recv.at[k],
                device_id=sibling, device_id_type=MESH).wait_recv()
        for cp in sends:
            cp.wait_send()

    return pl.pallas_call(
        body, name=name, in_specs=[_ANY], out_specs=_ANY,
        out_shape=jax.ShapeDtypeStruct((N_CHIPS, 2, R, C), src.dtype),
        scratch_shapes=[pltpu.SemaphoreType.DMA((3,))] * 4,
        compiler_params=pltpu.CompilerParams(has_side_effects=True),
    )(src)


def chip_exchange(srcs, *, name):
    n = len(srcs)

    def body(*refs):
        src_refs, out_refs = refs[:n], refs[n:2 * n]
        send_sems, recv_sems = refs[2 * n:]
        x, y, c, chips = _chip_peers()
        me = 2 * x + y
        copies = []
        for i, (src_ref, out_ref) in enumerate(zip(src_refs, out_refs)):
            for k, (px, py) in enumerate(chips):
                cp = pltpu.make_async_remote_copy(
                    src_ref=src_ref.at[2 * px + py], dst_ref=out_ref.at[me],
                    send_sem=send_sems.at[i, k], recv_sem=recv_sems.at[i, k],
                    device_id=(px, py, c), device_id_type=MESH)
                cp.start()
                copies.append(cp)
        for i, (src_ref, out_ref) in enumerate(zip(src_refs, out_refs)):
            for k, (px, py) in enumerate(chips):
                pltpu.make_async_remote_copy(
                    src_ref=src_ref.at[me], dst_ref=out_ref.at[2 * px + py],
                    send_sem=send_sems.at[i, k], recv_sem=recv_sems.at[i, k],
                    device_id=(px, py, c), device_id_type=MESH).wait_recv()
        for cp in copies:
            cp.wait_send()

    return pl.pallas_call(
        body, name=name, in_specs=[_ANY] * n, out_specs=[_ANY] * n,
        out_shape=[jax.ShapeDtypeStruct(s.shape, s.dtype) for s in srcs],
        scratch_shapes=[pltpu.SemaphoreType.DMA((n, 3)), pltpu.SemaphoreType.DMA((n, 3))],
        compiler_params=pltpu.CompilerParams(has_side_effects=True),
    )(*srcs)


def sibling_exchange(srcs, *, name):
    n = len(srcs)

    def body(*refs):
        src_refs, out_refs, (send_sems, recv_sems) = refs[:n], refs[n:2 * n], refs[2 * n:]
        x, y, c = lax.axis_index("x"), lax.axis_index("y"), lax.axis_index("c")
        copies = []
        for i, (src_ref, out_ref) in enumerate(zip(src_refs, out_refs)):
            cp = pltpu.make_async_remote_copy(
                src_ref=src_ref.at[:, 1 - c], dst_ref=out_ref, send_sem=send_sems.at[i], recv_sem=recv_sems.at[i],
                device_id=(x, y, 1 - c), device_id_type=MESH)
            cp.start()
            copies.append(cp)
        for cp in copies:
            cp.wait()

    return pl.pallas_call(
        body, name=name, in_specs=[_ANY] * n, out_specs=[_ANY] * n,
        out_shape=[jax.ShapeDtypeStruct((s.shape[0],) + s.shape[2:], s.dtype) for s in srcs],
        scratch_shapes=[pltpu.SemaphoreType.DMA((n,)), pltpu.SemaphoreType.DMA((n,))],
        compiler_params=pltpu.CompilerParams(has_side_effects=True),
    )(*srcs)


def sibling_share(bufs, *, name):
    n = len(bufs)

    def body(*refs):
        out_refs, (send_sems, recv_sems) = refs[n:2 * n], refs[2 * n:]
        x, y, c = lax.axis_index("x"), lax.axis_index("y"), lax.axis_index("c")
        copies = []
        for i, out_ref in enumerate(out_refs):
            cp = pltpu.make_async_remote_copy(
                src_ref=out_ref.at[c], dst_ref=out_ref.at[c], send_sem=send_sems.at[i], recv_sem=recv_sems.at[i],
                device_id=(x, y, 1 - c), device_id_type=MESH)
            cp.start()
            copies.append(cp)
        for cp in copies:
            cp.wait()

    return pl.pallas_call(
        body, name=name, in_specs=[_ANY] * n, out_specs=[_ANY] * n,
        out_shape=[jax.ShapeDtypeStruct(b.shape, b.dtype) for b in bufs],
        input_output_aliases={i: i for i in range(n)},
        scratch_shapes=[pltpu.SemaphoreType.DMA((n,))] * 2,
        compiler_params=pltpu.CompilerParams(has_side_effects=True),
    )(*bufs)


def add_pair(a, b, out_dtype, *, name):
    n, _, R, C = a.shape
    tr = _tile(R, 256)

    def body(c_ref, a_ref, b_ref, o_ref):
        o_ref[...] = (a_ref[0] + b_ref[...]).astype(o_ref.dtype)

    return pl.pallas_call(
        body, name=name,
        grid_spec=pltpu.PrefetchScalarGridSpec(
            num_scalar_prefetch=1, grid=(n, R // tr),
            in_specs=[pl.BlockSpec((1, 1, tr, C), lambda s, i, c_ref: (s, c_ref[0], i, 0)),
                      pl.BlockSpec((1, tr, C), lambda s, i, c_ref: (s, i, 0))],
            out_specs=pl.BlockSpec((1, tr, C), lambda s, i, c_ref: (s, i, 0))),
        out_shape=jax.ShapeDtypeStruct((n, R, C), out_dtype),
        compiler_params=_params(("parallel", "parallel")),
    )(lax.axis_index("c").reshape(1).astype(jnp.int32), a, b)


def sum_slots(slots, own, *, name):
    n, R, C = slots.shape
    tr = _tile(R, 256)

    def body(ids_ref, *refs):
        slot_refs, own_ref, o_ref = refs[:n], refs[n], refs[n + 1]
        me = ids_ref[0]
        acc = None
        for s, r in enumerate(slot_refs):
            t = jnp.where(me == s, own_ref[0], r[0]).astype(F32)
            acc = t if acc is None else acc + t
        o_ref[0] = acc

    def slot_spec(s):
        return pl.BlockSpec((1, tr, C), lambda i, ids: (jnp.where(ids[0] == s, (s + 1) % n, s), i, 0))

    x, y, c = lax.axis_index("x"), lax.axis_index("y"), lax.axis_index("c")
    return pl.pallas_call(
        body, name=name,
        grid_spec=pltpu.PrefetchScalarGridSpec(
            num_scalar_prefetch=1, grid=(R // tr,),
            in_specs=[slot_spec(s) for s in range(n)] + [pl.BlockSpec((1, tr, C), lambda i, ids: (ids[0], i, 0))],
            out_specs=pl.BlockSpec((1, tr, C), lambda i, ids: (ids[1], i, 0))),
        out_shape=jax.ShapeDtypeStruct((2, R, C), F32),
        compiler_params=_params(("parallel",)),
    )(jnp.stack([2 * x + y, c]).astype(jnp.int32), *([slots] * n), own)


def _flat_rows(n_elems, mult):
    rows = -(-n_elems // FLAT_COLS)
    return -(-rows // mult) * mult


def _to_flat(parts, mult):
    v = jnp.concatenate([p.reshape(-1) for p in parts])
    rows = _flat_rows(v.shape[0], mult)
    return jnp.pad(v, (0, rows * FLAT_COLS - v.shape[0])).reshape(rows, FLAT_COLS)


def _from_flat(flat, shapes):
    v = flat.reshape(-1)
    out, off = [], 0
    for s in shapes:
        n = int(np.prod(s))
        out.append(v[off:off + n].reshape(s))
        off += n
    return out


def _unshard(blocks, axis):
    return jnp.concatenate([blocks[s] for s in range(N_CHIPS)], axis=axis)


def _by_shard(full, axis):
    shp = full.shape
    cut = full.reshape(shp[:axis] + (N_CHIPS, shp[axis] // N_CHIPS) + shp[axis + 1:])
    return jnp.moveaxis(cut, axis, 0)


def _gradient_blocks(grads):
    blocks = []
    for n in MATRICES:
        t = _by_shard(jnp.stack(grads[n]), SHARD_AXIS[n])
        blocks.append(t.reshape(N_CHIPS, 2, -1, t.shape[-1]))
    rest = []
    for s in range(N_CHIPS):
        parts = [jnp.stack(grads[n]) if n in REPLICATED else _by_shard(jnp.stack(grads[n]), SHARD_AXIS[n])[s]
                 for n in WEIGHTS if n not in MATRICES]
        rest.append(_to_flat(parts, 16))
    rest = jnp.stack(rest)
    blocks.append(rest.reshape(N_CHIPS, 2, rest.shape[1] // 2, FLAT_COLS))
    return blocks


def _gather_weights(w):
    full = {}
    for names, dtype, mult, call in ((BIG, BF16, 32, "gather_weights"), (SMALL_SHARDED, F32, 16, "gather_vectors")):
        parts = [w[n].astype(dtype) for n in names]
        flat = _to_flat(parts, mult)
        got = gather_shards(flat.reshape(2, flat.shape[0] // 2, FLAT_COLS), name=call)
        got = got.reshape(N_CHIPS, flat.shape[0], FLAT_COLS)
        me = 2 * lax.axis_index("x") + lax.axis_index("y")
        got = lax.dynamic_update_slice(got, flat[None], (me, 0, 0))
        per_chip = [_from_flat(got[s], [p.shape for p in parts]) for s in range(N_CHIPS)]
        for i, n in enumerate(names):
            full[n] = _unshard([per_chip[s][i] for s in range(N_CHIPS)], SHARD_AXIS[n])
    return full


def _reduce_gradients(grads, w):
    halves = _gradient_blocks(grads)
    other = sibling_exchange(halves, name="grad_pair_exchange")
    pair = [add_pair(a, b, BF16 if i < len(MATRICES) else F32, name=f"grad_pair_add{i}")
            for i, (a, b) in enumerate(zip(halves, other))]
    slots = chip_exchange(pair, name="grad_chip_exchange")
    mine = [sum_slots(a, p, name=f"grad_chip_sum{i}") for i, (a, p) in enumerate(zip(slots, pair))]
    both = sibling_share(mine, name="grad_half_share")
    g = {n: t.reshape(w[n].shape) for n, t in zip(MATRICES, both)}
    rest = [n for n in WEIGHTS if n not in MATRICES]
    g.update(zip(rest, _from_flat(both[-1], [w[n].shape for n in rest])))
    return g


def _layer_fwd(i, h, p, aux):
    j = i // 2
    sv = {"h0": h}
    if i % 2 == 0:
        hns = rmsnorm_fwd(h, p["attn_norm"][i][None], dils=DILS, name=f"l{i}_norm1")
        qkvs, qkvps, os_, lses = [], [], [], []
        for g, d in enumerate(DILS):
            qkv = matmul(hns[g], p["a_w_in_g"][j][g], tn=3 * A_WIDTH // 2, name=f"l{i}_a_in{g}")
            qkvp = qk_prep_fwd(qkv, aux["a_gain"][j][g], aux["cos"][g], aux["sin"][g], aux["ones"], name=f"l{i}_a_prep{g}")
            o, l = attn_fwd(qkvp, d, name=f"l{i}_a_attn{g}")
            qkvs.append(qkv)
            qkvps.append(qkvp)
            os_.append(o)
            lses.append(l)
        out, lse = attn_merge(os_, lses, DILS, name=f"l{i}_a_merge")
        sv.update(hns=hns, qkv=qkvs, qkvp=qkvps, out=out, lse=lse)
        h1 = matmul(out, p["a_w_out"][j], res=h, name=f"l{i}_a_out")
    else:
        hn = rmsnorm_fwd(h, p["attn_norm"][i][None], name=f"l{i}_norm1")[0]
        sv["hn"] = hn
        proj = matmul(hn, p["b_w_in"][j], tn=640, name=f"l{i}_b_in")
        loga = gate_fwd(proj, aux["b_wblk"][j], aux["b_bias"][j], name=f"l{i}_b_gate")
        o_f, st_f = gla_fwd(proj, loga, aux["tri_f"], False, name=f"l{i}_b_gla_f")
        o_b, st_b = gla_fwd(proj, loga, aux["tri_b"], True, name=f"l{i}_b_gla_b")
        y = gla_post_fwd(o_f, o_b, aux["b_gain"][j], proj, name=f"l{i}_b_post")
        sv.update(proj=proj, loga=loga, o_f=o_f, o_b=o_b, st_f=st_f, st_b=st_b, y=y)
        h1 = matmul(y, p["b_w_out"][j], res=h, name=f"l{i}_b_out")
    sv["h1"] = h1
    hn2 = rmsnorm_fwd(h1, p["ffn_norm"][i][None], name=f"l{i}_norm2")[0]
    gu = matmul(hn2, p["ffn_w_gate_up"][i], out_dtype=BF16, tn=FFN_HIDDEN // 2, name=f"l{i}_f_up")
    act = swiglu_fwd(gu, name=f"l{i}_f_act")
    h2 = matmul(act, p["ffn_w_down"][i], res=h1, tk=2816, name=f"l{i}_f_down")
    sv.update(hn2=hn2, gu=gu, act=act)
    return h2, sv


def _layer_bwd(i, dh, p, pt, aux, sv, grads):
    j = i // 2
    dhb = dh.astype(BF16)
    grads["ffn_w_down"][i] = matmul(sv["act"].T, dhb, tm=FFN_HIDDEN // 2, tk=2048, name=f"l{i}_f_down_dw")
    dact = matmul(dhb, pt["ffn_w_down"][i], out_dtype=BF16, tn=FFN_HIDDEN // 2, name=f"l{i}_f_down_dx")
    dgu = swiglu_bwd(sv["gu"], dact, name=f"l{i}_f_act_bwd")
    grads["ffn_w_gate_up"][i] = matmul(sv["hn2"].T, dgu, tk=2048, name=f"l{i}_f_up_dw")
    dhn2 = matmul(dgu, pt["ffn_w_gate_up"][i], tk=2816, name=f"l{i}_f_up_dx")
    dh1, dg = rmsnorm_bwd(sv["h1"], p["ffn_norm"][i][None], [dhn2], dh, name=f"l{i}_norm2_bwd")
    grads["ffn_norm"][i] = dg[0]
    dh1b = dh1.astype(BF16)
    if i % 2 == 0:
        grads["a_w_out"][j] = matmul(sv["out"].T, dh1b, tk=2048, name=f"l{i}_a_out_dw")
        dout = matmul(dh1b, pt["a_w_out"][j], out_dtype=BF16, name=f"l{i}_a_out_dx")
        douts, deltas = attn_delta(dout, sv["out"], aux["ones"], DILS, name=f"l{i}_a_delta")
        dws, dhns, dgq, dgk = [], [], [], []
        for g, d in enumerate(DILS):
            qkvp, lse = sv["qkvp"][g], sv["lse"][g]
            dq = attn_bwd_dq(qkvp, douts[g], lse, deltas[g], d, name=f"l{i}_a_dq{g}")
            dk, dv = attn_bwd_dkv(qkvp, douts[g], lse, deltas[g], d, name=f"l{i}_a_dkv{g}")
            dqkv, dgain = qk_prep_bwd(sv["qkv"][g], aux["a_gain"][j][g], aux["cos"][g], aux["sin"][g], aux["ones"],
                                      [dq, dk, dv], name=f"l{i}_a_prep_bwd{g}")
            dgh = dgain[0].reshape(3, A_HEADS, A_HEAD_DIM).sum(axis=1)
            dgq.append(dgh[0])
            dgk.append(dgh[1])
            dws.append(matmul(sv["hns"][g].T, dqkv, tk=2048, name=f"l{i}_a_in_dw{g}"))
            dhns.append(matmul(dqkv, pt["a_w_in_g"][j][g], tk=3072, name=f"l{i}_a_in_dx{g}"))
        grads["a_q_norm"][j] = jnp.stack(dgq)
        grads["a_k_norm"][j] = jnp.stack(dgk)
        grads["a_w_in"][j] = jnp.concatenate(dws, axis=1)
        dh0, dg = rmsnorm_bwd(sv["h0"], p["attn_norm"][i][None], dhns, dh1, dils=DILS, name=f"l{i}_norm1_bwd")
    else:
        grads["b_w_out"][j] = matmul(sv["y"].T, dh1b, tk=2048, name=f"l{i}_b_out_dw")
        dy = matmul(dh1b, pt["b_w_out"][j], name=f"l{i}_b_out_dx")
        do, dr, dgn = gla_post_bwd(sv["o_f"], sv["o_b"], aux["b_gain"][j], sv["proj"], dy, name=f"l{i}_b_post_bwd")
        grads["b_out_norm"][j] = dgn[0].reshape(B_HEADS, B_VAL_DIM)
        pf = gla_bwd(sv["proj"], sv["loga"], sv["st_f"], do, aux["tri_f"], aux["tri_b"], False, name=f"l{i}_b_gla_f_bwd")
        pb = gla_bwd(sv["proj"], sv["loga"], sv["st_b"], do, aux["tri_b"], aux["tri_f"], True, name=f"l{i}_b_gla_b_bwd")
        dloga = jnp.concatenate([pf[3], pb[3]], axis=1)
        dz, dwblk, dbias = gate_bwd(sv["proj"], aux["b_wblk"][j], aux["b_wblk_t"][j], aux["b_bias"][j], dloga,
                                    name=f"l{i}_b_gate_bwd")
        grads["b_w_gate_f"][j] = dwblk[:B_GATE_RANK, :B_QK_WIDTH]
        grads["b_w_gate_b"][j] = dwblk[B_GATE_RANK:2 * B_GATE_RANK, B_QK_WIDTH:]
        grads["b_gate_bias_f"][j] = dbias[0, :B_QK_WIDTH]
        grads["b_gate_bias_b"][j] = dbias[0, B_QK_WIDTH:]
        dproj = gla_combine(pf[:3], pb[:3], dr, dz, name=f"l{i}_b_combine")
        grads["b_w_in"][j] = matmul(sv["hn"].T, dproj, tn=640, tk=2048, name=f"l{i}_b_in_dw")[:, :B_IN_WIDTH]
        dhn = matmul(dproj, pt["b_w_in"][j], tk=B_IN_PAD, name=f"l{i}_b_in_dx")
        dh0, dg = rmsnorm_bwd(sv["h0"], p["attn_norm"][i][None], [dhn], dh1, name=f"l{i}_norm1_bwd")
    grads["attn_norm"][i] = dg[0]
    return dh0


def _local_step(x, target, p, small):
    S = x.shape[0]
    cos, sin = _rope_tables(S)
    to_phase = lambda t, d: t.reshape(S // d, d, LANES).swapaxes(0, 1).reshape(S, LANES)
    cos, sin = [to_phase(cos, d) for d in DILS], [to_phase(sin, d) for d in DILS]
    ones_v = jnp.ones((A_WIDTH,), F32)
    a_gain = [[jnp.concatenate([jnp.tile(small["a_q_norm"][j][g], A_HEADS), jnp.tile(small["a_k_norm"][j][g], A_HEADS),
                                ones_v])[None] for g in range(len(DILS))] for j in range(2)]
    b_wblk = [_gate_block_weight(p["b_w_gate_f"][j].astype(F32), p["b_w_gate_b"][j].astype(F32)) for j in range(2)]
    aux = dict(cos=cos, sin=sin, ones=_head_block_ones(), a_gain=a_gain, tri_f=_tri(False), tri_b=_tri(True),
               b_wblk=b_wblk, b_wblk_t=[w.T for w in b_wblk],
               b_bias=[jnp.concatenate([small["b_gate_bias_f"][j], small["b_gate_bias_b"][j]])[None] for j in range(2)],
               b_gain=[small["b_out_norm"][j].reshape(1, B_V_WIDTH) for j in range(2)])
    pw = dict(p)
    pw["b_w_in"] = jnp.pad(p["b_w_in"], ((0, 0), (0, 0), (0, B_IN_PAD - B_IN_WIDTH)))
    pw["attn_norm"], pw["ffn_norm"] = small["attn_norm"], small["ffn_norm"]
    gw = 3 * A_WIDTH
    pw["a_w_in_g"] = [[p["a_w_in"][j][:, g * gw:(g + 1) * gw] for g in range(len(DILS))] for j in range(2)]
    pt = {n: jnp.swapaxes(pw[n], 1, 2) for n in ("a_w_out", "b_w_in", "b_w_out", "ffn_w_gate_up", "ffn_w_down")}
    pt["a_w_in_g"] = [[wg.T for wg in row] for row in pw["a_w_in_g"]]

    h = x
    saved = []
    for i in range(DEPTH):
        h, sv = _layer_fwd(i, h, pw, aux)
        saved.append(sv)
    loss_sq, dh = loss_head(h, target, name="loss_head")
    grads = {n: [None] * (DEPTH if n in ("attn_norm", "ffn_norm", "ffn_w_gate_up", "ffn_w_down") else 2) for n in WEIGHTS}
    for i in reversed(range(DEPTH)):
        dh = _layer_bwd(i, dh, pw, pt, aux, saved[i], grads)
    return loss_sq[0, 0] * (0.5 / D_MODEL), dh, grads


def kernel(x, attn_norm, ffn_norm, a_w_in, a_q_norm, a_k_norm, a_w_out, b_w_in, b_w_gate_f, b_gate_bias_f, b_w_gate_b, b_gate_bias_b, b_out_norm, b_w_out, ffn_w_gate_up, ffn_w_down, loss_target, m_attn_norm, m_ffn_norm, m_a_w_in, m_a_q_norm, m_a_k_norm, m_a_w_out, m_b_w_in, m_b_w_gate_f, m_b_gate_bias_f, m_b_w_gate_b, m_b_gate_bias_b, m_b_out_norm, m_b_w_out, m_ffn_w_gate_up, m_ffn_w_down, v_attn_norm, v_ffn_norm, v_a_w_in, v_a_q_norm, v_a_k_norm, v_a_w_out, v_b_w_in, v_b_w_gate_f, v_b_gate_bias_f, v_b_w_gate_b, v_b_gate_bias_b, v_b_out_norm, v_b_w_out, v_ffn_w_gate_up, v_ffn_w_down):
    w = dict(attn_norm=attn_norm, ffn_norm=ffn_norm, a_w_in=a_w_in, a_q_norm=a_q_norm, a_k_norm=a_k_norm, a_w_out=a_w_out,
             b_w_in=b_w_in, b_w_gate_f=b_w_gate_f, b_gate_bias_f=b_gate_bias_f, b_w_gate_b=b_w_gate_b,
             b_gate_bias_b=b_gate_bias_b, b_out_norm=b_out_norm, b_w_out=b_w_out, ffn_w_gate_up=ffn_w_gate_up,
             ffn_w_down=ffn_w_down)
    m = dict(attn_norm=m_attn_norm, ffn_norm=m_ffn_norm, a_w_in=m_a_w_in, a_q_norm=m_a_q_norm, a_k_norm=m_a_k_norm,
             a_w_out=m_a_w_out, b_w_in=m_b_w_in, b_w_gate_f=m_b_w_gate_f, b_gate_bias_f=m_b_gate_bias_f,
             b_w_gate_b=m_b_w_gate_b, b_gate_bias_b=m_b_gate_bias_b, b_out_norm=m_b_out_norm, b_w_out=m_b_w_out,
             ffn_w_gate_up=m_ffn_w_gate_up, ffn_w_down=m_ffn_w_down)
    v = dict(attn_norm=v_attn_norm, ffn_norm=v_ffn_norm, a_w_in=v_a_w_in, a_q_norm=v_a_q_norm, a_k_norm=v_a_k_norm,
             a_w_out=v_a_w_out, b_w_in=v_b_w_in, b_w_gate_f=v_b_w_gate_f, b_gate_bias_f=v_b_gate_bias_f,
             b_w_gate_b=v_b_w_gate_b, b_gate_bias_b=v_b_gate_bias_b, b_out_norm=v_b_out_norm, b_w_out=v_b_w_out,
             ffn_w_gate_up=v_ffn_w_gate_up, ffn_w_down=v_ffn_w_down)

    full = _gather_weights(w)
    p = {n: full[n] for n in BIG}
    small = {n: full[n] for n in SMALL_SHARDED}
    small.update({n: w[n] for n in REPLICATED})
    loss_local, dx, grads = _local_step(x[0], loss_target[0], p, small)
    loss = lax.psum(loss_local, ("x", "y", "c"))

    g = _reduce_gradients(grads, w)
    delta, new_m, new_v = {}, {}, {}
    rows = lambda t: t.reshape(-1, t.shape[-1])
    for n in MATRICES:
        outs = adamw(rows(w[n]), rows(g[n]), rows(m[n]), rows(v[n]), name=f"adamw_{n}")
        delta[n], new_m[n], new_v[n] = [o.reshape(w[n].shape) for o in outs]
    rest = [n for n in WEIGHTS if n not in MATRICES]
    flat = lambda d: _to_flat([d[n] for n in rest], 8)
    outs = adamw(flat(w), flat(g), flat(m), flat(v), name="adamw_vectors")
    for d, o in zip((delta, new_m, new_v), outs):
        d.update(zip(rest, _from_flat(o, [w[n].shape for n in rest])))
    return (loss, dx[None], *[g[n] for n in WEIGHTS], *[delta[n] for n in WEIGHTS],
            *[new_m[n] for n in WEIGHTS], *[new_v[n] for n in WEIGHTS])
```

```python
import functools

import numpy as np
import jax
import jax.numpy as jnp
from jax import lax
from jax.experimental import pallas as pl
from jax.experimental.pallas import tpu as pltpu

F32, BF16 = jnp.float32, jnp.bfloat16
HI = lax.Precision.HIGHEST
MESH = pl.DeviceIdType.MESH

D_MODEL = 1024
DEPTH = 4
RMS_EPS = 1e-6
NEG_INF = -1e30
A_GROUPS = ((128, 1), (512, 4), (2048, 16))
DILS = tuple(d for _, d in A_GROUPS)
A_HALF = 64
A_HEADS = 16
A_HEAD_DIM = 64
A_WIDTH = 1024
A_IN_WIDTH = 9216
ROPE_THETA = 10000.0
B_HEADS = 4
B_KEY_DIM = 128
B_VAL_DIM = 256
B_QK_WIDTH = 512
B_V_WIDTH = 1024
B_GATE_RANK = 16
B_GATE_TAU = 16.0
B_CHUNK = 64
B_IN_WIDTH = 3104
B_IN_PAD = 3200
FFN_HIDDEN = 2816
ADAM_LR, ADAM_B1, ADAM_B2, ADAM_EPS, ADAM_WD, ADAM_STEP = 0.001, 0.9, 0.999, 1e-08, 0.01, 10
LANES = 128
VMEM_LIMIT = 48 * 1024 * 1024
FLAT_COLS = 1024
N_CHIPS = 4

WEIGHTS = ['attn_norm', 'ffn_norm', 'a_w_in', 'a_q_norm', 'a_k_norm', 'a_w_out', 'b_w_in', 'b_w_gate_f',
           'b_gate_bias_f', 'b_w_gate_b', 'b_gate_bias_b', 'b_out_norm', 'b_w_out', 'ffn_w_gate_up', 'ffn_w_down']
REPLICATED = ('attn_norm', 'ffn_norm', 'a_q_norm', 'a_k_norm')
SHARD_AXIS = {'a_w_in': 2, 'a_w_out': 1, 'b_w_in': 2, 'b_w_gate_f': 2, 'b_gate_bias_f': 1, 'b_w_gate_b': 2,
              'b_gate_bias_b': 1, 'b_out_norm': 2, 'b_w_out': 1, 'ffn_w_gate_up': 2, 'ffn_w_down': 1}
BIG = ('a_w_in', 'a_w_out', 'b_w_in', 'b_w_gate_f', 'b_w_gate_b', 'b_w_out', 'ffn_w_gate_up', 'ffn_w_down')
SMALL_SHARDED = ('b_gate_bias_f', 'b_gate_bias_b', 'b_out_norm')
MATRICES = ('a_w_in', 'a_w_out', 'b_w_in', 'b_w_out', 'ffn_w_gate_up', 'ffn_w_down')


def _params(sem):
    return pltpu.CompilerParams(dimension_semantics=sem, vmem_limit_bytes=VMEM_LIMIT)


def _tile(n, pref):
    t = min(n, pref)
    while n % t:
        t //= 2
    return t


def _const_spec(shape):
    nd = len(shape)
    return pl.BlockSpec(shape, lambda *_: (0,) * nd)


def matmul(a, b, *, name, out_dtype=F32, res=None, tm=1024, tn=512, tk=1024):
    M, K = a.shape
    N = b.shape[1]
    assert b.shape[0] == K
    tm, tn, tk = _tile(M, tm), _tile(N, tn), _tile(K, tk)
    nk = K // tk

    def body(*refs):
        a_ref, b_ref = refs[:2]
        r_ref = refs[2] if res is not None else None
        o_ref = refs[3 if res is not None else 2]
        part = jnp.dot(a_ref[...], b_ref[...], preferred_element_type=F32)

        def finish(v):
            if res is not None:
                v = v + r_ref[...]
            o_ref[...] = v.astype(o_ref.dtype)

        if nk == 1:
            finish(part)
            return
        acc_ref = refs[-1]
        k = pl.program_id(2)

        @pl.when(k == 0)
        def _():
            acc_ref[...] = part

        @pl.when((k > 0) & (k < nk - 1))
        def _():
            acc_ref[...] += part

        @pl.when(k == nk - 1)
        def _():
            finish(acc_ref[...] + part)

    in_specs = [pl.BlockSpec((tm, tk), lambda i, j, k: (i, k)), pl.BlockSpec((tk, tn), lambda i, j, k: (k, j))]
    args = [a, b]
    if res is not None:
        in_specs.append(pl.BlockSpec((tm, tn), lambda i, j, k: (i, j)))
        args.append(res)
    return pl.pallas_call(
        body, name=name, grid=(M // tm, N // tn, nk), in_specs=in_specs,
        out_specs=pl.BlockSpec((tm, tn), lambda i, j, k: (i, j)),
        out_shape=jax.ShapeDtypeStruct((M, N), out_dtype),
        scratch_shapes=[pltpu.VMEM((tm, tn), F32)] if nk > 1 else [],
        compiler_params=_params(("parallel", "parallel", "arbitrary")),
    )(*args)


def _phase_spec(d, ts, W):
    if d == 1:
        return pl.BlockSpec((ts, W), lambda i: (i, 0))
    return pl.BlockSpec((d, ts // d, W), lambda i: (0, i, 0))


def _phase_view(a, d):
    return a if d == 1 else a.reshape(d, a.shape[0] // d, a.shape[1])


def _phase_shape(S, W, d, dtype):
    return jax.ShapeDtypeStruct((S, W) if d == 1 else (d, S // d, W), dtype)


def _nat_scratch(ts, W):
    return pltpu.VMEM((W // LANES, ts, LANES), F32)


def _put_natural(nat_ref, value):
    for c in range(nat_ref.shape[0]):
        nat_ref[c] = value[:, c * LANES:(c + 1) * LANES]


def _get_natural(nat_ref):
    return jnp.concatenate([nat_ref[c] for c in range(nat_ref.shape[0])], axis=1)


def _store_phases(nat_ref, o_ref, d, ts):
    for p in range(d):
        for c in range(nat_ref.shape[0]):
            o_ref[p, :, c * LANES:(c + 1) * LANES] = nat_ref[c, pl.ds(p, ts // d, stride=d), :].astype(o_ref.dtype)


def _load_phases(i_ref, nat_ref, d, ts):
    for p in range(d):
        for c in range(nat_ref.shape[0]):
            nat_ref[c, pl.ds(p, ts // d, stride=d), :] = i_ref[p, :, c * LANES:(c + 1) * LANES].astype(F32)


def rmsnorm_fwd(x, gain, *, name, dils=(1,)):
    S, Dm = x.shape
    ts = _tile(S, 512)

    def body(x_ref, g_ref, *rest):
        o_refs, scr = rest[:len(dils)], rest[len(dils)]
        xv = x_ref[...]
        r = lax.rsqrt(jnp.mean(xv * xv, axis=-1, keepdims=True) + RMS_EPS)
        y = (xv * r) * g_ref[...]
        if any(d > 1 for d in dils):
            _put_natural(scr, y)
        for d, o_ref in zip(dils, o_refs):
            if d == 1:
                o_ref[...] = y.astype(o_ref.dtype)
            else:
                _store_phases(scr, o_ref, d, ts)

    outs = pl.pallas_call(
        body, name=name, grid=(S // ts,),
        in_specs=[pl.BlockSpec((ts, Dm), lambda i: (i, 0)), _const_spec((1, Dm))],
        out_specs=[_phase_spec(d, ts, Dm) for d in dils],
        out_shape=[_phase_shape(S, Dm, d, BF16) for d in dils],
        scratch_shapes=[_nat_scratch(ts, Dm)],
        compiler_params=_params(("parallel",)),
    )(x, gain)
    return [o.reshape(S, Dm) for o in outs]


def rmsnorm_bwd(x, gain, dys, dres, *, name, dils=(1,)):
    S, Dm = x.shape
    ts = _tile(S, 256)
    nd = len(dils)

    def body(x_ref, g_ref, *rest):
        dy_refs, (dr_ref, dx_ref, dg_ref, scr) = rest[:nd], rest[nd:]

        @pl.when(pl.program_id(0) == 0)
        def _():
            dg_ref[...] = jnp.zeros_like(dg_ref)

        dyv = None
        for d, dy_ref in zip(dils, dy_refs):
            if d == 1:
                t = dy_ref[...].astype(F32)
            else:
                _load_phases(dy_ref, scr, d, ts)
                t = _get_natural(scr)
            dyv = t if dyv is None else dyv + t
        xv = x_ref[...]
        r = lax.rsqrt(jnp.mean(xv * xv, axis=-1, keepdims=True) + RMS_EPS)
        xhat = xv * r
        dyg = dyv * g_ref[...]
        dx = r * (dyg - xhat * jnp.mean(dyg * xhat, axis=-1, keepdims=True))
        dx_ref[...] = dr_ref[...] + dx
        dg_ref[0:1, :] += jnp.sum(dyv * xhat, axis=0, keepdims=True)

    row = pl.BlockSpec((ts, Dm), lambda i: (i, 0))
    return pl.pallas_call(
        body, name=name, grid=(S // ts,),
        in_specs=[row, _const_spec((1, Dm))] + [_phase_spec(d, ts, Dm) for d in dils] + [row],
        out_specs=[row, _const_spec((8, Dm))],
        out_shape=[jax.ShapeDtypeStruct((S, Dm), F32), jax.ShapeDtypeStruct((8, Dm), F32)],
        scratch_shapes=[_nat_scratch(ts, Dm)],
        compiler_params=_params(("arbitrary",)),
    )(x, gain, *[_phase_view(dy, d) for dy, d in zip(dys, dils)], dres)


def swiglu_fwd(gu, *, name):
    S, F2 = gu.shape
    Fh = F2 // 2
    ts = _tile(S, 512)

    def body(g_ref, u_ref, o_ref):
        g = g_ref[...].astype(F32)
        u = u_ref[...].astype(F32)
        o_ref[...] = (g * (1.0 / (1.0 + jnp.exp(-g))) * u).astype(o_ref.dtype)

    return pl.pallas_call(
        body, name=name, grid=(S // ts,),
        in_specs=[pl.BlockSpec((ts, Fh), lambda i: (i, 0)), pl.BlockSpec((ts, Fh), lambda i: (i, 1))],
        out_specs=pl.BlockSpec((ts, Fh), lambda i: (i, 0)),
        out_shape=jax.ShapeDtypeStruct((S, Fh), BF16),
        compiler_params=_params(("parallel",)),
    )(gu, gu)


def swiglu_bwd(gu, dact, *, name):
    S, F2 = gu.shape
    Fh = F2 // 2
    ts = _tile(S, 256)

    def body(gu_ref, d_ref, o_ref):
        g = gu_ref[:, :Fh].astype(F32)
        u = gu_ref[:, Fh:].astype(F32)
        d = d_ref[...].astype(F32)
        sig = 1.0 / (1.0 + jnp.exp(-g))
        o_ref[:, :Fh] = (d * u * (sig * (1.0 + g * (1.0 - sig)))).astype(o_ref.dtype)
        o_ref[:, Fh:] = (d * (g * sig)).astype(o_ref.dtype)

    return pl.pallas_call(
        body, name=name, grid=(S // ts,),
        in_specs=[pl.BlockSpec((ts, F2), lambda i: (i, 0)), pl.BlockSpec((ts, Fh), lambda i: (i, 0))],
        out_specs=pl.BlockSpec((ts, F2), lambda i: (i, 0)),
        out_shape=jax.ShapeDtypeStruct((S, F2), BF16),
        compiler_params=_params(("parallel",)),
    )(gu, dact)


def loss_head(y, target, *, name):
    S, Dm = y.shape
    ts = _tile(S, 512)

    def body(y_ref, t_ref, l_ref, d_ref):
        @pl.when(pl.program_id(0) == 0)
        def _():
            l_ref[...] = jnp.zeros_like(l_ref)

        e = y_ref[...] - t_ref[...]
        d_ref[...] = e * (1.0 / Dm)
        l_ref[...] += jnp.sum(e * e)

    return pl.pallas_call(
        body, name=name, grid=(S // ts,),
        in_specs=[pl.BlockSpec((ts, Dm), lambda i: (i, 0)), pl.BlockSpec((ts, Dm), lambda i: (i, 0))],
        out_specs=[_const_spec((8, LANES)), pl.BlockSpec((ts, Dm), lambda i: (i, 0))],
        out_shape=[jax.ShapeDtypeStruct((8, LANES), F32), jax.ShapeDtypeStruct((S, Dm), F32)],
        compiler_params=_params(("arbitrary",)),
    )(y, target)


def _head_block_ones():
    i = np.arange(LANES)
    return jnp.asarray((i[:, None] // A_HEAD_DIM == i[None, :] // A_HEAD_DIM).astype(np.float32)).astype(BF16)


def _rope_tables(S):
    half = A_HEAD_DIM // 2
    inv_freq = ROPE_THETA ** (-jnp.arange(half, dtype=F32) / half)
    ang = jnp.arange(S).astype(F32)[:, None] * inv_freq[None, :]
    cos = jnp.tile(jnp.cos(ang), (1, LANES // half))
    sin = jnp.tile(jnp.sin(ang), (1, LANES // half))
    return cos, sin


def _rot_half(x, lo):
    return jnp.where(lo, -pltpu.roll(x, LANES - 32, 1), pltpu.roll(x, 32, 1))


def _seg_sum(v, ones_ref):
    hi = v.astype(BF16)
    lo = (v - hi.astype(F32)).astype(BF16)
    ones = ones_ref[...]
    return jnp.dot(hi, ones, preferred_element_type=F32) + jnp.dot(lo, ones, preferred_element_type=F32)


def _seg_mean(v, ones_ref):
    return _seg_sum(v, ones_ref) * (1.0 / A_HEAD_DIM)


def qk_prep_fwd(qkv, gain, cos, sin, ones, *, name):
    S, W = qkv.shape
    ts = _tile(S, 256)
    nchunk = A_WIDTH // LANES

    def body(x_ref, g_ref, c_ref, s_ref, ones_ref, o_ref):
        kind = pl.program_id(1) % 3

        @pl.when(kind < 2)
        def _():
            scale = jnp.where(kind == 0, A_HEAD_DIM ** -0.5, 1.0).astype(F32)
            lo = (lax.broadcasted_iota(jnp.int32, (ts, LANES), 1) % A_HEAD_DIM) < (A_HEAD_DIM // 2)
            cv, sv = c_ref[...], s_ref[...]
            for c in range(nchunk):
                sl = slice(c * LANES, (c + 1) * LANES)
                xv = x_ref[:, sl]
                r = lax.rsqrt(_seg_mean(xv * xv, ones_ref) + RMS_EPS)
                y = (xv * r) * g_ref[:, sl]
                y = y * cv + _rot_half(y, lo) * sv
                o_ref[:, sl] = (y * scale).astype(o_ref.dtype)

        @pl.when(kind == 2)
        def _():
            o_ref[...] = x_ref[...].astype(o_ref.dtype)

    return pl.pallas_call(
        body, name=name, grid=(S // ts, W // A_WIDTH),
        in_specs=[pl.BlockSpec((ts, A_WIDTH), lambda i, j: (i, j)), pl.BlockSpec((1, A_WIDTH), lambda i, j: (0, j)),
                  pl.BlockSpec((ts, LANES), lambda i, j: (i, 0)), pl.BlockSpec((ts, LANES), lambda i, j: (i, 0)),
                  _const_spec((LANES, LANES))],
        out_specs=pl.BlockSpec((ts, A_WIDTH), lambda i, j: (i, j)),
        out_shape=jax.ShapeDtypeStruct((S, W), BF16),
        compiler_params=_params(("parallel", "arbitrary")),
    )(qkv, gain, cos, sin, ones)


def qk_prep_bwd(qkv, gain, cos, sin, ones, grads, *, name):
    S, W = qkv.shape
    ts = _tile(S, 256)
    nchunk = A_WIDTH // LANES
    nj = W // A_WIDTH

    def body(x_ref, g_ref, c_ref, s_ref, ones_ref, *rest):
        g_refs, (o_ref, dg_ref) = rest[:nj], rest[nj:]
        j = pl.program_id(0)
        kind = j % 3

        @pl.when(pl.program_id(1) == 0)
        def _():
            dg_ref[...] = jnp.zeros_like(dg_ref)

        for n in range(nj):
            @pl.when(j == n)
            def _(n=n):
                d_ref = g_refs[n]
                if n % 3 == 2:
                    o_ref[...] = d_ref[...].astype(o_ref.dtype)
                    return
                scale = A_HEAD_DIM ** -0.5 if n % 3 == 0 else 1.0
                lo = (lax.broadcasted_iota(jnp.int32, (ts, LANES), 1) % A_HEAD_DIM) < (A_HEAD_DIM // 2)
                cv, sv = c_ref[...], s_ref[...]
                for c in range(nchunk):
                    sl = slice(c * LANES, (c + 1) * LANES)
                    dy = d_ref[:, sl].astype(F32) * scale
                    dn = dy * cv - _rot_half(dy, lo) * sv
                    xv = x_ref[:, sl]
                    r = lax.rsqrt(_seg_mean(xv * xv, ones_ref) + RMS_EPS)
                    xhat = xv * r
                    dyg = dn * g_ref[:, sl]
                    dx = r * (dyg - xhat * _seg_mean(dyg * xhat, ones_ref))
                    o_ref[:, sl] = dx.astype(o_ref.dtype)
                    dg_ref[0:1, sl] += jnp.sum(dn * xhat, axis=0, keepdims=True)

    def gspec(n):
        return pl.BlockSpec((ts, A_WIDTH), lambda j, i: (jnp.where(j == n, i, 0), 0))

    return pl.pallas_call(
        body, name=name, grid=(nj, S // ts),
        in_specs=[pl.BlockSpec((ts, A_WIDTH), lambda j, i: (i, j)), pl.BlockSpec((1, A_WIDTH), lambda j, i: (0, j)),
                  pl.BlockSpec((ts, LANES), lambda j, i: (i, 0)), pl.BlockSpec((ts, LANES), lambda j, i: (i, 0)),
                  _const_spec((LANES, LANES))] + [gspec(n) for n in range(nj)],
        out_specs=[pl.BlockSpec((ts, A_WIDTH), lambda j, i: (i, j)), pl.BlockSpec((8, A_WIDTH), lambda j, i: (0, j))],
        out_shape=[jax.ShapeDtypeStruct((S, W), BF16), jax.ShapeDtypeStruct((8, W), F32)],
        compiler_params=_params(("arbitrary", "arbitrary")),
    )(qkv, gain, cos, sin, ones, *grads)


def _band_specs(kind, tq, nlb):
    nhb = tq // A_HALF
    nb = nlb // nhb
    base = kind * (A_WIDTH // LANES)
    return [pl.BlockSpec((A_HALF, LANES), lambda ph, b, hp: (ph * nlb + jnp.maximum(b * nhb - 1, 0), base + hp)),
            pl.BlockSpec((tq, LANES), lambda ph, b, hp: (ph * nb + b, base + hp)),
            pl.BlockSpec((A_HALF, LANES), lambda ph, b, hp: (ph * nlb + jnp.minimum((b + 1) * nhb, nlb - 1), base + hp))]


A_BLOCK = 2048
A_SUB = 128


def _band_bias(sub, key_major):
    i = np.arange(sub)[:, None]
    j = np.arange(sub + 2 * A_HALF)[None, :] - A_HALF
    ok = np.abs(j - i) <= A_HALF
    return jnp.asarray(np.where(ok.T if key_major else ok, 0.0, NEG_INF).astype(np.float32))


def _edge_bias(first, n, L, axis, at_start, at_end):
    if not (at_start or at_end):
        return None
    shape = (1, n) if axis == 1 else (n, 1)
    pos = first - A_HALF + lax.broadcasted_iota(jnp.int32, shape, axis)
    return jnp.where((pos < 0) | (pos >= L), NEG_INF, 0.0).astype(F32)


def _with_edge(bias, edge):
    return bias if edge is None else bias + edge


def _cat3(a_ref, b_ref, c_ref):
    return jnp.concatenate([a_ref[...], b_ref[...], c_ref[...]], axis=0)


def _lane_lo(rows):
    return lax.broadcasted_iota(jnp.int32, (rows, LANES), 1) < A_HEAD_DIM


NT = (((1,), (1,)), ((), ()))
TN = (((0,), (0,)), ((), ()))


def attn_fwd(qkvp, dil, *, name):
    S = qkvp.shape[0]
    L = S // dil
    tq = _tile(L, A_BLOCK)
    sub = min(tq, A_SUB)
    nsub = tq // sub
    nlb = L // A_HALF
    band = _band_bias(sub, False)

    def body(q_ref, kp_ref, ko_ref, kn_ref, vp_ref, vo_ref, vn_ref, band_ref, o_ref, l_ref):
        b = pl.program_id(1)
        K = _cat3(kp_ref, ko_ref, kn_ref)
        V = _cat3(vp_ref, vo_ref, vn_ref)
        lo_k = _lane_lo(tq + 2 * A_HALF)
        lo_q = _lane_lo(sub)
        Km = [jnp.where(sel, K, jnp.zeros_like(K)) for sel in (lo_k, ~lo_k)]
        Vm = [jnp.where(sel, V, jnp.zeros_like(V)) for sel in (lo_k, ~lo_k)]
        for r in range(nsub):
            rows = slice(r * sub, (r + 1) * sub)
            keys = slice(r * sub, (r + 1) * sub + 2 * A_HALF)
            bias = _with_edge(band_ref[...], _edge_bias(b * tq + r * sub, sub + 2 * A_HALF, L, 1, r == 0, r == nsub - 1))
            q = q_ref[rows, :]
            outs, lses = [], []
            for hh in range(2):
                s = lax.dot_general(q, Km[hh][keys], NT, preferred_element_type=F32) + bias
                m = jnp.max(s, axis=1, keepdims=True)
                p = jnp.exp(s - m)
                l = jnp.sum(p, axis=1, keepdims=True)
                outs.append(jnp.dot(p.astype(BF16), Vm[hh][keys], preferred_element_type=F32) * (1.0 / l))
                lses.append(m + jnp.log(l))
            o_ref[rows, :] = outs[0] + outs[1]
            l_ref[rows, :] = jnp.where(lo_q, lses[0], lses[1])

    ospec = _band_specs(0, tq, nlb)[1]
    return pl.pallas_call(
        body, name=name, grid=(dil, L // tq, A_WIDTH // LANES),
        in_specs=[_band_specs(0, tq, nlb)[1]] + _band_specs(1, tq, nlb) + _band_specs(2, tq, nlb)
        + [_const_spec(band.shape)],
        out_specs=[ospec, ospec],
        out_shape=[jax.ShapeDtypeStruct((S, A_WIDTH), F32)] * 2,
        compiler_params=_params(("parallel", "parallel", "parallel")),
    )(*([qkvp] * 7), band)


def attn_merge(os_, lses, dils, *, name):
    S = os_[0].shape[0]
    ts = _tile(S, 256)
    ng = len(dils)

    def body(*refs):
        o_refs, l_refs, out_ref = refs[:ng], refs[ng:2 * ng], refs[2 * ng]
        lse_refs, scrs = refs[2 * ng + 1:3 * ng + 1], refs[3 * ng + 1:]
        ov, ls, k = [], [], 0
        for d, o_ref, l_ref in zip(dils, o_refs, l_refs):
            if d == 1:
                ov.append(o_ref[...])
                ls.append(l_ref[...])
            else:
                _load_phases(o_ref, scrs[k], d, ts)
                _load_phases(l_ref, scrs[k + 1], d, ts)
                ov.append(_get_natural(scrs[k]))
                ls.append(_get_natural(scrs[k + 1]))
                k += 2
        m = functools.reduce(jnp.maximum, ls)
        es = [jnp.exp(l - m) for l in ls]
        tot = functools.reduce(jnp.add, es)
        acc = None
        for e, o in zip(es, ov):
            t = (e / tot) * o
            acc = t if acc is None else acc + t
        out_ref[...] = acc.astype(out_ref.dtype)
        total = m + jnp.log(tot)
        _put_natural(scrs[k], total)
        for d, lse_ref in zip(dils, lse_refs):
            if d == 1:
                lse_ref[...] = total
            else:
                _store_phases(scrs[k], lse_ref, d, ts)

    n_scr = 2 * sum(d > 1 for d in dils) + 1
    outs = pl.pallas_call(
        body, name=name, grid=(S // ts,),
        in_specs=[_phase_spec(d, ts, A_WIDTH) for d in dils] * 2,
        out_specs=[pl.BlockSpec((ts, A_WIDTH), lambda i: (i, 0))] + [_phase_spec(d, ts, A_WIDTH) for d in dils],
        out_shape=[jax.ShapeDtypeStruct((S, A_WIDTH), BF16)] + [_phase_shape(S, A_WIDTH, d, F32) for d in dils],
        scratch_shapes=[_nat_scratch(ts, A_WIDTH)] * n_scr,
        compiler_params=_params(("parallel",)),
    )(*[_phase_view(o, d) for o, d in zip(os_, dils)], *[_phase_view(l, d) for l, d in zip(lses, dils)])
    return outs[0], [l.reshape(S, A_WIDTH) for l in outs[1:]]


def attn_delta(dout, out, ones, dils, *, name):
    S = dout.shape[0]
    ts = _tile(S, 256)
    nd = len(dils)

    def body(d_ref, o_ref, ones_ref, *rest):
        do_refs, dl_refs, (scr_do, scr_dl) = rest[:nd], rest[nd:2 * nd], rest[2 * nd:]
        sums = []
        for c in range(A_WIDTH // LANES):
            sl = slice(c * LANES, (c + 1) * LANES)
            prod = d_ref[:, sl].astype(F32) * o_ref[:, sl].astype(F32)
            sums.append(_seg_sum(prod, ones_ref))
        _put_natural(scr_do, d_ref[...].astype(F32))
        _put_natural(scr_dl, jnp.concatenate(sums, axis=1))
        for d, do_ref, dl_ref in zip(dils, do_refs, dl_refs):
            if d == 1:
                do_ref[...] = d_ref[...]
                dl_ref[...] = _get_natural(scr_dl)
            else:
                _store_phases(scr_do, do_ref, d, ts)
                _store_phases(scr_dl, dl_ref, d, ts)

    spec = pl.BlockSpec((ts, A_WIDTH), lambda i: (i, 0))
    outs = pl.pallas_call(
        body, name=name, grid=(S // ts,), in_specs=[spec, spec, _const_spec((LANES, LANES))],
        out_specs=[_phase_spec(d, ts, A_WIDTH) for d in dils] * 2,
        out_shape=[_phase_shape(S, A_WIDTH, d, BF16) for d in dils] + [_phase_shape(S, A_WIDTH, d, F32) for d in dils],
        scratch_shapes=[_nat_scratch(ts, A_WIDTH)] * 2,
        compiler_params=_params(("parallel",)),
    )(dout, out, ones)
    outs = [o.reshape(S, A_WIDTH) for o in outs]
    return outs[:nd], outs[nd:]


def _head_col(x, hh):
    c = hh * A_HEAD_DIM
    return x[:, c:c + 1]


def attn_bwd_dq(qkvp, dout, lse, delta, dil, *, name):
    S = qkvp.shape[0]
    L = S // dil
    tq = _tile(L, A_BLOCK)
    sub = min(tq, A_SUB)
    nsub = tq // sub
    nlb = L // A_HALF
    band = _band_bias(sub, False)

    def body(q_ref, kp_ref, ko_ref, kn_ref, vp_ref, vo_ref, vn_ref, do_ref, l_ref, d_ref, band_ref, dq_ref):
        b = pl.program_id(1)
        K = _cat3(kp_ref, ko_ref, kn_ref)
        V = _cat3(vp_ref, vo_ref, vn_ref)
        lo_k = _lane_lo(tq + 2 * A_HALF)
        Km = [jnp.where(sel, K, jnp.zeros_like(K)) for sel in (lo_k, ~lo_k)]
        Vm = [jnp.where(sel, V, jnp.zeros_like(V)) for sel in (lo_k, ~lo_k)]
        for r in range(nsub):
            rows = slice(r * sub, (r + 1) * sub)
            keys = slice(r * sub, (r + 1) * sub + 2 * A_HALF)
            bias = _with_edge(band_ref[...], _edge_bias(b * tq + r * sub, sub + 2 * A_HALF, L, 1, r == 0, r == nsub - 1))
            q, do = q_ref[rows, :], do_ref[rows, :]
            lse_v, dl_v = l_ref[rows, :], d_ref[rows, :]
            acc = None
            for hh in range(2):
                s = lax.dot_general(q, Km[hh][keys], NT, preferred_element_type=F32) + bias
                p = jnp.exp(s - _head_col(lse_v, hh))
                dp = lax.dot_general(do, Vm[hh][keys], NT, preferred_element_type=F32)
                ds = p * (dp - _head_col(dl_v, hh))
                t = jnp.dot(ds.astype(BF16), Km[hh][keys], preferred_element_type=F32)
                acc = t if acc is None else acc + t
            dq_ref[rows, :] = acc.astype(dq_ref.dtype)

    nspec = _band_specs(0, tq, nlb)[1]
    return pl.pallas_call(
        body, name=name, grid=(dil, L // tq, A_WIDTH // LANES),
        in_specs=[nspec] + _band_specs(1, tq, nlb) + _band_specs(2, tq, nlb) + [nspec, nspec, nspec, _const_spec(band.shape)],
        out_specs=nspec,
        out_shape=jax.ShapeDtypeStruct((S, A_WIDTH), BF16),
        compiler_params=_params(("parallel", "parallel", "parallel")),
    )(*([qkvp] * 7), dout, lse, delta, band)


def attn_bwd_dkv(qkvp, dout, lse, delta, dil, *, name):
    S = qkvp.shape[0]
    L = S // dil
    tk = _tile(L, A_BLOCK)
    sub = min(tk, A_SUB)
    nsub = tk // sub
    nlb = L // A_HALF
    band = _band_bias(sub, False)

    def body(qp_ref, qo_ref, qn_ref, k_ref, v_ref, dp_ref, do_ref, dn_ref, lp_ref, lo_ref, ln_ref,
             ep_ref, eo_ref, en_ref, band_ref, dk_ref, dv_ref):
        b = pl.program_id(1)
        Q = _cat3(qp_ref, qo_ref, qn_ref)
        DO = _cat3(dp_ref, do_ref, dn_ref)
        lse_v = _cat3(lp_ref, lo_ref, ln_ref)
        dl_v = _cat3(ep_ref, eo_ref, en_ref)
        lo_q = _lane_lo(tk + 2 * A_HALF)
        Qm = [jnp.where(sel, Q, jnp.zeros_like(Q)) for sel in (lo_q, ~lo_q)]
        DOm = [jnp.where(sel, DO, jnp.zeros_like(DO)) for sel in (lo_q, ~lo_q)]
        for r in range(nsub):
            keys = slice(r * sub, (r + 1) * sub)
            qs = slice(r * sub, (r + 1) * sub + 2 * A_HALF)
            bias = _with_edge(band_ref[...], _edge_bias(b * tk + r * sub, sub + 2 * A_HALF, L, 1, r == 0, r == nsub - 1))
            K, V = k_ref[keys, :], v_ref[keys, :]
            lse_t, dl_t = lse_v[qs].T, dl_v[qs].T
            dk = dv = None
            for hh in range(2):
                hr = slice(hh * A_HEAD_DIM, hh * A_HEAD_DIM + 1)
                st = lax.dot_general(K, Qm[hh][qs], NT, preferred_element_type=F32) + bias
                pt = jnp.exp(st - lse_t[hr, :])
                dpt = lax.dot_general(V, DOm[hh][qs], NT, preferred_element_type=F32)
                dst = pt * (dpt - dl_t[hr, :])
                tv = jnp.dot(pt.astype(BF16), DOm[hh][qs], preferred_element_type=F32)
                tk_ = jnp.dot(dst.astype(BF16), Qm[hh][qs], preferred_element_type=F32)
                dv = tv if dv is None else dv + tv
                dk = tk_ if dk is None else dk + tk_
            dk_ref[keys, :] = dk.astype(dk_ref.dtype)
            dv_ref[keys, :] = dv.astype(dv_ref.dtype)

    nspec = _band_specs(0, tk, nlb)[1]
    return pl.pallas_call(
        body, name=name, grid=(dil, L // tk, A_WIDTH // LANES),
        in_specs=_band_specs(0, tk, nlb) + [_band_specs(1, tk, nlb)[1], _band_specs(2, tk, nlb)[1]]
        + _band_specs(0, tk, nlb) * 3 + [_const_spec(band.shape)],
        out_specs=[nspec, nspec],
        out_shape=[jax.ShapeDtypeStruct((S, A_WIDTH), BF16)] * 2,
        compiler_params=_params(("parallel", "parallel", "parallel")),
    )(*([qkvp] * 5), *([dout] * 3), *([lse] * 3), *([delta] * 3), band)


def _gate_block_weight(wf, wb):
    w = jnp.zeros((LANES, 2 * B_QK_WIDTH), F32)
    w = w.at[:B_GATE_RANK, :B_QK_WIDTH].set(wf)
    w = w.at[B_GATE_RANK:2 * B_GATE_RANK, B_QK_WIDTH:].set(wb)
    return w.astype(BF16)


def gate_fwd(proj, wblk, bias, *, name):
    S = proj.shape[0]
    ts = _tile(S, 512)
    W = 2 * B_QK_WIDTH
    zcol = (2 * B_QK_WIDTH + 2 * B_V_WIDTH) // LANES

    def body(z_ref, w_ref, b_ref, o_ref):
        x = jnp.dot(z_ref[...].astype(BF16), w_ref[...], preferred_element_type=F32) + b_ref[...]
        o_ref[...] = (jnp.minimum(x, 0.0) - jnp.log(1.0 + jnp.exp(-jnp.abs(x)))) * (1.0 / B_GATE_TAU)

    return pl.pallas_call(
        body, name=name, grid=(S // ts,),
        in_specs=[pl.BlockSpec((ts, LANES), lambda i: (i, zcol)), _const_spec((LANES, W)), _const_spec((1, W))],
        out_specs=pl.BlockSpec((ts, W), lambda i: (i, 0)),
        out_shape=jax.ShapeDtypeStruct((S, W), F32),
        compiler_params=_params(("parallel",)),
    )(proj, wblk, bias)


def gate_bwd(proj, wblk, wblk_t, bias, dloga, *, name):
    S = proj.shape[0]
    ts = _tile(S, 512)
    W = 2 * B_QK_WIDTH
    zcol = (2 * B_QK_WIDTH + 2 * B_V_WIDTH) // LANES

    def body(z_ref, w_ref, wt_ref, b_ref, d_ref, dz_ref, dw_ref, db_ref):
        @pl.when(pl.program_id(0) == 0)
        def _():
            dw_ref[...] = jnp.zeros_like(dw_ref)
            db_ref[...] = jnp.zeros_like(db_ref)

        z = z_ref[...].astype(BF16)
        x = jnp.dot(z, w_ref[...], preferred_element_type=F32) + b_ref[...]
        e = jnp.exp(-jnp.abs(x))
        sig_neg = jnp.where(x >= 0, e, 1.0) / (1.0 + e)
        dx = d_ref[...] * (1.0 / B_GATE_TAU) * sig_neg
        dxb = dx.astype(BF16)
        dz_ref[...] = jnp.dot(dxb, wt_ref[...], preferred_element_type=F32)
        dw_ref[...] += lax.dot_general(z, dxb, TN, preferred_element_type=F32)
        db_ref[0:1, :] += jnp.sum(dx, axis=0, keepdims=True)

    return pl.pallas_call(
        body, name=name, grid=(S // ts,),
        in_specs=[pl.BlockSpec((ts, LANES), lambda i: (i, zcol)), _const_spec((LANES, W)), _const_spec((W, LANES)),
                  _const_spec((1, W)), pl.BlockSpec((ts, W), lambda i: (i, 0))],
        out_specs=[pl.BlockSpec((ts, LANES), lambda i: (i, 0)), _const_spec((LANES, W)), _const_spec((8, W))],
        out_shape=[jax.ShapeDtypeStruct((S, LANES), F32), jax.ShapeDtypeStruct((LANES, W), F32),
                   jax.ShapeDtypeStruct((8, W), F32)],
        compiler_params=_params(("arbitrary",)),
    )(proj, wblk, wblk_t, bias, dloga)


def _tri(reverse):
    i = np.arange(B_CHUNK)
    t = (i[None, :] >= i[:, None]) if reverse else (i[None, :] <= i[:, None])
    return jnp.asarray(t.astype(np.float32))


def _gla_terms(q, k, la, t_ref, reverse):
    b = jnp.dot(t_ref[...], la, precision=HI, preferred_element_type=F32)
    b_last = b[0:1, :] if reverse else b[B_CHUNK - 1:B_CHUNK, :]
    e_b = jnp.exp(b)
    qt = (q * (B_KEY_DIM ** -0.5)) * e_b
    e_nb = jnp.exp(-b)
    kt = k * e_nb
    e_end = jnp.exp(b_last - b)
    kend = k * e_end
    dec = jnp.exp(b_last)
    return e_nb, e_b, qt, kt, e_end, kend, dec


def _chunk_mask(reverse, transpose=False):
    r = lax.broadcasted_iota(jnp.int32, (B_CHUNK, B_CHUNK), 0)
    c = lax.broadcasted_iota(jnp.int32, (B_CHUNK, B_CHUNK), 1)
    if transpose:
        r, c = c, r
    return (c > r) if reverse else (c <= r)


def gla_fwd(proj, loga, tmat, reverse, *, name):
    S = proj.shape[0]
    tb = _tile(S, 512)
    nb = S // tb
    cpb = tb // B_CHUNK
    nc = S // B_CHUNK
    lb = 1 if reverse else 0
    blk = (lambda i: nb - 1 - i) if reverse else (lambda i: i)

    def body(qk_ref, v_ref, la_ref, t_ref, o_ref, st_ref, s_scr):
        @pl.when(pl.program_id(0) == 0)
        def _():
            s_scr[...] = jnp.zeros_like(s_scr)

        mask = _chunk_mask(reverse)
        order = range(cpb - 1, -1, -1) if reverse else range(cpb)
        for c in order:
            rows = slice(c * B_CHUNK, (c + 1) * B_CHUNK)
            _, _, qt, kt, _, kend, dec = _gla_terms(qk_ref[rows, :B_QK_WIDTH], qk_ref[rows, B_QK_WIDTH:],
                                                   la_ref[rows, :], t_ref, reverse)
            qt, kt, kend = qt.astype(BF16), kt.astype(BF16), kend.astype(BF16)
            for h in range(B_HEADS):
                kc = slice(h * B_KEY_DIM, (h + 1) * B_KEY_DIM)
                vc = slice(h * B_VAL_DIM, (h + 1) * B_VAL_DIM)
                v = v_ref[rows, vc].astype(BF16)
                st = s_scr[h]
                st_ref[h, c] = st
                a = jnp.where(mask, lax.dot_general(qt[:, kc], kt[:, kc], NT, preferred_element_type=F32), 0.0)
                o = jnp.dot(a.astype(BF16), v, preferred_element_type=F32)
                o = o + lax.dot_general(qt[:, kc], st.astype(BF16), NT, preferred_element_type=F32)
                o_ref[rows, vc] = o
                s_scr[h] = st * dec[:, kc] + lax.dot_general(v, kend[:, kc], TN, preferred_element_type=F32)

    return pl.pallas_call(
        body, name=name, grid=(nb,),
        in_specs=[pl.BlockSpec((tb, 2 * B_QK_WIDTH), lambda i: (blk(i), 0)),
                  pl.BlockSpec((tb, B_V_WIDTH), lambda i: (blk(i), 1)),
                  pl.BlockSpec((tb, B_QK_WIDTH), lambda i: (blk(i), lb)),
                  _const_spec((B_CHUNK, B_CHUNK))],
        out_specs=[pl.BlockSpec((tb, B_V_WIDTH), lambda i: (blk(i), 0)),
                   pl.BlockSpec((B_HEADS, cpb, B_VAL_DIM, B_KEY_DIM), lambda i: (0, blk(i), 0, 0))],
        out_shape=[jax.ShapeDtypeStruct((S, B_V_WIDTH), F32),
                   jax.ShapeDtypeStruct((B_HEADS, nc, B_VAL_DIM, B_KEY_DIM), F32)],
        scratch_shapes=[pltpu.VMEM((B_HEADS, B_VAL_DIM, B_KEY_DIM), F32)],
        compiler_params=_params(("arbitrary",)),
    )(proj, proj, loga, tmat)


def gla_bwd(proj, loga, states, do, tmat, tmat_t, reverse, *, name):
    S = proj.shape[0]
    tb = _tile(S, 256)
    nb = S // tb
    cpb = tb // B_CHUNK
    lb = 1 if reverse else 0
    blk = (lambda i: i) if reverse else (lambda i: nb - 1 - i)
    scale = B_KEY_DIM ** -0.5

    def body(qk_ref, v_ref, la_ref, st_ref, do_ref, t_ref, tt_ref, dq_ref, dk_ref, dv_ref, dla_ref, ds_scr):
        @pl.when(pl.program_id(0) == 0)
        def _():
            ds_scr[...] = jnp.zeros_like(ds_scr)

        mask = _chunk_mask(reverse)
        mask_t = _chunk_mask(reverse, transpose=True)
        last = 0 if reverse else B_CHUNK - 1
        is_last = lax.broadcasted_iota(jnp.int32, (B_CHUNK, B_QK_WIDTH), 0) == last
        order = range(cpb) if reverse else range(cpb - 1, -1, -1)
        for c in order:
            rows = slice(c * B_CHUNK, (c + 1) * B_CHUNK)
            e_nb, e_b, qt, kt, e_end, kend, dec = _gla_terms(qk_ref[rows, :B_QK_WIDTH], qk_ref[rows, B_QK_WIDTH:],
                                                             la_ref[rows, :], t_ref, reverse)
            qtb, ktb, kendb = qt.astype(BF16), kt.astype(BF16), kend.astype(BF16)
            dqt_h, dkt_h, dkend_h, ddec_h = [], [], [], []
            for h in range(B_HEADS):
                kc = slice(h * B_KEY_DIM, (h + 1) * B_KEY_DIM)
                vc = slice(h * B_VAL_DIM, (h + 1) * B_VAL_DIM)
                v = v_ref[rows, vc].astype(BF16)
                dob = do_ref[rows, vc].astype(BF16)
                st = st_ref[h, c]
                dst = ds_scr[h]
                dstb = dst.astype(BF16)
                a_t = jnp.where(mask_t, lax.dot_general(ktb[:, kc], qtb[:, kc], NT, preferred_element_type=F32), 0.0)
                da = jnp.where(mask, lax.dot_general(dob, v, NT, preferred_element_type=F32), 0.0)
                da_t = jnp.where(mask_t, lax.dot_general(v, dob, NT, preferred_element_type=F32), 0.0)
                dv = jnp.dot(a_t.astype(BF16), dob, preferred_element_type=F32)
                dv_ref[rows, vc] = dv + lax.dot_general(kendb[:, kc], dstb, NT, preferred_element_type=F32)
                dqt = jnp.dot(da.astype(BF16), ktb[:, kc], preferred_element_type=F32)
                dqt_h.append(dqt + jnp.dot(dob, st.astype(BF16), preferred_element_type=F32))
                dkt_h.append(jnp.dot(da_t.astype(BF16), qtb[:, kc], preferred_element_type=F32))
                dkend_h.append(jnp.dot(v, dstb, preferred_element_type=F32))
                ddec_h.append(jnp.sum(dst * st, axis=0, keepdims=True))
                ds_scr[h] = dst * dec[:, kc] + lax.dot_general(dob, qtb[:, kc], TN, preferred_element_type=F32)
            dqt, dkt = jnp.concatenate(dqt_h, axis=1), jnp.concatenate(dkt_h, axis=1)
            dkend, ddec = jnp.concatenate(dkend_h, axis=1), jnp.concatenate(ddec_h, axis=1)
            ke = dkend * kend
            db = dqt * qt - dkt * kt - ke
            db_last = jnp.sum(ke, axis=0, keepdims=True) + ddec * dec
            db = db + jnp.where(is_last, db_last, 0.0)
            dq_ref[rows, :] = dqt * e_b * scale
            dk_ref[rows, :] = dkt * e_nb + dkend * e_end
            dla_ref[rows, :] = jnp.dot(tt_ref[...], db, precision=HI, preferred_element_type=F32)

    qspec = pl.BlockSpec((tb, B_QK_WIDTH), lambda i: (blk(i), 0))
    vspec = pl.BlockSpec((tb, B_V_WIDTH), lambda i: (blk(i), 0))
    return pl.pallas_call(
        body, name=name, grid=(nb,),
        in_specs=[pl.BlockSpec((tb, 2 * B_QK_WIDTH), lambda i: (blk(i), 0)),
                  pl.BlockSpec((tb, B_V_WIDTH), lambda i: (blk(i), 1)),
                  pl.BlockSpec((tb, B_QK_WIDTH), lambda i: (blk(i), lb)),
                  pl.BlockSpec((B_HEADS, cpb, B_VAL_DIM, B_KEY_DIM), lambda i: (0, blk(i), 0, 0)),
                  vspec, _const_spec((B_CHUNK, B_CHUNK)), _const_spec((B_CHUNK, B_CHUNK))],
        out_specs=[qspec, qspec, vspec, qspec],
        out_shape=[jax.ShapeDtypeStruct((S, B_QK_WIDTH), F32), jax.ShapeDtypeStruct((S, B_QK_WIDTH), F32),
                   jax.ShapeDtypeStruct((S, B_V_WIDTH), F32), jax.ShapeDtypeStruct((S, B_QK_WIDTH), F32)],
        scratch_shapes=[pltpu.VMEM((B_HEADS, B_VAL_DIM, B_KEY_DIM), F32)],
        compiler_params=_params(("arbitrary",)),
    )(proj, proj, loga, states, do, tmat, tmat_t)


def gla_post_fwd(o_f, o_b, gain, proj, *, name):
    S = o_f.shape[0]
    ts = _tile(S, 512)
    rcol = (2 * B_QK_WIDTH + B_V_WIDTH) // B_V_WIDTH

    def body(f_ref, b_ref, g_ref, r_ref, y_ref):
        for h in range(B_HEADS):
            sl = slice(h * B_VAL_DIM, (h + 1) * B_VAL_DIM)
            o = f_ref[:, sl] + b_ref[:, sl]
            n = (o * lax.rsqrt(jnp.mean(o * o, axis=-1, keepdims=True) + RMS_EPS)) * g_ref[:, sl]
            r = r_ref[:, sl]
            y_ref[:, sl] = (n * (r * (1.0 / (1.0 + jnp.exp(-r))))).astype(y_ref.dtype)

    spec = pl.BlockSpec((ts, B_V_WIDTH), lambda i: (i, 0))
    return pl.pallas_call(
        body, name=name, grid=(S // ts,),
        in_specs=[spec, spec, _const_spec((1, B_V_WIDTH)), pl.BlockSpec((ts, B_V_WIDTH), lambda i: (i, rcol))],
        out_specs=spec, out_shape=jax.ShapeDtypeStruct((S, B_V_WIDTH), BF16),
        compiler_params=_params(("parallel",)),
    )(o_f, o_b, gain, proj)


def gla_post_bwd(o_f, o_b, gain, proj, dy, *, name):
    S = o_f.shape[0]
    ts = _tile(S, 512)
    rcol = (2 * B_QK_WIDTH + B_V_WIDTH) // B_V_WIDTH

    def body(f_ref, b_ref, g_ref, r_ref, dy_ref, do_ref, dr_ref, dg_ref):
        @pl.when(pl.program_id(0) == 0)
        def _():
            dg_ref[...] = jnp.zeros_like(dg_ref)

        for h in range(B_HEADS):
            sl = slice(h * B_VAL_DIM, (h + 1) * B_VAL_DIM)
            o = f_ref[:, sl] + b_ref[:, sl]
            rs = lax.rsqrt(jnp.mean(o * o, axis=-1, keepdims=True) + RMS_EPS)
            ohat = o * rs
            g = g_ref[:, sl]
            r = r_ref[:, sl]
            sig = 1.0 / (1.0 + jnp.exp(-r))
            dyv = dy_ref[:, sl].astype(F32)
            dn = dyv * (r * sig)
            dr_ref[:, sl] = dyv * (ohat * g) * (sig * (1.0 + r * (1.0 - sig)))
            dng = dn * g
            do_ref[:, sl] = rs * (dng - ohat * jnp.mean(dng * ohat, axis=-1, keepdims=True))
            dg_ref[0:1, sl] += jnp.sum(dn * ohat, axis=0, keepdims=True)

    spec = pl.BlockSpec((ts, B_V_WIDTH), lambda i: (i, 0))
    return pl.pallas_call(
        body, name=name, grid=(S // ts,),
        in_specs=[spec, spec, _const_spec((1, B_V_WIDTH)), pl.BlockSpec((ts, B_V_WIDTH), lambda i: (i, rcol)), spec],
        out_specs=[spec, spec, _const_spec((8, B_V_WIDTH))],
        out_shape=[jax.ShapeDtypeStruct((S, B_V_WIDTH), F32), jax.ShapeDtypeStruct((S, B_V_WIDTH), F32),
                   jax.ShapeDtypeStruct((8, B_V_WIDTH), F32)],
        compiler_params=_params(("arbitrary",)),
    )(o_f, o_b, gain, proj, dy)


def gla_combine(parts_f, parts_b, dr, dz, *, name):
    S = dr.shape[0]
    ts = _tile(S, 512)

    def body(qf, kf, vf, qb, kb, vb, r_ref, z_ref, o_ref):
        o_ref[:, 0:512] = (qf[...] + qb[...]).astype(o_ref.dtype)
        o_ref[:, 512:1024] = (kf[...] + kb[...]).astype(o_ref.dtype)
        o_ref[:, 1024:2048] = (vf[...] + vb[...]).astype(o_ref.dtype)
        o_ref[:, 2048:3072] = r_ref[...].astype(o_ref.dtype)
        o_ref[:, 3072:3200] = z_ref[...].astype(o_ref.dtype)

    s512 = pl.BlockSpec((ts, B_QK_WIDTH), lambda i: (i, 0))
    s1024 = pl.BlockSpec((ts, B_V_WIDTH), lambda i: (i, 0))
    return pl.pallas_call(
        body, name=name, grid=(S // ts,),
        in_specs=[s512, s512, s1024, s512, s512, s1024, s1024, pl.BlockSpec((ts, LANES), lambda i: (i, 0))],
        out_specs=pl.BlockSpec((ts, B_IN_PAD), lambda i: (i, 0)),
        out_shape=jax.ShapeDtypeStruct((S, B_IN_PAD), BF16),
        compiler_params=_params(("parallel",)),
    )(*parts_f, *parts_b, dr, dz)


def adamw(w, g, m, v, *, name):
    R, C = w.shape
    tr = _tile(R, 256)
    c1 = 1.0 / (1.0 - ADAM_B1 ** ADAM_STEP)
    c2 = 1.0 / (1.0 - ADAM_B2 ** ADAM_STEP)

    def body(w_ref, g_ref, m_ref, v_ref, d_ref, mo_ref, vo_ref):
        gv = g_ref[...]
        mn = ADAM_B1 * m_ref[...] + (1.0 - ADAM_B1) * gv
        vn = ADAM_B2 * v_ref[...] + (1.0 - ADAM_B2) * (gv * gv)
        mo_ref[...] = mn
        vo_ref[...] = vn
        d_ref[...] = -ADAM_LR * ((mn * c1) / (jnp.sqrt(vn * c2) + ADAM_EPS) + ADAM_WD * w_ref[...])

    spec = pl.BlockSpec((tr, C), lambda i: (i, 0))
    return pl.pallas_call(
        body, name=name, grid=(R // tr,), in_specs=[spec] * 4, out_specs=[spec] * 3,
        out_shape=[jax.ShapeDtypeStruct((R, C), F32)] * 3,
        compiler_params=_params(("parallel",)),
    )(w, g, m, v)


def _chip_peers():
    x, y, c = lax.axis_index("x"), lax.axis_index("y"), lax.axis_index("c")
    return x, y, c, [(1 - x, y), (x, 1 - y), (1 - x, 1 - y)]


_ANY = pl.BlockSpec(memory_space=pl.ANY)


def gather_shards(src, *, name):
    _, R, C = src.shape

    def body(src_ref, out_ref, ici_send, ici_recv, d2d_send, d2d_recv):
        x, y, c, chips = _chip_peers()
        me = 2 * x + y
        sibling = (x, y, 1 - c)
        sends = []
        for k, (px, py) in enumerate(chips):
            cp = pltpu.make_async_remote_copy(
                src_ref=src_ref.at[c], dst_ref=out_ref.at[me, c], send_sem=ici_send.at[k], recv_sem=ici_recv.at[k],
                device_id=(px, py, c), device_id_type=MESH)
            cp.start()
            sends.append(cp)
        for k, (px, py) in enumerate(chips):
            landed = out_ref.at[2 * px + py, c]
            pltpu.make_async_remote_copy(
                src_ref=src_ref.at[c], dst_ref=landed, send_sem=ici_send.at[k], recv_sem=ici_recv.at[k],
                device_id=(px, py, c), device_id_type=MESH).wait_recv()
            cp = pltpu.make_async_remote_copy(
                src_ref=landed, dst_ref=landed, send_sem=d2d_send.at[k], recv_sem=d2d_recv.at[k],
                device_id=sibling, device_id_type=MESH)
            cp.start()
            sends.append(cp)
        for k, (px, py) in enumerate(chips):
            other_half = out_ref.at[2 * px + py, 1 - c]
            pltpu.make_async_remote_copy(
                src_ref=other_half, dst_ref=other_half, send_sem=d2d_send.at[k], recv_sem=d2d_recv.at[k],
                device_id=sibling, device_id_type=MESH).wait_recv()
        for cp in sends:
            cp.wait_send()

    return pl.pallas_call(
        body, name=name, in_specs=[_ANY], out_specs=_ANY,
        out_shape=jax.ShapeDtypeStruct((N_CHIPS, 2, R, C), src.dtype),
        scratch_shapes=[pltpu.SemaphoreType.DMA((3,))] * 4,
        compiler_params=pltpu.CompilerParams(has_side_effects=True),
    )(src)


def chip_exchange(srcs, *, name):
    n = len(srcs)

    def body(*refs):
        src_refs, out_refs = refs[:n], refs[n:2 * n]
        send_sems, recv_sems = refs[2 * n:]
        x, y, c, chips = _chip_peers()
        me = 2 * x + y
        copies = []
        for i, (src_ref, out_ref) in enumerate(zip(src_refs, out_refs)):
            for k, (px, py) in enumerate(chips):
                cp = pltpu.make_async_remote_copy(
                    src_ref=src_ref.at[2 * px + py], dst_ref=out_ref.at[me],
                    send_sem=send_sems.at[i, k], recv_sem=recv_sems.at[i, k],
                    device_id=(px, py, c), device_id_type=MESH)
                cp.start()
                copies.append(cp)
        for i, (src_ref, out_ref) in enumerate(zip(src_refs, out_refs)):
            for k, (px, py) in enumerate(chips):
                pltpu.make_async_remote_copy(
                    src_ref=src_ref.at[me], dst_ref=out_ref.at[2 * px + py],
                    send_sem=send_sems.at[i, k], recv_sem=recv_sems.at[i, k],
                    device_id=(px, py, c), device_id_type=MESH).wait_recv()
        for cp in copies:
            cp.wait_send()

    return pl.pallas_call(
        body, name=name, in_specs=[_ANY] * n, out_specs=[_ANY] * n,
        out_shape=[jax.ShapeDtypeStruct(s.shape, s.dtype) for s in srcs],
        scratch_shapes=[pltpu.SemaphoreType.DMA((n, 3)), pltpu.SemaphoreType.DMA((n, 3))],
        compiler_params=pltpu.CompilerParams(has_side_effects=True),
    )(*srcs)


def sibling_exchange(srcs, *, name):
    n = len(srcs)

    def body(*refs):
        src_refs, out_refs, (send_sems, recv_sems) = refs[:n], refs[n:2 * n], refs[2 * n:]
        x, y, c = lax.axis_index("x"), lax.axis_index("y"), lax.axis_index("c")
        copies = []
        for i, (src_ref, out_ref) in enumerate(zip(src_refs, out_refs)):
            cp = pltpu.make_async_remote_copy(
                src_ref=src_ref.at[:, 1 - c], dst_ref=out_ref, send_sem=send_sems.at[i], recv_sem=recv_sems.at[i],
                device_id=(x, y, 1 - c), device_id_type=MESH)
            cp.start()
            copies.append(cp)
        for cp in copies:
            cp.wait()

    return pl.pallas_call(
        body, name=name, in_specs=[_ANY] * n, out_specs=[_ANY] * n,
        out_shape=[jax.ShapeDtypeStruct((s.shape[0],) + s.shape[2:], s.dtype) for s in srcs],
        scratch_shapes=[pltpu.SemaphoreType.DMA((n,)), pltpu.SemaphoreType.DMA((n,))],
        compiler_params=pltpu.CompilerParams(has_side_effects=True),
    )(*srcs)


def sibling_share(bufs, *, name):
    n = len(bufs)

    def body(*refs):
        out_refs, (send_sems, recv_sems) = refs[n:2 * n], refs[2 * n:]
        x, y, c = lax.axis_index("x"), lax.axis_index("y"), lax.axis_index("c")
        copies = []
        for i, out_ref in enumerate(out_refs):
            cp = pltpu.make_async_remote_copy(
                src_ref=out_ref.at[c], dst_ref=out_ref.at[c], send_sem=send_sems.at[i], recv_sem=recv_sems.at[i],
                device_id=(x, y, 1 - c), device_id_type=MESH)
            cp.start()
            copies.append(cp)
        for cp in copies:
            cp.wait()

    return pl.pallas_call(
        body, name=name, in_specs=[_ANY] * n, out_specs=[_ANY] * n,
        out_shape=[jax.ShapeDtypeStruct(b.shape, b.dtype) for b in bufs],
        input_output_aliases={i: i for i in range(n)},
        scratch_shapes=[pltpu.SemaphoreType.DMA((n,))] * 2,
        compiler_params=pltpu.CompilerParams(has_side_effects=True),
    )(*bufs)


def add_pair(a, b, out_dtype, *, name):
    n, _, R, C = a.shape
    tr = _tile(R, 256)

    def body(c_ref, a_ref, b_ref, o_ref):
        o_ref[...] = (a_ref[0] + b_ref[...]).astype(o_ref.dtype)

    return pl.pallas_call(
        body, name=name,
        grid_spec=pltpu.PrefetchScalarGridSpec(
            num_scalar_prefetch=1, grid=(n, R // tr),
            in_specs=[pl.BlockSpec((1, 1, tr, C), lambda s, i, c_ref: (s, c_ref[0], i, 0)),
                      pl.BlockSpec((1, tr, C), lambda s, i, c_ref: (s, i, 0))],
            out_specs=pl.BlockSpec((1, tr, C), lambda s, i, c_ref: (s, i, 0))),
        out_shape=jax.ShapeDtypeStruct((n, R, C), out_dtype),
        compiler_params=_params(("parallel", "parallel")),
    )(lax.axis_index("c").reshape(1).astype(jnp.int32), a, b)


def sum_slots(slots, own, *, name):
    n, R, C = slots.shape
    tr = _tile(R, 256)

    def body(ids_ref, *refs):
        slot_refs, own_ref, o_ref = refs[:n], refs[n], refs[n + 1]
        me = ids_ref[0]
        acc = None
        for s, r in enumerate(slot_refs):
            t = jnp.where(me == s, own_ref[0], r[0]).astype(F32)
            acc = t if acc is None else acc + t
        o_ref[0] = acc

    def slot_spec(s):
        return pl.BlockSpec((1, tr, C), lambda i, ids: (jnp.where(ids[0] == s, (s + 1) % n, s), i, 0))

    x, y, c = lax.axis_index("x"), lax.axis_index("y"), lax.axis_index("c")
    return pl.pallas_call(
        body, name=name,
        grid_spec=pltpu.PrefetchScalarGridSpec(
            num_scalar_prefetch=1, grid=(R // tr,),
            in_specs=[slot_spec(s) for s in range(n)] + [pl.BlockSpec((1, tr, C), lambda i, ids: (ids[0], i, 0))],
            out_specs=pl.BlockSpec((1, tr, C), lambda i, ids: (ids[1], i, 0))),
        out_shape=jax.ShapeDtypeStruct((2, R, C), F32),
        compiler_params=_params(("parallel",)),
    )(jnp.stack([2 * x + y, c]).astype(jnp.int32), *([slots] * n), own)


def _flat_rows(n_elems, mult):
    rows = -(-n_elems // FLAT_COLS)
    return -(-rows // mult) * mult


def _to_flat(parts, mult):
    v = jnp.concatenate([p.reshape(-1) for p in parts])
    rows = _flat_rows(v.shape[0], mult)
    return jnp.pad(v, (0, rows * FLAT_COLS - v.shape[0])).reshape(rows, FLAT_COLS)


def _from_flat(flat, shapes):
    v = flat.reshape(-1)
    out, off = [], 0
    for s in shapes:
        n = int(np.prod(s))
        out.append(v[off:off + n].reshape(s))
        off += n
    return out


def _unshard(blocks, axis):
    return jnp.concatenate([blocks[s] for s in range(N_CHIPS)], axis=axis)


def _by_shard(full, axis):
    shp = full.shape
    cut = full.reshape(shp[:axis] + (N_CHIPS, shp[axis] // N_CHIPS) + shp[axis + 1:])
    return jnp.moveaxis(cut, axis, 0)


def _gradient_blocks(grads):
    blocks = []
    for n in MATRICES:
        t = _by_shard(jnp.stack(grads[n]), SHARD_AXIS[n])
        blocks.append(t.reshape(N_CHIPS, 2, -1, t.shape[-1]))
    rest = []
    for s in range(N_CHIPS):
        parts = [jnp.stack(grads[n]) if n in REPLICATED else _by_shard(jnp.stack(grads[n]), SHARD_AXIS[n])[s]
                 for n in WEIGHTS if n not in MATRICES]
        rest.append(_to_flat(parts, 16))
    rest = jnp.stack(rest)
    blocks.append(rest.reshape(N_CHIPS, 2, rest.shape[1] // 2, FLAT_COLS))
    return blocks


def _gather_weights(w):
    full = {}
    for names, dtype, mult, call in ((BIG, BF16, 32, "gather_weights"), (SMALL_SHARDED, F32, 16, "gather_vectors")):
        parts = [w[n].astype(dtype) for n in names]
        flat = _to_flat(parts, mult)
        got = gather_shards(flat.reshape(2, flat.shape[0] // 2, FLAT_COLS), name=call)
        got = got.reshape(N_CHIPS, flat.shape[0], FLAT_COLS)
        me = 2 * lax.axis_index("x") + lax.axis_index("y")
        got = lax.dynamic_update_slice(got, flat[None], (me, 0, 0))
        per_chip = [_from_flat(got[s], [p.shape for p in parts]) for s in range(N_CHIPS)]
        for i, n in enumerate(names):
            full[n] = _unshard([per_chip[s][i] for s in range(N_CHIPS)], SHARD_AXIS[n])
    return full


def _reduce_gradients(grads, w):
    halves = _gradient_blocks(grads)
    other = sibling_exchange(halves, name="grad_pair_exchange")
    pair = [add_pair(a, b, BF16 if i < len(MATRICES) else F32, name=f"grad_pair_add{i}")
            for i, (a, b) in enumerate(zip(halves, other))]
    slots = chip_exchange(pair, name="grad_chip_exchange")
    mine = [sum_slots(a, p, name=f"grad_chip_sum{i}") for i, (a, p) in enumerate(zip(slots, pair))]
    both = sibling_share(mine, name="grad_half_share")
    g = {n: t.reshape(w[n].shape) for n, t in zip(MATRICES, both)}
    rest = [n for n in WEIGHTS if n not in MATRICES]
    g.update(zip(rest, _from_flat(both[-1], [w[n].shape for n in rest])))
    return g


def _layer_fwd(i, h, p, aux):
    j = i // 2
    sv = {"h0": h}
    if i % 2 == 0:
        hns = rmsnorm_fwd(h, p["attn_norm"][i][None], dils=DILS, name=f"l{i}_norm1")
        qkvs, qkvps, os_, lses = [], [], [], []
        for g, d in enumerate(DILS):
            qkv = matmul(hns[g], p["a_w_in_g"][j][g], tn=3 * A_WIDTH // 2, name=f"l{i}_a_in{g}")
            qkvp = qk_prep_fwd(qkv, aux["a_gain"][j][g], aux["cos"][g], aux["sin"][g], aux["ones"], name=f"l{i}_a_prep{g}")
            o, l = attn_fwd(qkvp, d, name=f"l{i}_a_attn{g}")
            qkvs.append(qkv)
            qkvps.append(qkvp)
            os_.append(o)
            lses.append(l)
        out, lse = attn_merge(os_, lses, DILS, name=f"l{i}_a_merge")
        sv.update(hns=hns, qkv=qkvs, qkvp=qkvps, out=out, lse=lse)
        h1 = matmul(out, p["a_w_out"][j], res=h, name=f"l{i}_a_out")
    else:
        hn = rmsnorm_fwd(h, p["attn_norm"][i][None], name=f"l{i}_norm1")[0]
        sv["hn"] = hn
        proj = matmul(hn, p["b_w_in"][j], tn=640, name=f"l{i}_b_in")
        loga = gate_fwd(proj, aux["b_wblk"][j], aux["b_bias"][j], name=f"l{i}_b_gate")
        o_f, st_f = gla_fwd(proj, loga, aux["tri_f"], False, name=f"l{i}_b_gla_f")
        o_b, st_b = gla_fwd(proj, loga, aux["tri_b"], True, name=f"l{i}_b_gla_b")
        y = gla_post_fwd(o_f, o_b, aux["b_gain"][j], proj, name=f"l{i}_b_post")
        sv.update(proj=proj, loga=loga, o_f=o_f, o_b=o_b, st_f=st_f, st_b=st_b, y=y)
        h1 = matmul(y, p["b_w_out"][j], res=h, name=f"l{i}_b_out")
    sv["h1"] = h1
    hn2 = rmsnorm_fwd(h1, p["ffn_norm"][i][None], name=f"l{i}_norm2")[0]
    gu = matmul(hn2, p["ffn_w_gate_up"][i], out_dtype=BF16, tn=FFN_HIDDEN // 2, name=f"l{i}_f_up")
    act = swiglu_fwd(gu, name=f"l{i}_f_act")
    h2 = matmul(act, p["ffn_w_down"][i], res=h1, tk=2816, name=f"l{i}_f_down")
    sv.update(hn2=hn2, gu=gu, act=act)
    return h2, sv


def _layer_bwd(i, dh, p, pt, aux, sv, grads):
    j = i // 2
    dhb = dh.astype(BF16)
    grads["ffn_w_down"][i] = matmul(sv["act"].T, dhb, tm=FFN_HIDDEN // 2, tk=2048, name=f"l{i}_f_down_dw")
    dact = matmul(dhb, pt["ffn_w_down"][i], out_dtype=BF16, tn=FFN_HIDDEN // 2, name=f"l{i}_f_down_dx")
    dgu = swiglu_bwd(sv["gu"], dact, name=f"l{i}_f_act_bwd")
    grads["ffn_w_gate_up"][i] = matmul(sv["hn2"].T, dgu, tk=2048, name=f"l{i}_f_up_dw")
    dhn2 = matmul(dgu, pt["ffn_w_gate_up"][i], tk=2816, name=f"l{i}_f_up_dx")
    dh1, dg = rmsnorm_bwd(sv["h1"], p["ffn_norm"][i][None], [dhn2], dh, name=f"l{i}_norm2_bwd")
    grads["ffn_norm"][i] = dg[0]
    dh1b = dh1.astype(BF16)
    if i % 2 == 0:
        grads["a_w_out"][j] = matmul(sv["out"].T, dh1b, tk=2048, name=f"l{i}_a_out_dw")
        dout = matmul(dh1b, pt["a_w_out"][j], out_dtype=BF16, name=f"l{i}_a_out_dx")
        douts, deltas = attn_delta(dout, sv["out"], aux["ones"], DILS, name=f"l{i}_a_delta")
        dws, dhns, dgq, dgk = [], [], [], []
        for g, d in enumerate(DILS):
            qkvp, lse = sv["qkvp"][g], sv["lse"][g]
            dq = attn_bwd_dq(qkvp, douts[g], lse, deltas[g], d, name=f"l{i}_a_dq{g}")
            dk, dv = attn_bwd_dkv(qkvp, douts[g], lse, deltas[g], d, name=f"l{i}_a_dkv{g}")
            dqkv, dgain = qk_prep_bwd(sv["qkv"][g], aux["a_gain"][j][g], aux["cos"][g], aux["sin"][g], aux["ones"],
                                      [dq, dk, dv], name=f"l{i}_a_prep_bwd{g}")
            dgh = dgain[0].reshape(3, A_HEADS, A_HEAD_DIM).sum(axis=1)
            dgq.append(dgh[0])
            dgk.append(dgh[1])
            dws.append(matmul(sv["hns"][g].T, dqkv, tk=2048, name=f"l{i}_a_in_dw{g}"))
            dhns.append(matmul(dqkv, pt["a_w_in_g"][j][g], tk=3072, name=f"l{i}_a_in_dx{g}"))
        grads["a_q_norm"][j] = jnp.stack(dgq)
        grads["a_k_norm"][j] = jnp.stack(dgk)
        grads["a_w_in"][j] = jnp.concatenate(dws, axis=1)
        dh0, dg = rmsnorm_bwd(sv["h0"], p["attn_norm"][i][None], dhns, dh1, dils=DILS, name=f"l{i}_norm1_bwd")
    else:
        grads["b_w_out"][j] = matmul(sv["y"].T, dh1b, tk=2048, name=f"l{i}_b_out_dw")
        dy = matmul(dh1b, pt["b_w_out"][j], name=f"l{i}_b_out_dx")
        do, dr, dgn = gla_post_bwd(sv["o_f"], sv["o_b"], aux["b_gain"][j], sv["proj"], dy, name=f"l{i}_b_post_bwd")
        grads["b_out_norm"][j] = dgn[0].reshape(B_HEADS, B_VAL_DIM)
        pf = gla_bwd(sv["proj"], sv["loga"], sv["st_f"], do, aux["tri_f"], aux["tri_b"], False, name=f"l{i}_b_gla_f_bwd")
        pb = gla_bwd(sv["proj"], sv["loga"], sv["st_b"], do, aux["tri_b"], aux["tri_f"], True, name=f"l{i}_b_gla_b_bwd")
        dloga = jnp.concatenate([pf[3], pb[3]], axis=1)
        dz, dwblk, dbias = gate_bwd(sv["proj"], aux["b_wblk"][j], aux["b_wblk_t"][j], aux["b_bias"][j], dloga,
                                    name=f"l{i}_b_gate_bwd")
        grads["b_w_gate_f"][j] = dwblk[:B_GATE_RANK, :B_QK_WIDTH]
        grads["b_w_gate_b"][j] = dwblk[B_GATE_RANK:2 * B_GATE_RANK, B_QK_WIDTH:]
        grads["b_gate_bias_f"][j] = dbias[0, :B_QK_WIDTH]
        grads["b_gate_bias_b"][j] = dbias[0, B_QK_WIDTH:]
        dproj = gla_combine(pf[:3], pb[:3], dr, dz, name=f"l{i}_b_combine")
        grads["b_w_in"][j] = matmul(sv["hn"].T, dproj, tn=640, tk=2048, name=f"l{i}_b_in_dw")[:, :B_IN_WIDTH]
        dhn = matmul(dproj, pt["b_w_in"][j], tk=B_IN_PAD, name=f"l{i}_b_in_dx")
        dh0, dg = rmsnorm_bwd(sv["h0"], p["attn_norm"][i][None], [dhn], dh1, name=f"l{i}_norm1_bwd")
    grads["attn_norm"][i] = dg[0]
    return dh0


def _local_step(x, target, p, small):
    S = x.shape[0]
    cos, sin = _rope_tables(S)
    to_phase = lambda t, d: t.reshape(S // d, d, LANES).swapaxes(0, 1).reshape(S, LANES)
    cos, sin = [to_phase(cos, d) for d in DILS], [to_phase(sin, d) for d in DILS]
    ones_v = jnp.ones((A_WIDTH,), F32)
    a_gain = [[jnp.concatenate([jnp.tile(small["a_q_norm"][j][g], A_HEADS), jnp.tile(small["a_k_norm"][j][g], A_HEADS),
                                ones_v])[None] for g in range(len(DILS))] for j in range(2)]
    b_wblk = [_gate_block_weight(p["b_w_gate_f"][j].astype(F32), p["b_w_gate_b"][j].astype(F32)) for j in range(2)]
    aux = dict(cos=cos, sin=sin, ones=_head_block_ones(), a_gain=a_gain, tri_f=_tri(False), tri_b=_tri(True),
               b_wblk=b_wblk, b_wblk_t=[w.T for w in b_wblk],
               b_bias=[jnp.concatenate([small["b_gate_bias_f"][j], small["b_gate_bias_b"][j]])[None] for j in range(2)],
               b_gain=[small["b_out_norm"][j].reshape(1, B_V_WIDTH) for j in range(2)])
    pw = dict(p)
    pw["b_w_in"] = jnp.pad(p["b_w_in"], ((0, 0), (0, 0), (0, B_IN_PAD - B_IN_WIDTH)))
    pw["attn_norm"], pw["ffn_norm"] = small["attn_norm"], small["ffn_norm"]
    gw = 3 * A_WIDTH
    pw["a_w_in_g"] = [[p["a_w_in"][j][:, g * gw:(g + 1) * gw] for g in range(len(DILS))] for j in range(2)]
    pt = {n: jnp.swapaxes(pw[n], 1, 2) for n in ("a_w_out", "b_w_in", "b_w_out", "ffn_w_gate_up", "ffn_w_down")}
    pt["a_w_in_g"] = [[wg.T for wg in row] for row in pw["a_w_in_g"]]

    h = x
    saved = []
    for i in range(DEPTH):
        h, sv = _layer_fwd(i, h, pw, aux)
        saved.append(sv)
    loss_sq, dh = loss_head(h, target, name="loss_head")
    grads = {n: [None] * (DEPTH if n in ("attn_norm", "ffn_norm", "ffn_w_gate_up", "ffn_w_down") else 2) for n in WEIGHTS}
    for i in reversed(range(DEPTH)):
        dh = _layer_bwd(i, dh, pw, pt, aux, saved[i], grads)
    return loss_sq[0, 0] * (0.5 / D_MODEL), dh, grads


def kernel(x, attn_norm, ffn_norm, a_w_in, a_q_norm, a_k_norm, a_w_out, b_w_in, b_w_gate_f, b_gate_bias_f, b_w_gate_b, b_gate_bias_b, b_out_norm, b_w_out, ffn_w_gate_up, ffn_w_down, loss_target, m_attn_norm, m_ffn_norm, m_a_w_in, m_a_q_norm, m_a_k_norm, m_a_w_out, m_b_w_in, m_b_w_gate_f, m_b_gate_bias_f, m_b_w_gate_b, m_b_gate_bias_b, m_b_out_norm, m_b_w_out, m_ffn_w_gate_up, m_ffn_w_down, v_attn_norm, v_ffn_norm, v_a_w_in, v_a_q_norm, v_a_k_norm, v_a_w_out, v_b_w_in, v_b_w_gate_f, v_b_gate_bias_f, v_b_w_gate_b, v_b_gate_bias_b, v_b_out_norm, v_b_w_out, v_ffn_w_gate_up, v_ffn_w_down):
    w = dict(attn_norm=attn_norm, ffn_norm=ffn_norm, a_w_in=a_w_in, a_q_norm=a_q_norm, a_k_norm=a_k_norm, a_w_out=a_w_out,
             b_w_in=b_w_in, b_w_gate_f=b_w_gate_f, b_gate_bias_f=b_gate_bias_f, b_w_gate_b=b_w_gate_b,
             b_gate_bias_b=b_gate_bias_b, b_out_norm=b_out_norm, b_w_out=b_w_out, ffn_w_gate_up=ffn_w_gate_up,
             ffn_w_down=ffn_w_down)
    m = dict(attn_norm=m_attn_norm, ffn_norm=m_ffn_norm, a_w_in=m_a_w_in, a_q_norm=m_a_q_norm, a_k_norm=m_a_k_norm,
             a_w_out=m_a_w_out, b_w_in=m_b_w_in, b_w_gate_f=m_b_w_gate_f, b_gate_bias_f=m_b_gate_bias_f,
             b_w_gate_b=m_b_w_gate_b, b_gate_bias_b=m_b_gate_bias_b, b_out_norm=m_b_out_norm, b_w_out=m_b_w_out,
             ffn_w_gate_up=m_ffn_w_gate_up, ffn_w_down=m_ffn_w_down)
    v = dict(attn_norm=v_attn_norm, ffn_norm=v_ffn_norm, a_w_in=v_a_w_in, a_q_norm=v_a_q_norm, a_k_norm=v_a_k_norm,
             a_w_out=v_a_w_out, b_w_in=v_b_w_in, b_w_gate_f=v_b_w_gate_f, b_gate_bias_f=v_b_gate_bias_f,
             b_w_gate_b=v_b_w_gate_b, b_gate_bias_b=v_b_gate_bias_b, b_out_norm=v_b_out_norm, b_w_out=v_b_w_out,
             ffn_w_gate_up=v_ffn_w_gate_up, ffn_w_down=v_ffn_w_down)

    full = _gather_weights(w)
    p = {n: full[n] for n in BIG}
    small = {n: full[n] for n in SMALL_SHARDED}
    small.update({n: w[n] for n in REPLICATED})
    loss_local, dx, grads = _local_step(x[0], loss_target[0], p, small)
    loss = lax.psum(loss_local, ("x", "y", "c"))

    g = _reduce_gradients(grads, w)
    delta, new_m, new_v = {}, {}, {}
    rows = lambda t: t.reshape(-1, t.shape[-1])
    for n in MATRICES:
        outs = adamw(rows(w[n]), rows(g[n]), rows(m[n]), rows(v[n]), name=f"adamw_{n}")
        delta[n], new_m[n], new_v[n] = [o.reshape(w[n].shape) for o in outs]
    rest = [n for n in WEIGHTS if n not in MATRICES]
    flat = lambda d: _to_flat([d[n] for n in rest], 8)
    outs = adamw(flat(w), flat(g), flat(m), flat(v), name="adamw_vectors")
    for d, o in zip((delta, new_m, new_v), outs):
        d.update(zip(rest, _from_flat(o, [w[n].shape for n in rest])))
    return (loss, dx[None], *[g[n] for n in WEIGHTS], *[delta[n] for n in WEIGHTS],
            *[new_m[n] for n in WEIGHTS], *[new_v[n] for n in WEIGHTS])
```

```python
import functools

import numpy as np
import jax
import jax.numpy as jnp
from jax import lax
from jax.experimental import pallas as pl
from jax.experimental.pallas import tpu as pltpu

F32, BF16 = jnp.float32, jnp.bfloat16
HI = lax.Precision.HIGHEST
MESH = pl.DeviceIdType.MESH

D_MODEL = 1024
DEPTH = 4
RMS_EPS = 1e-6
NEG_INF = -1e30
A_GROUPS = ((128, 1), (512, 4), (2048, 16))
DILS = tuple(d for _, d in A_GROUPS)
A_HALF = 64
A_HEADS = 16
A_HEAD_DIM = 64
A_WIDTH = 1024
A_IN_WIDTH = 9216
ROPE_THETA = 10000.0
B_HEADS = 4
B_KEY_DIM = 128
B_VAL_DIM = 256
B_QK_WIDTH = 512
B_V_WIDTH = 1024
B_GATE_RANK = 16
B_GATE_TAU = 16.0
B_CHUNK = 64
B_IN_WIDTH = 3104
B_IN_PAD = 3200
FFN_HIDDEN = 2816
ADAM_LR, ADAM_B1, ADAM_B2, ADAM_EPS, ADAM_WD, ADAM_STEP = 0.001, 0.9, 0.999, 1e-08, 0.01, 10
LANES = 128
VMEM_LIMIT = 48 * 1024 * 1024
FLAT_COLS = 1024
N_CHIPS = 4

WEIGHTS = ['attn_norm', 'ffn_norm', 'a_w_in', 'a_q_norm', 'a_k_norm', 'a_w_out', 'b_w_in', 'b_w_gate_f',
           'b_gate_bias_f', 'b_w_gate_b', 'b_gate_bias_b', 'b_out_norm', 'b_w_out', 'ffn_w_gate_up', 'ffn_w_down']
REPLICATED = ('attn_norm', 'ffn_norm', 'a_q_norm', 'a_k_norm')
SHARD_AXIS = {'a_w_in': 2, 'a_w_out': 1, 'b_w_in': 2, 'b_w_gate_f': 2, 'b_gate_bias_f': 1, 'b_w_gate_b': 2,
              'b_gate_bias_b': 1, 'b_out_norm': 2, 'b_w_out': 1, 'ffn_w_gate_up': 2, 'ffn_w_down': 1}
BIG = ('a_w_in', 'a_w_out', 'b_w_in', 'b_w_gate_f', 'b_w_gate_b', 'b_w_out', 'ffn_w_gate_up', 'ffn_w_down')
SMALL_SHARDED = ('b_gate_bias_f', 'b_gate_bias_b', 'b_out_norm')
MATRICES = ('a_w_in', 'a_w_out', 'b_w_in', 'b_w_out', 'ffn_w_gate_up', 'ffn_w_down')


def _params(sem):
    return pltpu.CompilerParams(dimension_semantics=sem, vmem_limit_bytes=VMEM_LIMIT)


def _tile(n, pref):
    t = min(n, pref)
    while n % t:
        t //= 2
    return t


def _const_spec(shape):
    nd = len(shape)
    return pl.BlockSpec(shape, lambda *_: (0,) * nd)


def matmul(a, b, *, name, out_dtype=F32, res=None, tm=1024, tn=512, tk=1024):
    M, K = a.shape
    N = b.shape[1]
    assert b.shape[0] == K
    tm, tn, tk = _tile(M, tm), _tile(N, tn), _tile(K, tk)
    nk = K // tk

    def body(*refs):
        a_ref, b_ref = refs[:2]
        r_ref = refs[2] if res is not None else None
        o_ref = refs[3 if res is not None else 2]
        part = jnp.dot(a_ref[...], b_ref[...], preferred_element_type=F32)

        def finish(v):
            if res is not None:
                v = v + r_ref[...]
            o_ref[...] = v.astype(o_ref.dtype)

        if nk == 1:
            finish(part)
            return
        acc_ref = refs[-1]
        k = pl.program_id(2)

        @pl.when(k == 0)
        def _():
            acc_ref[...] = part

        @pl.when((k > 0) & (k < nk - 1))
        def _():
            acc_ref[...] += part

        @pl.when(k == nk - 1)
        def _():
            finish(acc_ref[...] + part)

    in_specs = [pl.BlockSpec((tm, tk), lambda i, j, k: (i, k)), pl.BlockSpec((tk, tn), lambda i, j, k: (k, j))]
    args = [a, b]
    if res is not None:
        in_specs.append(pl.BlockSpec((tm, tn), lambda i, j, k: (i, j)))
        args.append(res)
    return pl.pallas_call(
        body, name=name, grid=(M // tm, N // tn, nk), in_specs=in_specs,
        out_specs=pl.BlockSpec((tm, tn), lambda i, j, k: (i, j)),
        out_shape=jax.ShapeDtypeStruct((M, N), out_dtype),
        scratch_shapes=[pltpu.VMEM((tm, tn), F32)] if nk > 1 else [],
        compiler_params=_params(("parallel", "parallel", "arbitrary")),
    )(*args)


def _phase_spec(d, ts, W):
    if d == 1:
        return pl.BlockSpec((ts, W), lambda i: (i, 0))
    return pl.BlockSpec((d, ts // d, W), lambda i: (0, i, 0))


def _phase_view(a, d):
    return a if d == 1 else a.reshape(d, a.shape[0] // d, a.shape[1])


def _phase_shape(S, W, d, dtype):
    return jax.ShapeDtypeStruct((S, W) if d == 1 else (d, S // d, W), dtype)


def _nat_scratch(ts, W):
    return pltpu.VMEM((W // LANES, ts, LANES), F32)


def _put_natural(nat_ref, value):
    for c in range(nat_ref.shape[0]):
        nat_ref[c] = value[:, c * LANES:(c + 1) * LANES]


def _get_natural(nat_ref):
    return jnp.concatenate([nat_ref[c] for c in range(nat_ref.shape[0])], axis=1)


def _store_phases(nat_ref, o_ref, d, ts):
    for p in range(d):
        for c in range(nat_ref.shape[0]):
            o_ref[p, :, c * LANES:(c + 1) * LANES] = nat_ref[c, pl.ds(p, ts // d, stride=d), :].astype(o_ref.dtype)


def _load_phases(i_ref, nat_ref, d, ts):
    for p in range(d):
        for c in range(nat_ref.shape[0]):
            nat_ref[c, pl.ds(p, ts // d, stride=d), :] = i_ref[p, :, c * LANES:(c + 1) * LANES].astype(F32)


def rmsnorm_fwd(x, gain, *, name, dils=(1,), with_transpose=False):
    S, Dm = x.shape
    ts = _tile(S, 512)
    nd = len(dils)

    def body(x_ref, g_ref, *rest):
        o_refs, scr = rest[:nd], rest[-1]
        xv = x_ref[...]
        r = lax.rsqrt(jnp.mean(xv * xv, axis=-1, keepdims=True) + RMS_EPS)
        y = (xv * r) * g_ref[...]
        if any(d > 1 for d in dils):
            _put_natural(scr, y)
        for d, o_ref in zip(dils, o_refs):
            if d == 1:
                o_ref[...] = y.astype(o_ref.dtype)
            else:
                _store_phases(scr, o_ref, d, ts)
        if with_transpose:
            rest[nd][...] = y.T.astype(BF16)

    outs = pl.pallas_call(
        body, name=name, grid=(S // ts,),
        in_specs=[pl.BlockSpec((ts, Dm), lambda i: (i, 0)), _const_spec((1, Dm))],
        out_specs=[_phase_spec(d, ts, Dm) for d in dils] + [pl.BlockSpec((Dm, ts), lambda i: (0, i))] * with_transpose,
        out_shape=[_phase_shape(S, Dm, d, BF16) for d in dils] + [jax.ShapeDtypeStruct((Dm, S), BF16)] * with_transpose,
        scratch_shapes=[_nat_scratch(ts, Dm)],
        compiler_params=_params(("parallel",)),
    )(x, gain)
    return [o.reshape(S, Dm) for o in outs[:nd]] + list(outs[nd:])


def rmsnorm_bwd(x, gain, dys, dres, *, name, dils=(1,)):
    S, Dm = x.shape
    ts = _tile(S, 256)
    nd = len(dils)

    def body(x_ref, g_ref, *rest):
        dy_refs, (dr_ref, dx_ref, dg_ref, scr) = rest[:nd], rest[nd:]

        @pl.when(pl.program_id(0) == 0)
        def _():
            dg_ref[...] = jnp.zeros_like(dg_ref)

        dyv = None
        for d, dy_ref in zip(dils, dy_refs):
            if d == 1:
                t = dy_ref[...].astype(F32)
            else:
                _load_phases(dy_ref, scr, d, ts)
                t = _get_natural(scr)
            dyv = t if dyv is None else dyv + t
        xv = x_ref[...]
        r = lax.rsqrt(jnp.mean(xv * xv, axis=-1, keepdims=True) + RMS_EPS)
        xhat = xv * r
        dyg = dyv * g_ref[...]
        dx = r * (dyg - xhat * jnp.mean(dyg * xhat, axis=-1, keepdims=True))
        dx_ref[...] = dr_ref[...] + dx
        dg_ref[0:1, :] += jnp.sum(dyv * xhat, axis=0, keepdims=True)

    row = pl.BlockSpec((ts, Dm), lambda i: (i, 0))
    return pl.pallas_call(
        body, name=name, grid=(S // ts,),
        in_specs=[row, _const_spec((1, Dm))] + [_phase_spec(d, ts, Dm) for d in dils] + [row],
        out_specs=[row, _const_spec((8, Dm))],
        out_shape=[jax.ShapeDtypeStruct((S, Dm), F32), jax.ShapeDtypeStruct((8, Dm), F32)],
        scratch_shapes=[_nat_scratch(ts, Dm)],
        compiler_params=_params(("arbitrary",)),
    )(x, gain, *[_phase_view(dy, d) for dy, d in zip(dys, dils)], dres)


def swiglu_fwd(gu, *, name):
    S, F2 = gu.shape
    Fh = F2 // 2
    ts = _tile(S, 512)

    def body(g_ref, u_ref, o_ref, ot_ref):
        g = g_ref[...].astype(F32)
        u = u_ref[...].astype(F32)
        act = g * (1.0 / (1.0 + jnp.exp(-g))) * u
        o_ref[...] = act.astype(o_ref.dtype)
        ot_ref[...] = act.T.astype(ot_ref.dtype)

    return pl.pallas_call(
        body, name=name, grid=(S // ts,),
        in_specs=[pl.BlockSpec((ts, Fh), lambda i: (i, 0)), pl.BlockSpec((ts, Fh), lambda i: (i, 1))],
        out_specs=[pl.BlockSpec((ts, Fh), lambda i: (i, 0)), pl.BlockSpec((Fh, ts), lambda i: (0, i))],
        out_shape=[jax.ShapeDtypeStruct((S, Fh), BF16), jax.ShapeDtypeStruct((Fh, S), BF16)],
        compiler_params=_params(("parallel",)),
    )(gu, gu)


def swiglu_bwd(gu, dact, *, name):
    S, F2 = gu.shape
    Fh = F2 // 2
    ts = _tile(S, 256)

    def body(gu_ref, d_ref, o_ref):
        g = gu_ref[:, :Fh].astype(F32)
        u = gu_ref[:, Fh:].astype(F32)
        d = d_ref[...].astype(F32)
        sig = 1.0 / (1.0 + jnp.exp(-g))
        o_ref[:, :Fh] = (d * u * (sig * (1.0 + g * (1.0 - sig)))).astype(o_ref.dtype)
        o_ref[:, Fh:] = (d * (g * sig)).astype(o_ref.dtype)

    return pl.pallas_call(
        body, name=name, grid=(S // ts,),
        in_specs=[pl.BlockSpec((ts, F2), lambda i: (i, 0)), pl.BlockSpec((ts, Fh), lambda i: (i, 0))],
        out_specs=pl.BlockSpec((ts, F2), lambda i: (i, 0)),
        out_shape=jax.ShapeDtypeStruct((S, F2), BF16),
        compiler_params=_params(("parallel",)),
    )(gu, dact)


def loss_head(y, target, *, name):
    S, Dm = y.shape
    ts = _tile(S, 512)

    def body(y_ref, t_ref, l_ref, d_ref):
        @pl.when(pl.program_id(0) == 0)
        def _():
            l_ref[...] = jnp.zeros_like(l_ref)

        e = y_ref[...] - t_ref[...]
        d_ref[...] = e * (1.0 / Dm)
        l_ref[...] += jnp.sum(e * e)

    return pl.pallas_call(
        body, name=name, grid=(S // ts,),
        in_specs=[pl.BlockSpec((ts, Dm), lambda i: (i, 0)), pl.BlockSpec((ts, Dm), lambda i: (i, 0))],
        out_specs=[_const_spec((8, LANES)), pl.BlockSpec((ts, Dm), lambda i: (i, 0))],
        out_shape=[jax.ShapeDtypeStruct((8, LANES), F32), jax.ShapeDtypeStruct((S, Dm), F32)],
        compiler_params=_params(("arbitrary",)),
    )(y, target)


def _head_block_ones():
    i = np.arange(LANES)
    return jnp.asarray((i[:, None] // A_HEAD_DIM == i[None, :] // A_HEAD_DIM).astype(np.float32)).astype(BF16)


def _rope_tables(S):
    half = A_HEAD_DIM // 2
    inv_freq = ROPE_THETA ** (-jnp.arange(half, dtype=F32) / half)
    ang = jnp.arange(S).astype(F32)[:, None] * inv_freq[None, :]
    cos = jnp.tile(jnp.cos(ang), (1, LANES // half))
    sin = jnp.tile(jnp.sin(ang), (1, LANES // half))
    return cos, sin


def _rot_half(x, lo):
    return jnp.where(lo, -pltpu.roll(x, LANES - 32, 1), pltpu.roll(x, 32, 1))


def _seg_sum(v, ones_ref):
    hi = v.astype(BF16)
    lo = (v - hi.astype(F32)).astype(BF16)
    ones = ones_ref[...]
    return jnp.dot(hi, ones, preferred_element_type=F32) + jnp.dot(lo, ones, preferred_element_type=F32)


def _seg_mean(v, ones_ref):
    return _seg_sum(v, ones_ref) * (1.0 / A_HEAD_DIM)


def qk_prep_fwd(qkv, gain, cos, sin, ones, *, name):
    S, W = qkv.shape
    ts = _tile(S, 256)
    nchunk = A_WIDTH // LANES

    def body(x_ref, g_ref, c_ref, s_ref, ones_ref, o_ref):
        kind = pl.program_id(1) % 3

        @pl.when(kind < 2)
        def _():
            scale = jnp.where(kind == 0, A_HEAD_DIM ** -0.5, 1.0).astype(F32)
            lo = (lax.broadcasted_iota(jnp.int32, (ts, LANES), 1) % A_HEAD_DIM) < (A_HEAD_DIM // 2)
            cv, sv = c_ref[...], s_ref[...]
            for c in range(nchunk):
                sl = slice(c * LANES, (c + 1) * LANES)
                xv = x_ref[:, sl]
                r = lax.rsqrt(_seg_mean(xv * xv, ones_ref) + RMS_EPS)
                y = (xv * r) * g_ref[:, sl]
                y = y * cv + _rot_half(y, lo) * sv
                o_ref[:, sl] = (y * scale).astype(o_ref.dtype)

        @pl.when(kind == 2)
        def _():
            o_ref[...] = x_ref[...].astype(o_ref.dtype)

    return pl.pallas_call(
        body, name=name, grid=(S // ts, W // A_WIDTH),
        in_specs=[pl.BlockSpec((ts, A_WIDTH), lambda i, j: (i, j)), pl.BlockSpec((1, A_WIDTH), lambda i, j: (0, j)),
                  pl.BlockSpec((ts, LANES), lambda i, j: (i, 0)), pl.BlockSpec((ts, LANES), lambda i, j: (i, 0)),
                  _const_spec((LANES, LANES))],
        out_specs=pl.BlockSpec((ts, A_WIDTH), lambda i, j: (i, j)),
        out_shape=jax.ShapeDtypeStruct((S, W), BF16),
        compiler_params=_params(("parallel", "arbitrary")),
    )(qkv, gain, cos, sin, ones)


def qk_prep_bwd(qkv, gain, cos, sin, ones, grads, *, name):
    S, W = qkv.shape
    ts = _tile(S, 256)
    nchunk = A_WIDTH // LANES
    nj = W // A_WIDTH

    def body(x_ref, g_ref, c_ref, s_ref, ones_ref, *rest):
        g_refs, (o_ref, dg_ref) = rest[:nj], rest[nj:]
        j = pl.program_id(0)
        kind = j % 3

        @pl.when(pl.program_id(1) == 0)
        def _():
            dg_ref[...] = jnp.zeros_like(dg_ref)

        for n in range(nj):
            @pl.when(j == n)
            def _(n=n):
                d_ref = g_refs[n]
                if n % 3 == 2:
                    o_ref[...] = d_ref[...].astype(o_ref.dtype)
                    return
                scale = A_HEAD_DIM ** -0.5 if n % 3 == 0 else 1.0
                lo = (lax.broadcasted_iota(jnp.int32, (ts, LANES), 1) % A_HEAD_DIM) < (A_HEAD_DIM // 2)
                cv, sv = c_ref[...], s_ref[...]
                for c in range(nchunk):
                    sl = slice(c * LANES, (c + 1) * LANES)
                    dy = d_ref[:, sl].astype(F32) * scale
                    dn = dy * cv - _rot_half(dy, lo) * sv
                    xv = x_ref[:, sl]
                    r = lax.rsqrt(_seg_mean(xv * xv, ones_ref) + RMS_EPS)
                    xhat = xv * r
                    dyg = dn * g_ref[:, sl]
                    dx = r * (dyg - xhat * _seg_mean(dyg * xhat, ones_ref))
                    o_ref[:, sl] = dx.astype(o_ref.dtype)
                    dg_ref[0:1, sl] += jnp.sum(dn * xhat, axis=0, keepdims=True)

    def gspec(n):
        return pl.BlockSpec((ts, A_WIDTH), lambda j, i: (jnp.where(j == n, i, 0), 0))

    return pl.pallas_call(
        body, name=name, grid=(nj, S // ts),
        in_specs=[pl.BlockSpec((ts, A_WIDTH), lambda j, i: (i, j)), pl.BlockSpec((1, A_WIDTH), lambda j, i: (0, j)),
                  pl.BlockSpec((ts, LANES), lambda j, i: (i, 0)), pl.BlockSpec((ts, LANES), lambda j, i: (i, 0)),
                  _const_spec((LANES, LANES))] + [gspec(n) for n in range(nj)],
        out_specs=[pl.BlockSpec((ts, A_WIDTH), lambda j, i: (i, j)), pl.BlockSpec((8, A_WIDTH), lambda j, i: (0, j))],
        out_shape=[jax.ShapeDtypeStruct((S, W), BF16), jax.ShapeDtypeStruct((8, W), F32)],
        compiler_params=_params(("arbitrary", "arbitrary")),
    )(qkv, gain, cos, sin, ones, *grads)


def _band_specs(kind, tq, nlb):
    nhb = tq // A_HALF
    nb = nlb // nhb
    base = kind * (A_WIDTH // LANES)
    return [pl.BlockSpec((A_HALF, LANES), lambda ph, b, hp: (ph * nlb + jnp.maximum(b * nhb - 1, 0), base + hp)),
            pl.BlockSpec((tq, LANES), lambda ph, b, hp: (ph * nb + b, base + hp)),
            pl.BlockSpec((A_HALF, LANES), lambda ph, b, hp: (ph * nlb + jnp.minimum((b + 1) * nhb, nlb - 1), base + hp))]


A_BLOCK = 2048
A_SUB = 128


def _band_bias(sub, key_major):
    i = np.arange(sub)[:, None]
    j = np.arange(sub + 2 * A_HALF)[None, :] - A_HALF
    ok = np.abs(j - i) <= A_HALF
    return jnp.asarray(np.where(ok.T if key_major else ok, 0.0, NEG_INF).astype(np.float32))


def _edge_bias(first, n, L, axis, at_start, at_end):
    if not (at_start or at_end):
        return None
    shape = (1, n) if axis == 1 else (n, 1)
    pos = first - A_HALF + lax.broadcasted_iota(jnp.int32, shape, axis)
    return jnp.where((pos < 0) | (pos >= L), NEG_INF, 0.0).astype(F32)


def _with_edge(bias, edge):
    return bias if edge is None else bias + edge


def _cat3(a_ref, b_ref, c_ref):
    return jnp.concatenate([a_ref[...], b_ref[...], c_ref[...]], axis=0)


def _lane_lo(rows):
    return lax.broadcasted_iota(jnp.int32, (rows, LANES), 1) < A_HEAD_DIM


NT = (((1,), (1,)), ((), ()))
TN = (((0,), (0,)), ((), ()))


def attn_fwd(qkvp, dil, *, name):
    S = qkvp.shape[0]
    L = S // dil
    tq = _tile(L, A_BLOCK)
    sub = min(tq, A_SUB)
    nsub = tq // sub
    nlb = L // A_HALF
    band = _band_bias(sub, False)

    def body(q_ref, kp_ref, ko_ref, kn_ref, vp_ref, vo_ref, vn_ref, band_ref, o_ref, l_ref):
        b = pl.program_id(1)
        K = _cat3(kp_ref, ko_ref, kn_ref)
        V = _cat3(vp_ref, vo_ref, vn_ref)
        lo_k = _lane_lo(tq + 2 * A_HALF)
        lo_q = _lane_lo(sub)
        Km = [jnp.where(sel, K, jnp.zeros_like(K)) for sel in (lo_k, ~lo_k)]
        Vm = [jnp.where(sel, V, jnp.zeros_like(V)) for sel in (lo_k, ~lo_k)]
        for r in range(nsub):
            rows = slice(r * sub, (r + 1) * sub)
            keys = slice(r * sub, (r + 1) * sub + 2 * A_HALF)
            bias = _with_edge(band_ref[...], _edge_bias(b * tq + r * sub, sub + 2 * A_HALF, L, 1, r == 0, r == nsub - 1))
            q = q_ref[rows, :]
            outs, lses = [], []
            for hh in range(2):
                s = lax.dot_general(q, Km[hh][keys], NT, preferred_element_type=F32) + bias
                m = jnp.max(s, axis=1, keepdims=True)
                p = jnp.exp(s - m)
                l = jnp.sum(p, axis=1, keepdims=True)
                outs.append(jnp.dot(p.astype(BF16), Vm[hh][keys], preferred_element_type=F32) * (1.0 / l))
                lses.append(m + jnp.log(l))
            o_ref[rows, :] = outs[0] + outs[1]
            l_ref[rows, :] = jnp.where(lo_q, lses[0], lses[1])

    ospec = _band_specs(0, tq, nlb)[1]
    return pl.pallas_call(
        body, name=name, grid=(dil, L // tq, A_WIDTH // LANES),
        in_specs=[_band_specs(0, tq, nlb)[1]] + _band_specs(1, tq, nlb) + _band_specs(2, tq, nlb)
        + [_const_spec(band.shape)],
        out_specs=[ospec, ospec],
        out_shape=[jax.ShapeDtypeStruct((S, A_WIDTH), F32)] * 2,
        compiler_params=_params(("parallel", "parallel", "parallel")),
    )(*([qkvp] * 7), band)


def attn_merge(os_, lses, dils, *, name):
    S = os_[0].shape[0]
    ts = _tile(S, 256)
    ng = len(dils)

    def body(*refs):
        o_refs, l_refs, out_ref = refs[:ng], refs[ng:2 * ng], refs[2 * ng]
        lse_refs, scrs = refs[2 * ng + 1:3 * ng + 1], refs[3 * ng + 1:]
        ov, ls, k = [], [], 0
        for d, o_ref, l_ref in zip(dils, o_refs, l_refs):
            if d == 1:
                ov.append(o_ref[...])
                ls.append(l_ref[...])
            else:
                _load_phases(o_ref, scrs[k], d, ts)
                _load_phases(l_ref, scrs[k + 1], d, ts)
                ov.append(_get_natural(scrs[k]))
                ls.append(_get_natural(scrs[k + 1]))
                k += 2
        m = functools.reduce(jnp.maximum, ls)
        es = [jnp.exp(l - m) for l in ls]
        tot = functools.reduce(jnp.add, es)
        acc = None
        for e, o in zip(es, ov):
            t = (e / tot) * o
            acc = t if acc is None else acc + t
        out_ref[...] = acc.astype(out_ref.dtype)
        total = m + jnp.log(tot)
        _put_natural(scrs[k], total)
        for d, lse_ref in zip(dils, lse_refs):
            if d == 1:
                lse_ref[...] = total
            else:
                _store_phases(scrs[k], lse_ref, d, ts)

    n_scr = 2 * sum(d > 1 for d in dils) + 1
    outs = pl.pallas_call(
        body, name=name, grid=(S // ts,),
        in_specs=[_phase_spec(d, ts, A_WIDTH) for d in dils] * 2,
        out_specs=[pl.BlockSpec((ts, A_WIDTH), lambda i: (i, 0))] + [_phase_spec(d, ts, A_WIDTH) for d in dils],
        out_shape=[jax.ShapeDtypeStruct((S, A_WIDTH), BF16)] + [_phase_shape(S, A_WIDTH, d, F32) for d in dils],
        scratch_shapes=[_nat_scratch(ts, A_WIDTH)] * n_scr,
        compiler_params=_params(("parallel",)),
    )(*[_phase_view(o, d) for o, d in zip(os_, dils)], *[_phase_view(l, d) for l, d in zip(lses, dils)])
    return outs[0], [l.reshape(S, A_WIDTH) for l in outs[1:]]


def attn_delta(dout, out, ones, dils, *, name):
    S = dout.shape[0]
    ts = _tile(S, 256)
    nd = len(dils)

    def body(d_ref, o_ref, ones_ref, *rest):
        do_refs, dl_refs, (scr_do, scr_dl) = rest[:nd], rest[nd:2 * nd], rest[2 * nd:]
        sums = []
        for c in range(A_WIDTH // LANES):
            sl = slice(c * LANES, (c + 1) * LANES)
            prod = d_ref[:, sl].astype(F32) * o_ref[:, sl].astype(F32)
            sums.append(_seg_sum(prod, ones_ref))
        _put_natural(scr_do, d_ref[...].astype(F32))
        _put_natural(scr_dl, jnp.concatenate(sums, axis=1))
        for d, do_ref, dl_ref in zip(dils, do_refs, dl_refs):
            if d == 1:
                do_ref[...] = d_ref[...]
                dl_ref[...] = _get_natural(scr_dl)
            else:
                _store_phases(scr_do, do_ref, d, ts)
                _store_phases(scr_dl, dl_ref, d, ts)

    spec = pl.BlockSpec((ts, A_WIDTH), lambda i: (i, 0))
    outs = pl.pallas_call(
        body, name=name, grid=(S // ts,), in_specs=[spec, spec, _const_spec((LANES, LANES))],
        out_specs=[_phase_spec(d, ts, A_WIDTH) for d in dils] * 2,
        out_shape=[_phase_shape(S, A_WIDTH, d, BF16) for d in dils] + [_phase_shape(S, A_WIDTH, d, F32) for d in dils],
        scratch_shapes=[_nat_scratch(ts, A_WIDTH)] * 2,
        compiler_params=_params(("parallel",)),
    )(dout, out, ones)
    outs = [o.reshape(S, A_WIDTH) for o in outs]
    return outs[:nd], outs[nd:]


def _head_col(x, hh):
    c = hh * A_HEAD_DIM
    return x[:, c:c + 1]


def attn_bwd_dq(qkvp, dout, lse, delta, dil, *, name):
    S = qkvp.shape[0]
    L = S // dil
    tq = _tile(L, A_BLOCK)
    sub = min(tq, A_SUB)
    nsub = tq // sub
    nlb = L // A_HALF
    band = _band_bias(sub, False)

    def body(q_ref, kp_ref, ko_ref, kn_ref, vp_ref, vo_ref, vn_ref, do_ref, l_ref, d_ref, band_ref, dq_ref):
        b = pl.program_id(1)
        K = _cat3(kp_ref, ko_ref, kn_ref)
        V = _cat3(vp_ref, vo_ref, vn_ref)
        lo_k = _lane_lo(tq + 2 * A_HALF)
        Km = [jnp.where(sel, K, jnp.zeros_like(K)) for sel in (lo_k, ~lo_k)]
        Vm = [jnp.where(sel, V, jnp.zeros_like(V)) for sel in (lo_k, ~lo_k)]
        for r in range(nsub):
            rows = slice(r * sub, (r + 1) * sub)
            keys = slice(r * sub, (r + 1) * sub + 2 * A_HALF)
            bias = _with_edge(band_ref[...], _edge_bias(b * tq + r * sub, sub + 2 * A_HALF, L, 1, r == 0, r == nsub - 1))
            q, do = q_ref[rows, :], do_ref[rows, :]
            lse_v, dl_v = l_ref[rows, :], d_ref[rows, :]
            acc = None
            for hh in range(2):
                s = lax.dot_general(q, Km[hh][keys], NT, preferred_element_type=F32) + bias
                p = jnp.exp(s - _head_col(lse_v, hh))
                dp = lax.dot_general(do, Vm[hh][keys], NT, preferred_element_type=F32)
                ds = p * (dp - _head_col(dl_v, hh))
                t = jnp.dot(ds.astype(BF16), Km[hh][keys], preferred_element_type=F32)
                acc = t if acc is None else acc + t
            dq_ref[rows, :] = acc.astype(dq_ref.dtype)

    nspec = _band_specs(0, tq, nlb)[1]
    return pl.pallas_call(
        body, name=name, grid=(dil, L // tq, A_WIDTH // LANES),
        in_specs=[nspec] + _band_specs(1, tq, nlb) + _band_specs(2, tq, nlb) + [nspec, nspec, nspec, _const_spec(band.shape)],
        out_specs=nspec,
        out_shape=jax.ShapeDtypeStruct((S, A_WIDTH), BF16),
        compiler_params=_params(("parallel", "parallel", "parallel")),
    )(*([qkvp] * 7), dout, lse, delta, band)


def attn_bwd_dkv(qkvp, dout, lse, delta, dil, *, name):
    S = qkvp.shape[0]
    L = S // dil
    tk = _tile(L, A_BLOCK)
    sub = min(tk, A_SUB)
    nsub = tk // sub
    nlb = L // A_HALF
    band = _band_bias(sub, False)

    def body(qp_ref, qo_ref, qn_ref, k_ref, v_ref, dp_ref, do_ref, dn_ref, lp_ref, lo_ref, ln_ref,
             ep_ref, eo_ref, en_ref, band_ref, dk_ref, dv_ref):
        b = pl.program_id(1)
        Q = _cat3(qp_ref, qo_ref, qn_ref)
        DO = _cat3(dp_ref, do_ref, dn_ref)
        lse_v = _cat3(lp_ref, lo_ref, ln_ref)
        dl_v = _cat3(ep_ref, eo_ref, en_ref)
        lo_q = _lane_lo(tk + 2 * A_HALF)
        Qm = [jnp.where(sel, Q, jnp.zeros_like(Q)) for sel in (lo_q, ~lo_q)]
        DOm = [jnp.where(sel, DO, jnp.zeros_like(DO)) for sel in (lo_q, ~lo_q)]
        for r in range(nsub):
            keys = slice(r * sub, (r + 1) * sub)
            qs = slice(r * sub, (r + 1) * sub + 2 * A_HALF)
            bias = _with_edge(band_ref[...], _edge_bias(b * tk + r * sub, sub + 2 * A_HALF, L, 1, r == 0, r == nsub - 1))
            K, V = k_ref[keys, :], v_ref[keys, :]
            lse_t, dl_t = lse_v[qs].T, dl_v[qs].T
            dk = dv = None
            for hh in range(2):
                hr = slice(hh * A_HEAD_DIM, hh * A_HEAD_DIM + 1)
                st = lax.dot_general(K, Qm[hh][qs], NT, preferred_element_type=F32) + bias
                pt = jnp.exp(st - lse_t[hr, :])
                dpt = lax.dot_general(V, DOm[hh][qs], NT, preferred_element_type=F32)
                dst = pt * (dpt - dl_t[hr, :])
                tv = jnp.dot(pt.astype(BF16), DOm[hh][qs], preferred_element_type=F32)
                tk_ = jnp.dot(dst.astype(BF16), Qm[hh][qs], preferred_element_type=F32)
                dv = tv if dv is None else dv + tv
                dk = tk_ if dk is None else dk + tk_
            dk_ref[keys, :] = dk.astype(dk_ref.dtype)
            dv_ref[keys, :] = dv.astype(dv_ref.dtype)

    nspec = _band_specs(0, tk, nlb)[1]
    return pl.pallas_call(
        body, name=name, grid=(dil, L // tk, A_WIDTH // LANES),
        in_specs=_band_specs(0, tk, nlb) + [_band_specs(1, tk, nlb)[1], _band_specs(2, tk, nlb)[1]]
        + _band_specs(0, tk, nlb) * 3 + [_const_spec(band.shape)],
        out_specs=[nspec, nspec],
        out_shape=[jax.ShapeDtypeStruct((S, A_WIDTH), BF16)] * 2,
        compiler_params=_params(("parallel", "parallel", "parallel")),
    )(*([qkvp] * 5), *([dout] * 3), *([lse] * 3), *([delta] * 3), band)


def _gate_block_weight(wf, wb):
    w = jnp.zeros((LANES, 2 * B_QK_WIDTH), F32)
    w = w.at[:B_GATE_RANK, :B_QK_WIDTH].set(wf)
    w = w.at[B_GATE_RANK:2 * B_GATE_RANK, B_QK_WIDTH:].set(wb)
    return w.astype(BF16)


def gate_fwd(proj, wblk, bias, *, name):
    S = proj.shape[0]
    ts = _tile(S, 512)
    W = 2 * B_QK_WIDTH
    zcol = (2 * B_QK_WIDTH + 2 * B_V_WIDTH) // LANES

    def body(z_ref, w_ref, b_ref, o_ref):
        x = jnp.dot(z_ref[...].astype(BF16), w_ref[...], preferred_element_type=F32) + b_ref[...]
        o_ref[...] = (jnp.minimum(x, 0.0) - jnp.log(1.0 + jnp.exp(-jnp.abs(x)))) * (1.0 / B_GATE_TAU)

    return pl.pallas_call(
        body, name=name, grid=(S // ts,),
        in_specs=[pl.BlockSpec((ts, LANES), lambda i: (i, zcol)), _const_spec((LANES, W)), _const_spec((1, W))],
        out_specs=pl.BlockSpec((ts, W), lambda i: (i, 0)),
        out_shape=jax.ShapeDtypeStruct((S, W), F32),
        compiler_params=_params(("parallel",)),
    )(proj, wblk, bias)


def gate_bwd(proj, wblk, wblk_t, bias, dloga, *, name):
    S = proj.shape[0]
    ts = _tile(S, 512)
    W = 2 * B_QK_WIDTH
    zcol = (2 * B_QK_WIDTH + 2 * B_V_WIDTH) // LANES

    def body(z_ref, w_ref, wt_ref, b_ref, d_ref, dz_ref, dw_ref, db_ref):
        @pl.when(pl.program_id(0) == 0)
        def _():
            dw_ref[...] = jnp.zeros_like(dw_ref)
            db_ref[...] = jnp.zeros_like(db_ref)

        z = z_ref[...].astype(BF16)
        x = jnp.dot(z, w_ref[...], preferred_element_type=F32) + b_ref[...]
        e = jnp.exp(-jnp.abs(x))
        sig_neg = jnp.where(x >= 0, e, 1.0) / (1.0 + e)
        dx = d_ref[...] * (1.0 / B_GATE_TAU) * sig_neg
        dxb = dx.astype(BF16)
        dz_ref[...] = jnp.dot(dxb, wt_ref[...], preferred_element_type=F32)
        dw_ref[...] += lax.dot_general(z, dxb, TN, preferred_element_type=F32)
        db_ref[0:1, :] += jnp.sum(dx, axis=0, keepdims=True)

    return pl.pallas_call(
        body, name=name, grid=(S // ts,),
        in_specs=[pl.BlockSpec((ts, LANES), lambda i: (i, zcol)), _const_spec((LANES, W)), _const_spec((W, LANES)),
                  _const_spec((1, W)), pl.BlockSpec((ts, W), lambda i: (i, 0))],
        out_specs=[pl.BlockSpec((ts, LANES), lambda i: (i, 0)), _const_spec((LANES, W)), _const_spec((8, W))],
        out_shape=[jax.ShapeDtypeStruct((S, LANES), F32), jax.ShapeDtypeStruct((LANES, W), F32),
                   jax.ShapeDtypeStruct((8, W), F32)],
        compiler_params=_params(("arbitrary",)),
    )(proj, wblk, wblk_t, bias, dloga)


def _tri(reverse):
    i = np.arange(B_CHUNK)
    t = (i[None, :] >= i[:, None]) if reverse else (i[None, :] <= i[:, None])
    return jnp.asarray(t.astype(np.float32))


def _gla_terms(q, k, la, t_ref, reverse):
    b = jnp.dot(t_ref[...], la, precision=HI, preferred_element_type=F32)
    b_last = b[0:1, :] if reverse else b[B_CHUNK - 1:B_CHUNK, :]
    e_b = jnp.exp(b)
    qt = (q * (B_KEY_DIM ** -0.5)) * e_b
    e_nb = jnp.exp(-b)
    kt = k * e_nb
    e_end = jnp.exp(b_last - b)
    kend = k * e_end
    dec = jnp.exp(b_last)
    return e_nb, e_b, qt, kt, e_end, kend, dec


def _chunk_mask(reverse, transpose=False):
    r = lax.broadcasted_iota(jnp.int32, (B_CHUNK, B_CHUNK), 0)
    c = lax.broadcasted_iota(jnp.int32, (B_CHUNK, B_CHUNK), 1)
    if transpose:
        r, c = c, r
    return (c > r) if reverse else (c <= r)


def gla_fwd(proj, loga, tmat, reverse, *, name):
    S = proj.shape[0]
    tb = _tile(S, 512)
    nb = S // tb
    cpb = tb // B_CHUNK
    nc = S // B_CHUNK
    lb = 1 if reverse else 0
    blk = (lambda i: nb - 1 - i) if reverse else (lambda i: i)

    def body(qk_ref, v_ref, la_ref, t_ref, o_ref, st_ref, s_scr):
        @pl.when(pl.program_id(0) == 0)
        def _():
            s_scr[...] = jnp.zeros_like(s_scr)

        mask = _chunk_mask(reverse)
        order = range(cpb - 1, -1, -1) if reverse else range(cpb)
        for c in order:
            rows = slice(c * B_CHUNK, (c + 1) * B_CHUNK)
            _, _, qt, kt, _, kend, dec = _gla_terms(qk_ref[rows, :B_QK_WIDTH], qk_ref[rows, B_QK_WIDTH:],
                                                   la_ref[rows, :], t_ref, reverse)
            qt, kt, kend = qt.astype(BF16), kt.astype(BF16), kend.astype(BF16)
            for h in range(B_HEADS):
                kc = slice(h * B_KEY_DIM, (h + 1) * B_KEY_DIM)
                vc = slice(h * B_VAL_DIM, (h + 1) * B_VAL_DIM)
                v = v_ref[rows, vc].astype(BF16)
                st = s_scr[h]
                st_ref[h, c] = st
                a = jnp.where(mask, lax.dot_general(qt[:, kc], kt[:, kc], NT, preferred_element_type=F32), 0.0)
                o = jnp.dot(a.astype(BF16), v, preferred_element_type=F32)
                o = o + lax.dot_general(qt[:, kc], st.astype(BF16), NT, preferred_element_type=F32)
                o_ref[rows, vc] = o
                s_scr[h] = st * dec[:, kc] + lax.dot_general(v, kend[:, kc], TN, preferred_element_type=F32)

    return pl.pallas_call(
        body, name=name, grid=(nb,),
        in_specs=[pl.BlockSpec((tb, 2 * B_QK_WIDTH), lambda i: (blk(i), 0)),
                  pl.BlockSpec((tb, B_V_WIDTH), lambda i: (blk(i), 1)),
                  pl.BlockSpec((tb, B_QK_WIDTH), lambda i: (blk(i), lb)),
                  _const_spec((B_CHUNK, B_CHUNK))],
        out_specs=[pl.BlockSpec((tb, B_V_WIDTH), lambda i: (blk(i), 0)),
                   pl.BlockSpec((B_HEADS, cpb, B_VAL_DIM, B_KEY_DIM), lambda i: (0, blk(i), 0, 0))],
        out_shape=[jax.ShapeDtypeStruct((S, B_V_WIDTH), F32),
                   jax.ShapeDtypeStruct((B_HEADS, nc, B_VAL_DIM, B_KEY_DIM), F32)],
        scratch_shapes=[pltpu.VMEM((B_HEADS, B_VAL_DIM, B_KEY_DIM), F32)],
        compiler_params=_params(("arbitrary",)),
    )(proj, proj, loga, tmat)


def gla_bwd(proj, loga, states, do, tmat, tmat_t, reverse, *, name):
    S = proj.shape[0]
    tb = _tile(S, 256)
    nb = S // tb
    cpb = tb // B_CHUNK
    lb = 1 if reverse else 0
    blk = (lambda i: i) if reverse else (lambda i: nb - 1 - i)
    scale = B_KEY_DIM ** -0.5

    def body(qk_ref, v_ref, la_ref, st_ref, do_ref, t_ref, tt_ref, dq_ref, dk_ref, dv_ref, dla_ref, ds_scr):
        @pl.when(pl.program_id(0) == 0)
        def _():
            ds_scr[...] = jnp.zeros_like(ds_scr)

        mask = _chunk_mask(reverse)
        mask_t = _chunk_mask(reverse, transpose=True)
        last = 0 if reverse else B_CHUNK - 1
        is_last = lax.broadcasted_iota(jnp.int32, (B_CHUNK, B_QK_WIDTH), 0) == last
        order = range(cpb) if reverse else range(cpb - 1, -1, -1)
        for c in order:
            rows = slice(c * B_CHUNK, (c + 1) * B_CHUNK)
            e_nb, e_b, qt, kt, e_end, kend, dec = _gla_terms(qk_ref[rows, :B_QK_WIDTH], qk_ref[rows, B_QK_WIDTH:],
                                                             la_ref[rows, :], t_ref, reverse)
            qtb, ktb, kendb = qt.astype(BF16), kt.astype(BF16), kend.astype(BF16)
            dqt_h, dkt_h, dkend_h, ddec_h = [], [], [], []
            for h in range(B_HEADS):
                kc = slice(h * B_KEY_DIM, (h + 1) * B_KEY_DIM)
                vc = slice(h * B_VAL_DIM, (h + 1) * B_VAL_DIM)
                v = v_ref[rows, vc].astype(BF16)
                dob = do_ref[rows, vc].astype(BF16)
                st = st_ref[h, c]
                dst = ds_scr[h]
                dstb = dst.astype(BF16)
                a_t = jnp.where(mask_t, lax.dot_general(ktb[:, kc], qtb[:, kc], NT, preferred_element_type=F32), 0.0)
                da = jnp.where(mask, lax.dot_general(dob, v, NT, preferred_element_type=F32), 0.0)
                da_t = jnp.where(mask_t, lax.dot_general(v, dob, NT, preferred_element_type=F32), 0.0)
                dv = jnp.dot(a_t.astype(BF16), dob, preferred_element_type=F32)
                dv_ref[rows, vc] = dv + lax.dot_general(kendb[:, kc], dstb, NT, preferred_element_type=F32)
                dqt = jnp.dot(da.astype(BF16), ktb[:, kc], preferred_element_type=F32)
                dqt_h.append(dqt + jnp.dot(dob, st.astype(BF16), preferred_element_type=F32))
                dkt_h.append(jnp.dot(da_t.astype(BF16), qtb[:, kc], preferred_element_type=F32))
                dkend_h.append(jnp.dot(v, dstb, preferred_element_type=F32))
                ddec_h.append(jnp.sum(dst * st, axis=0, keepdims=True))
                ds_scr[h] = dst * dec[:, kc] + lax.dot_general(dob, qtb[:, kc], TN, preferred_element_type=F32)
            dqt, dkt = jnp.concatenate(dqt_h, axis=1), jnp.concatenate(dkt_h, axis=1)
            dkend, ddec = jnp.concatenate(dkend_h, axis=1), jnp.concatenate(ddec_h, axis=1)
            ke = dkend * kend
            db = dqt * qt - dkt * kt - ke
            db_last = jnp.sum(ke, axis=0, keepdims=True) + ddec * dec
            db = db + jnp.where(is_last, db_last, 0.0)
            dq_ref[rows, :] = dqt * e_b * scale
            dk_ref[rows, :] = dkt * e_nb + dkend * e_end
            dla_ref[rows, :] = jnp.dot(tt_ref[...], db, precision=HI, preferred_element_type=F32)

    qspec = pl.BlockSpec((tb, B_QK_WIDTH), lambda i: (blk(i), 0))
    vspec = pl.BlockSpec((tb, B_V_WIDTH), lambda i: (blk(i), 0))
    return pl.pallas_call(
        body, name=name, grid=(nb,),
        in_specs=[pl.BlockSpec((tb, 2 * B_QK_WIDTH), lambda i: (blk(i), 0)),
                  pl.BlockSpec((tb, B_V_WIDTH), lambda i: (blk(i), 1)),
                  pl.BlockSpec((tb, B_QK_WIDTH), lambda i: (blk(i), lb)),
                  pl.BlockSpec((B_HEADS, cpb, B_VAL_DIM, B_KEY_DIM), lambda i: (0, blk(i), 0, 0)),
                  vspec, _const_spec((B_CHUNK, B_CHUNK)), _const_spec((B_CHUNK, B_CHUNK))],
        out_specs=[qspec, qspec, vspec, qspec],
        out_shape=[jax.ShapeDtypeStruct((S, B_QK_WIDTH), F32), jax.ShapeDtypeStruct((S, B_QK_WIDTH), F32),
                   jax.ShapeDtypeStruct((S, B_V_WIDTH), F32), jax.ShapeDtypeStruct((S, B_QK_WIDTH), F32)],
        scratch_shapes=[pltpu.VMEM((B_HEADS, B_VAL_DIM, B_KEY_DIM), F32)],
        compiler_params=_params(("arbitrary",)),
    )(proj, proj, loga, states, do, tmat, tmat_t)


def gla_post_fwd(o_f, o_b, gain, proj, *, name):
    S = o_f.shape[0]
    ts = _tile(S, 512)
    rcol = (2 * B_QK_WIDTH + B_V_WIDTH) // B_V_WIDTH

    def body(f_ref, b_ref, g_ref, r_ref, y_ref):
        for h in range(B_HEADS):
            sl = slice(h * B_VAL_DIM, (h + 1) * B_VAL_DIM)
            o = f_ref[:, sl] + b_ref[:, sl]
            n = (o * lax.rsqrt(jnp.mean(o * o, axis=-1, keepdims=True) + RMS_EPS)) * g_ref[:, sl]
            r = r_ref[:, sl]
            y_ref[:, sl] = (n * (r * (1.0 / (1.0 + jnp.exp(-r))))).astype(y_ref.dtype)

    spec = pl.BlockSpec((ts, B_V_WIDTH), lambda i: (i, 0))
    return pl.pallas_call(
        body, name=name, grid=(S // ts,),
        in_specs=[spec, spec, _const_spec((1, B_V_WIDTH)), pl.BlockSpec((ts, B_V_WIDTH), lambda i: (i, rcol))],
        out_specs=spec, out_shape=jax.ShapeDtypeStruct((S, B_V_WIDTH), BF16),
        compiler_params=_params(("parallel",)),
    )(o_f, o_b, gain, proj)


def gla_post_bwd(o_f, o_b, gain, proj, dy, *, name):
    S = o_f.shape[0]
    ts = _tile(S, 512)
    rcol = (2 * B_QK_WIDTH + B_V_WIDTH) // B_V_WIDTH

    def body(f_ref, b_ref, g_ref, r_ref, dy_ref, do_ref, dr_ref, dg_ref):
        @pl.when(pl.program_id(0) == 0)
        def _():
            dg_ref[...] = jnp.zeros_like(dg_ref)

        for h in range(B_HEADS):
            sl = slice(h * B_VAL_DIM, (h + 1) * B_VAL_DIM)
            o = f_ref[:, sl] + b_ref[:, sl]
            rs = lax.rsqrt(jnp.mean(o * o, axis=-1, keepdims=True) + RMS_EPS)
            ohat = o * rs
            g = g_ref[:, sl]
            r = r_ref[:, sl]
            sig = 1.0 / (1.0 + jnp.exp(-r))
            dyv = dy_ref[:, sl].astype(F32)
            dn = dyv * (r * sig)
            dr_ref[:, sl] = dyv * (ohat * g) * (sig * (1.0 + r * (1.0 - sig)))
            dng = dn * g
            do_ref[:, sl] = rs * (dng - ohat * jnp.mean(dng * ohat, axis=-1, keepdims=True))
            dg_ref[0:1, sl] += jnp.sum(dn * ohat, axis=0, keepdims=True)

    spec = pl.BlockSpec((ts, B_V_WIDTH), lambda i: (i, 0))
    return pl.pallas_call(
        body, name=name, grid=(S // ts,),
        in_specs=[spec, spec, _const_spec((1, B_V_WIDTH)), pl.BlockSpec((ts, B_V_WIDTH), lambda i: (i, rcol)), spec],
        out_specs=[spec, spec, _const_spec((8, B_V_WIDTH))],
        out_shape=[jax.ShapeDtypeStruct((S, B_V_WIDTH), F32), jax.ShapeDtypeStruct((S, B_V_WIDTH), F32),
                   jax.ShapeDtypeStruct((8, B_V_WIDTH), F32)],
        compiler_params=_params(("arbitrary",)),
    )(o_f, o_b, gain, proj, dy)


def gla_combine(parts_f, parts_b, dr, dz, *, name):
    S = dr.shape[0]
    ts = _tile(S, 512)

    def body(qf, kf, vf, qb, kb, vb, r_ref, z_ref, o_ref):
        o_ref[:, 0:512] = (qf[...] + qb[...]).astype(o_ref.dtype)
        o_ref[:, 512:1024] = (kf[...] + kb[...]).astype(o_ref.dtype)
        o_ref[:, 1024:2048] = (vf[...] + vb[...]).astype(o_ref.dtype)
        o_ref[:, 2048:3072] = r_ref[...].astype(o_ref.dtype)
        o_ref[:, 3072:3200] = z_ref[...].astype(o_ref.dtype)

    s512 = pl.BlockSpec((ts, B_QK_WIDTH), lambda i: (i, 0))
    s1024 = pl.BlockSpec((ts, B_V_WIDTH), lambda i: (i, 0))
    return pl.pallas_call(
        body, name=name, grid=(S // ts,),
        in_specs=[s512, s512, s1024, s512, s512, s1024, s1024, pl.BlockSpec((ts, LANES), lambda i: (i, 0))],
        out_specs=pl.BlockSpec((ts, B_IN_PAD), lambda i: (i, 0)),
        out_shape=jax.ShapeDtypeStruct((S, B_IN_PAD), BF16),
        compiler_params=_params(("parallel",)),
    )(*parts_f, *parts_b, dr, dz)


def adamw(w, g, m, v, *, name):
    R, C = w.shape
    tr = _tile(R, 256)
    c1 = 1.0 / (1.0 - ADAM_B1 ** ADAM_STEP)
    c2 = 1.0 / (1.0 - ADAM_B2 ** ADAM_STEP)

    def body(w_ref, g_ref, m_ref, v_ref, d_ref, mo_ref, vo_ref):
        gv = g_ref[...]
        mn = ADAM_B1 * m_ref[...] + (1.0 - ADAM_B1) * gv
        vn = ADAM_B2 * v_ref[...] + (1.0 - ADAM_B2) * (gv * gv)
        mo_ref[...] = mn
        vo_ref[...] = vn
        d_ref[...] = -ADAM_LR * ((mn * c1) / (jnp.sqrt(vn * c2) + ADAM_EPS) + ADAM_WD * w_ref[...])

    spec = pl.BlockSpec((tr, C), lambda i: (i, 0))
    return pl.pallas_call(
        body, name=name, grid=(R // tr,), in_specs=[spec] * 4, out_specs=[spec] * 3,
        out_shape=[jax.ShapeDtypeStruct((R, C), F32)] * 3,
        compiler_params=_params(("parallel",)),
    )(w, g, m, v)


def _chip_peers():
    x, y, c = lax.axis_index("x"), lax.axis_index("y"), lax.axis_index("c")
    return x, y, c, [(1 - x, y), (x, 1 - y), (1 - x, 1 - y)]


_ANY = pl.BlockSpec(memory_space=pl.ANY)


def gather_shards(src, *, name):
    _, R, C = src.shape

    def body(src_ref, out_ref, ici_send, ici_recv, d2d_send, d2d_recv):
        x, y, c, chips = _chip_peers()
        me = 2 * x + y
        sibling = (x, y, 1 - c)
        sends = []
        for k, (px, py) in enumerate(chips):
            cp = pltpu.make_async_remote_copy(
                src_ref=src_ref.at[c], dst_ref=out_ref.at[me, c], send_sem=ici_send.at[k], recv_sem=ici_recv.at[k],
                device_id=(px, py, c), device_id_type=MESH)
            cp.start()
            sends.append(cp)
        for k, (px, py) in enumerate(chips):
            landed = out_ref.at[2 * px + py, c]
            pltpu.make_async_remote_copy(
                src_ref=src_ref.at[c], dst_ref=landed, send_sem=ici_send.at[k], recv_sem=ici_recv.at[k],
                device_id=(px, py, c), device_id_type=MESH).wait_recv()
            cp = pltpu.make_async_remote_copy(
                src_ref=landed, dst_ref=landed, send_sem=d2d_send.at[k], recv_sem=d2d_recv.at[k],
                device_id=sibling, device_id_type=MESH)
            cp.start()
            sends.append(cp)
        for k, (px, py) in enumerate(chips):
            other_half = out_ref.at[2 * px + py, 1 - c]
            pltpu.make_async_remote_copy(
                src_ref=other_half, dst_ref=other_half, send_sem=d2d_send.at[k], recv_sem=d2d_recv.at[k],
                device_id=sibling, device_id_type=MESH).wait_recv()
        for cp in sends:
            cp.wait_send()

    return pl.pallas_call(
        body, name=name, in_specs=[_ANY], out_specs=_ANY,
        out_shape=jax.ShapeDtypeStruct((N_CHIPS, 2, R, C), src.dtype),
        scratch_shapes=[pltpu.SemaphoreType.DMA((3,))] * 4,
        compiler_params=pltpu.CompilerParams(has_side_effects=True),
    )(src)


def chip_exchange(srcs, *, name):
    n = len(srcs)

    def body(*refs):
        src_refs, out_refs = refs[:n], refs[n:2 * n]
        send_sems, recv_sems = refs[2 * n:]
        x, y, c, chips = _chip_peers()
        me = 2 * x + y
        copies = []
        for i, (src_ref, out_ref) in enumerate(zip(src_refs, out_refs)):
            for k, (px, py) in enumerate(chips):
                cp = pltpu.make_async_remote_copy(
                    src_ref=src_ref.at[2 * px + py], dst_ref=out_ref.at[me],
                    send_sem=send_sems.at[i, k], recv_sem=recv_sems.at[i, k],
                    device_id=(px, py, c), device_id_type=MESH)
                cp.start()
                copies.append(cp)
        for i, (src_ref, out_ref) in enumerate(zip(src_refs, out_refs)):
            for k, (px, py) in enumerate(chips):
                pltpu.make_async_remote_copy(
                    src_ref=src_ref.at[me], dst_ref=out_ref.at[2 * px + py],
                    send_sem=send_sems.at[i, k], recv_sem=recv_sems.at[i, k],
                    device_id=(px, py, c), device_id_type=MESH).wait_recv()
        for cp in copies:
            cp.wait_send()

    return pl.pallas_call(
        body, name=name, in_specs=[_ANY] * n, out_specs=[_ANY] * n,
        out_shape=[jax.ShapeDtypeStruct(s.shape, s.dtype) for s in srcs],
        scratch_shapes=[pltpu.SemaphoreType.DMA((n, 3)), pltpu.SemaphoreType.DMA((n, 3))],
        compiler_params=pltpu.CompilerParams(has_side_effects=True),
    )(*srcs)


def sibling_exchange(srcs, *, name):
    n = len(srcs)

    def body(*refs):
        src_refs, out_refs, (send_sems, recv_sems) = refs[:n], refs[n:2 * n], refs[2 * n:]
        x, y, c = lax.axis_index("x"), lax.axis_index("y"), lax.axis_index("c")
        copies = []
        for i, (src_ref, out_ref) in enumerate(zip(src_refs, out_refs)):
            cp = pltpu.make_async_remote_copy(
                src_ref=src_ref.at[:, 1 - c], dst_ref=out_ref, send_sem=send_sems.at[i], recv_sem=recv_sems.at[i],
                device_id=(x, y, 1 - c), device_id_type=MESH)
            cp.start()
            copies.append(cp)
        for cp in copies:
            cp.wait()

    return pl.pallas_call(
        body, name=name, in_specs=[_ANY] * n, out_specs=[_ANY] * n,
        out_shape=[jax.ShapeDtypeStruct((s.shape[0],) + s.shape[2:], s.dtype) for s in srcs],
        scratch_shapes=[pltpu.SemaphoreType.DMA((n,)), pltpu.SemaphoreType.DMA((n,))],
        compiler_params=pltpu.CompilerParams(has_side_effects=True),
    )(*srcs)


def sibling_share(bufs, *, name):
    n = len(bufs)

    def body(*refs):
        out_refs, (send_sems, recv_sems) = refs[n:2 * n], refs[2 * n:]
        x, y, c = lax.axis_index("x"), lax.axis_index("y"), lax.axis_index("c")
        copies = []
        for i, out_ref in enumerate(out_refs):
            cp = pltpu.make_async_remote_copy(
                src_ref=out_ref.at[c], dst_ref=out_ref.at[c], send_sem=send_sems.at[i], recv_sem=recv_sems.at[i],
                device_id=(x, y, 1 - c), device_id_type=MESH)
            cp.start()
            copies.append(cp)
        for cp in copies:
            cp.wait()

    return pl.pallas_call(
        body, name=name, in_specs=[_ANY] * n, out_specs=[_ANY] * n,
        out_shape=[jax.ShapeDtypeStruct(b.shape, b.dtype) for b in bufs],
        input_output_aliases={i: i for i in range(n)},
        scratch_shapes=[pltpu.SemaphoreType.DMA((n,))] * 2,
        compiler_params=pltpu.CompilerParams(has_side_effects=True),
    )(*bufs)


def add_pair(a, b, out_dtype, *, name):
    n, _, R, C = a.shape
    tr = _tile(R, 256)

    def body(c_ref, a_ref, b_ref, o_ref):
        o_ref[...] = (a_ref[0] + b_ref[...]).astype(o_ref.dtype)

    return pl.pallas_call(
        body, name=name,
        grid_spec=pltpu.PrefetchScalarGridSpec(
            num_scalar_prefetch=1, grid=(n, R // tr),
            in_specs=[pl.BlockSpec((1, 1, tr, C), lambda s, i, c_ref: (s, c_ref[0], i, 0)),
                      pl.BlockSpec((1, tr, C), lambda s, i, c_ref: (s, i, 0))],
            out_specs=pl.BlockSpec((1, tr, C), lambda s, i, c_ref: (s, i, 0))),
        out_shape=jax.ShapeDtypeStruct((n, R, C), out_dtype),
        compiler_params=_params(("parallel", "parallel")),
    )(lax.axis_index("c").reshape(1).astype(jnp.int32), a, b)


def sum_slots(slots, own, *, name):
    n, R, C = slots.shape
    tr = _tile(R, 256)

    def body(ids_ref, *refs):
        slot_refs, own_ref, o_ref = refs[:n], refs[n], refs[n + 1]
        me = ids_ref[0]
        acc = None
        for s, r in enumerate(slot_refs):
            t = jnp.where(me == s, own_ref[0], r[0]).astype(F32)
            acc = t if acc is None else acc + t
        o_ref[0] = acc

    def slot_spec(s):
        return pl.BlockSpec((1, tr, C), lambda i, ids: (jnp.where(ids[0] == s, (s + 1) % n, s), i, 0))

    x, y, c = lax.axis_index("x"), lax.axis_index("y"), lax.axis_index("c")
    return pl.pallas_call(
        body, name=name,
        grid_spec=pltpu.PrefetchScalarGridSpec(
            num_scalar_prefetch=1, grid=(R // tr,),
            in_specs=[slot_spec(s) for s in range(n)] + [pl.BlockSpec((1, tr, C), lambda i, ids: (ids[0], i, 0))],
            out_specs=pl.BlockSpec((1, tr, C), lambda i, ids: (ids[1], i, 0))),
        out_shape=jax.ShapeDtypeStruct((2, R, C), F32),
        compiler_params=_params(("parallel",)),
    )(jnp.stack([2 * x + y, c]).astype(jnp.int32), *([slots] * n), own)


def _flat_rows(n_elems, mult):
    rows = -(-n_elems // FLAT_COLS)
    return -(-rows // mult) * mult


def _to_flat(parts, mult):
    v = jnp.concatenate([p.reshape(-1) for p in parts])
    rows = _flat_rows(v.shape[0], mult)
    return jnp.pad(v, (0, rows * FLAT_COLS - v.shape[0])).reshape(rows, FLAT_COLS)


def _from_flat(flat, shapes):
    v = flat.reshape(-1)
    out, off = [], 0
    for s in shapes:
        n = int(np.prod(s))
        out.append(v[off:off + n].reshape(s))
        off += n
    return out


def _unshard(blocks, axis):
    return jnp.concatenate([blocks[s] for s in range(N_CHIPS)], axis=axis)


def _by_shard(full, axis):
    shp = full.shape
    cut = full.reshape(shp[:axis] + (N_CHIPS, shp[axis] // N_CHIPS) + shp[axis + 1:])
    return jnp.moveaxis(cut, axis, 0)


def _gradient_blocks(grads):
    blocks = []
    for n in MATRICES:
        t = _by_shard(jnp.stack(grads[n]), SHARD_AXIS[n])
        blocks.append(t.reshape(N_CHIPS, 2, -1, t.shape[-1]))
    rest = []
    for s in range(N_CHIPS):
        parts = [jnp.stack(grads[n]) if n in REPLICATED else _by_shard(jnp.stack(grads[n]), SHARD_AXIS[n])[s]
                 for n in WEIGHTS if n not in MATRICES]
        rest.append(_to_flat(parts, 16))
    rest = jnp.stack(rest)
    blocks.append(rest.reshape(N_CHIPS, 2, rest.shape[1] // 2, FLAT_COLS))
    return blocks


def _gather_weights(w):
    full = {}
    for names, dtype, mult, call in ((BIG, BF16, 32, "gather_weights"), (SMALL_SHARDED, F32, 16, "gather_vectors")):
        parts = [w[n].astype(dtype) for n in names]
        flat = _to_flat(parts, mult)
        got = gather_shards(flat.reshape(2, flat.shape[0] // 2, FLAT_COLS), name=call)
        got = got.reshape(N_CHIPS, flat.shape[0], FLAT_COLS)
        me = 2 * lax.axis_index("x") + lax.axis_index("y")
        got = lax.dynamic_update_slice(got, flat[None], (me, 0, 0))
        per_chip = [_from_flat(got[s], [p.shape for p in parts]) for s in range(N_CHIPS)]
        for i, n in enumerate(names):
            full[n] = _unshard([per_chip[s][i] for s in range(N_CHIPS)], SHARD_AXIS[n])
    return full


def _reduce_gradients(grads, w):
    halves = _gradient_blocks(grads)
    other = sibling_exchange(halves, name="grad_pair_exchange")
    pair = [add_pair(a, b, BF16 if i < len(MATRICES) else F32, name=f"grad_pair_add{i}")
            for i, (a, b) in enumerate(zip(halves, other))]
    slots = chip_exchange(pair, name="grad_chip_exchange")
    mine = [sum_slots(a, p, name=f"grad_chip_sum{i}") for i, (a, p) in enumerate(zip(slots, pair))]
    both = sibling_share(mine, name="grad_half_share")
    g = {n: t.reshape(w[n].shape) for n, t in zip(MATRICES, both)}
    rest = [n for n in WEIGHTS if n not in MATRICES]
    g.update(zip(rest, _from_flat(both[-1], [w[n].shape for n in rest])))
    return g


def _layer_fwd(i, h, p, aux):
    j = i // 2
    sv = {"h0": h}
    if i % 2 == 0:
        hns = rmsnorm_fwd(h, p["attn_norm"][i][None], dils=DILS, name=f"l{i}_norm1")
        qkvs, qkvps, os_, lses = [], [], [], []
        for g, d in enumerate(DILS):
            qkv = matmul(hns[g], p["a_w_in_g"][j][g], tn=3 * A_WIDTH // 2, name=f"l{i}_a_in{g}")
            qkvp = qk_prep_fwd(qkv, aux["a_gain"][j][g], aux["cos"][g], aux["sin"][g], aux["ones"], name=f"l{i}_a_prep{g}")
            o, l = attn_fwd(qkvp, d, name=f"l{i}_a_attn{g}")
            qkvs.append(qkv)
            qkvps.append(qkvp)
            os_.append(o)
            lses.append(l)
        out, lse = attn_merge(os_, lses, DILS, name=f"l{i}_a_merge")
        sv.update(hns=hns, qkv=qkvs, qkvp=qkvps, out=out, lse=lse)
        h1 = matmul(out, p["a_w_out"][j], res=h, name=f"l{i}_a_out")
    else:
        hn, sv["hn_t"] = rmsnorm_fwd(h, p["attn_norm"][i][None], with_transpose=True, name=f"l{i}_norm1")
        proj = matmul(hn, p["b_w_in"][j], tn=640, name=f"l{i}_b_in")
        loga = gate_fwd(proj, aux["b_wblk"][j], aux["b_bias"][j], name=f"l{i}_b_gate")
        o_f, st_f = gla_fwd(proj, loga, aux["tri_f"], False, name=f"l{i}_b_gla_f")
        o_b, st_b = gla_fwd(proj, loga, aux["tri_b"], True, name=f"l{i}_b_gla_b")
        y = gla_post_fwd(o_f, o_b, aux["b_gain"][j], proj, name=f"l{i}_b_post")
        sv.update(proj=proj, loga=loga, o_f=o_f, o_b=o_b, st_f=st_f, st_b=st_b, y=y)
        h1 = matmul(y, p["b_w_out"][j], res=h, name=f"l{i}_b_out")
    sv["h1"] = h1
    hn2, hn2_t = rmsnorm_fwd(h1, p["ffn_norm"][i][None], with_transpose=True, name=f"l{i}_norm2")
    gu = matmul(hn2, p["ffn_w_gate_up"][i], out_dtype=BF16, tn=FFN_HIDDEN // 2, name=f"l{i}_f_up")
    act, act_t = swiglu_fwd(gu, name=f"l{i}_f_act")
    h2 = matmul(act, p["ffn_w_down"][i], res=h1, tk=2816, name=f"l{i}_f_down")
    sv.update(hn2_t=hn2_t, gu=gu, act_t=act_t)
    return h2, sv


def _layer_bwd(i, dh, p, pt, aux, sv, grads):
    j = i // 2
    dhb = dh.astype(BF16)
    grads["ffn_w_down"][i] = matmul(sv["act_t"], dhb, tm=FFN_HIDDEN // 2, tk=2048, name=f"l{i}_f_down_dw")
    dact = matmul(dhb, pt["ffn_w_down"][i], out_dtype=BF16, tn=FFN_HIDDEN // 2, name=f"l{i}_f_down_dx")
    dgu = swiglu_bwd(sv["gu"], dact, name=f"l{i}_f_act_bwd")
    grads["ffn_w_gate_up"][i] = matmul(sv["hn2_t"], dgu, tk=2048, name=f"l{i}_f_up_dw")
    dhn2 = matmul(dgu, pt["ffn_w_gate_up"][i], tk=2816, name=f"l{i}_f_up_dx")
    dh1, dg = rmsnorm_bwd(sv["h1"], p["ffn_norm"][i][None], [dhn2], dh, name=f"l{i}_norm2_bwd")
    grads["ffn_norm"][i] = dg[0]
    dh1b = dh1.astype(BF16)
    if i % 2 == 0:
        grads["a_w_out"][j] = matmul(sv["out"].T, dh1b, tk=2048, name=f"l{i}_a_out_dw")
        dout = matmul(dh1b, pt["a_w_out"][j], out_dtype=BF16, name=f"l{i}_a_out_dx")
        douts, deltas = attn_delta(dout, sv["out"], aux["ones"], DILS, name=f"l{i}_a_delta")
        dws, dhns, dgq, dgk = [], [], [], []
        for g, d in enumerate(DILS):
            qkvp, lse = sv["qkvp"][g], sv["lse"][g]
            dq = attn_bwd_dq(qkvp, douts[g], lse, deltas[g], d, name=f"l{i}_a_dq{g}")
            dk, dv = attn_bwd_dkv(qkvp, douts[g], lse, deltas[g], d, name=f"l{i}_a_dkv{g}")
            dqkv, dgain = qk_prep_bwd(sv["qkv"][g], aux["a_gain"][j][g], aux["cos"][g], aux["sin"][g], aux["ones"],
                                      [dq, dk, dv], name=f"l{i}_a_prep_bwd{g}")
            dgh = dgain[0].reshape(3, A_HEADS, A_HEAD_DIM).sum(axis=1)
            dgq.append(dgh[0])
            dgk.append(dgh[1])
            dws.append(matmul(sv["hns"][g].T, dqkv, tk=2048, name=f"l{i}_a_in_dw{g}"))
            dhns.append(matmul(dqkv, pt["a_w_in_g"][j][g], tk=3072, name=f"l{i}_a_in_dx{g}"))
        grads["a_q_norm"][j] = jnp.stack(dgq)
        grads["a_k_norm"][j] = jnp.stack(dgk)
        grads["a_w_in"][j] = jnp.concatenate(dws, axis=1)
        dh0, dg = rmsnorm_bwd(sv["h0"], p["attn_norm"][i][None], dhns, dh1, dils=DILS, name=f"l{i}_norm1_bwd")
    else:
        grads["b_w_out"][j] = matmul(sv["y"].T, dh1b, tk=2048, name=f"l{i}_b_out_dw")
        dy = matmul(dh1b, pt["b_w_out"][j], name=f"l{i}_b_out_dx")
        do, dr, dgn = gla_post_bwd(sv["o_f"], sv["o_b"], aux["b_gain"][j], sv["proj"], dy, name=f"l{i}_b_post_bwd")
        grads["b_out_norm"][j] = dgn[0].reshape(B_HEADS, B_VAL_DIM)
        pf = gla_bwd(sv["proj"], sv["loga"], sv["st_f"], do, aux["tri_f"], aux["tri_b"], False, name=f"l{i}_b_gla_f_bwd")
        pb = gla_bwd(sv["proj"], sv["loga"], sv["st_b"], do, aux["tri_b"], aux["tri_f"], True, name=f"l{i}_b_gla_b_bwd")
        dloga = jnp.concatenate([pf[3], pb[3]], axis=1)
        dz, dwblk, dbias = gate_bwd(sv["proj"], aux["b_wblk"][j], aux["b_wblk_t"][j], aux["b_bias"][j], dloga,
                                    name=f"l{i}_b_gate_bwd")
        grads["b_w_gate_f"][j] = dwblk[:B_GATE_RANK, :B_QK_WIDTH]
        grads["b_w_gate_b"][j] = dwblk[B_GATE_RANK:2 * B_GATE_RANK, B_QK_WIDTH:]
        grads["b_gate_bias_f"][j] = dbias[0, :B_QK_WIDTH]
        grads["b_gate_bias_b"][j] = dbias[0, B_QK_WIDTH:]
        dproj = gla_combine(pf[:3], pb[:3], dr, dz, name=f"l{i}_b_combine")
        grads["b_w_in"][j] = matmul(sv["hn_t"], dproj, tn=640, tk=2048, name=f"l{i}_b_in_dw")[:, :B_IN_WIDTH]
        dhn = matmul(dproj, pt["b_w_in"][j], tk=B_IN_PAD, name=f"l{i}_b_in_dx")
        dh0, dg = rmsnorm_bwd(sv["h0"], p["attn_norm"][i][None], [dhn], dh1, name=f"l{i}_norm1_bwd")
    grads["attn_norm"][i] = dg[0]
    return dh0


def _local_step(x, target, p, small):
    S = x.shape[0]
    cos, sin = _rope_tables(S)
    to_phase = lambda t, d: t.reshape(S // d, d, LANES).swapaxes(0, 1).reshape(S, LANES)
    cos, sin = [to_phase(cos, d) for d in DILS], [to_phase(sin, d) for d in DILS]
    ones_v = jnp.ones((A_WIDTH,), F32)
    a_gain = [[jnp.concatenate([jnp.tile(small["a_q_norm"][j][g], A_HEADS), jnp.tile(small["a_k_norm"][j][g], A_HEADS),
                                ones_v])[None] for g in range(len(DILS))] for j in range(2)]
    b_wblk = [_gate_block_weight(p["b_w_gate_f"][j].astype(F32), p["b_w_gate_b"][j].astype(F32)) for j in range(2)]
    aux = dict(cos=cos, sin=sin, ones=_head_block_ones(), a_gain=a_gain, tri_f=_tri(False), tri_b=_tri(True),
               b_wblk=b_wblk, b_wblk_t=[w.T for w in b_wblk],
               b_bias=[jnp.concatenate([small["b_gate_bias_f"][j], small["b_gate_bias_b"][j]])[None] for j in range(2)],
               b_gain=[small["b_out_norm"][j].reshape(1, B_V_WIDTH) for j in range(2)])
    pw = dict(p)
    pw["b_w_in"] = jnp.pad(p["b_w_in"], ((0, 0), (0, 0), (0, B_IN_PAD - B_IN_WIDTH)))
    pw["attn_norm"], pw["ffn_norm"] = small["attn_norm"], small["ffn_norm"]
    gw = 3 * A_WIDTH
    pw["a_w_in_g"] = [[p["a_w_in"][j][:, g * gw:(g + 1) * gw] for g in range(len(DILS))] for j in range(2)]
    pt = {n: jnp.swapaxes(pw[n], 1, 2) for n in ("a_w_out", "b_w_in", "b_w_out", "ffn_w_gate_up", "ffn_w_down")}
    pt["a_w_in_g"] = [[wg.T for wg in row] for row in pw["a_w_in_g"]]

    h = x
    saved = []
    for i in range(DEPTH):
        h, sv = _layer_fwd(i, h, pw, aux)
        saved.append(sv)
    loss_sq, dh = loss_head(h, target, name="loss_head")
    grads = {n: [None] * (DEPTH if n in ("attn_norm", "ffn_norm", "ffn_w_gate_up", "ffn_w_down") else 2) for n in WEIGHTS}
    for i in reversed(range(DEPTH)):
        dh = _layer_bwd(i, dh, pw, pt, aux, saved[i], grads)
    return loss_sq[0, 0] * (0.5 / D_MODEL), dh, grads


def kernel(x, attn_norm, ffn_norm, a_w_in, a_q_norm, a_k_norm, a_w_out, b_w_in, b_w_gate_f, b_gate_bias_f, b_w_gate_b, b_gate_bias_b, b_out_norm, b_w_out, ffn_w_gate_up, ffn_w_down, loss_target, m_attn_norm, m_ffn_norm, m_a_w_in, m_a_q_norm, m_a_k_norm, m_a_w_out, m_b_w_in, m_b_w_gate_f, m_b_gate_bias_f, m_b_w_gate_b, m_b_gate_bias_b, m_b_out_norm, m_b_w_out, m_ffn_w_gate_up, m_ffn_w_down, v_attn_norm, v_ffn_norm, v_a_w_in, v_a_q_norm, v_a_k_norm, v_a_w_out, v_b_w_in, v_b_w_gate_f, v_b_gate_bias_f, v_b_w_gate_b, v_b_gate_bias_b, v_b_out_norm, v_b_w_out, v_ffn_w_gate_up, v_ffn_w_down):
    w = dict(attn_norm=attn_norm, ffn_norm=ffn_norm, a_w_in=a_w_in, a_q_norm=a_q_norm, a_k_norm=a_k_norm, a_w_out=a_w_out,
             b_w_in=b_w_in, b_w_gate_f=b_w_gate_f, b_gate_bias_f=b_gate_bias_f, b_w_gate_b=b_w_gate_b,
             b_gate_bias_b=b_gate_bias_b, b_out_norm=b_out_norm, b_w_out=b_w_out, ffn_w_gate_up=ffn_w_gate_up,
             ffn_w_down=ffn_w_down)
    m = dict(attn_norm=m_attn_norm, ffn_norm=m_ffn_norm, a_w_in=m_a_w_in, a_q_norm=m_a_q_norm, a_k_norm=m_a_k_norm,
             a_w_out=m_a_w_out, b_w_in=m_b_w_in, b_w_gate_f=m_b_w_gate_f, b_gate_bias_f=m_b_gate_bias_f,
             b_w_gate_b=m_b_w_gate_b, b_gate_bias_b=m_b_gate_bias_b, b_out_norm=m_b_out_norm, b_w_out=m_b_w_out,
             ffn_w_gate_up=m_ffn_w_gate_up, ffn_w_down=m_ffn_w_down)
    v = dict(attn_norm=v_attn_norm, ffn_norm=v_ffn_norm, a_w_in=v_a_w_in, a_q_norm=v_a_q_norm, a_k_norm=v_a_k_norm,
             a_w_out=v_a_w_out, b_w_in=v_b_w_in, b_w_gate_f=v_b_w_gate_f, b_gate_bias_f=v_b_gate_bias_f,
             b_w_gate_b=v_b_w_gate_b, b_gate_bias_b=v_b_gate_bias_b, b_out_norm=v_b_out_norm, b_w_out=v_b_w_out,
             ffn_w_gate_up=v_ffn_w_gate_up, ffn_w_down=v_ffn_w_down)

    full = _gather_weights(w)
    p = {n: full[n] for n in BIG}
    small = {n: full[n] for n in SMALL_SHARDED}
    small.update({n: w[n] for n in REPLICATED})
    loss_local, dx, grads = _local_step(x[0], loss_target[0], p, small)
    loss = lax.psum(loss_local, ("x", "y", "c"))

    g = _reduce_gradients(grads, w)
    delta, new_m, new_v = {}, {}, {}
    rows = lambda t: t.reshape(-1, t.shape[-1])
    for n in MATRICES:
        outs = adamw(rows(w[n]), rows(g[n]), rows(m[n]), rows(v[n]), name=f"adamw_{n}")
        delta[n], new_m[n], new_v[n] = [o.reshape(w[n].shape) for o in outs]
    rest = [n for n in WEIGHTS if n not in MATRICES]
    flat = lambda d: _to_flat([d[n] for n in rest], 8)
    outs = adamw(flat(w), flat(g), flat(m), flat(v), name="adamw_vectors")
    for d, o in zip((delta, new_m, new_v), outs):
        d.update(zip(rest, _from_flat(o, [w[n].shape for n in rest])))
    return (loss, dx[None], *[g[n] for n in WEIGHTS], *[delta[n] for n in WEIGHTS],
            *[new_m[n] for n in WEIGHTS], *[new_v[n] for n in WEIGHTS])
```

```python
import functools

import numpy as np
import jax
import jax.numpy as jnp
from jax import lax
from jax.experimental import pallas as pl
from jax.experimental.pallas import tpu as pltpu

F32, BF16 = jnp.float32, jnp.bfloat16
HI = lax.Precision.HIGHEST
MESH = pl.DeviceIdType.MESH

D_MODEL = 1024
DEPTH = 4
RMS_EPS = 1e-6
NEG_INF = -1e30
A_GROUPS = ((128, 1), (512, 4), (2048, 16))
DILS = tuple(d for _, d in A_GROUPS)
A_HALF = 64
A_HEADS = 16
A_HEAD_DIM = 64
A_WIDTH = 1024
A_IN_WIDTH = 9216
ROPE_THETA = 10000.0
B_HEADS = 4
B_KEY_DIM = 128
B_VAL_DIM = 256
B_QK_WIDTH = 512
B_V_WIDTH = 1024
B_GATE_RANK = 16
B_GATE_TAU = 16.0
B_CHUNK = 64
B_IN_WIDTH = 3104
B_IN_PAD = 3200
FFN_HIDDEN = 2816
ADAM_LR, ADAM_B1, ADAM_B2, ADAM_EPS, ADAM_WD, ADAM_STEP = 0.001, 0.9, 0.999, 1e-08, 0.01, 10
LANES = 128
VMEM_LIMIT = 48 * 1024 * 1024
FLAT_COLS = 1024
N_CHIPS = 4

WEIGHTS = ['attn_norm', 'ffn_norm', 'a_w_in', 'a_q_norm', 'a_k_norm', 'a_w_out', 'b_w_in', 'b_w_gate_f',
           'b_gate_bias_f', 'b_w_gate_b', 'b_gate_bias_b', 'b_out_norm', 'b_w_out', 'ffn_w_gate_up', 'ffn_w_down']
REPLICATED = ('attn_norm', 'ffn_norm', 'a_q_norm', 'a_k_norm')
SHARD_AXIS = {'a_w_in': 2, 'a_w_out': 1, 'b_w_in': 2, 'b_w_gate_f': 2, 'b_gate_bias_f': 1, 'b_w_gate_b': 2,
              'b_gate_bias_b': 1, 'b_out_norm': 2, 'b_w_out': 1, 'ffn_w_gate_up': 2, 'ffn_w_down': 1}
BIG = ('a_w_in', 'a_w_out', 'b_w_in', 'b_w_gate_f', 'b_w_gate_b', 'b_w_out', 'ffn_w_gate_up', 'ffn_w_down')
SMALL_SHARDED = ('b_gate_bias_f', 'b_gate_bias_b', 'b_out_norm')
MATRICES = ('a_w_in', 'a_w_out', 'b_w_in', 'b_w_out', 'ffn_w_gate_up', 'ffn_w_down')


def _params(sem):
    return pltpu.CompilerParams(dimension_semantics=sem, vmem_limit_bytes=VMEM_LIMIT)


def _tile(n, pref):
    t = min(n, pref)
    while n % t:
        t //= 2
    return t


def _const_spec(shape):
    nd = len(shape)
    return pl.BlockSpec(shape, lambda *_: (0,) * nd)


def matmul(a, b, *, name, out_dtype=F32, res=None, tm=1024, tn=512, tk=1024):
    M, K = a.shape
    N = b.shape[1]
    assert b.shape[0] == K
    tm, tn, tk = _tile(M, tm), _tile(N, tn), _tile(K, tk)
    nk = K // tk

    def body(*refs):
        a_ref, b_ref = refs[:2]
        r_ref = refs[2] if res is not None else None
        o_ref = refs[3 if res is not None else 2]
        part = jnp.dot(a_ref[...], b_ref[...], preferred_element_type=F32)

        def finish(v):
            if res is not None:
                v = v + r_ref[...]
            o_ref[...] = v.astype(o_ref.dtype)

        if nk == 1:
            finish(part)
            return
        acc_ref = refs[-1]
        k = pl.program_id(2)

        @pl.when(k == 0)
        def _():
            acc_ref[...] = part

        @pl.when((k > 0) & (k < nk - 1))
        def _():
            acc_ref[...] += part

        @pl.when(k == nk - 1)
        def _():
            finish(acc_ref[...] + part)

    in_specs = [pl.BlockSpec((tm, tk), lambda i, j, k: (i, k)), pl.BlockSpec((tk, tn), lambda i, j, k: (k, j))]
    args = [a, b]
    if res is not None:
        in_specs.append(pl.BlockSpec((tm, tn), lambda i, j, k: (i, j)))
        args.append(res)
    return pl.pallas_call(
        body, name=name, grid=(M // tm, N // tn, nk), in_specs=in_specs,
        out_specs=pl.BlockSpec((tm, tn), lambda i, j, k: (i, j)),
        out_shape=jax.ShapeDtypeStruct((M, N), out_dtype),
        scratch_shapes=[pltpu.VMEM((tm, tn), F32)] if nk > 1 else [],
        compiler_params=_params(("parallel", "parallel", "arbitrary")),
    )(*args)


def _phase_spec(d, ts, W):
    if d == 1:
        return pl.BlockSpec((ts, W), lambda i: (i, 0))
    return pl.BlockSpec((d, ts // d, W), lambda i: (0, i, 0))


def _phase_view(a, d):
    return a if d == 1 else a.reshape(d, a.shape[0] // d, a.shape[1])


def _phase_shape(S, W, d, dtype):
    return jax.ShapeDtypeStruct((S, W) if d == 1 else (d, S // d, W), dtype)


def _nat_scratch(ts, W):
    return pltpu.VMEM((W // LANES, ts, LANES), F32)


def _put_natural(nat_ref, value):
    for c in range(nat_ref.shape[0]):
        nat_ref[c] = value[:, c * LANES:(c + 1) * LANES]


def _get_natural(nat_ref):
    return jnp.concatenate([nat_ref[c] for c in range(nat_ref.shape[0])], axis=1)


def _store_phases(nat_ref, o_ref, d, ts):
    for p in range(d):
        for c in range(nat_ref.shape[0]):
            o_ref[p, :, c * LANES:(c + 1) * LANES] = nat_ref[c, pl.ds(p, ts // d, stride=d), :].astype(o_ref.dtype)


def _load_phases(i_ref, nat_ref, d, ts):
    for p in range(d):
        for c in range(nat_ref.shape[0]):
            nat_ref[c, pl.ds(p, ts // d, stride=d), :] = i_ref[p, :, c * LANES:(c + 1) * LANES].astype(F32)


def rmsnorm_fwd(x, gain, *, name, dils=(1,), with_transpose=False):
    S, Dm = x.shape
    ts = _tile(S, 512)
    nd = len(dils)

    def body(x_ref, g_ref, *rest):
        o_refs, scr = rest[:nd], rest[-1]
        xv = x_ref[...]
        r = lax.rsqrt(jnp.mean(xv * xv, axis=-1, keepdims=True) + RMS_EPS)
        y = (xv * r) * g_ref[...]
        if any(d > 1 for d in dils):
            _put_natural(scr, y)
        for d, o_ref in zip(dils, o_refs):
            if d == 1:
                o_ref[...] = y.astype(o_ref.dtype)
            else:
                _store_phases(scr, o_ref, d, ts)
        if with_transpose:
            rest[nd][...] = y.T.astype(BF16)

    outs = pl.pallas_call(
        body, name=name, grid=(S // ts,),
        in_specs=[pl.BlockSpec((ts, Dm), lambda i: (i, 0)), _const_spec((1, Dm))],
        out_specs=[_phase_spec(d, ts, Dm) for d in dils] + [pl.BlockSpec((Dm, ts), lambda i: (0, i))] * with_transpose,
        out_shape=[_phase_shape(S, Dm, d, BF16) for d in dils] + [jax.ShapeDtypeStruct((Dm, S), BF16)] * with_transpose,
        scratch_shapes=[_nat_scratch(ts, Dm)],
        compiler_params=_params(("parallel",)),
    )(x, gain)
    return [o.reshape(S, Dm) for o in outs[:nd]] + list(outs[nd:])


def rmsnorm_bwd(x, gain, dys, dres, *, name, dils=(1,)):
    S, Dm = x.shape
    ts = _tile(S, 256)
    nd = len(dils)

    def body(x_ref, g_ref, *rest):
        dy_refs, (dr_ref, dx_ref, dg_ref, scr) = rest[:nd], rest[nd:]

        @pl.when(pl.program_id(0) == 0)
        def _():
            dg_ref[...] = jnp.zeros_like(dg_ref)

        dyv = None
        for d, dy_ref in zip(dils, dy_refs):
            if d == 1:
                t = dy_ref[...].astype(F32)
            else:
                _load_phases(dy_ref, scr, d, ts)
                t = _get_natural(scr)
            dyv = t if dyv is None else dyv + t
        xv = x_ref[...]
        r = lax.rsqrt(jnp.mean(xv * xv, axis=-1, keepdims=True) + RMS_EPS)
        xhat = xv * r
        dyg = dyv * g_ref[...]
        dx = r * (dyg - xhat * jnp.mean(dyg * xhat, axis=-1, keepdims=True))
        dx_ref[...] = dr_ref[...] + dx
        dg_ref[0:1, :] += jnp.sum(dyv * xhat, axis=0, keepdims=True)

    row = pl.BlockSpec((ts, Dm), lambda i: (i, 0))
    return pl.pallas_call(
        body, name=name, grid=(S // ts,),
        in_specs=[row, _const_spec((1, Dm))] + [_phase_spec(d, ts, Dm) for d in dils] + [row],
        out_specs=[row, _const_spec((8, Dm))],
        out_shape=[jax.ShapeDtypeStruct((S, Dm), F32), jax.ShapeDtypeStruct((8, Dm), F32)],
        scratch_shapes=[_nat_scratch(ts, Dm)],
        compiler_params=_params(("arbitrary",)),
    )(x, gain, *[_phase_view(dy, d) for dy, d in zip(dys, dils)], dres)


def swiglu_fwd(gu, *, name):
    S, F2 = gu.shape
    Fh = F2 // 2
    ts = _tile(S, 512)

    def body(g_ref, u_ref, o_ref, ot_ref):
        g = g_ref[...].astype(F32)
        u = u_ref[...].astype(F32)
        act = g * (1.0 / (1.0 + jnp.exp(-g))) * u
        o_ref[...] = act.astype(o_ref.dtype)
        ot_ref[...] = act.T.astype(ot_ref.dtype)

    return pl.pallas_call(
        body, name=name, grid=(S // ts,),
        in_specs=[pl.BlockSpec((ts, Fh), lambda i: (i, 0)), pl.BlockSpec((ts, Fh), lambda i: (i, 1))],
        out_specs=[pl.BlockSpec((ts, Fh), lambda i: (i, 0)), pl.BlockSpec((Fh, ts), lambda i: (0, i))],
        out_shape=[jax.ShapeDtypeStruct((S, Fh), BF16), jax.ShapeDtypeStruct((Fh, S), BF16)],
        compiler_params=_params(("parallel",)),
    )(gu, gu)


def swiglu_bwd(gu, dact, *, name):
    S, F2 = gu.shape
    Fh = F2 // 2
    ts = _tile(S, 256)

    def body(gu_ref, d_ref, o_ref):
        g = gu_ref[:, :Fh].astype(F32)
        u = gu_ref[:, Fh:].astype(F32)
        d = d_ref[...].astype(F32)
        sig = 1.0 / (1.0 + jnp.exp(-g))
        o_ref[:, :Fh] = (d * u * (sig * (1.0 + g * (1.0 - sig)))).astype(o_ref.dtype)
        o_ref[:, Fh:] = (d * (g * sig)).astype(o_ref.dtype)

    return pl.pallas_call(
        body, name=name, grid=(S // ts,),
        in_specs=[pl.BlockSpec((ts, F2), lambda i: (i, 0)), pl.BlockSpec((ts, Fh), lambda i: (i, 0))],
        out_specs=pl.BlockSpec((ts, F2), lambda i: (i, 0)),
        out_shape=jax.ShapeDtypeStruct((S, F2), BF16),
        compiler_params=_params(("parallel",)),
    )(gu, dact)


def loss_head(y, target, *, name):
    S, Dm = y.shape
    ts = _tile(S, 512)

    def body(y_ref, t_ref, l_ref, d_ref):
        @pl.when(pl.program_id(0) == 0)
        def _():
            l_ref[...] = jnp.zeros_like(l_ref)

        e = y_ref[...] - t_ref[...]
        d_ref[...] = e * (1.0 / Dm)
        l_ref[...] += jnp.sum(e * e)

    return pl.pallas_call(
        body, name=name, grid=(S // ts,),
        in_specs=[pl.BlockSpec((ts, Dm), lambda i: (i, 0)), pl.BlockSpec((ts, Dm), lambda i: (i, 0))],
        out_specs=[_const_spec((8, LANES)), pl.BlockSpec((ts, Dm), lambda i: (i, 0))],
        out_shape=[jax.ShapeDtypeStruct((8, LANES), F32), jax.ShapeDtypeStruct((S, Dm), F32)],
        compiler_params=_params(("arbitrary",)),
    )(y, target)


def _head_block_ones():
    i = np.arange(LANES)
    return jnp.asarray((i[:, None] // A_HEAD_DIM == i[None, :] // A_HEAD_DIM).astype(np.float32)).astype(BF16)


def _rope_tables(S):
    half = A_HEAD_DIM // 2
    inv_freq = ROPE_THETA ** (-jnp.arange(half, dtype=F32) / half)
    ang = jnp.arange(S).astype(F32)[:, None] * inv_freq[None, :]
    cos = jnp.tile(jnp.cos(ang), (1, LANES // half))
    sin = jnp.tile(jnp.sin(ang), (1, LANES // half))
    return cos, sin


def _rot_half(x, lo):
    return jnp.where(lo, -pltpu.roll(x, LANES - 32, 1), pltpu.roll(x, 32, 1))


def _seg_sum(v, ones_ref):
    hi = v.astype(BF16)
    lo = (v - hi.astype(F32)).astype(BF16)
    ones = ones_ref[...]
    return jnp.dot(hi, ones, preferred_element_type=F32) + jnp.dot(lo, ones, preferred_element_type=F32)


def _seg_mean(v, ones_ref):
    return _seg_sum(v, ones_ref) * (1.0 / A_HEAD_DIM)


def qk_prep_fwd(qkv, gain, cos, sin, ones, *, name):
    S, W = qkv.shape
    ts = _tile(S, 1024)
    nchunk = A_WIDTH // LANES

    def body(x_ref, g_ref, c_ref, s_ref, ones_ref, o_ref):
        kind = pl.program_id(1) % 3

        @pl.when(kind < 2)
        def _():
            scale = jnp.where(kind == 0, A_HEAD_DIM ** -0.5, 1.0).astype(F32)
            lo = (lax.broadcasted_iota(jnp.int32, (ts, LANES), 1) % A_HEAD_DIM) < (A_HEAD_DIM // 2)
            cv, sv = c_ref[...], s_ref[...]
            for c in range(nchunk):
                sl = slice(c * LANES, (c + 1) * LANES)
                xv = x_ref[:, sl]
                r = lax.rsqrt(_seg_mean(xv * xv, ones_ref) + RMS_EPS)
                y = (xv * r) * g_ref[:, sl]
                y = y * cv + _rot_half(y, lo) * sv
                o_ref[:, sl] = (y * scale).astype(o_ref.dtype)

        @pl.when(kind == 2)
        def _():
            o_ref[...] = x_ref[...].astype(o_ref.dtype)

    return pl.pallas_call(
        body, name=name, grid=(S // ts, W // A_WIDTH),
        in_specs=[pl.BlockSpec((ts, A_WIDTH), lambda i, j: (i, j)), pl.BlockSpec((1, A_WIDTH), lambda i, j: (0, j)),
                  pl.BlockSpec((ts, LANES), lambda i, j: (i, 0)), pl.BlockSpec((ts, LANES), lambda i, j: (i, 0)),
                  _const_spec((LANES, LANES))],
        out_specs=pl.BlockSpec((ts, A_WIDTH), lambda i, j: (i, j)),
        out_shape=jax.ShapeDtypeStruct((S, W), BF16),
        compiler_params=_params(("parallel", "arbitrary")),
    )(qkv, gain, cos, sin, ones)


def qk_prep_bwd(qkv, gain, cos, sin, ones, grads, *, name):
    S, W = qkv.shape
    ts = _tile(S, 1024)
    nchunk = A_WIDTH // LANES
    nj = W // A_WIDTH

    def body(x_ref, g_ref, c_ref, s_ref, ones_ref, *rest):
        g_refs, (o_ref, dg_ref) = rest[:nj], rest[nj:]
        j = pl.program_id(0)
        kind = j % 3

        @pl.when(pl.program_id(1) == 0)
        def _():
            dg_ref[...] = jnp.zeros_like(dg_ref)

        for n in range(nj):
            @pl.when(j == n)
            def _(n=n):
                d_ref = g_refs[n]
                if n % 3 == 2:
                    o_ref[...] = d_ref[...].astype(o_ref.dtype)
                    return
                scale = A_HEAD_DIM ** -0.5 if n % 3 == 0 else 1.0
                lo = (lax.broadcasted_iota(jnp.int32, (ts, LANES), 1) % A_HEAD_DIM) < (A_HEAD_DIM // 2)
                cv, sv = c_ref[...], s_ref[...]
                for c in range(nchunk):
                    sl = slice(c * LANES, (c + 1) * LANES)
                    dy = d_ref[:, sl].astype(F32) * scale
                    dn = dy * cv - _rot_half(dy, lo) * sv
                    xv = x_ref[:, sl]
                    r = lax.rsqrt(_seg_mean(xv * xv, ones_ref) + RMS_EPS)
                    xhat = xv * r
                    dyg = dn * g_ref[:, sl]
                    dx = r * (dyg - xhat * _seg_mean(dyg * xhat, ones_ref))
                    o_ref[:, sl] = dx.astype(o_ref.dtype)
                    dg_ref[0:1, sl] += jnp.sum(dn * xhat, axis=0, keepdims=True)

    def gspec(n):
        return pl.BlockSpec((ts, A_WIDTH), lambda j, i: (jnp.where(j == n, i, 0), 0))

    return pl.pallas_call(
        body, name=name, grid=(nj, S // ts),
        in_specs=[pl.BlockSpec((ts, A_WIDTH), lambda j, i: (i, j)), pl.BlockSpec((1, A_WIDTH), lambda j, i: (0, j)),
                  pl.BlockSpec((ts, LANES), lambda j, i: (i, 0)), pl.BlockSpec((ts, LANES), lambda j, i: (i, 0)),
                  _const_spec((LANES, LANES))] + [gspec(n) for n in range(nj)],
        out_specs=[pl.BlockSpec((ts, A_WIDTH), lambda j, i: (i, j)), pl.BlockSpec((8, A_WIDTH), lambda j, i: (0, j))],
        out_shape=[jax.ShapeDtypeStruct((S, W), BF16), jax.ShapeDtypeStruct((8, W), F32)],
        compiler_params=_params(("arbitrary", "arbitrary")),
    )(qkv, gain, cos, sin, ones, *grads)


def _band_specs(kind, tq, nlb):
    nhb = tq // A_HALF
    nb = nlb // nhb
    base = kind * (A_WIDTH // LANES)
    return [pl.BlockSpec((A_HALF, LANES), lambda ph, b, hp: (ph * nlb + jnp.maximum(b * nhb - 1, 0), base + hp)),
            pl.BlockSpec((tq, LANES), lambda ph, b, hp: (ph * nb + b, base + hp)),
            pl.BlockSpec((A_HALF, LANES), lambda ph, b, hp: (ph * nlb + jnp.minimum((b + 1) * nhb, nlb - 1), base + hp))]


A_BLOCK = 2048
A_SUB = 128


def _band_bias(sub, key_major):
    i = np.arange(sub)[:, None]
    j = np.arange(sub + 2 * A_HALF)[None, :] - A_HALF
    ok = np.abs(j - i) <= A_HALF
    return jnp.asarray(np.where(ok.T if key_major else ok, 0.0, NEG_INF).astype(np.float32))


def _edge_bias(first, n, L, axis, at_start, at_end):
    if not (at_start or at_end):
        return None
    shape = (1, n) if axis == 1 else (n, 1)
    pos = first - A_HALF + lax.broadcasted_iota(jnp.int32, shape, axis)
    return jnp.where((pos < 0) | (pos >= L), NEG_INF, 0.0).astype(F32)


def _with_edge(bias, edge):
    return bias if edge is None else bias + edge


def _cat3(a_ref, b_ref, c_ref):
    return jnp.concatenate([a_ref[...], b_ref[...], c_ref[...]], axis=0)


def _lane_lo(rows):
    return lax.broadcasted_iota(jnp.int32, (rows, LANES), 1) < A_HEAD_DIM


NT = (((1,), (1,)), ((), ()))
TN = (((0,), (0,)), ((), ()))


def attn_fwd(qkvp, dil, *, name):
    S = qkvp.shape[0]
    L = S // dil
    tq = _tile(L, A_BLOCK)
    sub = min(tq, A_SUB)
    nsub = tq // sub
    nlb = L // A_HALF
    band = _band_bias(sub, False)

    def body(q_ref, kp_ref, ko_ref, kn_ref, vp_ref, vo_ref, vn_ref, band_ref, o_ref, l_ref):
        b = pl.program_id(1)
        K = _cat3(kp_ref, ko_ref, kn_ref)
        V = _cat3(vp_ref, vo_ref, vn_ref)
        lo_k = _lane_lo(tq + 2 * A_HALF)
        lo_q = _lane_lo(sub)
        Km = [jnp.where(sel, K, jnp.zeros_like(K)) for sel in (lo_k, ~lo_k)]
        Vm = [jnp.where(sel, V, jnp.zeros_like(V)) for sel in (lo_k, ~lo_k)]
        for r in range(nsub):
            rows = slice(r * sub, (r + 1) * sub)
            keys = slice(r * sub, (r + 1) * sub + 2 * A_HALF)
            bias = _with_edge(band_ref[...], _edge_bias(b * tq + r * sub, sub + 2 * A_HALF, L, 1, r == 0, r == nsub - 1))
            q = q_ref[rows, :]
            outs, lses = [], []
            for hh in range(2):
                s = lax.dot_general(q, Km[hh][keys], NT, preferred_element_type=F32) + bias
                m = jnp.max(s, axis=1, keepdims=True)
                p = jnp.exp(s - m)
                l = jnp.sum(p, axis=1, keepdims=True)
                outs.append(jnp.dot(p.astype(BF16), Vm[hh][keys], preferred_element_type=F32) * (1.0 / l))
                lses.append(m + jnp.log(l))
            o_ref[rows, :] = outs[0] + outs[1]
            l_ref[rows, :] = jnp.where(lo_q, lses[0], lses[1])

    ospec = _band_specs(0, tq, nlb)[1]
    return pl.pallas_call(
        body, name=name, grid=(dil, L // tq, A_WIDTH // LANES),
        in_specs=[_band_specs(0, tq, nlb)[1]] + _band_specs(1, tq, nlb) + _band_specs(2, tq, nlb)
        + [_const_spec(band.shape)],
        out_specs=[ospec, ospec],
        out_shape=[jax.ShapeDtypeStruct((S, A_WIDTH), F32)] * 2,
        compiler_params=_params(("parallel", "parallel", "parallel")),
    )(*([qkvp] * 7), band)


def attn_merge(os_, lses, dils, *, name):
    S = os_[0].shape[0]
    ts = _tile(S, 256)
    ng = len(dils)

    def body(*refs):
        o_refs, l_refs, out_ref = refs[:ng], refs[ng:2 * ng], refs[2 * ng]
        lse_refs, scrs = refs[2 * ng + 1:3 * ng + 1], refs[3 * ng + 1:]
        ov, ls, k = [], [], 0
        for d, o_ref, l_ref in zip(dils, o_refs, l_refs):
            if d == 1:
                ov.append(o_ref[...])
                ls.append(l_ref[...])
            else:
                _load_phases(o_ref, scrs[k], d, ts)
                _load_phases(l_ref, scrs[k + 1], d, ts)
                ov.append(_get_natural(scrs[k]))
                ls.append(_get_natural(scrs[k + 1]))
                k += 2
        m = functools.reduce(jnp.maximum, ls)
        es = [jnp.exp(l - m) for l in ls]
        tot = functools.reduce(jnp.add, es)
        acc = None
        for e, o in zip(es, ov):
            t = (e / tot) * o
            acc = t if acc is None else acc + t
        out_ref[...] = acc.astype(out_ref.dtype)
        total = m + jnp.log(tot)
        _put_natural(scrs[k], total)
        for d, lse_ref in zip(dils, lse_refs):
            if d == 1:
                lse_ref[...] = total
            else:
                _store_phases(scrs[k], lse_ref, d, ts)

    n_scr = 2 * sum(d > 1 for d in dils) + 1
    outs = pl.pallas_call(
        body, name=name, grid=(S // ts,),
        in_specs=[_phase_spec(d, ts, A_WIDTH) for d in dils] * 2,
        out_specs=[pl.BlockSpec((ts, A_WIDTH), lambda i: (i, 0))] + [_phase_spec(d, ts, A_WIDTH) for d in dils],
        out_shape=[jax.ShapeDtypeStruct((S, A_WIDTH), BF16)] + [_phase_shape(S, A_WIDTH, d, F32) for d in dils],
        scratch_shapes=[_nat_scratch(ts, A_WIDTH)] * n_scr,
        compiler_params=_params(("parallel",)),
    )(*[_phase_view(o, d) for o, d in zip(os_, dils)], *[_phase_view(l, d) for l, d in zip(lses, dils)])
    return outs[0], [l.reshape(S, A_WIDTH) for l in outs[1:]]


def attn_delta(dout, out, ones, dils, *, name):
    S = dout.shape[0]
    ts = _tile(S, 256)
    nd = len(dils)

    def body(d_ref, o_ref, ones_ref, *rest):
        do_refs, dl_refs, (scr_do, scr_dl) = rest[:nd], rest[nd:2 * nd], rest[2 * nd:]
        sums = []
        for c in range(A_WIDTH // LANES):
            sl = slice(c * LANES, (c + 1) * LANES)
            prod = d_ref[:, sl].astype(F32) * o_ref[:, sl].astype(F32)
            sums.append(_seg_sum(prod, ones_ref))
        _put_natural(scr_do, d_ref[...].astype(F32))
        _put_natural(scr_dl, jnp.concatenate(sums, axis=1))
        for d, do_ref, dl_ref in zip(dils, do_refs, dl_refs):
            if d == 1:
                do_ref[...] = d_ref[...]
                dl_ref[...] = _get_natural(scr_dl)
            else:
                _store_phases(scr_do, do_ref, d, ts)
                _store_phases(scr_dl, dl_ref, d, ts)

    spec = pl.BlockSpec((ts, A_WIDTH), lambda i: (i, 0))
    outs = pl.pallas_call(
        body, name=name, grid=(S // ts,), in_specs=[spec, spec, _const_spec((LANES, LANES))],
        out_specs=[_phase_spec(d, ts, A_WIDTH) for d in dils] * 2,
        out_shape=[_phase_shape(S, A_WIDTH, d, BF16) for d in dils] + [_phase_shape(S, A_WIDTH, d, F32) for d in dils],
        scratch_shapes=[_nat_scratch(ts, A_WIDTH)] * 2,
        compiler_params=_params(("parallel",)),
    )(dout, out, ones)
    outs = [o.reshape(S, A_WIDTH) for o in outs]
    return outs[:nd], outs[nd:]


def _head_col(x, hh):
    c = hh * A_HEAD_DIM
    return x[:, c:c + 1]


def attn_bwd_dq(qkvp, dout, lse, delta, dil, *, name):
    S = qkvp.shape[0]
    L = S // dil
    tq = _tile(L, A_BLOCK)
    sub = min(tq, A_SUB)
    nsub = tq // sub
    nlb = L // A_HALF
    band = _band_bias(sub, False)

    def body(q_ref, kp_ref, ko_ref, kn_ref, vp_ref, vo_ref, vn_ref, do_ref, l_ref, d_ref, band_ref, dq_ref):
        b = pl.program_id(1)
        K = _cat3(kp_ref, ko_ref, kn_ref)
        V = _cat3(vp_ref, vo_ref, vn_ref)
        lo_k = _lane_lo(tq + 2 * A_HALF)
        Km = [jnp.where(sel, K, jnp.zeros_like(K)) for sel in (lo_k, ~lo_k)]
        Vm = [jnp.where(sel, V, jnp.zeros_like(V)) for sel in (lo_k, ~lo_k)]
        for r in range(nsub):
            rows = slice(r * sub, (r + 1) * sub)
            keys = slice(r * sub, (r + 1) * sub + 2 * A_HALF)
            bias = _with_edge(band_ref[...], _edge_bias(b * tq + r * sub, sub + 2 * A_HALF, L, 1, r == 0, r == nsub - 1))
            q, do = q_ref[rows, :], do_ref[rows, :]
            lse_v, dl_v = l_ref[rows, :], d_ref[rows, :]
            acc = None
            for hh in range(2):
                s = lax.dot_general(q, Km[hh][keys], NT, preferred_element_type=F32) + bias
                p = jnp.exp(s - _head_col(lse_v, hh))
                dp = lax.dot_general(do, Vm[hh][keys], NT, preferred_element_type=F32)
                ds = p * (dp - _head_col(dl_v, hh))
                t = jnp.dot(ds.astype(BF16), Km[hh][keys], preferred_element_type=F32)
                acc = t if acc is None else acc + t
            dq_ref[rows, :] = acc.astype(dq_ref.dtype)

    nspec = _band_specs(0, tq, nlb)[1]
    return pl.pallas_call(
        body, name=name, grid=(dil, L // tq, A_WIDTH // LANES),
        in_specs=[nspec] + _band_specs(1, tq, nlb) + _band_specs(2, tq, nlb) + [nspec, nspec, nspec, _const_spec(band.shape)],
        out_specs=nspec,
        out_shape=jax.ShapeDtypeStruct((S, A_WIDTH), BF16),
        compiler_params=_params(("parallel", "parallel", "parallel")),
    )(*([qkvp] * 7), dout, lse, delta, band)


def attn_bwd_dkv(qkvp, dout, lse, delta, dil, *, name):
    S = qkvp.shape[0]
    L = S // dil
    tk = _tile(L, A_BLOCK)
    sub = min(tk, A_SUB)
    nsub = tk // sub
    nlb = L // A_HALF
    band = _band_bias(sub, False)

    def body(qp_ref, qo_ref, qn_ref, k_ref, v_ref, dp_ref, do_ref, dn_ref, lp_ref, lo_ref, ln_ref,
             ep_ref, eo_ref, en_ref, band_ref, dk_ref, dv_ref):
        b = pl.program_id(1)
        Q = _cat3(qp_ref, qo_ref, qn_ref)
        DO = _cat3(dp_ref, do_ref, dn_ref)
        lse_v = _cat3(lp_ref, lo_ref, ln_ref)
        dl_v = _cat3(ep_ref, eo_ref, en_ref)
        lo_q = _lane_lo(tk + 2 * A_HALF)
        Qm = [jnp.where(sel, Q, jnp.zeros_like(Q)) for sel in (lo_q, ~lo_q)]
        DOm = [jnp.where(sel, DO, jnp.zeros_like(DO)) for sel in (lo_q, ~lo_q)]
        for r in range(nsub):
            keys = slice(r * sub, (r + 1) * sub)
            qs = slice(r * sub, (r + 1) * sub + 2 * A_HALF)
            bias = _with_edge(band_ref[...], _edge_bias(b * tk + r * sub, sub + 2 * A_HALF, L, 1, r == 0, r == nsub - 1))
            K, V = k_ref[keys, :], v_ref[keys, :]
            lse_t, dl_t = lse_v[qs].T, dl_v[qs].T
            dk = dv = None
            for hh in range(2):
                hr = slice(hh * A_HEAD_DIM, hh * A_HEAD_DIM + 1)
                st = lax.dot_general(K, Qm[hh][qs], NT, preferred_element_type=F32) + bias
                pt = jnp.exp(st - lse_t[hr, :])
                dpt = lax.dot_general(V, DOm[hh][qs], NT, preferred_element_type=F32)
                dst = pt * (dpt - dl_t[hr, :])
                tv = jnp.dot(pt.astype(BF16), DOm[hh][qs], preferred_element_type=F32)
                tk_ = jnp.dot(dst.astype(BF16), Qm[hh][qs], preferred_element_type=F32)
                dv = tv if dv is None else dv + tv
                dk = tk_ if dk is None else dk + tk_
            dk_ref[keys, :] = dk.astype(dk_ref.dtype)
            dv_ref[keys, :] = dv.astype(dv_ref.dtype)

    nspec = _band_specs(0, tk, nlb)[1]
    return pl.pallas_call(
        body, name=name, grid=(dil, L // tk, A_WIDTH // LANES),
        in_specs=_band_specs(0, tk, nlb) + [_band_specs(1, tk, nlb)[1], _band_specs(2, tk, nlb)[1]]
        + _band_specs(0, tk, nlb) * 3 + [_const_spec(band.shape)],
        out_specs=[nspec, nspec],
        out_shape=[jax.ShapeDtypeStruct((S, A_WIDTH), BF16)] * 2,
        compiler_params=_params(("parallel", "parallel", "parallel")),
    )(*([qkvp] * 5), *([dout] * 3), *([lse] * 3), *([delta] * 3), band)


def _gate_block_weight(wf, wb):
    w = jnp.zeros((LANES, 2 * B_QK_WIDTH), F32)
    w = w.at[:B_GATE_RANK, :B_QK_WIDTH].set(wf)
    w = w.at[B_GATE_RANK:2 * B_GATE_RANK, B_QK_WIDTH:].set(wb)
    return w.astype(BF16)


def gate_fwd(proj, wblk, bias, *, name):
    S = proj.shape[0]
    ts = _tile(S, 512)
    W = 2 * B_QK_WIDTH
    zcol = (2 * B_QK_WIDTH + 2 * B_V_WIDTH) // LANES

    def body(z_ref, w_ref, b_ref, o_ref):
        x = jnp.dot(z_ref[...].astype(BF16), w_ref[...], preferred_element_type=F32) + b_ref[...]
        o_ref[...] = (jnp.minimum(x, 0.0) - jnp.log(1.0 + jnp.exp(-jnp.abs(x)))) * (1.0 / B_GATE_TAU)

    return pl.pallas_call(
        body, name=name, grid=(S // ts,),
        in_specs=[pl.BlockSpec((ts, LANES), lambda i: (i, zcol)), _const_spec((LANES, W)), _const_spec((1, W))],
        out_specs=pl.BlockSpec((ts, W), lambda i: (i, 0)),
        out_shape=jax.ShapeDtypeStruct((S, W), F32),
        compiler_params=_params(("parallel",)),
    )(proj, wblk, bias)


def gate_bwd(proj, wblk, wblk_t, bias, dloga, *, name):
    S = proj.shape[0]
    ts = _tile(S, 512)
    W = 2 * B_QK_WIDTH
    zcol = (2 * B_QK_WIDTH + 2 * B_V_WIDTH) // LANES

    def body(z_ref, w_ref, wt_ref, b_ref, d_ref, dz_ref, dw_ref, db_ref):
        @pl.when(pl.program_id(0) == 0)
        def _():
            dw_ref[...] = jnp.zeros_like(dw_ref)
            db_ref[...] = jnp.zeros_like(db_ref)

        z = z_ref[...].astype(BF16)
        x = jnp.dot(z, w_ref[...], preferred_element_type=F32) + b_ref[...]
        e = jnp.exp(-jnp.abs(x))
        sig_neg = jnp.where(x >= 0, e, 1.0) / (1.0 + e)
        dx = d_ref[...] * (1.0 / B_GATE_TAU) * sig_neg
        dxb = dx.astype(BF16)
        dz_ref[...] = jnp.dot(dxb, wt_ref[...], preferred_element_type=F32)
        dw_ref[...] += lax.dot_general(z, dxb, TN, preferred_element_type=F32)
        db_ref[0:1, :] += jnp.sum(dx, axis=0, keepdims=True)

    return pl.pallas_call(
        body, name=name, grid=(S // ts,),
        in_specs=[pl.BlockSpec((ts, LANES), lambda i: (i, zcol)), _const_spec((LANES, W)), _const_spec((W, LANES)),
                  _const_spec((1, W)), pl.BlockSpec((ts, W), lambda i: (i, 0))],
        out_specs=[pl.BlockSpec((ts, LANES), lambda i: (i, 0)), _const_spec((LANES, W)), _const_spec((8, W))],
        out_shape=[jax.ShapeDtypeStruct((S, LANES), F32), jax.ShapeDtypeStruct((LANES, W), F32),
                   jax.ShapeDtypeStruct((8, W), F32)],
        compiler_params=_params(("arbitrary",)),
    )(proj, wblk, wblk_t, bias, dloga)


def _tri(reverse):
    i = np.arange(B_CHUNK)
    t = (i[None, :] >= i[:, None]) if reverse else (i[None, :] <= i[:, None])
    return jnp.asarray(t.astype(np.float32))


def _gla_terms(q, k, la, t_ref, reverse):
    b = jnp.dot(t_ref[...], la, precision=HI, preferred_element_type=F32)
    b_last = b[0:1, :] if reverse else b[B_CHUNK - 1:B_CHUNK, :]
    e_b = jnp.exp(b)
    qt = (q * (B_KEY_DIM ** -0.5)) * e_b
    e_nb = jnp.exp(-b)
    kt = k * e_nb
    e_end = jnp.exp(b_last - b)
    kend = k * e_end
    dec = jnp.exp(b_last)
    return e_nb, e_b, qt, kt, e_end, kend, dec


def _chunk_mask(reverse, transpose=False):
    r = lax.broadcasted_iota(jnp.int32, (B_CHUNK, B_CHUNK), 0)
    c = lax.broadcasted_iota(jnp.int32, (B_CHUNK, B_CHUNK), 1)
    if transpose:
        r, c = c, r
    return (c > r) if reverse else (c <= r)


def gla_fwd(proj, loga, tmat, reverse, *, name):
    S = proj.shape[0]
    tb = _tile(S, 512)
    nb = S // tb
    cpb = tb // B_CHUNK
    nc = S // B_CHUNK
    lb = 1 if reverse else 0
    blk = (lambda i: nb - 1 - i) if reverse else (lambda i: i)

    def body(qk_ref, v_ref, la_ref, t_ref, o_ref, st_ref, s_scr):
        @pl.when(pl.program_id(0) == 0)
        def _():
            s_scr[...] = jnp.zeros_like(s_scr)

        mask = _chunk_mask(reverse)
        order = range(cpb - 1, -1, -1) if reverse else range(cpb)
        for c in order:
            rows = slice(c * B_CHUNK, (c + 1) * B_CHUNK)
            _, _, qt, kt, _, kend, dec = _gla_terms(qk_ref[rows, :B_QK_WIDTH], qk_ref[rows, B_QK_WIDTH:],
                                                   la_ref[rows, :], t_ref, reverse)
            qt, kt, kend = qt.astype(BF16), kt.astype(BF16), kend.astype(BF16)
            for h in range(B_HEADS):
                kc = slice(h * B_KEY_DIM, (h + 1) * B_KEY_DIM)
                vc = slice(h * B_VAL_DIM, (h + 1) * B_VAL_DIM)
                v = v_ref[rows, vc].astype(BF16)
                st = s_scr[h]
                st_ref[h, c] = st
                a = jnp.where(mask, lax.dot_general(qt[:, kc], kt[:, kc], NT, preferred_element_type=F32), 0.0)
                o = jnp.dot(a.astype(BF16), v, preferred_element_type=F32)
                o = o + lax.dot_general(qt[:, kc], st.astype(BF16), NT, preferred_element_type=F32)
                o_ref[rows, vc] = o
                s_scr[h] = st * dec[:, kc] + lax.dot_general(v, kend[:, kc], TN, preferred_element_type=F32)

    return pl.pallas_call(
        body, name=name, grid=(nb,),
        in_specs=[pl.BlockSpec((tb, 2 * B_QK_WIDTH), lambda i: (blk(i), 0)),
                  pl.BlockSpec((tb, B_V_WIDTH), lambda i: (blk(i), 1)),
                  pl.BlockSpec((tb, B_QK_WIDTH), lambda i: (blk(i), lb)),
                  _const_spec((B_CHUNK, B_CHUNK))],
        out_specs=[pl.BlockSpec((tb, B_V_WIDTH), lambda i: (blk(i), 0)),
                   pl.BlockSpec((B_HEADS, cpb, B_VAL_DIM, B_KEY_DIM), lambda i: (0, blk(i), 0, 0))],
        out_shape=[jax.ShapeDtypeStruct((S, B_V_WIDTH), F32),
                   jax.ShapeDtypeStruct((B_HEADS, nc, B_VAL_DIM, B_KEY_DIM), F32)],
        scratch_shapes=[pltpu.VMEM((B_HEADS, B_VAL_DIM, B_KEY_DIM), F32)],
        compiler_params=_params(("arbitrary",)),
    )(proj, proj, loga, tmat)


def gla_bwd(proj, loga, states, do, tmat, tmat_t, reverse, *, name):
    S = proj.shape[0]
    tb = _tile(S, 256)
    nb = S // tb
    cpb = tb // B_CHUNK
    lb = 1 if reverse else 0
    blk = (lambda i: i) if reverse else (lambda i: nb - 1 - i)
    scale = B_KEY_DIM ** -0.5

    def body(qk_ref, v_ref, la_ref, st_ref, do_ref, t_ref, tt_ref, dq_ref, dk_ref, dv_ref, dla_ref, ds_scr):
        @pl.when(pl.program_id(0) == 0)
        def _():
            ds_scr[...] = jnp.zeros_like(ds_scr)

        mask = _chunk_mask(reverse)
        mask_t = _chunk_mask(reverse, transpose=True)
        last = 0 if reverse else B_CHUNK - 1
        is_last = lax.broadcasted_iota(jnp.int32, (B_CHUNK, B_QK_WIDTH), 0) == last
        order = range(cpb) if reverse else range(cpb - 1, -1, -1)
        for c in order:
            rows = slice(c * B_CHUNK, (c + 1) * B_CHUNK)
            e_nb, e_b, qt, kt, e_end, kend, dec = _gla_terms(qk_ref[rows, :B_QK_WIDTH], qk_ref[rows, B_QK_WIDTH:],
                                                             la_ref[rows, :], t_ref, reverse)
            qtb, ktb, kendb = qt.astype(BF16), kt.astype(BF16), kend.astype(BF16)
            dqt_h, dkt_h, dkend_h, ddec_h = [], [], [], []
            for h in range(B_HEADS):
                kc = slice(h * B_KEY_DIM, (h + 1) * B_KEY_DIM)
                vc = slice(h * B_VAL_DIM, (h + 1) * B_VAL_DIM)
                v = v_ref[rows, vc].astype(BF16)
                dob = do_ref[rows, vc].astype(BF16)
                st = st_ref[h, c]
                dst = ds_scr[h]
                dstb = dst.astype(BF16)
                a_t = jnp.where(mask_t, lax.dot_general(ktb[:, kc], qtb[:, kc], NT, preferred_element_type=F32), 0.0)
                da = jnp.where(mask, lax.dot_general(dob, v, NT, preferred_element_type=F32), 0.0)
                da_t = jnp.where(mask_t, lax.dot_general(v, dob, NT, preferred_element_type=F32), 0.0)
                dv = jnp.dot(a_t.astype(BF16), dob, preferred_element_type=F32)
                dv_ref[rows, vc] = dv + lax.dot_general(kendb[:, kc], dstb, NT, preferred_element_type=F32)
                dqt = jnp.dot(da.astype(BF16), ktb[:, kc], preferred_element_type=F32)
                dqt_h.append(dqt + jnp.dot(dob, st.astype(BF16), preferred_element_type=F32))
                dkt_h.append(jnp.dot(da_t.astype(BF16), qtb[:, kc], preferred_element_type=F32))
                dkend_h.append(jnp.dot(v, dstb, preferred_element_type=F32))
                ddec_h.append(jnp.sum(dst * st, axis=0, keepdims=True))
                ds_scr[h] = dst * dec[:, kc] + lax.dot_general(dob, qtb[:, kc], TN, preferred_element_type=F32)
            dqt, dkt = jnp.concatenate(dqt_h, axis=1), jnp.concatenate(dkt_h, axis=1)
            dkend, ddec = jnp.concatenate(dkend_h, axis=1), jnp.concatenate(ddec_h, axis=1)
            ke = dkend * kend
            db = dqt * qt - dkt * kt - ke
            db_last = jnp.sum(ke, axis=0, keepdims=True) + ddec * dec
            db = db + jnp.where(is_last, db_last, 0.0)
            dq_ref[rows, :] = dqt * e_b * scale
            dk_ref[rows, :] = dkt * e_nb + dkend * e_end
            dla_ref[rows, :] = jnp.dot(tt_ref[...], db, precision=HI, preferred_element_type=F32)

    qspec = pl.BlockSpec((tb, B_QK_WIDTH), lambda i: (blk(i), 0))
    vspec = pl.BlockSpec((tb, B_V_WIDTH), lambda i: (blk(i), 0))
    return pl.pallas_call(
        body, name=name, grid=(nb,),
        in_specs=[pl.BlockSpec((tb, 2 * B_QK_WIDTH), lambda i: (blk(i), 0)),
                  pl.BlockSpec((tb, B_V_WIDTH), lambda i: (blk(i), 1)),
                  pl.BlockSpec((tb, B_QK_WIDTH), lambda i: (blk(i), lb)),
                  pl.BlockSpec((B_HEADS, cpb, B_VAL_DIM, B_KEY_DIM), lambda i: (0, blk(i), 0, 0)),
                  vspec, _const_spec((B_CHUNK, B_CHUNK)), _const_spec((B_CHUNK, B_CHUNK))],
        out_specs=[qspec, qspec, vspec, qspec],
        out_shape=[jax.ShapeDtypeStruct((S, B_QK_WIDTH), F32), jax.ShapeDtypeStruct((S, B_QK_WIDTH), F32),
                   jax.ShapeDtypeStruct((S, B_V_WIDTH), F32), jax.ShapeDtypeStruct((S, B_QK_WIDTH), F32)],
        scratch_shapes=[pltpu.VMEM((B_HEADS, B_VAL_DIM, B_KEY_DIM), F32)],
        compiler_params=_params(("arbitrary",)),
    )(proj, proj, loga, states, do, tmat, tmat_t)


def gla_post_fwd(o_f, o_b, gain, proj, *, name):
    S = o_f.shape[0]
    ts = _tile(S, 512)
    rcol = (2 * B_QK_WIDTH + B_V_WIDTH) // B_V_WIDTH

    def body(f_ref, b_ref, g_ref, r_ref, y_ref):
        for h in range(B_HEADS):
            sl = slice(h * B_VAL_DIM, (h + 1) * B_VAL_DIM)
            o = f_ref[:, sl] + b_ref[:, sl]
            n = (o * lax.rsqrt(jnp.mean(o * o, axis=-1, keepdims=True) + RMS_EPS)) * g_ref[:, sl]
            r = r_ref[:, sl]
            y_ref[:, sl] = (n * (r * (1.0 / (1.0 + jnp.exp(-r))))).astype(y_ref.dtype)

    spec = pl.BlockSpec((ts, B_V_WIDTH), lambda i: (i, 0))
    return pl.pallas_call(
        body, name=name, grid=(S // ts,),
        in_specs=[spec, spec, _const_spec((1, B_V_WIDTH)), pl.BlockSpec((ts, B_V_WIDTH), lambda i: (i, rcol))],
        out_specs=spec, out_shape=jax.ShapeDtypeStruct((S, B_V_WIDTH), BF16),
        compiler_params=_params(("parallel",)),
    )(o_f, o_b, gain, proj)


def gla_post_bwd(o_f, o_b, gain, proj, dy, *, name):
    S = o_f.shape[0]
    ts = _tile(S, 512)
    rcol = (2 * B_QK_WIDTH + B_V_WIDTH) // B_V_WIDTH

    def body(f_ref, b_ref, g_ref, r_ref, dy_ref, do_ref, dr_ref, dg_ref):
        @pl.when(pl.program_id(0) == 0)
        def _():
            dg_ref[...] = jnp.zeros_like(dg_ref)

        for h in range(B_HEADS):
            sl = slice(h * B_VAL_DIM, (h + 1) * B_VAL_DIM)
            o = f_ref[:, sl] + b_ref[:, sl]
            rs = lax.rsqrt(jnp.mean(o * o, axis=-1, keepdims=True) + RMS_EPS)
            ohat = o * rs
            g = g_ref[:, sl]
            r = r_ref[:, sl]
            sig = 1.0 / (1.0 + jnp.exp(-r))
            dyv = dy_ref[:, sl].astype(F32)
            dn = dyv * (r * sig)
            dr_ref[:, sl] = dyv * (ohat * g) * (sig * (1.0 + r * (1.0 - sig)))
            dng = dn * g
            do_ref[:, sl] = rs * (dng - ohat * jnp.mean(dng * ohat, axis=-1, keepdims=True))
            dg_ref[0:1, sl] += jnp.sum(dn * ohat, axis=0, keepdims=True)

    spec = pl.BlockSpec((ts, B_V_WIDTH), lambda i: (i, 0))
    return pl.pallas_call(
        body, name=name, grid=(S // ts,),
        in_specs=[spec, spec, _const_spec((1, B_V_WIDTH)), pl.BlockSpec((ts, B_V_WIDTH), lambda i: (i, rcol)), spec],
        out_specs=[spec, spec, _const_spec((8, B_V_WIDTH))],
        out_shape=[jax.ShapeDtypeStruct((S, B_V_WIDTH), F32), jax.ShapeDtypeStruct((S, B_V_WIDTH), F32),
                   jax.ShapeDtypeStruct((8, B_V_WIDTH), F32)],
        compiler_params=_params(("arbitrary",)),
    )(o_f, o_b, gain, proj, dy)


def gla_combine(parts_f, parts_b, dr, dz, *, name):
    S = dr.shape[0]
    ts = _tile(S, 512)

    def body(qf, kf, vf, qb, kb, vb, r_ref, z_ref, o_ref):
        o_ref[:, 0:512] = (qf[...] + qb[...]).astype(o_ref.dtype)
        o_ref[:, 512:1024] = (kf[...] + kb[...]).astype(o_ref.dtype)
        o_ref[:, 1024:2048] = (vf[...] + vb[...]).astype(o_ref.dtype)
        o_ref[:, 2048:3072] = r_ref[...].astype(o_ref.dtype)
        o_ref[:, 3072:3200] = z_ref[...].astype(o_ref.dtype)

    s512 = pl.BlockSpec((ts, B_QK_WIDTH), lambda i: (i, 0))
    s1024 = pl.BlockSpec((ts, B_V_WIDTH), lambda i: (i, 0))
    return pl.pallas_call(
        body, name=name, grid=(S // ts,),
        in_specs=[s512, s512, s1024, s512, s512, s1024, s1024, pl.BlockSpec((ts, LANES), lambda i: (i, 0))],
        out_specs=pl.BlockSpec((ts, B_IN_PAD), lambda i: (i, 0)),
        out_shape=jax.ShapeDtypeStruct((S, B_IN_PAD), BF16),
        compiler_params=_params(("parallel",)),
    )(*parts_f, *parts_b, dr, dz)


def adamw(w, g, m, v, *, name):
    R, C = w.shape
    tr = _tile(R, 256)
    c1 = 1.0 / (1.0 - ADAM_B1 ** ADAM_STEP)
    c2 = 1.0 / (1.0 - ADAM_B2 ** ADAM_STEP)

    def body(w_ref, g_ref, m_ref, v_ref, d_ref, mo_ref, vo_ref):
        gv = g_ref[...]
        mn = ADAM_B1 * m_ref[...] + (1.0 - ADAM_B1) * gv
        vn = ADAM_B2 * v_ref[...] + (1.0 - ADAM_B2) * (gv * gv)
        mo_ref[...] = mn
        vo_ref[...] = vn
        d_ref[...] = -ADAM_LR * ((mn * c1) / (jnp.sqrt(vn * c2) + ADAM_EPS) + ADAM_WD * w_ref[...])

    spec = pl.BlockSpec((tr, C), lambda i: (i, 0))
    return pl.pallas_call(
        body, name=name, grid=(R // tr,), in_specs=[spec] * 4, out_specs=[spec] * 3,
        out_shape=[jax.ShapeDtypeStruct((R, C), F32)] * 3,
        compiler_params=_params(("parallel",)),
    )(w, g, m, v)


def _chip_peers():
    x, y, c = lax.axis_index("x"), lax.axis_index("y"), lax.axis_index("c")
    return x, y, c, [(1 - x, y), (x, 1 - y), (1 - x, 1 - y)]


_ANY = pl.BlockSpec(memory_space=pl.ANY)


def gather_shards(src, *, name):
    _, R, C = src.shape

    def body(src_ref, out_ref, ici_send, ici_recv, d2d_send, d2d_recv):
        x, y, c, chips = _chip_peers()
        me = 2 * x + y
        sibling = (x, y, 1 - c)
        sends = []
        for k, (px, py) in enumerate(chips):
            cp = pltpu.make_async_remote_copy(
                src_ref=src_ref.at[c], dst_ref=out_ref.at[me, c], send_sem=ici_send.at[k], recv_sem=ici_recv.at[k],
                device_id=(px, py, c), device_id_type=MESH)
            cp.start()
            sends.append(cp)
        for k, (px, py) in enumerate(chips):
            landed = out_ref.at[2 * px + py, c]
            pltpu.make_async_remote_copy(
                src_ref=src_ref.at[c], dst_ref=landed, send_sem=ici_send.at[k], recv_sem=ici_recv.at[k],
                device_id=(px, py, c), device_id_type=MESH).wait_recv()
            cp = pltpu.make_async_remote_copy(
                src_ref=landed, dst_ref=landed, send_sem=d2d_send.at[k], recv_sem=d2d_recv.at[k],
                device_id=sibling, device_id_type=MESH)
            cp.start()
            sends.append(cp)
        for k, (px, py) in enumerate(chips):
            other_half = out_ref.at[2 * px + py, 1 - c]
            pltpu.make_async_remote_copy(
                src_ref=other_half, dst_ref=other_half, send_sem=d2d_send.at[k], recv_sem=d2d_recv.at[k],
                device_id=sibling, device_id_type=MESH).wait_recv()
        for cp in sends:
            cp.wait_send()

    return pl.pallas_call(
        body, name=name, in_specs=[_ANY], out_specs=_ANY,
        out_shape=jax.ShapeDtypeStruct((N_CHIPS, 2, R, C), src.dtype),
        scratch_shapes=[pltpu.SemaphoreType.DMA((3,))] * 4,
        compiler_params=pltpu.CompilerParams(has_side_effects=True),
    )(src)


def chip_exchange(srcs, *, name):
    n = len(srcs)

    def body(*refs):
        src_refs, out_refs = refs[:n], refs[n:2 * n]
        send_sems, recv_sems = refs[2 * n:]
        x, y, c, chips = _chip_peers()
        me = 2 * x + y
        copies = []
        for i, (src_ref, out_ref) in enumerate(zip(src_refs, out_refs)):
            for k, (px, py) in enumerate(chips):
                cp = pltpu.make_async_remote_copy(
                    src_ref=src_ref.at[2 * px + py], dst_ref=out_ref.at[me],
                    send_sem=send_sems.at[i, k], recv_sem=recv_sems.at[i, k],
                    device_id=(px, py, c), device_id_type=MESH)
                cp.start()
                copies.append(cp)
        for i, (src_ref, out_ref) in enumerate(zip(src_refs, out_refs)):
            for k, (px, py) in enumerate(chips):
                pltpu.make_async_remote_copy(
                    src_ref=src_ref.at[me], dst_ref=out_ref.at[2 * px + py],
                    send_sem=send_sems.at[i, k], recv_sem=recv_sems.at[i, k],
                    device_id=(px, py, c), device_id_type=MESH).wait_recv()
        for cp in copies:
            cp.wait_send()

    return pl.pallas_call(
        body, name=name, in_specs=[_ANY] * n, out_specs=[_ANY] * n,
        out_shape=[jax.ShapeDtypeStruct(s.shape, s.dtype) for s in srcs],
        scratch_shapes=[pltpu.SemaphoreType.DMA((n, 3)), pltpu.SemaphoreType.DMA((n, 3))],
        compiler_params=pltpu.CompilerParams(has_side_effects=True),
    )(*srcs)


def sibling_exchange(srcs, *, name):
    n = len(srcs)

    def body(*refs):
        src_refs, out_refs, (send_sems, recv_sems) = refs[:n], refs[n:2 * n], refs[2 * n:]
        x, y, c = lax.axis_index("x"), lax.axis_index("y"), lax.axis_index("c")
        copies = []
        for i, (src_ref, out_ref) in enumerate(zip(src_refs, out_refs)):
            cp = pltpu.make_async_remote_copy(
                src_ref=src_ref.at[:, 1 - c], dst_ref=out_ref, send_sem=send_sems.at[i], recv_sem=recv_sems.at[i],
                device_id=(x, y, 1 - c), device_id_type=MESH)
            cp.start()
            copies.append(cp)
        for cp in copies:
            cp.wait()

    return pl.pallas_call(
        body, name=name, in_specs=[_ANY] * n, out_specs=[_ANY] * n,
        out_shape=[jax.ShapeDtypeStruct((s.shape[0],) + s.shape[2:], s.dtype) for s in srcs],
        scratch_shapes=[pltpu.SemaphoreType.DMA((n,)), pltpu.SemaphoreType.DMA((n,))],
        compiler_params=pltpu.CompilerParams(has_side_effects=True),
    )(*srcs)


def sibling_share(bufs, *, name):
    n = len(bufs)

    def body(*refs):
        out_refs, (send_sems, recv_sems) = refs[n:2 * n], refs[2 * n:]
        x, y, c = lax.axis_index("x"), lax.axis_index("y"), lax.axis_index("c")
        copies = []
        for i, out_ref in enumerate(out_refs):
            cp = pltpu.make_async_remote_copy(
                src_ref=out_ref.at[c], dst_ref=out_ref.at[c], send_sem=send_sems.at[i], recv_sem=recv_sems.at[i],
                device_id=(x, y, 1 - c), device_id_type=MESH)
            cp.start()
            copies.append(cp)
        for cp in copies:
            cp.wait()

    return pl.pallas_call(
        body, name=name, in_specs=[_ANY] * n, out_specs=[_ANY] * n,
        out_shape=[jax.ShapeDtypeStruct(b.shape, b.dtype) for b in bufs],
        input_output_aliases={i: i for i in range(n)},
        scratch_shapes=[pltpu.SemaphoreType.DMA((n,))] * 2,
        compiler_params=pltpu.CompilerParams(has_side_effects=True),
    )(*bufs)


def add_pair(a, b, out_dtype, *, name):
    n, _, R, C = a.shape
    tr = _tile(R, 256)

    def body(c_ref, a_ref, b_ref, o_ref):
        o_ref[...] = (a_ref[0] + b_ref[...]).astype(o_ref.dtype)

    return pl.pallas_call(
        body, name=name,
        grid_spec=pltpu.PrefetchScalarGridSpec(
            num_scalar_prefetch=1, grid=(n, R // tr),
            in_specs=[pl.BlockSpec((1, 1, tr, C), lambda s, i, c_ref: (s, c_ref[0], i, 0)),
                      pl.BlockSpec((1, tr, C), lambda s, i, c_ref: (s, i, 0))],
            out_specs=pl.BlockSpec((1, tr, C), lambda s, i, c_ref: (s, i, 0))),
        out_shape=jax.ShapeDtypeStruct((n, R, C), out_dtype),
        compiler_params=_params(("parallel", "parallel")),
    )(lax.axis_index("c").reshape(1).astype(jnp.int32), a, b)


def sum_slots(slots, own, *, name):
    n, R, C = slots.shape
    tr = _tile(R, 256)

    def body(ids_ref, *refs):
        slot_refs, own_ref, o_ref = refs[:n], refs[n], refs[n + 1]
        me = ids_ref[0]
        acc = None
        for s, r in enumerate(slot_refs):
            t = jnp.where(me == s, own_ref[0], r[0]).astype(F32)
            acc = t if acc is None else acc + t
        o_ref[0] = acc

    def slot_spec(s):
        return pl.BlockSpec((1, tr, C), lambda i, ids: (jnp.where(ids[0] == s, (s + 1) % n, s), i, 0))

    x, y, c = lax.axis_index("x"), lax.axis_index("y"), lax.axis_index("c")
    return pl.pallas_call(
        body, name=name,
        grid_spec=pltpu.PrefetchScalarGridSpec(
            num_scalar_prefetch=1, grid=(R // tr,),
            in_specs=[slot_spec(s) for s in range(n)] + [pl.BlockSpec((1, tr, C), lambda i, ids: (ids[0], i, 0))],
            out_specs=pl.BlockSpec((1, tr, C), lambda i, ids: (ids[1], i, 0))),
        out_shape=jax.ShapeDtypeStruct((2, R, C), F32),
        compiler_params=_params(("parallel",)),
    )(jnp.stack([2 * x + y, c]).astype(jnp.int32), *([slots] * n), own)


def _flat_rows(n_elems, mult):
    rows = -(-n_elems // FLAT_COLS)
    return -(-rows // mult) * mult


def _to_flat(parts, mult):
    v = jnp.concatenate([p.reshape(-1) for p in parts])
    rows = _flat_rows(v.shape[0], mult)
    return jnp.pad(v, (0, rows * FLAT_COLS - v.shape[0])).reshape(rows, FLAT_COLS)


def _from_flat(flat, shapes):
    v = flat.reshape(-1)
    out, off = [], 0
    for s in shapes:
        n = int(np.prod(s))
        out.append(v[off:off + n].reshape(s))
        off += n
    return out


def _unshard(blocks, axis):
    return jnp.concatenate([blocks[s] for s in range(N_CHIPS)], axis=axis)


def _by_shard(full, axis):
    shp = full.shape
    cut = full.reshape(shp[:axis] + (N_CHIPS, shp[axis] // N_CHIPS) + shp[axis + 1:])
    return jnp.moveaxis(cut, axis, 0)


def _gradient_blocks(grads):
    blocks = []
    for n in MATRICES:
        t = _by_shard(jnp.stack(grads[n]), SHARD_AXIS[n])
        blocks.append(t.reshape(N_CHIPS, 2, -1, t.shape[-1]))
    rest = []
    for s in range(N_CHIPS):
        parts = [jnp.stack(grads[n]) if n in REPLICATED else _by_shard(jnp.stack(grads[n]), SHARD_AXIS[n])[s]
                 for n in WEIGHTS if n not in MATRICES]
        rest.append(_to_flat(parts, 16))
    rest = jnp.stack(rest)
    blocks.append(rest.reshape(N_CHIPS, 2, rest.shape[1] // 2, FLAT_COLS))
    return blocks


def _gather_weights(w):
    full = {}
    for names, dtype, mult, call in ((BIG, BF16, 32, "gather_weights"), (SMALL_SHARDED, F32, 16, "gather_vectors")):
        parts = [w[n].astype(dtype) for n in names]
        flat = _to_flat(parts, mult)
        got = gather_shards(flat.reshape(2, flat.shape[0] // 2, FLAT_COLS), name=call)
        got = got.reshape(N_CHIPS, flat.shape[0], FLAT_COLS)
        me = 2 * lax.axis_index("x") + lax.axis_index("y")
        got = lax.dynamic_update_slice(got, flat[None], (me, 0, 0))
        per_chip = [_from_flat(got[s], [p.shape for p in parts]) for s in range(N_CHIPS)]
        for i, n in enumerate(names):
            full[n] = _unshard([per_chip[s][i] for s in range(N_CHIPS)], SHARD_AXIS[n])
    return full


def _reduce_gradients(grads, w):
    halves = _gradient_blocks(grads)
    other = sibling_exchange(halves, name="grad_pair_exchange")
    pair = [add_pair(a, b, BF16 if i < len(MATRICES) else F32, name=f"grad_pair_add{i}")
            for i, (a, b) in enumerate(zip(halves, other))]
    slots = chip_exchange(pair, name="grad_chip_exchange")
    mine = [sum_slots(a, p, name=f"grad_chip_sum{i}") for i, (a, p) in enumerate(zip(slots, pair))]
    both = sibling_share(mine, name="grad_half_share")
    g = {n: t.reshape(w[n].shape) for n, t in zip(MATRICES, both)}
    rest = [n for n in WEIGHTS if n not in MATRICES]
    g.update(zip(rest, _from_flat(both[-1], [w[n].shape for n in rest])))
    return g


def _layer_fwd(i, h, p, aux):
    j = i // 2
    sv = {"h0": h}
    if i % 2 == 0:
        hns = rmsnorm_fwd(h, p["attn_norm"][i][None], dils=DILS, name=f"l{i}_norm1")
        qkvs, qkvps, os_, lses = [], [], [], []
        for g, d in enumerate(DILS):
            qkv = matmul(hns[g], p["a_w_in_g"][j][g], tn=3 * A_WIDTH // 2, name=f"l{i}_a_in{g}")
            qkvp = qk_prep_fwd(qkv, aux["a_gain"][j][g], aux["cos"][g], aux["sin"][g], aux["ones"], name=f"l{i}_a_prep{g}")
            o, l = attn_fwd(qkvp, d, name=f"l{i}_a_attn{g}")
            qkvs.append(qkv)
            qkvps.append(qkvp)
            os_.append(o)
            lses.append(l)
        out, lse = attn_merge(os_, lses, DILS, name=f"l{i}_a_merge")
        sv.update(hns=hns, qkv=qkvs, qkvp=qkvps, out=out, lse=lse)
        h1 = matmul(out, p["a_w_out"][j], res=h, name=f"l{i}_a_out")
    else:
        hn, sv["hn_t"] = rmsnorm_fwd(h, p["attn_norm"][i][None], with_transpose=True, name=f"l{i}_norm1")
        proj = matmul(hn, p["b_w_in"][j], tn=640, name=f"l{i}_b_in")
        loga = gate_fwd(proj, aux["b_wblk"][j], aux["b_bias"][j], name=f"l{i}_b_gate")
        o_f, st_f = gla_fwd(proj, loga, aux["tri_f"], False, name=f"l{i}_b_gla_f")
        o_b, st_b = gla_fwd(proj, loga, aux["tri_b"], True, name=f"l{i}_b_gla_b")
        y = gla_post_fwd(o_f, o_b, aux["b_gain"][j], proj, name=f"l{i}_b_post")
        sv.update(proj=proj, loga=loga, o_f=o_f, o_b=o_b, st_f=st_f, st_b=st_b, y=y)
        h1 = matmul(y, p["b_w_out"][j], res=h, name=f"l{i}_b_out")
    sv["h1"] = h1
    hn2, hn2_t = rmsnorm_fwd(h1, p["ffn_norm"][i][None], with_transpose=True, name=f"l{i}_norm2")
    gu = matmul(hn2, p["ffn_w_gate_up"][i], out_dtype=BF16, tn=FFN_HIDDEN // 2, name=f"l{i}_f_up")
    act, act_t = swiglu_fwd(gu, name=f"l{i}_f_act")
    h2 = matmul(act, p["ffn_w_down"][i], res=h1, tk=2816, name=f"l{i}_f_down")
    sv.update(hn2_t=hn2_t, gu=gu, act_t=act_t)
    return h2, sv


def _layer_bwd(i, dh, p, pt, aux, sv, grads):
    j = i // 2
    dhb = dh.astype(BF16)
    grads["ffn_w_down"][i] = matmul(sv["act_t"], dhb, tm=FFN_HIDDEN // 2, tk=2048, name=f"l{i}_f_down_dw")
    dact = matmul(dhb, pt["ffn_w_down"][i], out_dtype=BF16, tn=FFN_HIDDEN // 2, name=f"l{i}_f_down_dx")
    dgu = swiglu_bwd(sv["gu"], dact, name=f"l{i}_f_act_bwd")
    grads["ffn_w_gate_up"][i] = matmul(sv["hn2_t"], dgu, tk=2048, name=f"l{i}_f_up_dw")
    dhn2 = matmul(dgu, pt["ffn_w_gate_up"][i], tk=2816, name=f"l{i}_f_up_dx")
    dh1, dg = rmsnorm_bwd(sv["h1"], p["ffn_norm"][i][None], [dhn2], dh, name=f"l{i}_norm2_bwd")
    grads["ffn_norm"][i] = dg[0]
    dh1b = dh1.astype(BF16)
    if i % 2 == 0:
        grads["a_w_out"][j] = matmul(sv["out"].T, dh1b, tk=2048, name=f"l{i}_a_out_dw")
        dout = matmul(dh1b, pt["a_w_out"][j], out_dtype=BF16, name=f"l{i}_a_out_dx")
        douts, deltas = attn_delta(dout, sv["out"], aux["ones"], DILS, name=f"l{i}_a_delta")
        dws, dhns, dgq, dgk = [], [], [], []
        for g, d in enumerate(DILS):
            qkvp, lse = sv["qkvp"][g], sv["lse"][g]
            dq = attn_bwd_dq(qkvp, douts[g], lse, deltas[g], d, name=f"l{i}_a_dq{g}")
            dk, dv = attn_bwd_dkv(qkvp, douts[g], lse, deltas[g], d, name=f"l{i}_a_dkv{g}")
            dqkv, dgain = qk_prep_bwd(sv["qkv"][g], aux["a_gain"][j][g], aux["cos"][g], aux["sin"][g], aux["ones"],
                                      [dq, dk, dv], name=f"l{i}_a_prep_bwd{g}")
            dgh = dgain[0].reshape(3, A_HEADS, A_HEAD_DIM).sum(axis=1)
            dgq.append(dgh[0])
            dgk.append(dgh[1])
            dws.append(matmul(sv["hns"][g].T, dqkv, tk=2048, name=f"l{i}_a_in_dw{g}"))
            dhns.append(matmul(dqkv, pt["a_w_in_g"][j][g], tk=3072, name=f"l{i}_a_in_dx{g}"))
        grads["a_q_norm"][j] = jnp.stack(dgq)
        grads["a_k_norm"][j] = jnp.stack(dgk)
        grads["a_w_in"][j] = jnp.concatenate(dws, axis=1)
        dh0, dg = rmsnorm_bwd(sv["h0"], p["attn_norm"][i][None], dhns, dh1, dils=DILS, name=f"l{i}_norm1_bwd")
    else:
        grads["b_w_out"][j] = matmul(sv["y"].T, dh1b, tk=2048, name=f"l{i}_b_out_dw")
        dy = matmul(dh1b, pt["b_w_out"][j], name=f"l{i}_b_out_dx")
        do, dr, dgn = gla_post_bwd(sv["o_f"], sv["o_b"], aux["b_gain"][j], sv["proj"], dy, name=f"l{i}_b_post_bwd")
        grads["b_out_norm"][j] = dgn[0].reshape(B_HEADS, B_VAL_DIM)
        pf = gla_bwd(sv["proj"], sv["loga"], sv["st_f"], do, aux["tri_f"], aux["tri_b"], False, name=f"l{i}_b_gla_f_bwd")
        pb = gla_bwd(sv["proj"], sv["loga"], sv["st_b"], do, aux["tri_b"], aux["tri_f"], True, name=f"l{i}_b_gla_b_bwd")
        dloga = jnp.concatenate([pf[3], pb[3]], axis=1)
        dz, dwblk, dbias = gate_bwd(sv["proj"], aux["b_wblk"][j], aux["b_wblk_t"][j], aux["b_bias"][j], dloga,
                                    name=f"l{i}_b_gate_bwd")
        grads["b_w_gate_f"][j] = dwblk[:B_GATE_RANK, :B_QK_WIDTH]
        grads["b_w_gate_b"][j] = dwblk[B_GATE_RANK:2 * B_GATE_RANK, B_QK_WIDTH:]
        grads["b_gate_bias_f"][j] = dbias[0, :B_QK_WIDTH]
        grads["b_gate_bias_b"][j] = dbias[0, B_QK_WIDTH:]
        dproj = gla_combine(pf[:3], pb[:3], dr, dz, name=f"l{i}_b_combine")
        grads["b_w_in"][j] = matmul(sv["hn_t"], dproj, tn=640, tk=2048, name=f"l{i}_b_in_dw")[:, :B_IN_WIDTH]
        dhn = matmul(dproj, pt["b_w_in"][j], tk=B_IN_PAD, name=f"l{i}_b_in_dx")
        dh0, dg = rmsnorm_bwd(sv["h0"], p["attn_norm"][i][None], [dhn], dh1, name=f"l{i}_norm1_bwd")
    grads["attn_norm"][i] = dg[0]
    return dh0


def _local_step(x, target, p, small):
    S = x.shape[0]
    cos, sin = _rope_tables(S)
    to_phase = lambda t, d: t.reshape(S // d, d, LANES).swapaxes(0, 1).reshape(S, LANES)
    cos, sin = [to_phase(cos, d) for d in DILS], [to_phase(sin, d) for d in DILS]
    ones_v = jnp.ones((A_WIDTH,), F32)
    a_gain = [[jnp.concatenate([jnp.tile(small["a_q_norm"][j][g], A_HEADS), jnp.tile(small["a_k_norm"][j][g], A_HEADS),
                                ones_v])[None] for g in range(len(DILS))] for j in range(2)]
    b_wblk = [_gate_block_weight(p["b_w_gate_f"][j].astype(F32), p["b_w_gate_b"][j].astype(F32)) for j in range(2)]
    aux = dict(cos=cos, sin=sin, ones=_head_block_ones(), a_gain=a_gain, tri_f=_tri(False), tri_b=_tri(True),
               b_wblk=b_wblk, b_wblk_t=[w.T for w in b_wblk],
               b_bias=[jnp.concatenate([small["b_gate_bias_f"][j], small["b_gate_bias_b"][j]])[None] for j in range(2)],
               b_gain=[small["b_out_norm"][j].reshape(1, B_V_WIDTH) for j in range(2)])
    pw = dict(p)
    pw["b_w_in"] = jnp.pad(p["b_w_in"], ((0, 0), (0, 0), (0, B_IN_PAD - B_IN_WIDTH)))
    pw["attn_norm"], pw["ffn_norm"] = small["attn_norm"], small["ffn_norm"]
    gw = 3 * A_WIDTH
    pw["a_w_in_g"] = [[p["a_w_in"][j][:, g * gw:(g + 1) * gw] for g in range(len(DILS))] for j in range(2)]
    pt = {n: jnp.swapaxes(pw[n], 1, 2) for n in ("a_w_out", "b_w_in", "b_w_out", "ffn_w_gate_up", "ffn_w_down")}
    pt["a_w_in_g"] = [[wg.T for wg in row] for row in pw["a_w_in_g"]]

    h = x
    saved = []
    for i in range(DEPTH):
        h, sv = _layer_fwd(i, h, pw, aux)
        saved.append(sv)
    loss_sq, dh = loss_head(h, target, name="loss_head")
    grads = {n: [None] * (DEPTH if n in ("attn_norm", "ffn_norm", "ffn_w_gate_up", "ffn_w_down") else 2) for n in WEIGHTS}
    for i in reversed(range(DEPTH)):
        dh = _layer_bwd(i, dh, pw, pt, aux, saved[i], grads)
    return loss_sq[0, 0] * (0.5 / D_MODEL), dh, grads


def kernel(x, attn_norm, ffn_norm, a_w_in, a_q_norm, a_k_norm, a_w_out, b_w_in, b_w_gate_f, b_gate_bias_f, b_w_gate_b, b_gate_bias_b, b_out_norm, b_w_out, ffn_w_gate_up, ffn_w_down, loss_target, m_attn_norm, m_ffn_norm, m_a_w_in, m_a_q_norm, m_a_k_norm, m_a_w_out, m_b_w_in, m_b_w_gate_f, m_b_gate_bias_f, m_b_w_gate_b, m_b_gate_bias_b, m_b_out_norm, m_b_w_out, m_ffn_w_gate_up, m_ffn_w_down, v_attn_norm, v_ffn_norm, v_a_w_in, v_a_q_norm, v_a_k_norm, v_a_w_out, v_b_w_in, v_b_w_gate_f, v_b_gate_bias_f, v_b_w_gate_b, v_b_gate_bias_b, v_b_out_norm, v_b_w_out, v_ffn_w_gate_up, v_ffn_w_down):
    w = dict(attn_norm=attn_norm, ffn_norm=ffn_norm, a_w_in=a_w_in, a_q_norm=a_q_norm, a_k_norm=a_k_norm, a_w_out=a_w_out,
             b_w_in=b_w_in, b_w_gate_f=b_w_gate_f, b_gate_bias_f=b_gate_bias_f, b_w_gate_b=b_w_gate_b,
             b_gate_bias_b=b_gate_bias_b, b_out_norm=b_out_norm, b_w_out=b_w_out, ffn_w_gate_up=ffn_w_gate_up,
             ffn_w_down=ffn_w_down)
    m = dict(attn_norm=m_attn_norm, ffn_norm=m_ffn_norm, a_w_in=m_a_w_in, a_q_norm=m_a_q_norm, a_k_norm=m_a_k_norm,
             a_w_out=m_a_w_out, b_w_in=m_b_w_in, b_w_gate_f=m_b_w_gate_f, b_gate_bias_f=m_b_gate_bias_f,
             b_w_gate_b=m_b_w_gate_b, b_gate_bias_b=m_b_gate_bias_b, b_out_norm=m_b_out_norm, b_w_out=m_b_w_out,
             ffn_w_gate_up=m_ffn_w_gate_up, ffn_w_down=m_ffn_w_down)
    v = dict(attn_norm=v_attn_norm, ffn_norm=v_ffn_norm, a_w_in=v_a_w_in, a_q_norm=v_a_q_norm, a_k_norm=v_a_k_norm,
             a_w_out=v_a_w_out, b_w_in=v_b_w_in, b_w_gate_f=v_b_w_gate_f, b_gate_bias_f=v_b_gate_bias_f,
             b_w_gate_b=v_b_w_gate_b, b_gate_bias_b=v_b_gate_bias_b, b_out_norm=v_b_out_norm, b_w_out=v_b_w_out,
             ffn_w_gate_up=v_ffn_w_gate_up, ffn_w_down=v_ffn_w_down)

    full = _gather_weights(w)
    p = {n: full[n] for n in BIG}
    small = {n: full[n] for n in SMALL_SHARDED}
    small.update({n: w[n] for n in REPLICATED})
    loss_local, dx, grads = _local_step(x[0], loss_target[0], p, small)
    loss = lax.psum(loss_local, ("x", "y", "c"))

    g = _reduce_gradients(grads, w)
    delta, new_m, new_v = {}, {}, {}
    rows = lambda t: t.reshape(-1, t.shape[-1])
    for n in MATRICES:
        outs = adamw(rows(w[n]), rows(g[n]), rows(m[n]), rows(v[n]), name=f"adamw_{n}")
        delta[n], new_m[n], new_v[n] = [o.reshape(w[n].shape) for o in outs]
    rest = [n for n in WEIGHTS if n not in MATRICES]
    flat = lambda d: _to_flat([d[n] for n in rest], 8)
    outs = adamw(flat(w), flat(g), flat(m), flat(v), name="adamw_vectors")
    for d, o in zip((delta, new_m, new_v), outs):
        d.update(zip(rest, _from_flat(o, [w[n].shape for n in rest])))
    return (loss, dx[None], *[g[n] for n in WEIGHTS], *[delta[n] for n in WEIGHTS],
            *[new_m[n] for n in WEIGHTS], *[new_v[n] for n in WEIGHTS])
```

```python
import functools

import numpy as np
import jax
import jax.numpy as jnp
from jax import lax
from jax.experimental import pallas as pl
from jax.experimental.pallas import tpu as pltpu

F32, BF16 = jnp.float32, jnp.bfloat16
HI = lax.Precision.HIGHEST
MESH = pl.DeviceIdType.MESH

D_MODEL = 1024
DEPTH = 4
RMS_EPS = 1e-6
NEG_INF = -1e30
A_GROUPS = ((128, 1), (512, 4), (2048, 16))
DILS = tuple(d for _, d in A_GROUPS)
A_HALF = 64
A_HEADS = 16
A_HEAD_DIM = 64
A_WIDTH = 1024
A_IN_WIDTH = 9216
ROPE_THETA = 10000.0
B_HEADS = 4
B_KEY_DIM = 128
B_VAL_DIM = 256
B_QK_WIDTH = 512
B_V_WIDTH = 1024
B_GATE_RANK = 16
B_GATE_TAU = 16.0
B_CHUNK = 64
B_IN_WIDTH = 3104
B_IN_PAD = 3200
FFN_HIDDEN = 2816
ADAM_LR, ADAM_B1, ADAM_B2, ADAM_EPS, ADAM_WD, ADAM_STEP = 0.001, 0.9, 0.999, 1e-08, 0.01, 10
LANES = 128
VMEM_LIMIT = 48 * 1024 * 1024
FLAT_COLS = 1024
N_CHIPS = 4

WEIGHTS = ['attn_norm', 'ffn_norm', 'a_w_in', 'a_q_norm', 'a_k_norm', 'a_w_out', 'b_w_in', 'b_w_gate_f',
           'b_gate_bias_f', 'b_w_gate_b', 'b_gate_bias_b', 'b_out_norm', 'b_w_out', 'ffn_w_gate_up', 'ffn_w_down']
REPLICATED = ('attn_norm', 'ffn_norm', 'a_q_norm', 'a_k_norm')
SHARD_AXIS = {'a_w_in': 2, 'a_w_out': 1, 'b_w_in': 2, 'b_w_gate_f': 2, 'b_gate_bias_f': 1, 'b_w_gate_b': 2,
              'b_gate_bias_b': 1, 'b_out_norm': 2, 'b_w_out': 1, 'ffn_w_gate_up': 2, 'ffn_w_down': 1}
BIG = ('a_w_in', 'a_w_out', 'b_w_in', 'b_w_gate_f', 'b_w_gate_b', 'b_w_out', 'ffn_w_gate_up', 'ffn_w_down')
SMALL_SHARDED = ('b_gate_bias_f', 'b_gate_bias_b', 'b_out_norm')
MATRICES = ('a_w_in', 'a_w_out', 'b_w_in', 'b_w_out', 'ffn_w_gate_up', 'ffn_w_down')


def _params(sem):
    return pltpu.CompilerParams(dimension_semantics=sem, vmem_limit_bytes=VMEM_LIMIT)


def _tile(n, pref):
    t = min(n, pref)
    while n % t:
        t //= 2
    return t


def _const_spec(shape):
    nd = len(shape)
    return pl.BlockSpec(shape, lambda *_: (0,) * nd)


def matmul(a, b, *, name, out_dtype=F32, res=None, tm=1024, tn=512, tk=1024):
    M, K = a.shape
    N = b.shape[1]
    assert b.shape[0] == K
    tm, tn, tk = _tile(M, tm), _tile(N, tn), _tile(K, tk)
    nk = K // tk

    def body(*refs):
        a_ref, b_ref = refs[:2]
        r_ref = refs[2] if res is not None else None
        o_ref = refs[3 if res is not None else 2]
        part = jnp.dot(a_ref[...], b_ref[...], preferred_element_type=F32)

        def finish(v):
            if res is not None:
                v = v + r_ref[...]
            o_ref[...] = v.astype(o_ref.dtype)

        if nk == 1:
            finish(part)
            return
        acc_ref = refs[-1]
        k = pl.program_id(2)

        @pl.when(k == 0)
        def _():
            acc_ref[...] = part

        @pl.when((k > 0) & (k < nk - 1))
        def _():
            acc_ref[...] += part

        @pl.when(k == nk - 1)
        def _():
            finish(acc_ref[...] + part)

    in_specs = [pl.BlockSpec((tm, tk), lambda i, j, k: (i, k)), pl.BlockSpec((tk, tn), lambda i, j, k: (k, j))]
    args = [a, b]
    if res is not None:
        in_specs.append(pl.BlockSpec((tm, tn), lambda i, j, k: (i, j)))
        args.append(res)
    return pl.pallas_call(
        body, name=name, grid=(M // tm, N // tn, nk), in_specs=in_specs,
        out_specs=pl.BlockSpec((tm, tn), lambda i, j, k: (i, j)),
        out_shape=jax.ShapeDtypeStruct((M, N), out_dtype),
        scratch_shapes=[pltpu.VMEM((tm, tn), F32)] if nk > 1 else [],
        compiler_params=_params(("parallel", "parallel", "arbitrary")),
    )(*args)


def _phase_spec(d, ts, W):
    if d == 1:
        return pl.BlockSpec((ts, W), lambda i: (i, 0))
    return pl.BlockSpec((d, ts // d, W), lambda i: (0, i, 0))


def _phase_view(a, d):
    return a if d == 1 else a.reshape(d, a.shape[0] // d, a.shape[1])


def _phase_shape(S, W, d, dtype):
    return jax.ShapeDtypeStruct((S, W) if d == 1 else (d, S // d, W), dtype)


def _nat_scratch(ts, W):
    return pltpu.VMEM((W // LANES, ts, LANES), F32)


def _put_natural(nat_ref, value):
    for c in range(nat_ref.shape[0]):
        nat_ref[c] = value[:, c * LANES:(c + 1) * LANES]


def _get_natural(nat_ref):
    return jnp.concatenate([nat_ref[c] for c in range(nat_ref.shape[0])], axis=1)


def _store_phases(nat_ref, o_ref, d, ts):
    for p in range(d):
        for c in range(nat_ref.shape[0]):
            o_ref[p, :, c * LANES:(c + 1) * LANES] = nat_ref[c, pl.ds(p, ts // d, stride=d), :].astype(o_ref.dtype)


def _load_phases(i_ref, nat_ref, d, ts):
    for p in range(d):
        for c in range(nat_ref.shape[0]):
            nat_ref[c, pl.ds(p, ts // d, stride=d), :] = i_ref[p, :, c * LANES:(c + 1) * LANES].astype(F32)


def rmsnorm_fwd(x, gain, *, name, dils=(1,), with_transpose=False):
    S, Dm = x.shape
    ts = _tile(S, 512)
    nd = len(dils)

    def body(x_ref, g_ref, *rest):
        o_refs, scr = rest[:nd], rest[-1]
        xv = x_ref[...]
        r = lax.rsqrt(jnp.mean(xv * xv, axis=-1, keepdims=True) + RMS_EPS)
        y = (xv * r) * g_ref[...]
        if any(d > 1 for d in dils):
            _put_natural(scr, y)
        for d, o_ref in zip(dils, o_refs):
            if d == 1:
                o_ref[...] = y.astype(o_ref.dtype)
            else:
                _store_phases(scr, o_ref, d, ts)
        if with_transpose:
            rest[nd][...] = y.T.astype(BF16)

    outs = pl.pallas_call(
        body, name=name, grid=(S // ts,),
        in_specs=[pl.BlockSpec((ts, Dm), lambda i: (i, 0)), _const_spec((1, Dm))],
        out_specs=[_phase_spec(d, ts, Dm) for d in dils] + [pl.BlockSpec((Dm, ts), lambda i: (0, i))] * with_transpose,
        out_shape=[_phase_shape(S, Dm, d, BF16) for d in dils] + [jax.ShapeDtypeStruct((Dm, S), BF16)] * with_transpose,
        scratch_shapes=[_nat_scratch(ts, Dm)],
        compiler_params=_params(("parallel",)),
    )(x, gain)
    return [o.reshape(S, Dm) for o in outs[:nd]] + list(outs[nd:])


def rmsnorm_bwd(x, gain, dys, dres, *, name, dils=(1,)):
    S, Dm = x.shape
    ts = _tile(S, 256)
    nd = len(dils)

    def body(x_ref, g_ref, *rest):
        dy_refs, (dr_ref, dx_ref, dg_ref, scr) = rest[:nd], rest[nd:]

        @pl.when(pl.program_id(0) == 0)
        def _():
            dg_ref[...] = jnp.zeros_like(dg_ref)

        dyv = None
        for d, dy_ref in zip(dils, dy_refs):
            if d == 1:
                t = dy_ref[...].astype(F32)
            else:
                _load_phases(dy_ref, scr, d, ts)
                t = _get_natural(scr)
            dyv = t if dyv is None else dyv + t
        xv = x_ref[...]
        r = lax.rsqrt(jnp.mean(xv * xv, axis=-1, keepdims=True) + RMS_EPS)
        xhat = xv * r
        dyg = dyv * g_ref[...]
        dx = r * (dyg - xhat * jnp.mean(dyg * xhat, axis=-1, keepdims=True))
        dx_ref[...] = dr_ref[...] + dx
        dg_ref[0:1, :] += jnp.sum(dyv * xhat, axis=0, keepdims=True)

    row = pl.BlockSpec((ts, Dm), lambda i: (i, 0))
    return pl.pallas_call(
        body, name=name, grid=(S // ts,),
        in_specs=[row, _const_spec((1, Dm))] + [_phase_spec(d, ts, Dm) for d in dils] + [row],
        out_specs=[row, _const_spec((8, Dm))],
        out_shape=[jax.ShapeDtypeStruct((S, Dm), F32), jax.ShapeDtypeStruct((8, Dm), F32)],
        scratch_shapes=[_nat_scratch(ts, Dm)],
        compiler_params=_params(("arbitrary",)),
    )(x, gain, *[_phase_view(dy, d) for dy, d in zip(dys, dils)], dres)


def swiglu_fwd(gu, *, name):
    S, F2 = gu.shape
    Fh = F2 // 2
    ts = _tile(S, 512)

    def body(g_ref, u_ref, o_ref, ot_ref):
        g = g_ref[...].astype(F32)
        u = u_ref[...].astype(F32)
        act = g * (1.0 / (1.0 + jnp.exp(-g))) * u
        o_ref[...] = act.astype(o_ref.dtype)
        ot_ref[...] = act.T.astype(ot_ref.dtype)

    return pl.pallas_call(
        body, name=name, grid=(S // ts,),
        in_specs=[pl.BlockSpec((ts, Fh), lambda i: (i, 0)), pl.BlockSpec((ts, Fh), lambda i: (i, 1))],
        out_specs=[pl.BlockSpec((ts, Fh), lambda i: (i, 0)), pl.BlockSpec((Fh, ts), lambda i: (0, i))],
        out_shape=[jax.ShapeDtypeStruct((S, Fh), BF16), jax.ShapeDtypeStruct((Fh, S), BF16)],
        compiler_params=_params(("parallel",)),
    )(gu, gu)


def swiglu_bwd(gu, dact, *, name):
    S, F2 = gu.shape
    Fh = F2 // 2
    ts = _tile(S, 256)

    def body(gu_ref, d_ref, o_ref):
        g = gu_ref[:, :Fh].astype(F32)
        u = gu_ref[:, Fh:].astype(F32)
        d = d_ref[...].astype(F32)
        sig = 1.0 / (1.0 + jnp.exp(-g))
        o_ref[:, :Fh] = (d * u * (sig * (1.0 + g * (1.0 - sig)))).astype(o_ref.dtype)
        o_ref[:, Fh:] = (d * (g * sig)).astype(o_ref.dtype)

    return pl.pallas_call(
        body, name=name, grid=(S // ts,),
        in_specs=[pl.BlockSpec((ts, F2), lambda i: (i, 0)), pl.BlockSpec((ts, Fh), lambda i: (i, 0))],
        out_specs=pl.BlockSpec((ts, F2), lambda i: (i, 0)),
        out_shape=jax.ShapeDtypeStruct((S, F2), BF16),
        compiler_params=_params(("parallel",)),
    )(gu, dact)


def loss_head(y, target, *, name):
    S, Dm = y.shape
    ts = _tile(S, 512)

    def body(y_ref, t_ref, l_ref, d_ref):
        @pl.when(pl.program_id(0) == 0)
        def _():
            l_ref[...] = jnp.zeros_like(l_ref)

        e = y_ref[...] - t_ref[...]
        d_ref[...] = e * (1.0 / Dm)
        l_ref[...] += jnp.sum(e * e)

    return pl.pallas_call(
        body, name=name, grid=(S // ts,),
        in_specs=[pl.BlockSpec((ts, Dm), lambda i: (i, 0)), pl.BlockSpec((ts, Dm), lambda i: (i, 0))],
        out_specs=[_const_spec((8, LANES)), pl.BlockSpec((ts, Dm), lambda i: (i, 0))],
        out_shape=[jax.ShapeDtypeStruct((8, LANES), F32), jax.ShapeDtypeStruct((S, Dm), F32)],
        compiler_params=_params(("arbitrary",)),
    )(y, target)


def _head_block_ones():
    i = np.arange(LANES)
    return jnp.asarray((i[:, None] // A_HEAD_DIM == i[None, :] // A_HEAD_DIM).astype(np.float32)).astype(BF16)


def _rope_tables(S):
    half = A_HEAD_DIM // 2
    inv_freq = ROPE_THETA ** (-jnp.arange(half, dtype=F32) / half)
    ang = jnp.arange(S).astype(F32)[:, None] * inv_freq[None, :]
    cos = jnp.tile(jnp.cos(ang), (1, LANES // half))
    sin = jnp.tile(jnp.sin(ang), (1, LANES // half))
    return cos, sin


def _rot_half(x, lo):
    return jnp.where(lo, -pltpu.roll(x, LANES - 32, 1), pltpu.roll(x, 32, 1))


def _seg_sum(v, ones_ref):
    hi = v.astype(BF16)
    lo = (v - hi.astype(F32)).astype(BF16)
    ones = ones_ref[...]
    return jnp.dot(hi, ones, preferred_element_type=F32) + jnp.dot(lo, ones, preferred_element_type=F32)


def _seg_mean(v, ones_ref):
    return _seg_sum(v, ones_ref) * (1.0 / A_HEAD_DIM)


def qk_prep_fwd(qkv, gain, cos, sin, ones, *, name):
    S, W = qkv.shape
    ts = _tile(S, 1024)
    nchunk = A_WIDTH // LANES

    def body(x_ref, g_ref, c_ref, s_ref, ones_ref, o_ref):
        kind = pl.program_id(1) % 3

        @pl.when(kind < 2)
        def _():
            scale = jnp.where(kind == 0, A_HEAD_DIM ** -0.5, 1.0).astype(F32)
            lo = (lax.broadcasted_iota(jnp.int32, (ts, LANES), 1) % A_HEAD_DIM) < (A_HEAD_DIM // 2)
            cv, sv = c_ref[...], s_ref[...]
            for c in range(nchunk):
                sl = slice(c * LANES, (c + 1) * LANES)
                xv = x_ref[:, sl]
                r = lax.rsqrt(_seg_mean(xv * xv, ones_ref) + RMS_EPS)
                y = (xv * r) * g_ref[:, sl]
                y = y * cv + _rot_half(y, lo) * sv
                o_ref[:, sl] = (y * scale).astype(o_ref.dtype)

        @pl.when(kind == 2)
        def _():
            o_ref[...] = x_ref[...].astype(o_ref.dtype)

    return pl.pallas_call(
        body, name=name, grid=(S // ts, W // A_WIDTH),
        in_specs=[pl.BlockSpec((ts, A_WIDTH), lambda i, j: (i, j)), pl.BlockSpec((1, A_WIDTH), lambda i, j: (0, j)),
                  pl.BlockSpec((ts, LANES), lambda i, j: (i, 0)), pl.BlockSpec((ts, LANES), lambda i, j: (i, 0)),
                  _const_spec((LANES, LANES))],
        out_specs=pl.BlockSpec((ts, A_WIDTH), lambda i, j: (i, j)),
        out_shape=jax.ShapeDtypeStruct((S, W), BF16),
        compiler_params=_params(("parallel", "arbitrary")),
    )(qkv, gain, cos, sin, ones)


def qk_prep_bwd(qkv, gain, cos, sin, ones, grads, *, name):
    S, W = qkv.shape
    ts = _tile(S, 1024)
    nchunk = A_WIDTH // LANES
    nj = W // A_WIDTH

    def body(x_ref, g_ref, c_ref, s_ref, ones_ref, *rest):
        g_refs, (o_ref, dg_ref) = rest[:nj], rest[nj:]
        j = pl.program_id(0)
        kind = j % 3

        @pl.when(pl.program_id(1) == 0)
        def _():
            dg_ref[...] = jnp.zeros_like(dg_ref)

        for n in range(nj):
            @pl.when(j == n)
            def _(n=n):
                d_ref = g_refs[n]
                if n % 3 == 2:
                    o_ref[...] = d_ref[...].astype(o_ref.dtype)
                    return
                scale = A_HEAD_DIM ** -0.5 if n % 3 == 0 else 1.0
                lo = (lax.broadcasted_iota(jnp.int32, (ts, LANES), 1) % A_HEAD_DIM) < (A_HEAD_DIM // 2)
                cv, sv = c_ref[...], s_ref[...]
                for c in range(nchunk):
                    sl = slice(c * LANES, (c + 1) * LANES)
                    dy = d_ref[:, sl].astype(F32) * scale
                    dn = dy * cv - _rot_half(dy, lo) * sv
                    xv = x_ref[:, sl]
                    r = lax.rsqrt(_seg_mean(xv * xv, ones_ref) + RMS_EPS)
                    xhat = xv * r
                    dyg = dn * g_ref[:, sl]
                    dx = r * (dyg - xhat * _seg_mean(dyg * xhat, ones_ref))
                    o_ref[:, sl] = dx.astype(o_ref.dtype)
                    dg_ref[0:1, sl] += jnp.sum(dn * xhat, axis=0, keepdims=True)

    def gspec(n):
        return pl.BlockSpec((ts, A_WIDTH), lambda j, i: (jnp.where(j == n, i, 0), 0))

    return pl.pallas_call(
        body, name=name, grid=(nj, S // ts),
        in_specs=[pl.BlockSpec((ts, A_WIDTH), lambda j, i: (i, j)), pl.BlockSpec((1, A_WIDTH), lambda j, i: (0, j)),
                  pl.BlockSpec((ts, LANES), lambda j, i: (i, 0)), pl.BlockSpec((ts, LANES), lambda j, i: (i, 0)),
                  _const_spec((LANES, LANES))] + [gspec(n) for n in range(nj)],
        out_specs=[pl.BlockSpec((ts, A_WIDTH), lambda j, i: (i, j)), pl.BlockSpec((8, A_WIDTH), lambda j, i: (0, j))],
        out_shape=[jax.ShapeDtypeStruct((S, W), BF16), jax.ShapeDtypeStruct((8, W), F32)],
        compiler_params=_params(("arbitrary", "arbitrary")),
    )(qkv, gain, cos, sin, ones, *grads)


def _band_specs(kind, tq, nlb):
    nhb = tq // A_HALF
    nb = nlb // nhb
    base = kind * (A_WIDTH // LANES)
    return [pl.BlockSpec((A_HALF, LANES), lambda ph, b, hp: (ph * nlb + jnp.maximum(b * nhb - 1, 0), base + hp)),
            pl.BlockSpec((tq, LANES), lambda ph, b, hp: (ph * nb + b, base + hp)),
            pl.BlockSpec((A_HALF, LANES), lambda ph, b, hp: (ph * nlb + jnp.minimum((b + 1) * nhb, nlb - 1), base + hp))]


A_BLOCK = 4096
A_SUB = 128


def _band_bias(sub, key_major):
    i = np.arange(sub)[:, None]
    j = np.arange(sub + 2 * A_HALF)[None, :] - A_HALF
    ok = np.abs(j - i) <= A_HALF
    return jnp.asarray(np.where(ok.T if key_major else ok, 0.0, NEG_INF).astype(np.float32))


def _edge_bias(first, n, L, axis, at_start, at_end):
    if not (at_start or at_end):
        return None
    shape = (1, n) if axis == 1 else (n, 1)
    pos = first - A_HALF + lax.broadcasted_iota(jnp.int32, shape, axis)
    return jnp.where((pos < 0) | (pos >= L), NEG_INF, 0.0).astype(F32)


def _with_edge(bias, edge):
    return bias if edge is None else bias + edge


def _cat3(a_ref, b_ref, c_ref):
    return jnp.concatenate([a_ref[...], b_ref[...], c_ref[...]], axis=0)


def _lane_lo(rows):
    return lax.broadcasted_iota(jnp.int32, (rows, LANES), 1) < A_HEAD_DIM


NT = (((1,), (1,)), ((), ()))
TN = (((0,), (0,)), ((), ()))


def attn_fwd(qkvp, dil, *, name):
    S = qkvp.shape[0]
    L = S // dil
    tq = _tile(L, A_BLOCK)
    sub = min(tq, A_SUB)
    nsub = tq // sub
    nlb = L // A_HALF
    band = _band_bias(sub, False)

    def body(q_ref, kp_ref, ko_ref, kn_ref, vp_ref, vo_ref, vn_ref, band_ref, o_ref, l_ref):
        b = pl.program_id(1)
        K = _cat3(kp_ref, ko_ref, kn_ref)
        V = _cat3(vp_ref, vo_ref, vn_ref)
        lo_k = _lane_lo(tq + 2 * A_HALF)
        lo_q = _lane_lo(sub)
        Km = [jnp.where(sel, K, jnp.zeros_like(K)) for sel in (lo_k, ~lo_k)]
        Vm = [jnp.where(sel, V, jnp.zeros_like(V)) for sel in (lo_k, ~lo_k)]
        for r in range(nsub):
            rows = slice(r * sub, (r + 1) * sub)
            keys = slice(r * sub, (r + 1) * sub + 2 * A_HALF)
            bias = _with_edge(band_ref[...], _edge_bias(b * tq + r * sub, sub + 2 * A_HALF, L, 1, r == 0, r == nsub - 1))
            q = q_ref[rows, :]
            outs, lses = [], []
            for hh in range(2):
                s = lax.dot_general(q, Km[hh][keys], NT, preferred_element_type=F32) + bias
                m = jnp.max(s, axis=1, keepdims=True)
                p = jnp.exp(s - m)
                l = jnp.sum(p, axis=1, keepdims=True)
                outs.append(jnp.dot(p.astype(BF16), Vm[hh][keys], preferred_element_type=F32) * (1.0 / l))
                lses.append(m + jnp.log(l))
            o_ref[rows, :] = outs[0] + outs[1]
            l_ref[rows, :] = jnp.where(lo_q, lses[0], lses[1])

    ospec = _band_specs(0, tq, nlb)[1]
    return pl.pallas_call(
        body, name=name, grid=(dil, L // tq, A_WIDTH // LANES),
        in_specs=[_band_specs(0, tq, nlb)[1]] + _band_specs(1, tq, nlb) + _band_specs(2, tq, nlb)
        + [_const_spec(band.shape)],
        out_specs=[ospec, ospec],
        out_shape=[jax.ShapeDtypeStruct((S, A_WIDTH), F32)] * 2,
        compiler_params=_params(("parallel", "parallel", "parallel")),
    )(*([qkvp] * 7), band)


def attn_merge(os_, lses, dils, *, name):
    S = os_[0].shape[0]
    ts = _tile(S, 256)
    ng = len(dils)

    def body(*refs):
        o_refs, l_refs, out_ref = refs[:ng], refs[ng:2 * ng], refs[2 * ng]
        lse_refs, scrs = refs[2 * ng + 1:3 * ng + 1], refs[3 * ng + 1:]
        ov, ls, k = [], [], 0
        for d, o_ref, l_ref in zip(dils, o_refs, l_refs):
            if d == 1:
                ov.append(o_ref[...])
                ls.append(l_ref[...])
            else:
                _load_phases(o_ref, scrs[k], d, ts)
                _load_phases(l_ref, scrs[k + 1], d, ts)
                ov.append(_get_natural(scrs[k]))
                ls.append(_get_natural(scrs[k + 1]))
                k += 2
        m = functools.reduce(jnp.maximum, ls)
        es = [jnp.exp(l - m) for l in ls]
        tot = functools.reduce(jnp.add, es)
        acc = None
        for e, o in zip(es, ov):
            t = (e / tot) * o
            acc = t if acc is None else acc + t
        out_ref[...] = acc.astype(out_ref.dtype)
        total = m + jnp.log(tot)
        _put_natural(scrs[k], total)
        for d, lse_ref in zip(dils, lse_refs):
            if d == 1:
                lse_ref[...] = total
            else:
                _store_phases(scrs[k], lse_ref, d, ts)

    n_scr = 2 * sum(d > 1 for d in dils) + 1
    outs = pl.pallas_call(
        body, name=name, grid=(S // ts,),
        in_specs=[_phase_spec(d, ts, A_WIDTH) for d in dils] * 2,
        out_specs=[pl.BlockSpec((ts, A_WIDTH), lambda i: (i, 0))] + [_phase_spec(d, ts, A_WIDTH) for d in dils],
        out_shape=[jax.ShapeDtypeStruct((S, A_WIDTH), BF16)] + [_phase_shape(S, A_WIDTH, d, F32) for d in dils],
        scratch_shapes=[_nat_scratch(ts, A_WIDTH)] * n_scr,
        compiler_params=_params(("parallel",)),
    )(*[_phase_view(o, d) for o, d in zip(os_, dils)], *[_phase_view(l, d) for l, d in zip(lses, dils)])
    return outs[0], [l.reshape(S, A_WIDTH) for l in outs[1:]]


def attn_delta(dout, out, ones, dils, *, name):
    S = dout.shape[0]
    ts = _tile(S, 256)
    nd = len(dils)

    def body(d_ref, o_ref, ones_ref, *rest):
        do_refs, dl_refs, (scr_do, scr_dl) = rest[:nd], rest[nd:2 * nd], rest[2 * nd:]
        sums = []
        for c in range(A_WIDTH // LANES):
            sl = slice(c * LANES, (c + 1) * LANES)
            prod = d_ref[:, sl].astype(F32) * o_ref[:, sl].astype(F32)
            sums.append(_seg_sum(prod, ones_ref))
        _put_natural(scr_do, d_ref[...].astype(F32))
        _put_natural(scr_dl, jnp.concatenate(sums, axis=1))
        for d, do_ref, dl_ref in zip(dils, do_refs, dl_refs):
            if d == 1:
                do_ref[...] = d_ref[...]
                dl_ref[...] = _get_natural(scr_dl)
            else:
                _store_phases(scr_do, do_ref, d, ts)
                _store_phases(scr_dl, dl_ref, d, ts)

    spec = pl.BlockSpec((ts, A_WIDTH), lambda i: (i, 0))
    outs = pl.pallas_call(
        body, name=name, grid=(S // ts,), in_specs=[spec, spec, _const_spec((LANES, LANES))],
        out_specs=[_phase_spec(d, ts, A_WIDTH) for d in dils] * 2,
        out_shape=[_phase_shape(S, A_WIDTH, d, BF16) for d in dils] + [_phase_shape(S, A_WIDTH, d, F32) for d in dils],
        scratch_shapes=[_nat_scratch(ts, A_WIDTH)] * 2,
        compiler_params=_params(("parallel",)),
    )(dout, out, ones)
    outs = [o.reshape(S, A_WIDTH) for o in outs]
    return outs[:nd], outs[nd:]


def _head_col(x, hh):
    c = hh * A_HEAD_DIM
    return x[:, c:c + 1]


def attn_bwd_dq(qkvp, dout, lse, delta, dil, *, name):
    S = qkvp.shape[0]
    L = S // dil
    tq = _tile(L, A_BLOCK)
    sub = min(tq, A_SUB)
    nsub = tq // sub
    nlb = L // A_HALF
    band = _band_bias(sub, False)

    def body(q_ref, kp_ref, ko_ref, kn_ref, vp_ref, vo_ref, vn_ref, do_ref, l_ref, d_ref, band_ref, dq_ref):
        b = pl.program_id(1)
        K = _cat3(kp_ref, ko_ref, kn_ref)
        V = _cat3(vp_ref, vo_ref, vn_ref)
        lo_k = _lane_lo(tq + 2 * A_HALF)
        Km = [jnp.where(sel, K, jnp.zeros_like(K)) for sel in (lo_k, ~lo_k)]
        Vm = [jnp.where(sel, V, jnp.zeros_like(V)) for sel in (lo_k, ~lo_k)]
        for r in range(nsub):
            rows = slice(r * sub, (r + 1) * sub)
            keys = slice(r * sub, (r + 1) * sub + 2 * A_HALF)
            bias = _with_edge(band_ref[...], _edge_bias(b * tq + r * sub, sub + 2 * A_HALF, L, 1, r == 0, r == nsub - 1))
            q, do = q_ref[rows, :], do_ref[rows, :]
            lse_v, dl_v = l_ref[rows, :], d_ref[rows, :]
            acc = None
            for hh in range(2):
                s = lax.dot_general(q, Km[hh][keys], NT, preferred_element_type=F32) + bias
                p = jnp.exp(s - _head_col(lse_v, hh))
                dp = lax.dot_general(do, Vm[hh][keys], NT, preferred_element_type=F32)
                ds = p * (dp - _head_col(dl_v, hh))
                t = jnp.dot(ds.astype(BF16), Km[hh][keys], preferred_element_type=F32)
                acc = t if acc is None else acc + t
            dq_ref[rows, :] = acc.astype(dq_ref.dtype)

    nspec = _band_specs(0, tq, nlb)[1]
    return pl.pallas_call(
        body, name=name, grid=(dil, L // tq, A_WIDTH // LANES),
        in_specs=[nspec] + _band_specs(1, tq, nlb) + _band_specs(2, tq, nlb) + [nspec, nspec, nspec, _const_spec(band.shape)],
        out_specs=nspec,
        out_shape=jax.ShapeDtypeStruct((S, A_WIDTH), BF16),
        compiler_params=_params(("parallel", "parallel", "parallel")),
    )(*([qkvp] * 7), dout, lse, delta, band)


def attn_bwd_dkv(qkvp, dout, lse, delta, dil, *, name):
    S = qkvp.shape[0]
    L = S // dil
    tk = _tile(L, A_BLOCK)
    sub = min(tk, A_SUB)
    nsub = tk // sub
    nlb = L // A_HALF
    band = _band_bias(sub, False)

    def body(qp_ref, qo_ref, qn_ref, k_ref, v_ref, dp_ref, do_ref, dn_ref, lp_ref, lo_ref, ln_ref,
             ep_ref, eo_ref, en_ref, band_ref, dk_ref, dv_ref):
        b = pl.program_id(1)
        Q = _cat3(qp_ref, qo_ref, qn_ref)
        DO = _cat3(dp_ref, do_ref, dn_ref)
        lse_v = _cat3(lp_ref, lo_ref, ln_ref)
        dl_v = _cat3(ep_ref, eo_ref, en_ref)
        lo_q = _lane_lo(tk + 2 * A_HALF)
        Qm = [jnp.where(sel, Q, jnp.zeros_like(Q)) for sel in (lo_q, ~lo_q)]
        DOm = [jnp.where(sel, DO, jnp.zeros_like(DO)) for sel in (lo_q, ~lo_q)]
        for r in range(nsub):
            keys = slice(r * sub, (r + 1) * sub)
            qs = slice(r * sub, (r + 1) * sub + 2 * A_HALF)
            bias = _with_edge(band_ref[...], _edge_bias(b * tk + r * sub, sub + 2 * A_HALF, L, 1, r == 0, r == nsub - 1))
            K, V = k_ref[keys, :], v_ref[keys, :]
            lse_t, dl_t = lse_v[qs].T, dl_v[qs].T
            dk = dv = None
            for hh in range(2):
                hr = slice(hh * A_HEAD_DIM, hh * A_HEAD_DIM + 1)
                st = lax.dot_general(K, Qm[hh][qs], NT, preferred_element_type=F32) + bias
                pt = jnp.exp(st - lse_t[hr, :])
                dpt = lax.dot_general(V, DOm[hh][qs], NT, preferred_element_type=F32)
                dst = pt * (dpt - dl_t[hr, :])
                tv = jnp.dot(pt.astype(BF16), DOm[hh][qs], preferred_element_type=F32)
                tk_ = jnp.dot(dst.astype(BF16), Qm[hh][qs], preferred_element_type=F32)
                dv = tv if dv is None else dv + tv
                dk = tk_ if dk is None else dk + tk_
            dk_ref[keys, :] = dk.astype(dk_ref.dtype)
            dv_ref[keys, :] = dv.astype(dv_ref.dtype)

    nspec = _band_specs(0, tk, nlb)[1]
    return pl.pallas_call(
        body, name=name, grid=(dil, L // tk, A_WIDTH // LANES),
        in_specs=_band_specs(0, tk, nlb) + [_band_specs(1, tk, nlb)[1], _band_specs(2, tk, nlb)[1]]
        + _band_specs(0, tk, nlb) * 3 + [_const_spec(band.shape)],
        out_specs=[nspec, nspec],
        out_shape=[jax.ShapeDtypeStruct((S, A_WIDTH), BF16)] * 2,
        compiler_params=_params(("parallel", "parallel", "parallel")),
    )(*([qkvp] * 5), *([dout] * 3), *([lse] * 3), *([delta] * 3), band)


def _gate_block_weight(wf, wb):
    w = jnp.zeros((LANES, 2 * B_QK_WIDTH), F32)
    w = w.at[:B_GATE_RANK, :B_QK_WIDTH].set(wf)
    w = w.at[B_GATE_RANK:2 * B_GATE_RANK, B_QK_WIDTH:].set(wb)
    return w.astype(BF16)


def gate_fwd(proj, wblk, bias, *, name):
    S = proj.shape[0]
    ts = _tile(S, 512)
    W = 2 * B_QK_WIDTH
    zcol = (2 * B_QK_WIDTH + 2 * B_V_WIDTH) // LANES

    def body(z_ref, w_ref, b_ref, o_ref):
        x = jnp.dot(z_ref[...].astype(BF16), w_ref[...], preferred_element_type=F32) + b_ref[...]
        o_ref[...] = (jnp.minimum(x, 0.0) - jnp.log(1.0 + jnp.exp(-jnp.abs(x)))) * (1.0 / B_GATE_TAU)

    return pl.pallas_call(
        body, name=name, grid=(S // ts,),
        in_specs=[pl.BlockSpec((ts, LANES), lambda i: (i, zcol)), _const_spec((LANES, W)), _const_spec((1, W))],
        out_specs=pl.BlockSpec((ts, W), lambda i: (i, 0)),
        out_shape=jax.ShapeDtypeStruct((S, W), F32),
        compiler_params=_params(("parallel",)),
    )(proj, wblk, bias)


def gate_bwd(proj, wblk, wblk_t, bias, dloga, *, name):
    S = proj.shape[0]
    ts = _tile(S, 512)
    W = 2 * B_QK_WIDTH
    zcol = (2 * B_QK_WIDTH + 2 * B_V_WIDTH) // LANES

    def body(z_ref, w_ref, wt_ref, b_ref, d_ref, dz_ref, dw_ref, db_ref):
        @pl.when(pl.program_id(0) == 0)
        def _():
            dw_ref[...] = jnp.zeros_like(dw_ref)
            db_ref[...] = jnp.zeros_like(db_ref)

        z = z_ref[...].astype(BF16)
        x = jnp.dot(z, w_ref[...], preferred_element_type=F32) + b_ref[...]
        e = jnp.exp(-jnp.abs(x))
        sig_neg = jnp.where(x >= 0, e, 1.0) / (1.0 + e)
        dx = d_ref[...] * (1.0 / B_GATE_TAU) * sig_neg
        dxb = dx.astype(BF16)
        dz_ref[...] = jnp.dot(dxb, wt_ref[...], preferred_element_type=F32)
        dw_ref[...] += lax.dot_general(z, dxb, TN, preferred_element_type=F32)
        db_ref[0:1, :] += jnp.sum(dx, axis=0, keepdims=True)

    return pl.pallas_call(
        body, name=name, grid=(S // ts,),
        in_specs=[pl.BlockSpec((ts, LANES), lambda i: (i, zcol)), _const_spec((LANES, W)), _const_spec((W, LANES)),
                  _const_spec((1, W)), pl.BlockSpec((ts, W), lambda i: (i, 0))],
        out_specs=[pl.BlockSpec((ts, LANES), lambda i: (i, 0)), _const_spec((LANES, W)), _const_spec((8, W))],
        out_shape=[jax.ShapeDtypeStruct((S, LANES), F32), jax.ShapeDtypeStruct((LANES, W), F32),
                   jax.ShapeDtypeStruct((8, W), F32)],
        compiler_params=_params(("arbitrary",)),
    )(proj, wblk, wblk_t, bias, dloga)


def _tri(reverse):
    i = np.arange(B_CHUNK)
    t = (i[None, :] >= i[:, None]) if reverse else (i[None, :] <= i[:, None])
    return jnp.asarray(t.astype(np.float32))


def _gla_terms(q, k, la, t_ref, reverse):
    b = jnp.dot(t_ref[...], la, precision=HI, preferred_element_type=F32)
    b_last = b[0:1, :] if reverse else b[B_CHUNK - 1:B_CHUNK, :]
    e_b = jnp.exp(b)
    qt = (q * (B_KEY_DIM ** -0.5)) * e_b
    e_nb = jnp.exp(-b)
    kt = k * e_nb
    e_end = jnp.exp(b_last - b)
    kend = k * e_end
    dec = jnp.exp(b_last)
    return e_nb, e_b, qt, kt, e_end, kend, dec


def _chunk_mask(reverse, transpose=False):
    r = lax.broadcasted_iota(jnp.int32, (B_CHUNK, B_CHUNK), 0)
    c = lax.broadcasted_iota(jnp.int32, (B_CHUNK, B_CHUNK), 1)
    if transpose:
        r, c = c, r
    return (c > r) if reverse else (c <= r)


def gla_fwd(proj, loga, tmat, reverse, *, name):
    S = proj.shape[0]
    tb = _tile(S, 512)
    nb = S // tb
    cpb = tb // B_CHUNK
    nc = S // B_CHUNK
    lb = 1 if reverse else 0
    blk = (lambda i: nb - 1 - i) if reverse else (lambda i: i)

    def body(qk_ref, v_ref, la_ref, t_ref, o_ref, st_ref, s_scr):
        @pl.when(pl.program_id(0) == 0)
        def _():
            s_scr[...] = jnp.zeros_like(s_scr)

        mask = _chunk_mask(reverse)
        order = range(cpb - 1, -1, -1) if reverse else range(cpb)
        for c in order:
            rows = slice(c * B_CHUNK, (c + 1) * B_CHUNK)
            _, _, qt, kt, _, kend, dec = _gla_terms(qk_ref[rows, :B_QK_WIDTH], qk_ref[rows, B_QK_WIDTH:],
                                                   la_ref[rows, :], t_ref, reverse)
            qt, kt, kend = qt.astype(BF16), kt.astype(BF16), kend.astype(BF16)
            for h in range(B_HEADS):
                kc = slice(h * B_KEY_DIM, (h + 1) * B_KEY_DIM)
                vc = slice(h * B_VAL_DIM, (h + 1) * B_VAL_DIM)
                v = v_ref[rows, vc].astype(BF16)
                st = s_scr[h]
                st_ref[h, c] = st
                a = jnp.where(mask, lax.dot_general(qt[:, kc], kt[:, kc], NT, preferred_element_type=F32), 0.0)
                o = jnp.dot(a.astype(BF16), v, preferred_element_type=F32)
                o = o + lax.dot_general(qt[:, kc], st.astype(BF16), NT, preferred_element_type=F32)
                o_ref[rows, vc] = o
                s_scr[h] = st * dec[:, kc] + lax.dot_general(v, kend[:, kc], TN, preferred_element_type=F32)

    return pl.pallas_call(
        body, name=name, grid=(nb,),
        in_specs=[pl.BlockSpec((tb, 2 * B_QK_WIDTH), lambda i: (blk(i), 0)),
                  pl.BlockSpec((tb, B_V_WIDTH), lambda i: (blk(i), 1)),
                  pl.BlockSpec((tb, B_QK_WIDTH), lambda i: (blk(i), lb)),
                  _const_spec((B_CHUNK, B_CHUNK))],
        out_specs=[pl.BlockSpec((tb, B_V_WIDTH), lambda i: (blk(i), 0)),
                   pl.BlockSpec((B_HEADS, cpb, B_VAL_DIM, B_KEY_DIM), lambda i: (0, blk(i), 0, 0))],
        out_shape=[jax.ShapeDtypeStruct((S, B_V_WIDTH), F32),
                   jax.ShapeDtypeStruct((B_HEADS, nc, B_VAL_DIM, B_KEY_DIM), F32)],
        scratch_shapes=[pltpu.VMEM((B_HEADS, B_VAL_DIM, B_KEY_DIM), F32)],
        compiler_params=_params(("arbitrary",)),
    )(proj, proj, loga, tmat)


def gla_bwd(proj, loga, states, do, tmat, tmat_t, reverse, *, name):
    S = proj.shape[0]
    tb = _tile(S, 256)
    nb = S // tb
    cpb = tb // B_CHUNK
    lb = 1 if reverse else 0
    blk = (lambda i: i) if reverse else (lambda i: nb - 1 - i)
    scale = B_KEY_DIM ** -0.5

    def body(qk_ref, v_ref, la_ref, st_ref, do_ref, t_ref, tt_ref, dq_ref, dk_ref, dv_ref, dla_ref, ds_scr):
        @pl.when(pl.program_id(0) == 0)
        def _():
            ds_scr[...] = jnp.zeros_like(ds_scr)

        mask = _chunk_mask(reverse)
        mask_t = _chunk_mask(reverse, transpose=True)
        last = 0 if reverse else B_CHUNK - 1
        is_last = lax.broadcasted_iota(jnp.int32, (B_CHUNK, B_QK_WIDTH), 0) == last
        order = range(cpb) if reverse else range(cpb - 1, -1, -1)
        for c in order:
            rows = slice(c * B_CHUNK, (c + 1) * B_CHUNK)
            e_nb, e_b, qt, kt, e_end, kend, dec = _gla_terms(qk_ref[rows, :B_QK_WIDTH], qk_ref[rows, B_QK_WIDTH:],
                                                             la_ref[rows, :], t_ref, reverse)
            qtb, ktb, kendb = qt.astype(BF16), kt.astype(BF16), kend.astype(BF16)
            dqt_h, dkt_h, dkend_h, ddec_h = [], [], [], []
            for h in range(B_HEADS):
                kc = slice(h * B_KEY_DIM, (h + 1) * B_KEY_DIM)
                vc = slice(h * B_VAL_DIM, (h + 1) * B_VAL_DIM)
                v = v_ref[rows, vc].astype(BF16)
                dob = do_ref[rows, vc].astype(BF16)
                st = st_ref[h, c]
                dst = ds_scr[h]
                dstb = dst.astype(BF16)
                a_t = jnp.where(mask_t, lax.dot_general(ktb[:, kc], qtb[:, kc], NT, preferred_element_type=F32), 0.0)
                da = jnp.where(mask, lax.dot_general(dob, v, NT, preferred_element_type=F32), 0.0)
                da_t = jnp.where(mask_t, lax.dot_general(v, dob, NT, preferred_element_type=F32), 0.0)
                dv = jnp.dot(a_t.astype(BF16), dob, preferred_element_type=F32)
                dv_ref[rows, vc] = dv + lax.dot_general(kendb[:, kc], dstb, NT, preferred_element_type=F32)
                dqt = jnp.dot(da.astype(BF16), ktb[:, kc], preferred_element_type=F32)
                dqt_h.append(dqt + jnp.dot(dob, st.astype(BF16), preferred_element_type=F32))
                dkt_h.append(jnp.dot(da_t.astype(BF16), qtb[:, kc], preferred_element_type=F32))
                dkend_h.append(jnp.dot(v, dstb, preferred_element_type=F32))
                ddec_h.append(jnp.sum(dst * st, axis=0, keepdims=True))
                ds_scr[h] = dst * dec[:, kc] + lax.dot_general(dob, qtb[:, kc], TN, preferred_element_type=F32)
            dqt, dkt = jnp.concatenate(dqt_h, axis=1), jnp.concatenate(dkt_h, axis=1)
            dkend, ddec = jnp.concatenate(dkend_h, axis=1), jnp.concatenate(ddec_h, axis=1)
            ke = dkend * kend
            db = dqt * qt - dkt * kt - ke
            db_last = jnp.sum(ke, axis=0, keepdims=True) + ddec * dec
            db = db + jnp.where(is_last, db_last, 0.0)
            dq_ref[rows, :] = dqt * e_b * scale
            dk_ref[rows, :] = dkt * e_nb + dkend * e_end
            dla_ref[rows, :] = jnp.dot(tt_ref[...], db, precision=HI, preferred_element_type=F32)

    qspec = pl.BlockSpec((tb, B_QK_WIDTH), lambda i: (blk(i), 0))
    vspec = pl.BlockSpec((tb, B_V_WIDTH), lambda i: (blk(i), 0))
    return pl.pallas_call(
        body, name=name, grid=(nb,),
        in_specs=[pl.BlockSpec((tb, 2 * B_QK_WIDTH), lambda i: (blk(i), 0)),
                  pl.BlockSpec((tb, B_V_WIDTH), lambda i: (blk(i), 1)),
                  pl.BlockSpec((tb, B_QK_WIDTH), lambda i: (blk(i), lb)),
                  pl.BlockSpec((B_HEADS, cpb, B_VAL_DIM, B_KEY_DIM), lambda i: (0, blk(i), 0, 0)),
                  vspec, _const_spec((B_CHUNK, B_CHUNK)), _const_spec((B_CHUNK, B_CHUNK))],
        out_specs=[qspec, qspec, vspec, qspec],
        out_shape=[jax.ShapeDtypeStruct((S, B_QK_WIDTH), F32), jax.ShapeDtypeStruct((S, B_QK_WIDTH), F32),
                   jax.ShapeDtypeStruct((S, B_V_WIDTH), F32), jax.ShapeDtypeStruct((S, B_QK_WIDTH), F32)],
        scratch_shapes=[pltpu.VMEM((B_HEADS, B_VAL_DIM, B_KEY_DIM), F32)],
        compiler_params=_params(("arbitrary",)),
    )(proj, proj, loga, states, do, tmat, tmat_t)


def gla_post_fwd(o_f, o_b, gain, proj, *, name):
    S = o_f.shape[0]
    ts = _tile(S, 512)
    rcol = (2 * B_QK_WIDTH + B_V_WIDTH) // B_V_WIDTH

    def body(f_ref, b_ref, g_ref, r_ref, y_ref):
        for h in range(B_HEADS):
            sl = slice(h * B_VAL_DIM, (h + 1) * B_VAL_DIM)
            o = f_ref[:, sl] + b_ref[:, sl]
            n = (o * lax.rsqrt(jnp.mean(o * o, axis=-1, keepdims=True) + RMS_EPS)) * g_ref[:, sl]
            r = r_ref[:, sl]
            y_ref[:, sl] = (n * (r * (1.0 / (1.0 + jnp.exp(-r))))).astype(y_ref.dtype)

    spec = pl.BlockSpec((ts, B_V_WIDTH), lambda i: (i, 0))
    return pl.pallas_call(
        body, name=name, grid=(S // ts,),
        in_specs=[spec, spec, _const_spec((1, B_V_WIDTH)), pl.BlockSpec((ts, B_V_WIDTH), lambda i: (i, rcol))],
        out_specs=spec, out_shape=jax.ShapeDtypeStruct((S, B_V_WIDTH), BF16),
        compiler_params=_params(("parallel",)),
    )(o_f, o_b, gain, proj)


def gla_post_bwd(o_f, o_b, gain, proj, dy, *, name):
    S = o_f.shape[0]
    ts = _tile(S, 512)
    rcol = (2 * B_QK_WIDTH + B_V_WIDTH) // B_V_WIDTH

    def body(f_ref, b_ref, g_ref, r_ref, dy_ref, do_ref, dr_ref, dg_ref):
        @pl.when(pl.program_id(0) == 0)
        def _():
            dg_ref[...] = jnp.zeros_like(dg_ref)

        for h in range(B_HEADS):
            sl = slice(h * B_VAL_DIM, (h + 1) * B_VAL_DIM)
            o = f_ref[:, sl] + b_ref[:, sl]
            rs = lax.rsqrt(jnp.mean(o * o, axis=-1, keepdims=True) + RMS_EPS)
            ohat = o * rs
            g = g_ref[:, sl]
            r = r_ref[:, sl]
            sig = 1.0 / (1.0 + jnp.exp(-r))
            dyv = dy_ref[:, sl].astype(F32)
            dn = dyv * (r * sig)
            dr_ref[:, sl] = dyv * (ohat * g) * (sig * (1.0 + r * (1.0 - sig)))
            dng = dn * g
            do_ref[:, sl] = rs * (dng - ohat * jnp.mean(dng * ohat, axis=-1, keepdims=True))
            dg_ref[0:1, sl] += jnp.sum(dn * ohat, axis=0, keepdims=True)

    spec = pl.BlockSpec((ts, B_V_WIDTH), lambda i: (i, 0))
    return pl.pallas_call(
        body, name=name, grid=(S // ts,),
        in_specs=[spec, spec, _const_spec((1, B_V_WIDTH)), pl.BlockSpec((ts, B_V_WIDTH), lambda i: (i, rcol)), spec],
        out_specs=[spec, spec, _const_spec((8, B_V_WIDTH))],
        out_shape=[jax.ShapeDtypeStruct((S, B_V_WIDTH), F32), jax.ShapeDtypeStruct((S, B_V_WIDTH), F32),
                   jax.ShapeDtypeStruct((8, B_V_WIDTH), F32)],
        compiler_params=_params(("arbitrary",)),
    )(o_f, o_b, gain, proj, dy)


def gla_combine(parts_f, parts_b, dr, dz, *, name):
    S = dr.shape[0]
    ts = _tile(S, 512)

    def body(qf, kf, vf, qb, kb, vb, r_ref, z_ref, o_ref):
        o_ref[:, 0:512] = (qf[...] + qb[...]).astype(o_ref.dtype)
        o_ref[:, 512:1024] = (kf[...] + kb[...]).astype(o_ref.dtype)
        o_ref[:, 1024:2048] = (vf[...] + vb[...]).astype(o_ref.dtype)
        o_ref[:, 2048:3072] = r_ref[...].astype(o_ref.dtype)
        o_ref[:, 3072:3200] = z_ref[...].astype(o_ref.dtype)

    s512 = pl.BlockSpec((ts, B_QK_WIDTH), lambda i: (i, 0))
    s1024 = pl.BlockSpec((ts, B_V_WIDTH), lambda i: (i, 0))
    return pl.pallas_call(
        body, name=name, grid=(S // ts,),
        in_specs=[s512, s512, s1024, s512, s512, s1024, s1024, pl.BlockSpec((ts, LANES), lambda i: (i, 0))],
        out_specs=pl.BlockSpec((ts, B_IN_PAD), lambda i: (i, 0)),
        out_shape=jax.ShapeDtypeStruct((S, B_IN_PAD), BF16),
        compiler_params=_params(("parallel",)),
    )(*parts_f, *parts_b, dr, dz)


def adamw(w, g, m, v, *, name):
    R, C = w.shape
    tr = _tile(R, 256)
    c1 = 1.0 / (1.0 - ADAM_B1 ** ADAM_STEP)
    c2 = 1.0 / (1.0 - ADAM_B2 ** ADAM_STEP)

    def body(w_ref, g_ref, m_ref, v_ref, d_ref, mo_ref, vo_ref):
        gv = g_ref[...]
        mn = ADAM_B1 * m_ref[...] + (1.0 - ADAM_B1) * gv
        vn = ADAM_B2 * v_ref[...] + (1.0 - ADAM_B2) * (gv * gv)
        mo_ref[...] = mn
        vo_ref[...] = vn
        d_ref[...] = -ADAM_LR * ((mn * c1) / (jnp.sqrt(vn * c2) + ADAM_EPS) + ADAM_WD * w_ref[...])

    spec = pl.BlockSpec((tr, C), lambda i: (i, 0))
    return pl.pallas_call(
        body, name=name, grid=(R // tr,), in_specs=[spec] * 4, out_specs=[spec] * 3,
        out_shape=[jax.ShapeDtypeStruct((R, C), F32)] * 3,
        compiler_params=_params(("parallel",)),
    )(w, g, m, v)


def _chip_peers():
    x, y, c = lax.axis_index("x"), lax.axis_index("y"), lax.axis_index("c")
    return x, y, c, [(1 - x, y), (x, 1 - y), (1 - x, 1 - y)]


_ANY = pl.BlockSpec(memory_space=pl.ANY)


def gather_shards(src, *, name):
    _, R, C = src.shape

    def body(src_ref, out_ref, ici_send, ici_recv, d2d_send, d2d_recv):
        x, y, c, chips = _chip_peers()
        me = 2 * x + y
        sibling = (x, y, 1 - c)
        sends = []
        for k, (px, py) in enumerate(chips):
            cp = pltpu.make_async_remote_copy(
                src_ref=src_ref.at[c], dst_ref=out_ref.at[me, c], send_sem=ici_send.at[k], recv_sem=ici_recv.at[k],
                device_id=(px, py, c), device_id_type=MESH)
            cp.start()
            sends.append(cp)
        for k, (px, py) in enumerate(chips):
            landed = out_ref.at[2 * px + py, c]
            pltpu.make_async_remote_copy(
                src_ref=src_ref.at[c], dst_ref=landed, send_sem=ici_send.at[k], recv_sem=ici_recv.at[k],
                device_id=(px, py, c), device_id_type=MESH).wait_recv()
            cp = pltpu.make_async_remote_copy(
                src_ref=landed, dst_ref=landed, send_sem=d2d_send.at[k], recv_sem=d2d_recv.at[k],
                device_id=sibling, device_id_type=MESH)
            cp.start()
            sends.append(cp)
        for k, (px, py) in enumerate(chips):
            other_half = out_ref.at[2 * px + py, 1 - c]
            pltpu.make_async_remote_copy(
                src_ref=other_half, dst_ref=other_half, send_sem=d2d_send.at[k], recv_sem=d2d_recv.at[k],
                device_id=sibling, device_id_type=MESH).wait_recv()
        for cp in sends:
            cp.wait_send()

    return pl.pallas_call(
        body, name=name, in_specs=[_ANY], out_specs=_ANY,
        out_shape=jax.ShapeDtypeStruct((N_CHIPS, 2, R, C), src.dtype),
        scratch_shapes=[pltpu.SemaphoreType.DMA((3,))] * 4,
        compiler_params=pltpu.CompilerParams(has_side_effects=True),
    )(src)


def chip_exchange(srcs, *, name):
    n = len(srcs)

    def body(*refs):
        src_refs, out_refs = refs[:n], refs[n:2 * n]
        send_sems, recv_sems = refs[2 * n:]
        x, y, c, chips = _chip_peers()
        me = 2 * x + y
        copies = []
        for i, (src_ref, out_ref) in enumerate(zip(src_refs, out_refs)):
            for k, (px, py) in enumerate(chips):
                cp = pltpu.make_async_remote_copy(
                    src_ref=src_ref.at[2 * px + py], dst_ref=out_ref.at[me],
                    send_sem=send_sems.at[i, k], recv_sem=recv_sems.at[i, k],
                    device_id=(px, py, c), device_id_type=MESH)
                cp.start()
                copies.append(cp)
        for i, (src_ref, out_ref) in enumerate(zip(src_refs, out_refs)):
            for k, (px, py) in enumerate(chips):
                pltpu.make_async_remote_copy(
                    src_ref=src_ref.at[me], dst_ref=out_ref.at[2 * px + py],
                    send_sem=send_sems.at[i, k], recv_sem=recv_sems.at[i, k],
                    device_id=(px, py, c), device_id_type=MESH).wait_recv()
        for cp in copies:
            cp.wait_send()

    return pl.pallas_call(
        body, name=name, in_specs=[_ANY] * n, out_specs=[_ANY] * n,
        out_shape=[jax.ShapeDtypeStruct(s.shape, s.dtype) for s in srcs],
        scratch_shapes=[pltpu.SemaphoreType.DMA((n, 3)), pltpu.SemaphoreType.DMA((n, 3))],
        compiler_params=pltpu.CompilerParams(has_side_effects=True),
    )(*srcs)


def sibling_exchange(srcs, *, name):
    n = len(srcs)

    def body(*refs):
        src_refs, out_refs, (send_sems, recv_sems) = refs[:n], refs[n:2 * n], refs[2 * n:]
        x, y, c = lax.axis_index("x"), lax.axis_index("y"), lax.axis_index("c")
        copies = []
        for i, (src_ref, out_ref) in enumerate(zip(src_refs, out_refs)):
            cp = pltpu.make_async_remote_copy(
                src_ref=src_ref.at[:, 1 - c], dst_ref=out_ref, send_sem=send_sems.at[i], recv_sem=recv_sems.at[i],
                device_id=(x, y, 1 - c), device_id_type=MESH)
            cp.start()
            copies.append(cp)
        for cp in copies:
            cp.wait()

    return pl.pallas_call(
        body, name=name, in_specs=[_ANY] * n, out_specs=[_ANY] * n,
        out_shape=[jax.ShapeDtypeStruct((s.shape[0],) + s.shape[2:], s.dtype) for s in srcs],
        scratch_shapes=[pltpu.SemaphoreType.DMA((n,)), pltpu.SemaphoreType.DMA((n,))],
        compiler_params=pltpu.CompilerParams(has_side_effects=True),
    )(*srcs)


def sibling_share(bufs, *, name):
    n = len(bufs)

    def body(*refs):
        out_refs, (send_sems, recv_sems) = refs[n:2 * n], refs[2 * n:]
        x, y, c = lax.axis_index("x"), lax.axis_index("y"), lax.axis_index("c")
        copies = []
        for i, out_ref in enumerate(out_refs):
            cp = pltpu.make_async_remote_copy(
                src_ref=out_ref.at[c], dst_ref=out_ref.at[c], send_sem=send_sems.at[i], recv_sem=recv_sems.at[i],
                device_id=(x, y, 1 - c), device_id_type=MESH)
            cp.start()
            copies.append(cp)
        for cp in copies:
            cp.wait()

    return pl.pallas_call(
        body, name=name, in_specs=[_ANY] * n, out_specs=[_ANY] * n,
        out_shape=[jax.ShapeDtypeStruct(b.shape, b.dtype) for b in bufs],
        input_output_aliases={i: i for i in range(n)},
        scratch_shapes=[pltpu.SemaphoreType.DMA((n,))] * 2,
        compiler_params=pltpu.CompilerParams(has_side_effects=True),
    )(*bufs)


def add_pair(a, b, out_dtype, *, name):
    n, _, R, C = a.shape
    tr = _tile(R, 256)

    def body(c_ref, a_ref, b_ref, o_ref):
        o_ref[...] = (a_ref[0] + b_ref[...]).astype(o_ref.dtype)

    return pl.pallas_call(
        body, name=name,
        grid_spec=pltpu.PrefetchScalarGridSpec(
            num_scalar_prefetch=1, grid=(n, R // tr),
            in_specs=[pl.BlockSpec((1, 1, tr, C), lambda s, i, c_ref: (s, c_ref[0], i, 0)),
                      pl.BlockSpec((1, tr, C), lambda s, i, c_ref: (s, i, 0))],
            out_specs=pl.BlockSpec((1, tr, C), lambda s, i, c_ref: (s, i, 0))),
        out_shape=jax.ShapeDtypeStruct((n, R, C), out_dtype),
        compiler_params=_params(("parallel", "parallel")),
    )(lax.axis_index("c").reshape(1).astype(jnp.int32), a, b)


def sum_slots(slots, own, *, name):
    n, R, C = slots.shape
    tr = _tile(R, 256)

    def body(ids_ref, *refs):
        slot_refs, own_ref, o_ref = refs[:n], refs[n], refs[n + 1]
        me = ids_ref[0]
        acc = None
        for s, r in enumerate(slot_refs):
            t = jnp.where(me == s, own_ref[0], r[0]).astype(F32)
            acc = t if acc is None else acc + t
        o_ref[0] = acc

    def slot_spec(s):
        return pl.BlockSpec((1, tr, C), lambda i, ids: (jnp.where(ids[0] == s, (s + 1) % n, s), i, 0))

    x, y, c = lax.axis_index("x"), lax.axis_index("y"), lax.axis_index("c")
    return pl.pallas_call(
        body, name=name,
        grid_spec=pltpu.PrefetchScalarGridSpec(
            num_scalar_prefetch=1, grid=(R // tr,),
            in_specs=[slot_spec(s) for s in range(n)] + [pl.BlockSpec((1, tr, C), lambda i, ids: (ids[0], i, 0))],
            out_specs=pl.BlockSpec((1, tr, C), lambda i, ids: (ids[1], i, 0))),
        out_shape=jax.ShapeDtypeStruct((2, R, C), F32),
        compiler_params=_params(("parallel",)),
    )(jnp.stack([2 * x + y, c]).astype(jnp.int32), *([slots] * n), own)


def _flat_rows(n_elems, mult):
    rows = -(-n_elems // FLAT_COLS)
    return -(-rows // mult) * mult


def _to_flat(parts, mult):
    v = jnp.concatenate([p.reshape(-1) for p in parts])
    rows = _flat_rows(v.shape[0], mult)
    return jnp.pad(v, (0, rows * FLAT_COLS - v.shape[0])).reshape(rows, FLAT_COLS)


def _from_flat(flat, shapes):
    v = flat.reshape(-1)
    out, off = [], 0
    for s in shapes:
        n = int(np.prod(s))
        out.append(v[off:off + n].reshape(s))
        off += n
    return out


def _unshard(blocks, axis):
    return jnp.concatenate([blocks[s] for s in range(N_CHIPS)], axis=axis)


def _by_shard(full, axis):
    shp = full.shape
    cut = full.reshape(shp[:axis] + (N_CHIPS, shp[axis] // N_CHIPS) + shp[axis + 1:])
    return jnp.moveaxis(cut, axis, 0)


def _gradient_blocks(grads):
    blocks = []
    for n in MATRICES:
        t = _by_shard(jnp.stack(grads[n]), SHARD_AXIS[n])
        blocks.append(t.reshape(N_CHIPS, 2, -1, t.shape[-1]))
    rest = []
    for s in range(N_CHIPS):
        parts = [jnp.stack(grads[n]) if n in REPLICATED else _by_shard(jnp.stack(grads[n]), SHARD_AXIS[n])[s]
                 for n in WEIGHTS if n not in MATRICES]
        rest.append(_to_flat(parts, 16))
    rest = jnp.stack(rest)
    blocks.append(rest.reshape(N_CHIPS, 2, rest.shape[1] // 2, FLAT_COLS))
    return blocks


def _gather_weights(w):
    full = {}
    for names, dtype, mult, call in ((BIG, BF16, 32, "gather_weights"), (SMALL_SHARDED, F32, 16, "gather_vectors")):
        parts = [w[n].astype(dtype) for n in names]
        flat = _to_flat(parts, mult)
        got = gather_shards(flat.reshape(2, flat.shape[0] // 2, FLAT_COLS), name=call)
        got = got.reshape(N_CHIPS, flat.shape[0], FLAT_COLS)
        me = 2 * lax.axis_index("x") + lax.axis_index("y")
        got = lax.dynamic_update_slice(got, flat[None], (me, 0, 0))
        per_chip = [_from_flat(got[s], [p.shape for p in parts]) for s in range(N_CHIPS)]
        for i, n in enumerate(names):
            full[n] = _unshard([per_chip[s][i] for s in range(N_CHIPS)], SHARD_AXIS[n])
    return full


def _reduce_gradients(grads, w):
    halves = _gradient_blocks(grads)
    other = sibling_exchange(halves, name="grad_pair_exchange")
    pair = [add_pair(a, b, BF16 if i < len(MATRICES) else F32, name=f"grad_pair_add{i}")
            for i, (a, b) in enumerate(zip(halves, other))]
    slots = chip_exchange(pair, name="grad_chip_exchange")
    mine = [sum_slots(a, p, name=f"grad_chip_sum{i}") for i, (a, p) in enumerate(zip(slots, pair))]
    both = sibling_share(mine, name="grad_half_share")
    g = {n: t.reshape(w[n].shape) for n, t in zip(MATRICES, both)}
    rest = [n for n in WEIGHTS if n not in MATRICES]
    g.update(zip(rest, _from_flat(both[-1], [w[n].shape for n in rest])))
    return g


def _layer_fwd(i, h, p, aux):
    j = i // 2
    sv = {"h0": h}
    if i % 2 == 0:
        hns = rmsnorm_fwd(h, p["attn_norm"][i][None], dils=DILS, name=f"l{i}_norm1")
        qkvs, qkvps, os_, lses = [], [], [], []
        for g, d in enumerate(DILS):
            qkv = matmul(hns[g], p["a_w_in_g"][j][g], tn=3 * A_WIDTH // 2, name=f"l{i}_a_in{g}")
            qkvp = qk_prep_fwd(qkv, aux["a_gain"][j][g], aux["cos"][g], aux["sin"][g], aux["ones"], name=f"l{i}_a_prep{g}")
            o, l = attn_fwd(qkvp, d, name=f"l{i}_a_attn{g}")
            qkvs.append(qkv)
            qkvps.append(qkvp)
            os_.append(o)
            lses.append(l)
        out, lse = attn_merge(os_, lses, DILS, name=f"l{i}_a_merge")
        sv.update(hns=hns, qkv=qkvs, qkvp=qkvps, out=out, lse=lse)
        h1 = matmul(out, p["a_w_out"][j], res=h, name=f"l{i}_a_out")
    else:
        hn, sv["hn_t"] = rmsnorm_fwd(h, p["attn_norm"][i][None], with_transpose=True, name=f"l{i}_norm1")
        proj = matmul(hn, p["b_w_in"][j], tn=640, name=f"l{i}_b_in")
        loga = gate_fwd(proj, aux["b_wblk"][j], aux["b_bias"][j], name=f"l{i}_b_gate")
        o_f, st_f = gla_fwd(proj, loga, aux["tri_f"], False, name=f"l{i}_b_gla_f")
        o_b, st_b = gla_fwd(proj, loga, aux["tri_b"], True, name=f"l{i}_b_gla_b")
        y = gla_post_fwd(o_f, o_b, aux["b_gain"][j], proj, name=f"l{i}_b_post")
        sv.update(proj=proj, loga=loga, o_f=o_f, o_b=o_b, st_f=st_f, st_b=st_b, y=y)
        h1 = matmul(y, p["b_w_out"][j], res=h, name=f"l{i}_b_out")
    sv["h1"] = h1
    hn2, hn2_t = rmsnorm_fwd(h1, p["ffn_norm"][i][None], with_transpose=True, name=f"l{i}_norm2")
    gu = matmul(hn2, p["ffn_w_gate_up"][i], out_dtype=BF16, tn=FFN_HIDDEN // 2, name=f"l{i}_f_up")
    act, act_t = swiglu_fwd(gu, name=f"l{i}_f_act")
    h2 = matmul(act, p["ffn_w_down"][i], res=h1, tk=2816, name=f"l{i}_f_down")
    sv.update(hn2_t=hn2_t, gu=gu, act_t=act_t)
    return h2, sv


def _layer_bwd(i, dh, p, pt, aux, sv, grads):
    j = i // 2
    dhb = dh.astype(BF16)
    grads["ffn_w_down"][i] = matmul(sv["act_t"], dhb, tm=FFN_HIDDEN // 2, tk=2048, name=f"l{i}_f_down_dw")
    dact = matmul(dhb, pt["ffn_w_down"][i], out_dtype=BF16, tn=FFN_HIDDEN // 2, name=f"l{i}_f_down_dx")
    dgu = swiglu_bwd(sv["gu"], dact, name=f"l{i}_f_act_bwd")
    grads["ffn_w_gate_up"][i] = matmul(sv["hn2_t"], dgu, tk=2048, name=f"l{i}_f_up_dw")
    dhn2 = matmul(dgu, pt["ffn_w_gate_up"][i], tk=2816, name=f"l{i}_f_up_dx")
    dh1, dg = rmsnorm_bwd(sv["h1"], p["ffn_norm"][i][None], [dhn2], dh, name=f"l{i}_norm2_bwd")
    grads["ffn_norm"][i] = dg[0]
    dh1b = dh1.astype(BF16)
    if i % 2 == 0:
        grads["a_w_out"][j] = matmul(sv["out"].T, dh1b, tk=2048, name=f"l{i}_a_out_dw")
        dout = matmul(dh1b, pt["a_w_out"][j], out_dtype=BF16, name=f"l{i}_a_out_dx")
        douts, deltas = attn_delta(dout, sv["out"], aux["ones"], DILS, name=f"l{i}_a_delta")
        dws, dhns, dgq, dgk = [], [], [], []
        for g, d in enumerate(DILS):
            qkvp, lse = sv["qkvp"][g], sv["lse"][g]
            dq = attn_bwd_dq(qkvp, douts[g], lse, deltas[g], d, name=f"l{i}_a_dq{g}")
            dk, dv = attn_bwd_dkv(qkvp, douts[g], lse, deltas[g], d, name=f"l{i}_a_dkv{g}")
            dqkv, dgain = qk_prep_bwd(sv["qkv"][g], aux["a_gain"][j][g], aux["cos"][g], aux["sin"][g], aux["ones"],
                                      [dq, dk, dv], name=f"l{i}_a_prep_bwd{g}")
            dgh = dgain[0].reshape(3, A_HEADS, A_HEAD_DIM).sum(axis=1)
            dgq.append(dgh[0])
            dgk.append(dgh[1])
            dws.append(matmul(sv["hns"][g].T, dqkv, tk=2048, name=f"l{i}_a_in_dw{g}"))
            dhns.append(matmul(dqkv, pt["a_w_in_g"][j][g], tk=3072, name=f"l{i}_a_in_dx{g}"))
        grads["a_q_norm"][j] = jnp.stack(dgq)
        grads["a_k_norm"][j] = jnp.stack(dgk)
        grads["a_w_in"][j] = jnp.concatenate(dws, axis=1)
        dh0, dg = rmsnorm_bwd(sv["h0"], p["attn_norm"][i][None], dhns, dh1, dils=DILS, name=f"l{i}_norm1_bwd")
    else:
        grads["b_w_out"][j] = matmul(sv["y"].T, dh1b, tk=2048, name=f"l{i}_b_out_dw")
        dy = matmul(dh1b, pt["b_w_out"][j], name=f"l{i}_b_out_dx")
        do, dr, dgn = gla_post_bwd(sv["o_f"], sv["o_b"], aux["b_gain"][j], sv["proj"], dy, name=f"l{i}_b_post_bwd")
        grads["b_out_norm"][j] = dgn[0].reshape(B_HEADS, B_VAL_DIM)
        pf = gla_bwd(sv["proj"], sv["loga"], sv["st_f"], do, aux["tri_f"], aux["tri_b"], False, name=f"l{i}_b_gla_f_bwd")
        pb = gla_bwd(sv["proj"], sv["loga"], sv["st_b"], do, aux["tri_b"], aux["tri_f"], True, name=f"l{i}_b_gla_b_bwd")
        dloga = jnp.concatenate([pf[3], pb[3]], axis=1)
        dz, dwblk, dbias = gate_bwd(sv["proj"], aux["b_wblk"][j], aux["b_wblk_t"][j], aux["b_bias"][j], dloga,
                                    name=f"l{i}_b_gate_bwd")
        grads["b_w_gate_f"][j] = dwblk[:B_GATE_RANK, :B_QK_WIDTH]
        grads["b_w_gate_b"][j] = dwblk[B_GATE_RANK:2 * B_GATE_RANK, B_QK_WIDTH:]
        grads["b_gate_bias_f"][j] = dbias[0, :B_QK_WIDTH]
        grads["b_gate_bias_b"][j] = dbias[0, B_QK_WIDTH:]
        dproj = gla_combine(pf[:3], pb[:3], dr, dz, name=f"l{i}_b_combine")
        grads["b_w_in"][j] = matmul(sv["hn_t"], dproj, tn=640, tk=2048, name=f"l{i}_b_in_dw")[:, :B_IN_WIDTH]
        dhn = matmul(dproj, pt["b_w_in"][j], tk=B_IN_PAD, name=f"l{i}_b_in_dx")
        dh0, dg = rmsnorm_bwd(sv["h0"], p["attn_norm"][i][None], [dhn], dh1, name=f"l{i}_norm1_bwd")
    grads["attn_norm"][i] = dg[0]
    return dh0


def _local_step(x, target, p, small):
    S = x.shape[0]
    cos, sin = _rope_tables(S)
    to_phase = lambda t, d: t.reshape(S // d, d, LANES).swapaxes(0, 1).reshape(S, LANES)
    cos, sin = [to_phase(cos, d) for d in DILS], [to_phase(sin, d) for d in DILS]
    ones_v = jnp.ones((A_WIDTH,), F32)
    a_gain = [[jnp.concatenate([jnp.tile(small["a_q_norm"][j][g], A_HEADS), jnp.tile(small["a_k_norm"][j][g], A_HEADS),
                                ones_v])[None] for g in range(len(DILS))] for j in range(2)]
    b_wblk = [_gate_block_weight(p["b_w_gate_f"][j].astype(F32), p["b_w_gate_b"][j].astype(F32)) for j in range(2)]
    aux = dict(cos=cos, sin=sin, ones=_head_block_ones(), a_gain=a_gain, tri_f=_tri(False), tri_b=_tri(True),
               b_wblk=b_wblk, b_wblk_t=[w.T for w in b_wblk],
               b_bias=[jnp.concatenate([small["b_gate_bias_f"][j], small["b_gate_bias_b"][j]])[None] for j in range(2)],
               b_gain=[small["b_out_norm"][j].reshape(1, B_V_WIDTH) for j in range(2)])
    pw = dict(p)
    pw["b_w_in"] = jnp.pad(p["b_w_in"], ((0, 0), (0, 0), (0, B_IN_PAD - B_IN_WIDTH)))
    pw["attn_norm"], pw["ffn_norm"] = small["attn_norm"], small["ffn_norm"]
    gw = 3 * A_WIDTH
    pw["a_w_in_g"] = [[p["a_w_in"][j][:, g * gw:(g + 1) * gw] for g in range(len(DILS))] for j in range(2)]
    pt = {n: jnp.swapaxes(pw[n], 1, 2) for n in ("a_w_out", "b_w_in", "b_w_out", "ffn_w_gate_up", "ffn_w_down")}
    pt["a_w_in_g"] = [[wg.T for wg in row] for row in pw["a_w_in_g"]]

    h = x
    saved = []
    for i in range(DEPTH):
        h, sv = _layer_fwd(i, h, pw, aux)
        saved.append(sv)
    loss_sq, dh = loss_head(h, target, name="loss_head")
    grads = {n: [None] * (DEPTH if n in ("attn_norm", "ffn_norm", "ffn_w_gate_up", "ffn_w_down") else 2) for n in WEIGHTS}
    for i in reversed(range(DEPTH)):
        dh = _layer_bwd(i, dh, pw, pt, aux, saved[i], grads)
    return loss_sq[0, 0] * (0.5 / D_MODEL), dh, grads


def kernel(x, attn_norm, ffn_norm, a_w_in, a_q_norm, a_k_norm, a_w_out, b_w_in, b_w_gate_f, b_gate_bias_f, b_w_gate_b, b_gate_bias_b, b_out_norm, b_w_out, ffn_w_gate_up, ffn_w_down, loss_target, m_attn_norm, m_ffn_norm, m_a_w_in, m_a_q_norm, m_a_k_norm, m_a_w_out, m_b_w_in, m_b_w_gate_f, m_b_gate_bias_f, m_b_w_gate_b, m_b_gate_bias_b, m_b_out_norm, m_b_w_out, m_ffn_w_gate_up, m_ffn_w_down, v_attn_norm, v_ffn_norm, v_a_w_in, v_a_q_norm, v_a_k_norm, v_a_w_out, v_b_w_in, v_b_w_gate_f, v_b_gate_bias_f, v_b_w_gate_b, v_b_gate_bias_b, v_b_out_norm, v_b_w_out, v_ffn_w_gate_up, v_ffn_w_down):
    w = dict(attn_norm=attn_norm, ffn_norm=ffn_norm, a_w_in=a_w_in, a_q_norm=a_q_norm, a_k_norm=a_k_norm, a_w_out=a_w_out,
             b_w_in=b_w_in, b_w_gate_f=b_w_gate_f, b_gate_bias_f=b_gate_bias_f, b_w_gate_b=b_w_gate_b,
             b_gate_bias_b=b_gate_bias_b, b_out_norm=b_out_norm, b_w_out=b_w_out, ffn_w_gate_up=ffn_w_gate_up,
             ffn_w_down=ffn_w_down)
    m = dict(attn_norm=m_attn_norm, ffn_norm=m_ffn_norm, a_w_in=m_a_w_in, a_q_norm=m_a_q_norm, a_k_norm=m_a_k_norm,
             a_w_out=m_a_w_out, b_w_in=m_b_w_in, b_w_gate_f=m_b_w_gate_f, b_gate_bias_f=m_b_gate_bias_f,
             b_w_gate_b=m_b_w_gate_b, b_gate_bias_b=m_b_gate_bias_b, b_out_norm=m_b_out_norm, b_w_out=m_b_w_out,
             ffn_w_gate_up=m_ffn_w_gate_up, ffn_w_down=m_ffn_w_down)
    v = dict(attn_norm=v_attn_norm, ffn_norm=v_ffn_norm, a_w_in=v_a_w_in, a_q_norm=v_a_q_norm, a_k_norm=v_a_k_norm,
             a_w_out=v_a_w_out, b_w_in=v_b_w_in, b_w_gate_f=v_b_w_gate_f, b_gate_bias_f=v_b_gate_bias_f,
             b_w_gate_b=v_b_w_gate_b, b_gate_bias_b=v_b_gate_bias_b, b_out_norm=v_b_out_norm, b_w_out=v_b_w_out,
             ffn_w_gate_up=v_ffn_w_gate_up, ffn_w_down=v_ffn_w_down)

    full = _gather_weights(w)
    p = {n: full[n] for n in BIG}
    small = {n: full[n] for n in SMALL_SHARDED}
    small.update({n: w[n] for n in REPLICATED})
    loss_local, dx, grads = _local_step(x[0], loss_target[0], p, small)
    loss = lax.psum(loss_local, ("x", "y", "c"))

    g = _reduce_gradients(grads, w)
    delta, new_m, new_v = {}, {}, {}
    rows = lambda t: t.reshape(-1, t.shape[-1])
    for n in MATRICES:
        outs = adamw(rows(w[n]), rows(g[n]), rows(m[n]), rows(v[n]), name=f"adamw_{n}")
        delta[n], new_m[n], new_v[n] = [o.reshape(w[n].shape) for o in outs]
    rest = [n for n in WEIGHTS if n not in MATRICES]
    flat = lambda d: _to_flat([d[n] for n in rest], 8)
    outs = adamw(flat(w), flat(g), flat(m), flat(v), name="adamw_vectors")
    for d, o in zip((delta, new_m, new_v), outs):
        d.update(zip(rest, _from_flat(o, [w[n].shape for n in rest])))
    return (loss, dx[None], *[g[n] for n in WEIGHTS], *[delta[n] for n in WEIGHTS],
            *[new_m[n] for n in WEIGHTS], *[new_v[n] for n in WEIGHTS])
```
